```python
import math
import jax, jax.numpy as jnp
from jax import lax
import numpy as np

D_MODEL = 1024
BATCH = 8
SEQ = 4096
DEPTH = 4

N_A_LAYERS = DEPTH // 2
N_B_LAYERS = DEPTH - N_A_LAYERS
CONV_WIDTH = 31
HEAD_DIM = 64
N_HEADS = D_MODEL // HEAD_DIM
N_KV_HEADS = N_HEADS // 4
GROUP = N_HEADS // N_KV_HEADS
WINDOW = 128
BLOCK = 128
N_BUCKETS = 32
MAX_DISTANCE = 128
D_FF = -(-8 * D_MODEL // (3 * 256)) * 256
EPS = 1e-6
NEG_INF = -1e30

kernel_name = "yoco_conformer_swa_sink_hybrid"


def rmsnorm(x, g):
    xf = x.astype(jnp.float32)
    xf = xf * lax.rsqrt(jnp.mean(xf * xf, axis=-1, keepdims=True) + EPS)
    return (xf * g.astype(jnp.float32)).astype(x.dtype)


def layernorm(x, g, b):
    xf = x.astype(jnp.float32)
    mu = jnp.mean(xf, axis=-1, keepdims=True)
    var = jnp.mean(jnp.square(xf - mu), axis=-1, keepdims=True)
    y = (xf - mu) * lax.rsqrt(var + EPS) * g.astype(jnp.float32) + b.astype(jnp.float32)
    return y.astype(x.dtype)


def swiglu_ffn(x, w_up, w_down):
    gate, up = jnp.split(x @ w_up, 2, axis=-1)
    return (jax.nn.silu(gate) * up) @ w_down


def conformer_conv(x, w_pw1, b_pw1, w_dw, b_dw, ln_g, ln_b, w_pw2, b_pw2):
    a = jax.nn.glu(x @ w_pw1 + b_pw1, axis=-1)
    y = lax.conv_general_dilated(
        a, w_dw[:, None, :].astype(a.dtype), window_strides=(1,),
        padding=((CONV_WIDTH - 1, 0),),
        dimension_numbers=('NWC', 'WIO', 'NWC'),
        feature_group_count=D_MODEL) + b_dw
    y = jax.nn.silu(layernorm(y, ln_g, ln_b))
    return y @ w_pw2 + b_pw2


def t5_causal_bucket(dist):
    max_exact = N_BUCKETS // 2
    d = jnp.maximum(dist, 0)
    log_ratio = jnp.log(jnp.maximum(d, 1).astype(jnp.float32) / max_exact) / math.log(MAX_DISTANCE / max_exact)
    large = max_exact + (log_ratio * (N_BUCKETS - max_exact)).astype(jnp.int32)
    large = jnp.minimum(large, N_BUCKETS - 1)
    return jnp.where(d < max_exact, d, large)


def banded_sink_attention(q, k, v, sinks, rel_bias):
    B, S = q.shape[0], q.shape[1]
    nb = S // BLOCK
    qb = q.reshape(B, nb, BLOCK, N_KV_HEADS, GROUP, HEAD_DIM)
    kb = k.reshape(B, nb, BLOCK, N_KV_HEADS, HEAD_DIM)
    vb = v.reshape(B, nb, BLOCK, N_KV_HEADS, HEAD_DIM)
    pad = ((0, 0), (1, 0), (0, 0), (0, 0), (0, 0))
    k_band = jnp.concatenate([jnp.pad(kb, pad)[:, :-1], kb], axis=2)
    v_band = jnp.concatenate([jnp.pad(vb, pad)[:, :-1], vb], axis=2)

    s = jnp.einsum('bnqhgd,bnkhd->bnhgqk', qb, k_band,
                   preferred_element_type=jnp.float32) * (HEAD_DIM ** -0.5)

    qi = jnp.arange(BLOCK, dtype=jnp.int32)
    kj = jnp.arange(2 * BLOCK, dtype=jnp.int32)
    dist = qi[:, None] + BLOCK - kj[None, :]
    in_window = (dist >= 0) & (dist < WINDOW)
    bias = rel_bias.astype(jnp.float32)[t5_causal_bucket(dist)]
    bias = jnp.transpose(bias, (2, 0, 1)).reshape(N_KV_HEADS, GROUP, BLOCK, 2 * BLOCK)
    key_pos = (jnp.arange(nb, dtype=jnp.int32)[:, None] - 1) * BLOCK + kj[None, :]
    mask = in_window[None, :, :] & (key_pos >= 0)[:, None, :]

    s = jnp.where(mask[None, :, None, None], s + bias, NEG_INF)
    sink = sinks.astype(jnp.float32).reshape(N_KV_HEADS, GROUP, 1, 1)
    m = jnp.maximum(jnp.max(s, axis=-1, keepdims=True), sink)
    p = jnp.exp(s - m)
    probs = p / (jnp.sum(p, axis=-1, keepdims=True) + jnp.exp(sink - m))
    o = jnp.einsum('bnhgqk,bnkhd->bnqhgd', probs.astype(v.dtype), v_band)
    return o.reshape(B, S, N_HEADS * HEAD_DIM)


def _fwd_setup_inputs(seed: int = 0) -> dict:
    key = jax.random.key(seed)
    ks = jax.random.split(key, 24)
    f32 = jnp.float32
    D, HD, KVD = D_MODEL, N_HEADS * HEAD_DIM, N_KV_HEADS * HEAD_DIM
    nrm = lambda k, shape, scale: jax.random.normal(k, shape, f32) * scale
    gain = lambda k, shape: 1.0 + 0.05 * jax.random.normal(k, shape, f32)
    return {
        "x": jax.random.normal(ks[0], (BATCH, SEQ, D), f32),
        "norm_mix": gain(ks[1], (DEPTH, D)),
        "norm_ffn": gain(ks[2], (DEPTH, D)),
        "conv_w_pw1": nrm(ks[3], (N_A_LAYERS, D, 2 * D), D ** -0.5),
        "conv_b_pw1": nrm(ks[4], (N_A_LAYERS, 2 * D), 0.02),
        "conv_w_dw": nrm(ks[5], (N_A_LAYERS, CONV_WIDTH, D), CONV_WIDTH ** -0.5),
        "conv_b_dw": nrm(ks[6], (N_A_LAYERS, D), 0.02),
        "conv_ln_g": gain(ks[7], (N_A_LAYERS, D)),
        "conv_ln_b": nrm(ks[8], (N_A_LAYERS, D), 0.02),
        "conv_w_pw2": nrm(ks[9], (N_A_LAYERS, D, D), D ** -0.5),
        "conv_b_pw2": nrm(ks[10], (N_A_LAYERS, D), 0.02),
        "norm_kv": gain(ks[11], (D,)),
        "w_kv": nrm(ks[12], (D, 2 * KVD), D ** -0.5),
        "w_q": nrm(ks[13], (N_B_LAYERS, D, HD), D ** -0.5),
        "w_o": nrm(ks[14], (N_B_LAYERS, HD, D), HD ** -0.5),
        "sinks": nrm(ks[15], (N_B_LAYERS, N_HEADS), 0.5),
        "rel_bias": nrm(ks[16], (N_BUCKETS, N_HEADS), 0.5),
        "ffn_w_up": nrm(ks[17], (DEPTH, D, 2 * D_FF), D ** -0.5),
        "ffn_w_down": nrm(ks[18], (DEPTH, D_FF, D), D_FF ** -0.5),
        "norm_final": gain(ks[19], (D,)),
    }


def _fwd_reference(x, norm_mix, norm_ffn, conv_w_pw1, conv_b_pw1, conv_w_dw, conv_b_dw,
              conv_ln_g, conv_ln_b, conv_w_pw2, conv_b_pw2, norm_kv, w_kv, w_q, w_o,
              sinks, rel_bias, ffn_w_up, ffn_w_down, norm_final):
    B, S = x.shape[0], x.shape[1]
    h = x
    k_shared = v_shared = None
    for l in range(DEPTH):
        if l < N_A_LAYERS:
            i = l
            h = h + conformer_conv(rmsnorm(h, norm_mix[l]), conv_w_pw1[i], conv_b_pw1[i],
                                   conv_w_dw[i], conv_b_dw[i], conv_ln_g[i], conv_ln_b[i],
                                   conv_w_pw2[i], conv_b_pw2[i])
        else:
            if l == N_A_LAYERS:
                kv = rmsnorm(h, norm_kv) @ w_kv
                k_flat, v_flat = jnp.split(kv, 2, axis=-1)
                k_shared = k_flat.reshape(B, S, N_KV_HEADS, HEAD_DIM)
                v_shared = v_flat.reshape(B, S, N_KV_HEADS, HEAD_DIM)
            j = l - N_A_LAYERS
            q = (rmsnorm(h, norm_mix[l]) @ w_q[j]).reshape(B, S, N_HEADS, HEAD_DIM)
            attn = banded_sink_attention(q, k_shared, v_shared, sinks[j], rel_bias)
            h = h + attn @ w_o[j]
        h = h + swiglu_ffn(rmsnorm(h, norm_ffn[l]), ffn_w_up[l], ffn_w_down[l])
    return rmsnorm(h, norm_final)


import jax as _jax
import jax.numpy as _jnp

TWIN_FORMAT = 'train_step'
FWD_PARAMS = ['x', 'norm_mix', 'norm_ffn', 'conv_w_pw1', 'conv_b_pw1', 'conv_w_dw', 'conv_b_dw', 'conv_ln_g', 'conv_ln_b', 'conv_w_pw2', 'conv_b_pw2', 'norm_kv', 'w_kv', 'w_q', 'w_o', 'sinks', 'rel_bias', 'ffn_w_up', 'ffn_w_down', 'norm_final']
TWIN_WEIGHTS = ['norm_mix', 'norm_ffn', 'conv_w_pw1', 'conv_b_pw1', 'conv_w_dw', 'conv_b_dw', 'conv_ln_g', 'conv_ln_b', 'conv_w_pw2', 'conv_b_pw2', 'norm_kv', 'w_kv', 'w_q', 'w_o', 'sinks', 'rel_bias', 'ffn_w_up', 'ffn_w_down', 'norm_final']
TWIN_DIFF_INPUT = 'x'
TWIN_INPUTS = ['x', 'norm_mix', 'norm_ffn', 'conv_w_pw1', 'conv_b_pw1', 'conv_w_dw', 'conv_b_dw', 'conv_ln_g', 'conv_ln_b', 'conv_w_pw2', 'conv_b_pw2', 'norm_kv', 'w_kv', 'w_q', 'w_o', 'sinks', 'rel_bias', 'ffn_w_up', 'ffn_w_down', 'norm_final', 'loss_target', 'm_norm_mix', 'm_norm_ffn', 'm_conv_w_pw1', 'm_conv_b_pw1', 'm_conv_w_dw', 'm_conv_b_dw', 'm_conv_ln_g', 'm_conv_ln_b', 'm_conv_w_pw2', 'm_conv_b_pw2', 'm_norm_kv', 'm_w_kv', 'm_w_q', 'm_w_o', 'm_sinks', 'm_rel_bias', 'm_ffn_w_up', 'm_ffn_w_down', 'm_norm_final', 'v_norm_mix', 'v_norm_ffn', 'v_conv_w_pw1', 'v_conv_b_pw1', 'v_conv_w_dw', 'v_conv_b_dw', 'v_conv_ln_g', 'v_conv_ln_b', 'v_conv_w_pw2', 'v_conv_b_pw2', 'v_norm_kv', 'v_w_kv', 'v_w_q', 'v_w_o', 'v_sinks', 'v_rel_bias', 'v_ffn_w_up', 'v_ffn_w_down', 'v_norm_final']
TWIN_OUTPUTS = ['loss', 'grad_x', 'grad_norm_mix', 'grad_norm_ffn', 'grad_conv_w_pw1', 'grad_conv_b_pw1', 'grad_conv_w_dw', 'grad_conv_b_dw', 'grad_conv_ln_g', 'grad_conv_ln_b', 'grad_conv_w_pw2', 'grad_conv_b_pw2', 'grad_norm_kv', 'grad_w_kv', 'grad_w_q', 'grad_w_o', 'grad_sinks', 'grad_rel_bias', 'grad_ffn_w_up', 'grad_ffn_w_down', 'grad_norm_final', 'delta_norm_mix', 'delta_norm_ffn', 'delta_conv_w_pw1', 'delta_conv_b_pw1', 'delta_conv_w_dw', 'delta_conv_b_dw', 'delta_conv_ln_g', 'delta_conv_ln_b', 'delta_conv_w_pw2', 'delta_conv_b_pw2', 'delta_norm_kv', 'delta_w_kv', 'delta_w_q', 'delta_w_o', 'delta_sinks', 'delta_rel_bias', 'delta_ffn_w_up', 'delta_ffn_w_down', 'delta_norm_final', 'new_m_norm_mix', 'new_m_norm_ffn', 'new_m_conv_w_pw1', 'new_m_conv_b_pw1', 'new_m_conv_w_dw', 'new_m_conv_b_dw', 'new_m_conv_ln_g', 'new_m_conv_ln_b', 'new_m_conv_w_pw2', 'new_m_conv_b_pw2', 'new_m_norm_kv', 'new_m_w_kv', 'new_m_w_q', 'new_m_w_o', 'new_m_sinks', 'new_m_rel_bias', 'new_m_ffn_w_up', 'new_m_ffn_w_down', 'new_m_norm_final', 'new_v_norm_mix', 'new_v_norm_ffn', 'new_v_conv_w_pw1', 'new_v_conv_b_pw1', 'new_v_conv_w_dw', 'new_v_conv_b_dw', 'new_v_conv_ln_g', 'new_v_conv_ln_b', 'new_v_conv_w_pw2', 'new_v_conv_b_pw2', 'new_v_norm_kv', 'new_v_w_kv', 'new_v_w_q', 'new_v_w_o', 'new_v_sinks', 'new_v_rel_bias', 'new_v_ffn_w_up', 'new_v_ffn_w_down', 'new_v_norm_final']
TWIN_LEAF_KINDS = {'loss': 'loss', 'grad_x': 'grad_x', 'grad_norm_mix': 'grad_w', 'grad_norm_ffn': 'grad_w', 'grad_conv_w_pw1': 'grad_w', 'grad_conv_b_pw1': 'grad_w', 'grad_conv_w_dw': 'grad_w', 'grad_conv_b_dw': 'grad_w', 'grad_conv_ln_g': 'grad_w', 'grad_conv_ln_b': 'grad_w', 'grad_conv_w_pw2': 'grad_w', 'grad_conv_b_pw2': 'grad_w', 'grad_norm_kv': 'grad_w', 'grad_w_kv': 'grad_w', 'grad_w_q': 'grad_w', 'grad_w_o': 'grad_w', 'grad_sinks': 'grad_w', 'grad_rel_bias': 'grad_w', 'grad_ffn_w_up': 'grad_w', 'grad_ffn_w_down': 'grad_w', 'grad_norm_final': 'grad_w', 'delta_norm_mix': 'delta_w', 'delta_norm_ffn': 'delta_w', 'delta_conv_w_pw1': 'delta_w', 'delta_conv_b_pw1': 'delta_w', 'delta_conv_w_dw': 'delta_w', 'delta_conv_b_dw': 'delta_w', 'delta_conv_ln_g': 'delta_w', 'delta_conv_ln_b': 'delta_w', 'delta_conv_w_pw2': 'delta_w', 'delta_conv_b_pw2': 'delta_w', 'delta_norm_kv': 'delta_w', 'delta_w_kv': 'delta_w', 'delta_w_q': 'delta_w', 'delta_w_o': 'delta_w', 'delta_sinks': 'delta_w', 'delta_rel_bias': 'delta_w', 'delta_ffn_w_up': 'delta_w', 'delta_ffn_w_down': 'delta_w', 'delta_norm_final': 'delta_w', 'new_m_norm_mix': 'new_m', 'new_m_norm_ffn': 'new_m', 'new_m_conv_w_pw1': 'new_m', 'new_m_conv_b_pw1': 'new_m', 'new_m_conv_w_dw': 'new_m', 'new_m_conv_b_dw': 'new_m', 'new_m_conv_ln_g': 'new_m', 'new_m_conv_ln_b': 'new_m', 'new_m_conv_w_pw2': 'new_m', 'new_m_conv_b_pw2': 'new_m', 'new_m_norm_kv': 'new_m', 'new_m_w_kv': 'new_m', 'new_m_w_q': 'new_m', 'new_m_w_o': 'new_m', 'new_m_sinks': 'new_m', 'new_m_rel_bias': 'new_m', 'new_m_ffn_w_up': 'new_m', 'new_m_ffn_w_down': 'new_m', 'new_m_norm_final': 'new_m', 'new_v_norm_mix': 'new_v', 'new_v_norm_ffn': 'new_v', 'new_v_conv_w_pw1': 'new_v', 'new_v_conv_b_pw1': 'new_v', 'new_v_conv_w_dw': 'new_v', 'new_v_conv_b_dw': 'new_v', 'new_v_conv_ln_g': 'new_v', 'new_v_conv_ln_b': 'new_v', 'new_v_conv_w_pw2': 'new_v', 'new_v_conv_b_pw2': 'new_v', 'new_v_norm_kv': 'new_v', 'new_v_w_kv': 'new_v', 'new_v_w_q': 'new_v', 'new_v_w_o': 'new_v', 'new_v_sinks': 'new_v', 'new_v_rel_bias': 'new_v', 'new_v_ffn_w_up': 'new_v', 'new_v_ffn_w_down': 'new_v', 'new_v_norm_final': 'new_v'}


def _forward(args):
    return _fwd_reference(*[args[k] for k in FWD_PARAMS])


def _output_shape():
    out = _jax.eval_shape(lambda: _forward(_fwd_setup_inputs(0)))
    return out.shape, out.dtype

N_MICROBATCH = 1
ADAM_LR = 0.001
ADAM_B1 = 0.9
ADAM_B2 = 0.999
ADAM_EPS = 1e-08
ADAM_WD = 0.01
ADAM_STEP = 10
PER_EXAMPLE_BATCH_AXIS = {'x': 0, 'loss_target': 0}
SHARED_INPUTS = []
_WEIGHT_DTYPES = {'norm_mix': _jnp.float32, 'norm_ffn': _jnp.float32, 'conv_w_pw1': _jnp.float32, 'conv_b_pw1': _jnp.float32, 'conv_w_dw': _jnp.float32, 'conv_b_dw': _jnp.float32, 'conv_ln_g': _jnp.float32, 'conv_ln_b': _jnp.float32, 'conv_w_pw2': _jnp.float32, 'conv_b_pw2': _jnp.float32, 'norm_kv': _jnp.float32, 'w_kv': _jnp.float32, 'w_q': _jnp.float32, 'w_o': _jnp.float32, 'sinks': _jnp.float32, 'rel_bias': _jnp.float32, 'ffn_w_up': _jnp.float32, 'ffn_w_down': _jnp.float32, 'norm_final': _jnp.float32}
MOMENT_SCALE = {'norm_mix': 8.542115e-02, 'norm_ffn': 1.149033e-01, 'conv_w_pw1': 8.658367e-02, 'conv_b_pw1': 1.589419e-01, 'conv_w_dw': 1.159149e-01, 'conv_b_dw': 3.237915e-01, 'conv_ln_g': 1.832162e-01, 'conv_ln_b': 2.162590e-01, 'conv_w_pw2': 1.309749e-01, 'conv_b_pw2': 4.265874e-01, 'norm_kv': 7.292080e-02, 'w_kv': 1.032781e-01, 'w_q': 2.090423e-02, 'w_o': 6.146573e-02, 'sinks': 1.871335e-02, 'rel_bias': 3.824116e-02, 'ffn_w_up': 4.773662e-02, 'ffn_w_down': 7.860366e-02, 'norm_final': 3.211252e+01}


def _to_microbatches(a, axis):
    t = _jnp.moveaxis(a, axis, 0)
    t = t.reshape((N_MICROBATCH, t.shape[0] // N_MICROBATCH) + t.shape[1:])
    return _jnp.moveaxis(t, 1, axis + 1)


def setup_inputs(seed: int = 0) -> dict:
    inp = _fwd_setup_inputs(seed)
    key = _jax.random.fold_in(_jax.random.key(seed), 7919)
    shape, _ = _output_shape()
    out = dict(inp)
    out["loss_target"] = _jax.random.normal(_jax.random.fold_in(key, 0), shape, _jnp.float32)
    for i, name in enumerate(TWIN_WEIGHTS):
        w = inp[name].astype(_jnp.float32)
        if MOMENT_SCALE is None:
            s = _jnp.sqrt(_jnp.mean(_jnp.square(w)) + 1e-30)
        else:
            s = MOMENT_SCALE[name]
        km, kv = _jax.random.split(_jax.random.fold_in(key, i + 1))
        out[name] = w
        out["m_" + name] = s * _jax.random.normal(km, w.shape, _jnp.float32)
        out["v_" + name] = (s * s) * _jax.random.uniform(kv, w.shape, _jnp.float32, 0.5, 1.5)
    if N_MICROBATCH > 1:
        for name, axis in PER_EXAMPLE_BATCH_AXIS.items():
            out[name] = _to_microbatches(out[name], axis)
    return {'x': out['x'], 'norm_mix': out['norm_mix'], 'norm_ffn': out['norm_ffn'], 'conv_w_pw1': out['conv_w_pw1'], 'conv_b_pw1': out['conv_b_pw1'], 'conv_w_dw': out['conv_w_dw'], 'conv_b_dw': out['conv_b_dw'], 'conv_ln_g': out['conv_ln_g'], 'conv_ln_b': out['conv_ln_b'], 'conv_w_pw2': out['conv_w_pw2'], 'conv_b_pw2': out['conv_b_pw2'], 'norm_kv': out['norm_kv'], 'w_kv': out['w_kv'], 'w_q': out['w_q'], 'w_o': out['w_o'], 'sinks': out['sinks'], 'rel_bias': out['rel_bias'], 'ffn_w_up': out['ffn_w_up'], 'ffn_w_down': out['ffn_w_down'], 'norm_final': out['norm_final'], 'loss_target': out['loss_target'], 'm_norm_mix': out['m_norm_mix'], 'm_norm_ffn': out['m_norm_ffn'], 'm_conv_w_pw1': out['m_conv_w_pw1'], 'm_conv_b_pw1': out['m_conv_b_pw1'], 'm_conv_w_dw': out['m_conv_w_dw'], 'm_conv_b_dw': out['m_conv_b_dw'], 'm_conv_ln_g': out['m_conv_ln_g'], 'm_conv_ln_b': out['m_conv_ln_b'], 'm_conv_w_pw2': out['m_conv_w_pw2'], 'm_conv_b_pw2': out['m_conv_b_pw2'], 'm_norm_kv': out['m_norm_kv'], 'm_w_kv': out['m_w_kv'], 'm_w_q': out['m_w_q'], 'm_w_o': out['m_w_o'], 'm_sinks': out['m_sinks'], 'm_rel_bias': out['m_rel_bias'], 'm_ffn_w_up': out['m_ffn_w_up'], 'm_ffn_w_down': out['m_ffn_w_down'], 'm_norm_final': out['m_norm_final'], 'v_norm_mix': out['v_norm_mix'], 'v_norm_ffn': out['v_norm_ffn'], 'v_conv_w_pw1': out['v_conv_w_pw1'], 'v_conv_b_pw1': out['v_conv_b_pw1'], 'v_conv_w_dw': out['v_conv_w_dw'], 'v_conv_b_dw': out['v_conv_b_dw'], 'v_conv_ln_g': out['v_conv_ln_g'], 'v_conv_ln_b': out['v_conv_ln_b'], 'v_conv_w_pw2': out['v_conv_w_pw2'], 'v_conv_b_pw2': out['v_conv_b_pw2'], 'v_norm_kv': out['v_norm_kv'], 'v_w_kv': out['v_w_kv'], 'v_w_q': out['v_w_q'], 'v_w_o': out['v_w_o'], 'v_sinks': out['v_sinks'], 'v_rel_bias': out['v_rel_bias'], 'v_ffn_w_up': out['v_ffn_w_up'], 'v_ffn_w_down': out['v_ffn_w_down'], 'v_norm_final': out['v_norm_final']}


def _loss(weights, diff, rest, loss_target):
    with _jax.named_scope("forward"):
        args = {**rest, TWIN_DIFF_INPUT: diff, **{k: w.astype(_WEIGHT_DTYPES[k]) for k, w in weights.items()}}
        y = _forward(args)
    with _jax.named_scope("loss_head"):
        err = _jnp.square(y.astype(_jnp.float32) - loss_target)
        return 0.5 * _jnp.sum(_jnp.mean(err, axis=-1)) if err.ndim else 0.5 * err


def _adamw(w, g, m, v):
    m = ADAM_B1 * m + (1.0 - ADAM_B1) * g
    v = ADAM_B2 * v + (1.0 - ADAM_B2) * _jnp.square(g)
    m_hat = m / (1.0 - ADAM_B1 ** ADAM_STEP)
    v_hat = v / (1.0 - ADAM_B2 ** ADAM_STEP)
    delta = -ADAM_LR * (m_hat / (_jnp.sqrt(v_hat) + ADAM_EPS) + ADAM_WD * w)
    return delta, m, v


def reference(x, norm_mix, norm_ffn, conv_w_pw1, conv_b_pw1, conv_w_dw, conv_b_dw, conv_ln_g, conv_ln_b, conv_w_pw2, conv_b_pw2, norm_kv, w_kv, w_q, w_o, sinks, rel_bias, ffn_w_up, ffn_w_down, norm_final, loss_target, m_norm_mix, m_norm_ffn, m_conv_w_pw1, m_conv_b_pw1, m_conv_w_dw, m_conv_b_dw, m_conv_ln_g, m_conv_ln_b, m_conv_w_pw2, m_conv_b_pw2, m_norm_kv, m_w_kv, m_w_q, m_w_o, m_sinks, m_rel_bias, m_ffn_w_up, m_ffn_w_down, m_norm_final, v_norm_mix, v_norm_ffn, v_conv_w_pw1, v_conv_b_pw1, v_conv_w_dw, v_conv_b_dw, v_conv_ln_g, v_conv_ln_b, v_conv_w_pw2, v_conv_b_pw2, v_norm_kv, v_w_kv, v_w_q, v_w_o, v_sinks, v_rel_bias, v_ffn_w_up, v_ffn_w_down, v_norm_final):
    given = dict(x=x, norm_mix=norm_mix, norm_ffn=norm_ffn, conv_w_pw1=conv_w_pw1, conv_b_pw1=conv_b_pw1, conv_w_dw=conv_w_dw, conv_b_dw=conv_b_dw, conv_ln_g=conv_ln_g, conv_ln_b=conv_ln_b, conv_w_pw2=conv_w_pw2, conv_b_pw2=conv_b_pw2, norm_kv=norm_kv, w_kv=w_kv, w_q=w_q, w_o=w_o, sinks=sinks, rel_bias=rel_bias, ffn_w_up=ffn_w_up, ffn_w_down=ffn_w_down, norm_final=norm_final, loss_target=loss_target, m_norm_mix=m_norm_mix, m_norm_ffn=m_norm_ffn, m_conv_w_pw1=m_conv_w_pw1, m_conv_b_pw1=m_conv_b_pw1, m_conv_w_dw=m_conv_w_dw, m_conv_b_dw=m_conv_b_dw, m_conv_ln_g=m_conv_ln_g, m_conv_ln_b=m_conv_ln_b, m_conv_w_pw2=m_conv_w_pw2, m_conv_b_pw2=m_conv_b_pw2, m_norm_kv=m_norm_kv, m_w_kv=m_w_kv, m_w_q=m_w_q, m_w_o=m_w_o, m_sinks=m_sinks, m_rel_bias=m_rel_bias, m_ffn_w_up=m_ffn_w_up, m_ffn_w_down=m_ffn_w_down, m_norm_final=m_norm_final, v_norm_mix=v_norm_mix, v_norm_ffn=v_norm_ffn, v_conv_w_pw1=v_conv_w_pw1, v_conv_b_pw1=v_conv_b_pw1, v_conv_w_dw=v_conv_w_dw, v_conv_b_dw=v_conv_b_dw, v_conv_ln_g=v_conv_ln_g, v_conv_ln_b=v_conv_ln_b, v_conv_w_pw2=v_conv_w_pw2, v_conv_b_pw2=v_conv_b_pw2, v_norm_kv=v_norm_kv, v_w_kv=v_w_kv, v_w_q=v_w_q, v_w_o=v_w_o, v_sinks=v_sinks, v_rel_bias=v_rel_bias, v_ffn_w_up=v_ffn_w_up, v_ffn_w_down=v_ffn_w_down, v_norm_final=v_norm_final)
    weights = {n: given[n] for n in TWIN_WEIGHTS}
    shared = {n: given[n] for n in SHARED_INPUTS}
    per_example = {n: given[n] for n in ['x']}
    grad_fn = _jax.value_and_grad(_loss, argnums=(0, 1))

    def one_microbatch(ex, loss_target):
        ex = dict(ex)
        diff = ex.pop(TWIN_DIFF_INPUT)
        return grad_fn(weights, diff, {**shared, **ex}, loss_target)

    if N_MICROBATCH == 1:
        loss, (grad_w, grad_x) = one_microbatch(per_example, given["loss_target"])
    else:
        def body(carry, xs):
            loss_sum, grad_sum = carry
            l_k, (gw_k, gx_k) = one_microbatch(xs[0], xs[1])
            with _jax.named_scope("update"):
                return (loss_sum + l_k, _jax.tree.map(_jnp.add, grad_sum, gw_k)), gx_k

        init = (_jnp.zeros((), _jnp.float32), _jax.tree.map(_jnp.zeros_like, weights))
        (loss, grad_w), grad_x = _jax.lax.scan(body, init, (per_example, given["loss_target"]))
    with _jax.named_scope("update"):
        delta_w, new_m, new_v = {}, {}, {}
        for n in TWIN_WEIGHTS:
            delta_w[n], new_m[n], new_v[n] = _adamw(weights[n], grad_w[n], given["m_" + n], given["v_" + n])
    return (loss, grad_x, *[grad_w[n] for n in TWIN_WEIGHTS], *[delta_w[n] for n in TWIN_WEIGHTS],
            *[new_m[n] for n in TWIN_WEIGHTS], *[new_v[n] for n in TWIN_WEIGHTS])
```

```python
import functools
import math

import numpy as np
import jax
import jax.numpy as jnp
from jax import lax
from jax.experimental import pallas as pl
from jax.experimental.pallas import tpu as pltpu

F32 = jnp.float32
BF16 = jnp.bfloat16

D_MODEL = 1024
D_FF = 2816
N_HEADS = 16
N_KV_HEADS = 4
GROUP = N_HEADS // N_KV_HEADS
HEAD_DIM = 64
KV_DIM = N_KV_HEADS * HEAD_DIM
BLOCK = 128
CONV_WIDTH = 31
HALO = 32
N_BUCKETS = 32
MAX_DISTANCE = 128
EPS = 1e-6
NEG_INF = -1e30
N_DEV = 8
FF_CHUNK = D_FF // 4
PACK_ROWS = 40
ROW_BDW, ROW_LNG, ROW_LNB, ROW_BPW2, ROW_BPW1 = 31, 32, 33, 34, 35
REP_ROWS = 16
ROW_NKV, ROW_NFIN, ROW_SINK, ROW_RELB, ROW_LOSS = 8, 9, 10, 11, 12

ADAM_LR, ADAM_B1, ADAM_B2, ADAM_EPS, ADAM_WD, ADAM_STEP = 0.001, 0.9, 0.999, 1e-08, 0.01, 10

VMEM_LIMIT_BYTES = 56 * 1024 * 1024
ANY = pl.BlockSpec(memory_space=pl.ANY)
MESH = pl.DeviceIdType.MESH

NN = (((1,), (0,)), ((), ()))
NT = (((1,), (1,)), ((), ()))
TN = (((0,), (0,)), ((), ()))


def _dot(a, b, dims):
    return lax.dot_general(a, b, dims, preferred_element_type=F32)


def _pcall(body, name, out_shape, *, grid=None, in_specs=None, out_specs=None, scratch=(), sem=None, **kw):
    params = pltpu.CompilerParams(dimension_semantics=sem, vmem_limit_bytes=VMEM_LIMIT_BYTES)
    extra = {} if grid is None else {"grid": grid}
    return pl.pallas_call(body, name=name, out_shape=out_shape, in_specs=in_specs, out_specs=out_specs,
                          scratch_shapes=list(scratch), compiler_params=params, **extra, **kw)


def _sds(shape, dtype):
    return jax.ShapeDtypeStruct(tuple(shape), dtype)


def _row_tile(s, want):
    return want if s % want == 0 else s


def rms_fwd(h, g, name):
    s, d = h.shape
    tm = _row_tile(s, 512)

    def body(h_ref, g_ref, u_ref):
        x = h_ref[...]
        r = lax.rsqrt(jnp.mean(x * x, axis=-1, keepdims=True) + EPS)
        u_ref[...] = (x * r * g_ref[...]).astype(BF16)

    return _pcall(body, name, _sds((s, d), BF16), grid=(s // tm,),
                  in_specs=[pl.BlockSpec((tm, d), lambda i: (i, 0)), pl.BlockSpec((1, d), lambda i: (0, 0))],
                  out_specs=pl.BlockSpec((tm, d), lambda i: (i, 0)), sem=("parallel",))(h, g)


def rms_bwd(h, g, du, dh_in, name):
    s, d = h.shape
    tm = _row_tile(s, 512)

    def body(h_ref, g_ref, du_ref, dhi_ref, dh_ref, dg_ref):
        i = pl.program_id(0)
        x = h_ref[...]
        r = lax.rsqrt(jnp.mean(x * x, axis=-1, keepdims=True) + EPS)
        xh = x * r
        du_v = du_ref[...]
        dxh = du_v * g_ref[...]
        dx = r * (dxh - xh * jnp.mean(dxh * xh, axis=-1, keepdims=True))
        dh_ref[...] = dhi_ref[...] + dx
        part = jnp.sum(du_v * xh, axis=0, keepdims=True)

        @pl.when(i == 0)
        def _():
            dg_ref[...] = part

        @pl.when(i > 0)
        def _():
            dg_ref[...] += part

    row = pl.BlockSpec((tm, d), lambda i: (i, 0))
    vec = pl.BlockSpec((1, d), lambda i: (0, 0))
    return _pcall(body, name, (_sds((s, d), F32), _sds((1, d), F32)), grid=(s // tm,),
                  in_specs=[row, vec, row, row], out_specs=(row, vec), sem=("arbitrary",))(h, g, du, dh_in)


def _mm(name, a, b, *, dims, grid, a_spec, b_spec, o_spec, o_shape, nk=1, acc_shape=None,
        bias=None, res=None, colsum=None, sem=None):
    n_axes = len(grid)

    def body(*refs):
        it = iter(refs)
        a_ref, b_ref = next(it), next(it)
        bias_ref = next(it) if bias is not None else None
        res_ref = next(it) if res is not None else None
        o_ref = next(it)
        cs_ref = next(it) if colsum is not None else None
        acc_ref = next(it) if nk > 1 else None
        k = pl.program_id(n_axes - 1)
        p = _dot(a_ref[...].astype(BF16), b_ref[...].astype(BF16), dims)

        def finish(acc):
            if bias_ref is not None:
                acc = acc + bias_ref[...]
            if res_ref is not None:
                acc = acc + res_ref[...]
            o_ref[...] = acc.astype(o_ref.dtype)

        if nk == 1:
            finish(p)
        else:
            @pl.when(k == 0)
            def _():
                acc_ref[...] = p

            @pl.when(k > 0)
            def _():
                acc_ref[...] += p

            @pl.when(k == nk - 1)
            def _():
                finish(acc_ref[...])

        if cs_ref is not None:
            cs = jnp.sum(b_ref[...].astype(F32), axis=0, keepdims=True)

            @pl.when(k == 0)
            def _():
                cs_ref[...] = cs

            @pl.when(k > 0)
            def _():
                cs_ref[...] += cs

    ins, in_specs = [a, b], [a_spec, b_spec]
    for extra in (bias, res):
        if extra is not None:
            ins.append(extra[0])
            in_specs.append(extra[1])
    out_shape, out_specs = o_shape, o_spec
    if colsum is not None:
        out_shape, out_specs = (o_shape, _sds(colsum[0], F32)), (o_spec, colsum[1])
    scratch = [pltpu.VMEM(acc_shape, F32)] if nk > 1 else []
    if sem is None:
        sem = ("parallel",) * (n_axes - 1) + ("arbitrary",)
    return _pcall(body, name, out_shape, grid=grid, in_specs=in_specs, out_specs=out_specs, scratch=scratch,
                  sem=sem)(*ins)


def mm_nn(name, a, w, w_block, w_index, n, tn, out_dtype, bias=None, res=None, tm=512):
    s, k = a.shape
    tm = _row_tile(s, tm)
    col = lambda i, j: (i, j)
    extras = {}
    if bias is not None:
        extras["bias"] = bias
    if res is not None:
        extras["res"] = (res, pl.BlockSpec((tm, tn), col))
    return _mm(name, a, w, dims=NN, grid=(s // tm, n // tn), a_spec=pl.BlockSpec((tm, k), lambda i, j: (i, 0)),
               b_spec=pl.BlockSpec(w_block, w_index), o_spec=pl.BlockSpec((tm, tn), col), o_shape=_sds((s, n), out_dtype),
               sem=("parallel", "arbitrary"), **extras)


def mm_nt(name, a, w, w_block, w_index, kout, out_dtype, nk=1, tk=None, tm=512):
    s, n = a.shape
    tm = _row_tile(s, tm)
    tk = n if tk is None else tk
    return _mm(name, a, w, dims=NT, grid=(s // tm, nk), a_spec=pl.BlockSpec((tm, tk), lambda i, k: (i, k)),
               b_spec=pl.BlockSpec(w_block, w_index), o_spec=pl.BlockSpec((tm, kout), lambda i, k: (i, 0)),
               o_shape=_sds((s, kout), out_dtype), nk=nk, acc_shape=(tm, kout))


def mm_tn(name, a, b, *, groups, a_block, a_index, b_block, b_index, o_block, o_index, o_shape, acc_shape,
          colsum=None, tk=512):
    s = a.shape[-2]
    tk = _row_tile(s, tk)
    return _mm(name, a, b, dims=TN, grid=(groups, s // tk), a_spec=pl.BlockSpec(a_block(tk), a_index),
               b_spec=pl.BlockSpec(b_block(tk), b_index), o_spec=pl.BlockSpec(o_block, o_index),
               o_shape=_sds(o_shape, BF16), nk=s // tk, acc_shape=acc_shape, colsum=colsum)


def ffn_fwd(u, h, w_up, w_down, layer, name):
    s, d = u.shape
    tm = _row_tile(s, 512)
    nj = 4

    def body(u_ref, h_ref, wup_ref, wd_ref, hn_ref, gu_ref, acc_ref):
        j = pl.program_id(1)
        uv = u_ref[...]
        g = _dot(uv, wup_ref[0], NN)
        p = _dot(uv, wup_ref[1], NN)
        gu_ref[0] = g.astype(BF16)
        gu_ref[1] = p.astype(BF16)
        act = (g * jax.nn.sigmoid(g) * p).astype(BF16)
        part = _dot(act, wd_ref[...], NN)

        @pl.when(j == 0)
        def _():
            acc_ref[...] = part

        @pl.when(j > 0)
        def _():
            acc_ref[...] += part

        @pl.when(j == nj - 1)
        def _():
            hn_ref[...] = h_ref[...] + acc_ref[...]

    row = pl.BlockSpec((tm, d), lambda i, j: (i, 0))
    return _pcall(
        body, name, (_sds((s, d), F32), _sds((2, nj, s, FF_CHUNK), BF16)), grid=(s // tm, nj),
        in_specs=[row, row,
                  pl.BlockSpec((None, 2, None, d, FF_CHUNK), lambda i, j: (layer, 0, j, 0, 0)),
                  pl.BlockSpec((None, None, FF_CHUNK, d), lambda i, j: (layer, j, 0, 0))],
        out_specs=(row, pl.BlockSpec((2, None, tm, FF_CHUNK), lambda i, j: (0, j, i, 0))),
        scratch=[pltpu.VMEM((tm, d), F32)], sem=("parallel", "arbitrary"))(u, h, w_up, w_down)


def ffn_bwd(dh, gu, w_up, w_down, layer, name):
    s, d = dh.shape
    tm = _row_tile(s, 512)
    nj = 4

    def body(dh_ref, gu_ref, wup_ref, wd_ref, du_ref, act_ref, dgu_ref):
        j = pl.program_id(1)
        dact = _dot(dh_ref[...].astype(BF16), wd_ref[...], NT)
        g = gu_ref[0].astype(F32)
        p = gu_ref[1].astype(F32)
        sig = jax.nn.sigmoid(g)
        sl = g * sig
        act_ref[...] = (sl * p).astype(BF16)
        dp = (dact * sl).astype(BF16)
        dg = (dact * p * (sig * (1.0 + g * (1.0 - sig)))).astype(BF16)
        dgu_ref[0] = dg
        dgu_ref[1] = dp
        part = _dot(dg, wup_ref[0], NT) + _dot(dp, wup_ref[1], NT)

        @pl.when(j == 0)
        def _():
            du_ref[...] = part

        @pl.when(j > 0)
        def _():
            du_ref[...] += part

    row = pl.BlockSpec((tm, d), lambda i, j: (i, 0))
    gu_spec = pl.BlockSpec((2, None, tm, FF_CHUNK), lambda i, j: (0, j, i, 0))
    return _pcall(
        body, name, (_sds((s, d), F32), _sds((nj, s, FF_CHUNK), BF16), _sds((2, nj, s, FF_CHUNK), BF16)),
        grid=(s // tm, nj),
        in_specs=[row, gu_spec,
                  pl.BlockSpec((None, 2, None, d, FF_CHUNK), lambda i, j: (layer, 0, j, 0, 0)),
                  pl.BlockSpec((None, None, FF_CHUNK, d), lambda i, j: (layer, j, 0, 0))],
        out_specs=(row, pl.BlockSpec((None, tm, FF_CHUNK), lambda i, j: (j, i, 0)), gu_spec),
        sem=("parallel", "arbitrary"))(dh, gu, w_up, w_down)


def _glu(t):
    t = t.astype(F32)
    return t[:, :D_MODEL] * jax.nn.sigmoid(t[:, D_MODEL:])


def _conv_tile(s):
    return 256 if s % 256 == 0 else s


def conv_fwd(t, cp, name):
    s = t.shape[0]
    d = D_MODEL
    ts = _conv_tile(s)
    per = ts // HALO

    def body(t_ref, tp_ref, cp_ref, z_ref, buf_ref):
        i = pl.program_id(0)
        buf_ref[0:HALO, :] = jnp.where(i > 0, _glu(tp_ref[...]), 0.0)
        buf_ref[HALO:HALO + ts, :] = _glu(t_ref[...])
        y = jnp.zeros((ts, d), F32) + cp_ref[ROW_BDW:ROW_BDW + 1, :]
        for k in range(CONV_WIDTH):
            y = y + cp_ref[k:k + 1, :] * buf_ref[HALO - (CONV_WIDTH - 1) + k:HALO - (CONV_WIDTH - 1) + k + ts, :]
        mu = jnp.mean(y, axis=-1, keepdims=True)
        yc = y - mu
        rstd = lax.rsqrt(jnp.mean(yc * yc, axis=-1, keepdims=True) + EPS)
        yn = yc * rstd * cp_ref[ROW_LNG:ROW_LNG + 1, :] + cp_ref[ROW_LNB:ROW_LNB + 1, :]
        z_ref[...] = (yn * jax.nn.sigmoid(yn)).astype(BF16)

    return _pcall(
        body, name, _sds((s, d), BF16), grid=(s // ts,),
        in_specs=[pl.BlockSpec((ts, 2 * d), lambda i: (i, 0)),
                  pl.BlockSpec((HALO, 2 * d), lambda i: (jnp.maximum(i * per - 1, 0), 0)),
                  pl.BlockSpec((PACK_ROWS, d), lambda i: (0, 0))],
        out_specs=pl.BlockSpec((ts, d), lambda i: (i, 0)),
        scratch=[pltpu.VMEM((ts + HALO, d), F32)], sem=("parallel",))(t, t, cp)


def conv_bwd(t, dz, cp, name):
    s = t.shape[0]
    d = D_MODEL
    ts = _conv_tile(s)
    per = ts // HALO
    nt = s // ts
    te = ts + HALO
    lead = HALO - (CONV_WIDTH - 1)

    def body(t_ref, tp_ref, tn_ref, dz_ref, dzn_ref, cp_ref, dt_ref, st_ref, abuf_ref, dybuf_ref):
        i = pl.program_id(0)
        last = i == nt - 1
        tc = t_ref[...].astype(F32)
        abuf_ref[0:HALO, :] = jnp.where(i > 0, _glu(tp_ref[...]), 0.0)
        abuf_ref[HALO:HALO + ts, :] = _glu(tc)
        abuf_ref[HALO + ts:HALO + te, :] = _glu(tn_ref[...])
        y = jnp.zeros((te, d), F32) + cp_ref[ROW_BDW:ROW_BDW + 1, :]
        for k in range(CONV_WIDTH):
            y = y + cp_ref[k:k + 1, :] * abuf_ref[lead + k:lead + k + te, :]
        mu = jnp.mean(y, axis=-1, keepdims=True)
        yc = y - mu
        rstd = lax.rsqrt(jnp.mean(yc * yc, axis=-1, keepdims=True) + EPS)
        yh = yc * rstd
        gain = cp_ref[ROW_LNG:ROW_LNG + 1, :]
        yn = yh * gain + cp_ref[ROW_LNB:ROW_LNB + 1, :]
        sig = jax.nn.sigmoid(yn)
        dz_all = jnp.concatenate([dz_ref[...], jnp.where(last, 0.0, dzn_ref[...])], axis=0)
        dyn = dz_all * (sig * (1.0 + yn * (1.0 - sig)))
        dyh = dyn * gain
        dy = rstd * (dyh - jnp.mean(dyh, axis=-1, keepdims=True)
                     - yh * jnp.mean(dyh * yh, axis=-1, keepdims=True))
        dybuf_ref[...] = dy
        da = jnp.zeros((ts, d), F32)
        for k in range(CONV_WIDTH):
            off = CONV_WIDTH - 1 - k
            da = da + cp_ref[k:k + 1, :] * dybuf_ref[off:off + ts, :]
        sg = jax.nn.sigmoid(tc[:, d:])
        dt_ref[:, :d] = (da * sg).astype(BF16)
        dt_ref[:, d:] = (da * tc[:, :d] * sg * (1.0 - sg)).astype(BF16)

        @pl.when(i == 0)
        def _():
            st_ref[...] = jnp.zeros_like(st_ref)

        dy_c = dy[:ts]
        for k in range(CONV_WIDTH):
            st_ref[k:k + 1, :] += jnp.sum(dy_c * abuf_ref[lead + k:lead + k + ts, :], axis=0, keepdims=True)
        st_ref[ROW_BDW:ROW_BDW + 1, :] += jnp.sum(dy_c, axis=0, keepdims=True)
        st_ref[ROW_LNG:ROW_LNG + 1, :] += jnp.sum(dyn[:ts] * yh[:ts], axis=0, keepdims=True)
        st_ref[ROW_LNB:ROW_LNB + 1, :] += jnp.sum(dyn[:ts], axis=0, keepdims=True)

    last_halo = s // HALO - 1
    return _pcall(
        body, name, (_sds((s, 2 * d), BF16), _sds((PACK_ROWS, d), F32)), grid=(nt,),
        in_specs=[pl.BlockSpec((ts, 2 * d), lambda i: (i, 0)),
                  pl.BlockSpec((HALO, 2 * d), lambda i: (jnp.maximum(i * per - 1, 0), 0)),
                  pl.BlockSpec((HALO, 2 * d), lambda i: (jnp.minimum((i + 1) * per, last_halo), 0)),
                  pl.BlockSpec((ts, d), lambda i: (i, 0)),
                  pl.BlockSpec((HALO, d), lambda i: (jnp.minimum((i + 1) * per, last_halo), 0)),
                  pl.BlockSpec((PACK_ROWS, d), lambda i: (0, 0))],
        out_specs=(pl.BlockSpec((ts, 2 * d), lambda i: (i, 0)), pl.BlockSpec((PACK_ROWS, d), lambda i: (0, 0))),
        scratch=[pltpu.VMEM((HALO + te, d), F32), pltpu.VMEM((te, d), F32)],
        sem=("arbitrary",))(t, t, t, dz, dz, cp)


def _bucket_table():
    qi = np.arange(BLOCK, dtype=np.int64)[:, None]
    kj = np.arange(2 * BLOCK, dtype=np.int64)[None, :]
    dist = qi + BLOCK - kj
    max_exact = N_BUCKETS // 2
    dd = np.maximum(dist, 0)
    ratio = (np.maximum(dd, 1).astype(np.float32) / np.float32(max_exact)).astype(np.float32)
    log_ratio = (np.log(ratio).astype(np.float32) / np.float32(math.log(MAX_DISTANCE / max_exact))).astype(np.float32)
    large = max_exact + (log_ratio * np.float32(N_BUCKETS - max_exact)).astype(np.int32)
    large = np.minimum(large, N_BUCKETS - 1)
    bucket = np.where(dd < max_exact, dd, large)
    return np.where((dist >= 0) & (dist < BLOCK), bucket, -1).astype(np.int32)


def bias_table(rel_bias, bucket, name):
    def body(rb_ref, bk_ref, o_ref):
        bk = bk_ref[...]
        for h in range(N_HEADS):
            acc = jnp.full((BLOCK, 2 * BLOCK), NEG_INF, F32)
            for b in range(N_BUCKETS):
                acc = jnp.where(bk == b, rb_ref[b, h], acc)
            o_ref[h] = acc

    return _pcall(body, name, _sds((N_HEADS, BLOCK, 2 * BLOCK), F32),
                  in_specs=[pl.BlockSpec(memory_space=pltpu.SMEM), pl.BlockSpec(memory_space=pltpu.VMEM)],
                  out_specs=pl.BlockSpec(memory_space=pltpu.VMEM))(rel_bias, bucket)


def bias_grad(dba, dbb, bucket, name):
    def body(a_ref, b_ref, bk_ref, o_ref):
        bk = bk_ref[...]
        for h in range(N_HEADS):
            db = a_ref[h] + b_ref[h]
            for b in range(N_BUCKETS):
                o_ref[b, h] = jnp.sum(jnp.where(bk == b, db, 0.0))

    vm = pl.BlockSpec(memory_space=pltpu.VMEM)
    return _pcall(body, name, _sds((N_BUCKETS, N_HEADS), F32), in_specs=[vm, vm, vm],
                  out_specs=pl.BlockSpec(memory_space=pltpu.SMEM))(dba, dbb, bucket)


def _band_specs():
    cur = pl.BlockSpec((BLOCK, 2 * KV_DIM), lambda n: (n, 0))
    prev = pl.BlockSpec((BLOCK, 2 * KV_DIM), lambda n: (jnp.maximum(n - 1, 0), 0))
    return cur, prev


def _scores(q_h, k_h, bias_h, first_row, sink):
    sc = _dot(q_h, k_h, NT) * (HEAD_DIM ** -0.5) + bias_h + first_row
    m = jnp.maximum(jnp.max(sc, axis=-1, keepdims=True), sink)
    p = jnp.exp(sc - m)
    e_sink = jnp.exp(sink - m)
    den = jnp.sum(p, axis=-1, keepdims=True) + e_sink
    return p, e_sink, den


def _first_block_row(n):
    col = lax.broadcasted_iota(jnp.int32, (1, 2 * BLOCK), 1)
    return jnp.where((col < BLOCK) & (n == 0), NEG_INF, 0.0)


def attn_fwd(q, kv, bias, sinks, name):
    s = q.shape[0]
    nb = s // BLOCK

    def body(sink_ref, q_ref, kvc_ref, kvp_ref, bias_ref, o_ref, band_ref):
        n = pl.program_id(0)
        band_ref[0:BLOCK, :] = kvp_ref[...]
        band_ref[BLOCK:2 * BLOCK, :] = kvc_ref[...]
        first_row = _first_block_row(n)
        for h in range(N_HEADS):
            hk = h // GROUP
            lanes = slice(h * HEAD_DIM, (h + 1) * HEAD_DIM)
            k_h = band_ref[:, hk * HEAD_DIM:(hk + 1) * HEAD_DIM]
            v_h = band_ref[:, KV_DIM + hk * HEAD_DIM:KV_DIM + (hk + 1) * HEAD_DIM]
            p, _, den = _scores(q_ref[:, lanes], k_h, bias_ref[h], first_row, sink_ref[0, h])
            o_ref[:, lanes] = _dot((p * (1.0 / den)).astype(BF16), v_h, NN).astype(BF16)

    cur, prev = _band_specs()
    qs = pl.BlockSpec((BLOCK, D_MODEL), lambda n: (n, 0))
    return _pcall(
        body, name, _sds((s, D_MODEL), BF16), grid=(nb,),
        in_specs=[pl.BlockSpec(memory_space=pltpu.SMEM), qs, cur, prev,
                  pl.BlockSpec((N_HEADS, BLOCK, 2 * BLOCK), lambda n: (0, 0, 0))],
        out_specs=qs, scratch=[pltpu.VMEM((2 * BLOCK, 2 * KV_DIM), BF16)],
        sem=("parallel",))(sinks, q, kv, kv, bias)


def attn_bwd(q, kv, do, bias, sinks, name):
    s = q.shape[0]
    nb = s // BLOCK
    scale = HEAD_DIM ** -0.5

    def body(sink_ref, q_ref, do_ref, kvc_ref, kvp_ref, bias_ref, dq_ref, dkv_ref, db_ref, dsink_ref,
             band_ref, dsacc_ref):
        n = pl.program_id(0)
        band_ref[0:BLOCK, :] = kvp_ref[...]
        band_ref[BLOCK:2 * BLOCK, :] = kvc_ref[...]
        first_row = _first_block_row(n)
        lane = lax.broadcasted_iota(jnp.int32, (BLOCK, BLOCK), 1)

        @pl.when(n == 0)
        def _():
            db_ref[...] = jnp.zeros_like(db_ref)
            dsacc_ref[...] = jnp.zeros_like(dsacc_ref)

        for hk in range(N_KV_HEADS):
            k_h = band_ref[:, hk * HEAD_DIM:(hk + 1) * HEAD_DIM]
            v_h = band_ref[:, KV_DIM + hk * HEAD_DIM:KV_DIM + (hk + 1) * HEAD_DIM]
            dk = jnp.zeros((2 * BLOCK, HEAD_DIM), F32)
            dv = jnp.zeros((2 * BLOCK, HEAD_DIM), F32)
            for g in range(GROUP):
                h = hk * GROUP + g
                lanes = slice(h * HEAD_DIM, (h + 1) * HEAD_DIM)
                q_h = q_ref[:, lanes]
                do_h = do_ref[:, lanes]
                p, e_sink, den = _scores(q_h, k_h, bias_ref[h], first_row, sink_ref[0, h])
                inv = 1.0 / den
                p = p * inv
                dp = _dot(do_h, v_h, NT)
                delta = jnp.sum(p * dp, axis=-1, keepdims=True)
                ds = p * (dp - delta)
                db_ref[h] += ds
                dsacc_ref[...] += jnp.where(lane == h, -(e_sink * inv) * delta, 0.0)
                dsb = ds.astype(BF16)
                dq_ref[:, lanes] = (_dot(dsb, k_h, NN) * scale).astype(BF16)
                dk = dk + _dot(dsb, q_h, TN) * scale
                dv = dv + _dot(p.astype(BF16), do_h, TN)
            dkv_ref[:, hk * HEAD_DIM:(hk + 1) * HEAD_DIM] = dk
            dkv_ref[:, KV_DIM + hk * HEAD_DIM:KV_DIM + (hk + 1) * HEAD_DIM] = dv

        @pl.when(n == nb - 1)
        def _():
            dsink_ref[...] = jnp.sum(dsacc_ref[...], axis=0, keepdims=True)

    cur, prev = _band_specs()
    qs = pl.BlockSpec((BLOCK, D_MODEL), lambda n: (n, 0))
    full_b = pl.BlockSpec((N_HEADS, BLOCK, 2 * BLOCK), lambda n: (0, 0, 0))
    return _pcall(
        body, name,
        (_sds((s, D_MODEL), BF16), _sds((nb, 2 * BLOCK, 2 * KV_DIM), F32),
         _sds((N_HEADS, BLOCK, 2 * BLOCK), F32), _sds((1, BLOCK), F32)),
        grid=(nb,),
        in_specs=[pl.BlockSpec(memory_space=pltpu.SMEM), qs, qs, cur, prev, full_b],
        out_specs=(qs, pl.BlockSpec((None, 2 * BLOCK, 2 * KV_DIM), lambda n: (n, 0, 0)), full_b,
                   pl.BlockSpec((1, BLOCK), lambda n: (0, 0))),
        scratch=[pltpu.VMEM((2 * BLOCK, 2 * KV_DIM), BF16), pltpu.VMEM((BLOCK, BLOCK), F32)],
        sem=("arbitrary",))(sinks, q, do, kv, kv, bias)


def dkv_combine(pa, pb, name):
    nb = pa.shape[0]
    pa2 = pa.reshape(2 * nb, BLOCK, 2 * KV_DIM)
    pb2 = pb.reshape(2 * nb, BLOCK, 2 * KV_DIM)

    def body(ac_ref, an_ref, bc_ref, bn_ref, o_ref):
        n = pl.program_id(0)
        nxt = jnp.where(n == nb - 1, 0.0, an_ref[...] + bn_ref[...])
        o_ref[...] = (ac_ref[...] + bc_ref[...] + nxt).astype(BF16)

    cur = pl.BlockSpec((None, BLOCK, 2 * KV_DIM), lambda n: (2 * n + 1, 0, 0))
    nxt = pl.BlockSpec((None, BLOCK, 2 * KV_DIM), lambda n: (jnp.minimum(2 * n + 2, 2 * nb - 2), 0, 0))
    return _pcall(body, name, _sds((nb * BLOCK, 2 * KV_DIM), BF16), grid=(nb,),
                  in_specs=[cur, nxt, cur, nxt], out_specs=pl.BlockSpec((BLOCK, 2 * KV_DIM), lambda n: (n, 0)),
                  sem=("parallel",))(pa2, pa2, pb2, pb2)


def loss_head(h, g, target, name):
    s, d = h.shape
    tm = _row_tile(s, 512)

    def body(h_ref, g_ref, t_ref, dh_ref, dg_ref, loss_ref):
        i = pl.program_id(0)
        x = h_ref[...]
        r = lax.rsqrt(jnp.mean(x * x, axis=-1, keepdims=True) + EPS)
        xh = x * r
        gv = g_ref[...]
        err = xh * gv - t_ref[...]
        part_loss = jnp.zeros((1, BLOCK), F32) + 0.5 * jnp.sum(jnp.mean(err * err, axis=-1, keepdims=True))
        dy = err * (1.0 / d)
        dxh = dy * gv
        dh_ref[...] = r * (dxh - xh * jnp.mean(dxh * xh, axis=-1, keepdims=True))
        part_g = jnp.sum(dy * xh, axis=0, keepdims=True)

        @pl.when(i == 0)
        def _():
            dg_ref[...] = part_g
            loss_ref[...] = part_loss

        @pl.when(i > 0)
        def _():
            dg_ref[...] += part_g
            loss_ref[...] += part_loss

    row = pl.BlockSpec((tm, d), lambda i: (i, 0))
    vec = pl.BlockSpec((1, d), lambda i: (0, 0))
    return _pcall(body, name, (_sds((s, d), F32), _sds((1, d), F32), _sds((1, BLOCK), F32)), grid=(s // tm,),
                  in_specs=[row, vec, row], out_specs=(row, vec, pl.BlockSpec((1, BLOCK), lambda i: (0, 0))),
                  sem=("arbitrary",))(h, g, target)


def adamw(w, m, v, parts, name):
    nl, r, c = w.shape
    tr = max(t for t in range(1, min(r, 512) + 1) if r % t == 0 and (t % 16 == 0 or t == r))
    c1 = 1.0 / (1.0 - ADAM_B1 ** ADAM_STEP)
    c2 = 1.0 / (1.0 - ADAM_B2 ** ADAM_STEP)

    def body(w_ref, m_ref, v_ref, p_ref, g_ref, d_ref, nm_ref, nv_ref):
        g = p_ref[0].astype(F32)
        for dev in range(1, N_DEV):
            g = g + p_ref[dev].astype(F32)
        nm = ADAM_B1 * m_ref[...] + (1.0 - ADAM_B1) * g
        nv = ADAM_B2 * v_ref[...] + (1.0 - ADAM_B2) * (g * g)
        g_ref[...] = g
        nm_ref[...] = nm
        nv_ref[...] = nv
        d_ref[...] = -ADAM_LR * ((nm * c1) / (jnp.sqrt(nv * c2) + ADAM_EPS) + ADAM_WD * w_ref[...])

    blk = pl.BlockSpec((None, tr, c), lambda l, i: (l, i, 0))
    out = _sds((nl, r, c), F32)
    return _pcall(body, name, (out, out, out, out), grid=(nl, r // tr),
                  in_specs=[blk, blk, blk, pl.BlockSpec((N_DEV, None, tr, c), lambda l, i: (0, l, i, 0))],
                  out_specs=(blk, blk, blk, blk), sem=("parallel", "parallel"))(w, m, v, parts)


def _place():
    x, y, c = lax.axis_index("x"), lax.axis_index("y"), lax.axis_index("c")
    return x, y, c


def _lin(px, py, pc):
    return 4 * px + 2 * py + pc


def all_gather_weights(shards, out_shapes, slabs):
    nt = len(shards)

    def body(*refs):
        src = refs[:nt]
        out = refs[nt:2 * nt]
        send_sems, recv_sems, local_sems = refs[2 * nt:]
        x, y, c = _place()
        me, sibling = (x, y, c), (x, y, 1 - c)
        chips = [(1 - x, y), (x, 1 - y), (1 - x, 1 - y)]

        def copy(t, k, block, to, from_src=False):
            place = slabs[t](out[t], _lin(*block))
            return pltpu.make_async_remote_copy(
                src_ref=src[t] if from_src else place, dst_ref=place, send_sem=send_sems.at[t, k],
                recv_sem=recv_sems.at[t, k], device_id=to, device_id_type=MESH)

        mine = [pltpu.make_async_copy(src[t], slabs[t](out[t], _lin(*me)), local_sems.at[t]) for t in range(nt)]
        for cp in mine:
            cp.start()
        first = []
        for t in range(nt):
            first.append(copy(t, 0, me, sibling, from_src=True))
            first += [copy(t, 1 + j, me, (*chip, c), from_src=True) for j, chip in enumerate(chips)]
        for cp in first:
            cp.start()
        passed = []
        for j, chip in enumerate(chips):
            for t in range(nt):
                copy(t, 1 + j, (*chip, c), me).wait_recv()
                fwd = copy(t, 4 + j, (*chip, c), sibling)
                fwd.start()
                passed.append(fwd)
        for t in range(nt):
            copy(t, 0, sibling, me).wait_recv()
            for j, chip in enumerate(chips):
                copy(t, 4 + j, (*chip, 1 - c), me).wait_recv()
        for cp in first + passed:
            cp.wait_send()
        for cp in mine:
            cp.wait()

    return pl.pallas_call(
        body, name="all_gather_weights", out_shape=tuple(out_shapes), in_specs=[ANY] * nt, out_specs=tuple([ANY] * nt),
        scratch_shapes=[pltpu.SemaphoreType.DMA((nt, 7)), pltpu.SemaphoreType.DMA((nt, 7)),
                        pltpu.SemaphoreType.DMA((nt,))],
        compiler_params=pltpu.CompilerParams(has_side_effects=True))(*shards)


def exchange_grads(grads, out_shapes, plan):
    nt = len(grads)
    no = len(out_shapes)

    def body(*refs):
        src = refs[:nt]
        out = refs[nt:nt + no]
        send_sems, recv_sems, local_sems = refs[nt + no:]
        x, y, c = _place()
        me = _lin(x, y, c)
        peers = [(x, y, 1 - c), (1 - x, y, c), (x, 1 - y, c), (1 - x, 1 - y, c),
                 (1 - x, y, 1 - c), (x, 1 - y, 1 - c), (1 - x, 1 - y, 1 - c)]

        def landing(t, i):
            o, _, dst_slab = plan[t]
            return dst_slab(out[o], i)

        def owned(t, j):
            return plan[t][1](src[t], j)

        mine = [pltpu.make_async_copy(owned(t, me), landing(t, me), local_sems.at[t]) for t in range(nt)]
        for cp in mine:
            cp.start()
        sends = []
        for k, peer in enumerate(peers):
            for t in range(nt):
                sends.append(pltpu.make_async_remote_copy(
                    src_ref=owned(t, _lin(*peer)), dst_ref=landing(t, me), send_sem=send_sems.at[t, k],
                    recv_sem=recv_sems.at[t, k], device_id=peer, device_id_type=MESH))
        for cp in sends:
            cp.start()
        for k, peer in enumerate(peers):
            for t in range(nt):
                pltpu.make_async_remote_copy(
                    src_ref=owned(t, me), dst_ref=landing(t, _lin(*peer)), send_sem=send_sems.at[t, k],
                    recv_sem=recv_sems.at[t, k], device_id=peer, device_id_type=MESH).wait_recv()
        for cp in sends:
            cp.wait_send()
        for cp in mine:
            cp.wait()

    return pl.pallas_call(
        body, name="exchange_grads", out_shape=tuple(out_shapes), in_specs=[ANY] * nt, out_specs=tuple([ANY] * no),
        scratch_shapes=[pltpu.SemaphoreType.DMA((nt, 7)), pltpu.SemaphoreType.DMA((nt, 7)),
                        pltpu.SemaphoreType.DMA((nt,))],
        compiler_params=pltpu.CompilerParams(has_side_effects=True))(*grads)


def local_step(x, target, wt, rep):
    s = x.shape[0]
    bucket = jnp.asarray(_bucket_table())
    bias = bias_table(rep["rel_bias"], bucket, "bias_table")
    h = x
    saved = []
    kv = None
    h_kv = u_kv = None
    for l in range(4):
        g_mix = rep["norm_mix"][l:l + 1]
        g_ffn = rep["norm_ffn"][l:l + 1]
        rec = {"h_in": h}
        u = rms_fwd(h, g_mix, f"rms_mix_fwd{l}")
        rec["u"] = u
        if l < 2:
            t = mm_nn(f"pw1_fwd{l}", u, wt["pw1"], (None, None, D_MODEL, 256), lambda i, j, l=l: (l, j, 0, 0),
                      2 * D_MODEL, 256, BF16,
                      bias=(wt["b_pw1"], pl.BlockSpec((None, 1, 256), lambda i, j, l=l: (l, 0, j))))
            z = conv_fwd(t, wt["cp"][l], f"conv_fwd{l}")
            h = mm_nn(f"pw2_fwd{l}", z, wt["pw2"], (None, D_MODEL, 512), lambda i, j, l=l: (l, 0, j), D_MODEL, 512,
                      F32, bias=(wt["b_pw2"], pl.BlockSpec((None, 1, 512), lambda i, j, l=l: (l, 0, j))), res=h)
            rec.update(t=t, z=z)
        else:
            a = l - 2
            if a == 0:
                h_kv = h
                u_kv = rms_fwd(h, rep["norm_kv"], "rms_kv_fwd")
                kv = mm_nn("kv_fwd", u_kv, wt["wkv"], (D_MODEL, 2 * KV_DIM), lambda i, j: (0, 0), 2 * KV_DIM,
                           2 * KV_DIM, BF16)
            q = mm_nn(f"q_fwd{a}", u, wt["wq"], (None, D_MODEL, 512), lambda i, j, a=a: (a, 0, j), D_MODEL, 512, BF16)
            o = attn_fwd(q, kv, bias, rep["sinks"][a:a + 1], f"attn_fwd{a}")
            h = mm_nn(f"o_fwd{a}", o, wt["wo"], (None, D_MODEL, 512), lambda i, j, a=a: (a, 0, j), D_MODEL, 512, F32,
                      res=h)
            rec.update(q=q, o=o)
        rec["h_mid"] = h
        uf = rms_fwd(h, g_ffn, f"rms_ffn_fwd{l}")
        h, gu = ffn_fwd(uf, h, wt["up"], wt["down"], l, f"ffn_fwd{l}")
        rec.update(uf=uf, gu=gu)
        saved.append(rec)

    dh, d_nfin, loss = loss_head(h, rep["norm_final"], target, "loss_head")

    grads = {}
    d_mix, d_ffn = [None] * 4, [None] * 4
    cp_grads = [None, None]
    dkv_parts, dbias_parts, dsinks = [], [], [None, None]
    d_nkv = None
    full_rows = lambda tk: (tk, D_MODEL)
    for l in reversed(range(4)):
        rec = saved[l]
        g_mix = rep["norm_mix"][l:l + 1]
        g_ffn = rep["norm_ffn"][l:l + 1]
        du, act, dgu = ffn_bwd(dh, rec["gu"], wt["up"], wt["down"], l, f"ffn_bwd{l}")
        grads[f"down{l}"] = mm_tn(
            f"down_grad{l}", act, dh, groups=4, a_block=lambda tk: (None, tk, FF_CHUNK), a_index=lambda j, k: (j, k, 0),
            b_block=full_rows, b_index=lambda j, k: (k, 0), o_block=(None, FF_CHUNK, D_MODEL),
            o_index=lambda j, k: (j, 0, 0), o_shape=(4, FF_CHUNK, D_MODEL), acc_shape=(FF_CHUNK, D_MODEL))
        grads[f"up{l}"] = mm_tn(
            f"up_grad{l}", rec["uf"], dgu.reshape(8, s, FF_CHUNK), groups=8, a_block=full_rows,
            a_index=lambda j, k: (k, 0), b_block=lambda tk: (None, tk, FF_CHUNK), b_index=lambda j, k: (j, k, 0),
            o_block=(None, D_MODEL, FF_CHUNK), o_index=lambda j, k: (j, 0, 0), o_shape=(8, D_MODEL, FF_CHUNK),
            acc_shape=(D_MODEL, FF_CHUNK))
        dh, d_ffn[l] = rms_bwd(rec["h_mid"], g_ffn, du, dh, f"rms_ffn_bwd{l}")
        if l < 2:
            dz = mm_nt(f"pw2_bwd{l}", dh, wt["pw2"], (None, D_MODEL, D_MODEL), lambda i, k, l=l: (l, 0, 0), D_MODEL,
                       F32)
            grads[f"pw2{l}"], db2 = mm_tn(
                f"pw2_grad{l}", rec["z"], dh, groups=1, a_block=full_rows, a_index=lambda j, k: (k, 0),
                b_block=full_rows, b_index=lambda j, k: (k, 0), o_block=(D_MODEL, D_MODEL), o_index=lambda j, k: (0, 0),
                o_shape=(D_MODEL, D_MODEL), acc_shape=(D_MODEL, D_MODEL),
                colsum=((1, D_MODEL), pl.BlockSpec((1, D_MODEL), lambda j, k: (0, 0))))
            dt, stats = conv_bwd(rec["t"], dz, wt["cp"][l], f"conv_bwd{l}")
            du = mm_nt(f"pw1_bwd{l}", dt, wt["pw1"], (None, None, D_MODEL, 256), lambda i, k, l=l: (l, k, 0, 0),
                       D_MODEL, F32, nk=8, tk=256)
            grads[f"pw1{l}"], db1 = mm_tn(
                f"pw1_grad{l}", rec["u"], dt, groups=8, a_block=full_rows, a_index=lambda j, k: (k, 0),
                b_block=lambda tk: (tk, 256), b_index=lambda j, k: (k, j), o_block=(None, D_MODEL, 256),
                o_index=lambda j, k: (j, 0, 0), o_shape=(8, D_MODEL, 256), acc_shape=(D_MODEL, 256),
                colsum=((8, 1, 256), pl.BlockSpec((None, 1, 256), lambda j, k: (j, 0, 0))))
            cp_grads[l] = (stats, db2, db1)
        else:
            a = l - 2
            do = mm_nt(f"o_bwd{a}", dh, wt["wo"], (None, D_MODEL, D_MODEL), lambda i, k, a=a: (a, 0, 0), D_MODEL, BF16)
            grads[f"wo{a}"] = mm_tn(
                f"wo_grad{a}", rec["o"], dh, groups=1, a_block=full_rows, a_index=lambda j, k: (k, 0),
                b_block=full_rows, b_index=lambda j, k: (k, 0), o_block=(D_MODEL, D_MODEL), o_index=lambda j, k: (0, 0),
                o_shape=(D_MODEL, D_MODEL), acc_shape=(D_MODEL, D_MODEL))
            dq, dkv_p, dbias_p, dsinks[a] = attn_bwd(rec["q"], kv, do, bias, rep["sinks"][a:a + 1], f"attn_bwd{a}")
            dkv_parts.append(dkv_p)
            dbias_parts.append(dbias_p)
            du = mm_nt(f"q_bwd{a}", dq, wt["wq"], (None, D_MODEL, D_MODEL), lambda i, k, a=a: (a, 0, 0), D_MODEL, F32)
            grads[f"wq{a}"] = mm_tn(
                f"wq_grad{a}", rec["u"], dq, groups=1, a_block=full_rows, a_index=lambda j, k: (k, 0),
                b_block=full_rows, b_index=lambda j, k: (k, 0), o_block=(D_MODEL, D_MODEL), o_index=lambda j, k: (0, 0),
                o_shape=(D_MODEL, D_MODEL), acc_shape=(D_MODEL, D_MODEL))
        dh, d_mix[l] = rms_bwd(rec["h_in"], g_mix, du, dh, f"rms_mix_bwd{l}")
        if l == 2:
            dkv = dkv_combine(dkv_parts[0], dkv_parts[1], "dkv_combine")
            du_kv = mm_nt("kv_bwd", dkv, wt["wkv"], (D_MODEL, 2 * KV_DIM), lambda i, k: (0, 0), D_MODEL, F32)
            grads["wkv"] = mm_tn(
                "wkv_grad", u_kv, dkv, groups=1, a_block=full_rows, a_index=lambda j, k: (k, 0),
                b_block=lambda tk: (tk, 2 * KV_DIM), b_index=lambda j, k: (k, 0), o_block=(D_MODEL, 2 * KV_DIM),
                o_index=lambda j, k: (0, 0), o_shape=(D_MODEL, 2 * KV_DIM), acc_shape=(D_MODEL, 2 * KV_DIM))
            dh, d_nkv = rms_bwd(h_kv, rep["norm_kv"], du_kv, dh, "rms_kv_bwd")

    d_relb = bias_grad(dbias_parts[0], dbias_parts[1], bucket, "bias_grad")
    d_sinks = jnp.concatenate([dsinks[0][0, :N_HEADS], dsinks[1][0, :N_HEADS]])
    tail = jnp.zeros((D_MODEL,), F32)
    rep_grad = jnp.concatenate([
        jnp.concatenate(d_mix, axis=0), jnp.concatenate(d_ffn, axis=0), d_nkv, d_nfin,
        tail.at[:2 * N_HEADS].set(d_sinks)[None], tail.at[:N_BUCKETS * N_HEADS].set(d_relb.reshape(-1))[None],
        tail.at[0].set(loss[0, 0])[None], jnp.zeros((REP_ROWS - ROW_LOSS - 1, D_MODEL), F32)], axis=0)
    return dh, grads, cp_grads, rep_grad


def _pack_conv(w_dw, b_dw, ln_g, ln_b, b_pw2, b_pw1):
    rows = [w_dw, b_dw[:, None], ln_g[:, None], ln_b[:, None], b_pw2[:, None], b_pw1.reshape(2, 2, 128),
            jnp.zeros((2, PACK_ROWS - ROW_BPW1 - 2, 128), F32)]
    return jnp.concatenate(rows, axis=1)


def _unpack_conv(p):
    return (p[:, :CONV_WIDTH], p[:, ROW_BDW], p[:, ROW_LNG], p[:, ROW_LNB], p[:, ROW_BPW2],
            p[:, ROW_BPW1:ROW_BPW1 + 2].reshape(2, 256))


def _pack_rep(norm_mix, norm_ffn, norm_kv, norm_final, sinks, rel_bias):
    tail = jnp.zeros((D_MODEL,), F32)
    return jnp.concatenate([
        norm_mix, norm_ffn, norm_kv[None], norm_final[None], tail.at[:2 * N_HEADS].set(sinks.reshape(-1))[None],
        tail.at[:N_BUCKETS * N_HEADS].set(rel_bias.reshape(-1))[None],
        jnp.zeros((REP_ROWS - ROW_RELB - 1, D_MODEL), F32)], axis=0)


def _unpack_rep(p):
    return (p[0:4], p[4:8], p[ROW_NKV], p[ROW_NFIN], p[ROW_SINK, :2 * N_HEADS].reshape(2, N_HEADS),
            p[ROW_RELB, :N_BUCKETS * N_HEADS].reshape(N_BUCKETS, N_HEADS))


def kernel(x, norm_mix, norm_ffn, conv_w_pw1, conv_b_pw1, conv_w_dw, conv_b_dw, conv_ln_g, conv_ln_b, conv_w_pw2, conv_b_pw2, norm_kv, w_kv, w_q, w_o, sinks, rel_bias, ffn_w_up, ffn_w_down, norm_final, loss_target, m_norm_mix, m_norm_ffn, m_conv_w_pw1, m_conv_b_pw1, m_conv_w_dw, m_conv_b_dw, m_conv_ln_g, m_conv_ln_b, m_conv_w_pw2, m_conv_b_pw2, m_norm_kv, m_w_kv, m_w_q, m_w_o, m_sinks, m_rel_bias, m_ffn_w_up, m_ffn_w_down, m_norm_final, v_norm_mix, v_norm_ffn, v_conv_w_pw1, v_conv_b_pw1, v_conv_w_dw, v_conv_b_dw, v_conv_ln_g, v_conv_ln_b, v_conv_w_pw2, v_conv_b_pw2, v_norm_kv, v_w_kv, v_w_q, v_w_o, v_sinks, v_rel_bias, v_ffn_w_up, v_ffn_w_down, v_norm_final):
    s = x.shape[1]
    d = D_MODEL
    rsh = d // N_DEV
    dsh = D_FF // N_DEV

    conv_pack = _pack_conv(conv_w_dw, conv_b_dw, conv_ln_g, conv_ln_b, conv_b_pw2, conv_b_pw1)
    shards = [conv_w_pw1.astype(BF16), conv_w_pw2.astype(BF16), w_kv.astype(BF16), w_q.astype(BF16),
              w_o.astype(BF16), ffn_w_up.astype(BF16), ffn_w_down.astype(BF16), conv_pack]
    out_shapes = [_sds((2, N_DEV, d, 256), BF16), _sds((2, d, d), BF16), _sds((d, 2 * KV_DIM), BF16),
                  _sds((2, d, d), BF16), _sds((2, d, d), BF16), _sds((4, N_DEV, d, FF_CHUNK), BF16),
                  _sds((4, D_FF, d), BF16), _sds((N_DEV, 2, PACK_ROWS, 128), F32)]
    col_slab = lambda ref, j: ref.at[:, j]
    row_slab = lambda rows: (lambda ref, j: ref.at[:, pl.ds(j * rows, rows), :])
    slabs = [col_slab, row_slab(rsh), lambda ref, j: ref.at[pl.ds(j * rsh, rsh), :], row_slab(rsh), row_slab(rsh),
             col_slab, row_slab(dsh), lambda ref, j: ref.at[j]]
    pw1_g, pw2_g, wkv_g, wq_g, wo_g, up_g, down_g, pack_g = all_gather_weights(shards, out_shapes, slabs)
    pack_full = jnp.transpose(pack_g, (1, 2, 0, 3)).reshape(2, PACK_ROWS, d)
    b_pw1_full = pack_g[:, :, ROW_BPW1:ROW_BPW1 + 2, :].transpose(1, 0, 2, 3).reshape(2, 1, 2 * d)
    wt = {"pw1": pw1_g, "pw2": pw2_g, "wkv": wkv_g, "wq": wq_g, "wo": wo_g,
          "up": up_g.reshape(4, 2, 4, d, FF_CHUNK), "down": down_g.reshape(4, 4, FF_CHUNK, d),
          "cp": pack_full, "b_pw1": b_pw1_full, "b_pw2": pack_full[:, ROW_BPW2:ROW_BPW2 + 1, :]}
    rep = {"norm_mix": norm_mix, "norm_ffn": norm_ffn, "norm_kv": norm_kv[None], "norm_final": norm_final[None],
           "sinks": sinks, "rel_bias": rel_bias}

    grad_x, grads, cp_grads, rep_grad = local_step(x[0], loss_target[0], wt, rep)

    cp_full = []
    for l in range(2):
        stats, db2, db1 = cp_grads[l]
        cp_full.append(jnp.concatenate([
            stats[:ROW_BPW2], db2, db1.reshape(N_DEV, 2, 128).transpose(1, 0, 2).reshape(2, d),
            jnp.zeros((PACK_ROWS - ROW_BPW1 - 2, d), F32)], axis=0))
    cp_send = jnp.stack(cp_full).reshape(2, PACK_ROWS, N_DEV, 128).transpose(2, 0, 1, 3)

    g_list, g_out, plan, keys = [], [], [], []

    def add_tensor(key, arrs, shard_shape, src):
        o = len(g_out)
        g_out.append(_sds((N_DEV, len(arrs)) + tuple(shard_shape), arrs[0].dtype))
        keys.append(key)
        for l, arr in enumerate(arrs):
            g_list.append(arr)
            plan.append((o, src, lambda ref, i, l=l: ref.at[i, l]))

    lead_slab = lambda ref, j: ref.at[j]
    rows_of = lambda rows: (lambda ref, j: ref.at[pl.ds(j * rows, rows), :])
    add_tensor("pw1", [grads[f"pw1{l}"] for l in range(2)], (d, 256), lead_slab)
    add_tensor("pw2", [grads[f"pw2{l}"] for l in range(2)], (rsh, d), rows_of(rsh))
    add_tensor("wkv", [grads["wkv"]], (rsh, 2 * KV_DIM), rows_of(rsh))
    add_tensor("wq", [grads[f"wq{a}"] for a in range(2)], (rsh, d), rows_of(rsh))
    add_tensor("wo", [grads[f"wo{a}"] for a in range(2)], (rsh, d), rows_of(rsh))
    add_tensor("up", [grads[f"up{l}"] for l in range(4)], (d, FF_CHUNK), lead_slab)
    add_tensor("down", [grads[f"down{l}"].reshape(D_FF, d) for l in range(4)], (dsh, d), rows_of(dsh))
    add_tensor("cp", [cp_send[:, l] for l in range(2)], (PACK_ROWS, 128), lead_slab)
    add_tensor("rep", [rep_grad], (REP_ROWS, d), lambda ref, j: ref)
    parts = dict(zip(keys, exchange_grads(g_list, g_out, plan)))

    def update(key, w, m, v, name):
        p = parts[key]
        w3 = w.reshape(p.shape[1:])
        outs = adamw(w3, m.reshape(w3.shape), v.reshape(w3.shape), p, name)
        return [o.reshape(w.shape) for o in outs]

    res = {}
    res["conv_w_pw1"] = update("pw1", conv_w_pw1, m_conv_w_pw1, v_conv_w_pw1, "adam_pw1")
    res["conv_w_pw2"] = update("pw2", conv_w_pw2, m_conv_w_pw2, v_conv_w_pw2, "adam_pw2")
    res["w_kv"] = update("wkv", w_kv, m_w_kv, v_w_kv, "adam_wkv")
    res["w_q"] = update("wq", w_q, m_w_q, v_w_q, "adam_wq")
    res["w_o"] = update("wo", w_o, m_w_o, v_w_o, "adam_wo")
    res["ffn_w_up"] = update("up", ffn_w_up, m_ffn_w_up, v_ffn_w_up, "adam_up")
    res["ffn_w_down"] = update("down", ffn_w_down, m_ffn_w_down, v_ffn_w_down, "adam_down")
    m_pack = _pack_conv(m_conv_w_dw, m_conv_b_dw, m_conv_ln_g, m_conv_ln_b, m_conv_b_pw2, m_conv_b_pw1)
    v_pack = _pack_conv(v_conv_w_dw, v_conv_b_dw, v_conv_ln_g, v_conv_ln_b, v_conv_b_pw2, v_conv_b_pw1)
    cp_res = adamw(conv_pack, m_pack, v_pack, parts["cp"], "adam_conv_pack")
    rep_w = _pack_rep(norm_mix, norm_ffn, norm_kv, norm_final, sinks, rel_bias)
    rep_m = _pack_rep(m_norm_mix, m_norm_ffn, m_norm_kv, m_norm_final, m_sinks, m_rel_bias)
    rep_v = _pack_rep(v_norm_mix, v_norm_ffn, v_norm_kv, v_norm_final, v_sinks, v_rel_bias)
    rep_res = adamw(rep_w[None], rep_m[None], rep_v[None], parts["rep"], "adam_rep")
    loss = rep_res[0][0, ROW_LOSS, 0]

    outs = []
    for kind in range(4):
        cw_dw, cb_dw, cln_g, cln_b, cb_pw2, cb_pw1 = _unpack_conv(cp_res[kind])
        r_mix, r_ffn, r_nkv, r_nfin, r_sinks, r_relb = _unpack_rep(rep_res[kind][0])
        outs += [r_mix, r_ffn, res["conv_w_pw1"][kind], cb_pw1, cw_dw, cb_dw, cln_g, cln_b, res["conv_w_pw2"][kind],
                 cb_pw2, r_nkv, res["w_kv"][kind], res["w_q"][kind], res["w_o"][kind], r_sinks, r_relb,
                 res["ffn_w_up"][kind], res["ffn_w_down"][kind], r_nfin]
    return (loss, grad_x[None], *outs)
```

```python
import functools
import math

import numpy as np
import jax
import jax.numpy as jnp
from jax import lax
from jax.experimental import pallas as pl
from jax.experimental.pallas import tpu as pltpu

F32 = jnp.float32
BF16 = jnp.bfloat16

D_MODEL = 1024
D_FF = 2816
N_HEADS = 16
N_KV_HEADS = 4
GROUP = N_HEADS // N_KV_HEADS
HEAD_DIM = 64
KV_DIM = N_KV_HEADS * HEAD_DIM
BLOCK = 128
CONV_WIDTH = 31
HALO = 32
N_BUCKETS = 32
MAX_DISTANCE = 128
EPS = 1e-6
NEG_INF = -1e30
N_DEV = 8
FF_CHUNK = D_FF // 4
PACK_ROWS = 40
ROW_BDW, ROW_LNG, ROW_LNB, ROW_BPW2, ROW_BPW1 = 31, 32, 33, 34, 35
REP_ROWS = 16
ROW_NKV, ROW_NFIN, ROW_SINK, ROW_RELB, ROW_LOSS = 8, 9, 10, 11, 12

ADAM_LR, ADAM_B1, ADAM_B2, ADAM_EPS, ADAM_WD, ADAM_STEP = 0.001, 0.9, 0.999, 1e-08, 0.01, 10

VMEM_LIMIT_BYTES = 56 * 1024 * 1024
FFN_ROWS = 1024
GRAD_ROWS = 2048
ANY = pl.BlockSpec(memory_space=pl.ANY)
MESH = pl.DeviceIdType.MESH

NN = (((1,), (0,)), ((), ()))
NT = (((1,), (1,)), ((), ()))
TN = (((0,), (0,)), ((), ()))


def _dot(a, b, dims):
    return lax.dot_general(a, b, dims, preferred_element_type=F32)


def _pcall(body, name, out_shape, *, grid=None, in_specs=None, out_specs=None, scratch=(), sem=None, **kw):
    params = pltpu.CompilerParams(dimension_semantics=sem, vmem_limit_bytes=VMEM_LIMIT_BYTES)
    extra = {} if grid is None else {"grid": grid}
    return pl.pallas_call(body, name=name, out_shape=out_shape, in_specs=in_specs, out_specs=out_specs,
                          scratch_shapes=list(scratch), compiler_params=params, **extra, **kw)


def _sds(shape, dtype):
    return jax.ShapeDtypeStruct(tuple(shape), dtype)


def _row_tile(s, want):
    return want if s % want == 0 else s


def rms_fwd(h, g, name):
    s, d = h.shape
    tm = _row_tile(s, 512)

    def body(h_ref, g_ref, u_ref):
        x = h_ref[...]
        r = lax.rsqrt(jnp.mean(x * x, axis=-1, keepdims=True) + EPS)
        u_ref[...] = (x * r * g_ref[...]).astype(BF16)

    return _pcall(body, name, _sds((s, d), BF16), grid=(s // tm,),
                  in_specs=[pl.BlockSpec((tm, d), lambda i: (i, 0)), pl.BlockSpec((1, d), lambda i: (0, 0))],
                  out_specs=pl.BlockSpec((tm, d), lambda i: (i, 0)), sem=("parallel",))(h, g)


def rms_bwd(h, g, du, dh_in, name):
    s, d = h.shape
    tm = _row_tile(s, 512)

    def body(h_ref, g_ref, du_ref, dhi_ref, dh_ref, dg_ref):
        i = pl.program_id(0)
        x = h_ref[...]
        r = lax.rsqrt(jnp.mean(x * x, axis=-1, keepdims=True) + EPS)
        xh = x * r
        du_v = du_ref[...]
        dxh = du_v * g_ref[...]
        dx = r * (dxh - xh * jnp.mean(dxh * xh, axis=-1, keepdims=True))
        dh_ref[...] = dhi_ref[...] + dx
        part = jnp.sum(du_v * xh, axis=0, keepdims=True)

        @pl.when(i == 0)
        def _():
            dg_ref[...] = part

        @pl.when(i > 0)
        def _():
            dg_ref[...] += part

    row = pl.BlockSpec((tm, d), lambda i: (i, 0))
    vec = pl.BlockSpec((1, d), lambda i: (0, 0))
    return _pcall(body, name, (_sds((s, d), F32), _sds((1, d), F32)), grid=(s // tm,),
                  in_specs=[row, vec, row, row], out_specs=(row, vec), sem=("arbitrary",))(h, g, du, dh_in)


def _mm(name, a, b, *, dims, grid, a_spec, b_spec, o_spec, o_shape, nk=1, acc_shape=None,
        bias=None, res=None, colsum=None, sem=None):
    n_axes = len(grid)

    def body(*refs):
        it = iter(refs)
        a_ref, b_ref = next(it), next(it)
        bias_ref = next(it) if bias is not None else None
        res_ref = next(it) if res is not None else None
        o_ref = next(it)
        cs_ref = next(it) if colsum is not None else None
        acc_ref = next(it) if nk > 1 else None
        k = pl.program_id(n_axes - 1)
        p = _dot(a_ref[...].astype(BF16), b_ref[...].astype(BF16), dims)

        def finish(acc):
            if bias_ref is not None:
                acc = acc + bias_ref[...]
            if res_ref is not None:
                acc = acc + res_ref[...]
            o_ref[...] = acc.astype(o_ref.dtype)

        if nk == 1:
            finish(p)
        else:
            @pl.when(k == 0)
            def _():
                acc_ref[...] = p

            @pl.when(k > 0)
            def _():
                acc_ref[...] += p

            @pl.when(k == nk - 1)
            def _():
                finish(acc_ref[...])

        if cs_ref is not None:
            cs = jnp.sum(b_ref[...].astype(F32), axis=0, keepdims=True)

            @pl.when(k == 0)
            def _():
                cs_ref[...] = cs

            @pl.when(k > 0)
            def _():
                cs_ref[...] += cs

    ins, in_specs = [a, b], [a_spec, b_spec]
    for extra in (bias, res):
        if extra is not None:
            ins.append(extra[0])
            in_specs.append(extra[1])
    out_shape, out_specs = o_shape, o_spec
    if colsum is not None:
        out_shape, out_specs = (o_shape, _sds(colsum[0], F32)), (o_spec, colsum[1])
    scratch = [pltpu.VMEM(acc_shape, F32)] if nk > 1 else []
    if sem is None:
        sem = ("parallel",) * (n_axes - 1) + ("arbitrary",)
    return _pcall(body, name, out_shape, grid=grid, in_specs=in_specs, out_specs=out_specs, scratch=scratch,
                  sem=sem)(*ins)


def mm_nn(name, a, w, w_block, w_index, n, tn, out_dtype, bias=None, res=None, tm=512):
    s, k = a.shape
    tm = _row_tile(s, tm)
    col = lambda i, j: (i, j)
    extras = {}
    if bias is not None:
        extras["bias"] = bias
    if res is not None:
        extras["res"] = (res, pl.BlockSpec((tm, tn), col))
    return _mm(name, a, w, dims=NN, grid=(s // tm, n // tn), a_spec=pl.BlockSpec((tm, k), lambda i, j: (i, 0)),
               b_spec=pl.BlockSpec(w_block, w_index), o_spec=pl.BlockSpec((tm, tn), col), o_shape=_sds((s, n), out_dtype),
               sem=("parallel", "arbitrary"), **extras)


def mm_nt(name, a, w, w_block, w_index, kout, out_dtype, nk=1, tk=None, tm=512):
    s, n = a.shape
    tm = _row_tile(s, tm)
    tk = n if tk is None else tk
    return _mm(name, a, w, dims=NT, grid=(s // tm, nk), a_spec=pl.BlockSpec((tm, tk), lambda i, k: (i, k)),
               b_spec=pl.BlockSpec(w_block, w_index), o_spec=pl.BlockSpec((tm, kout), lambda i, k: (i, 0)),
               o_shape=_sds((s, kout), out_dtype), nk=nk, acc_shape=(tm, kout))


def mm_tn(name, a, b, *, groups, a_block, a_index, b_block, b_index, o_block, o_index, o_shape, acc_shape,
          colsum=None, tk=GRAD_ROWS):
    s = a.shape[-2]
    tk = _row_tile(s, tk)
    return _mm(name, a, b, dims=TN, grid=(groups, s // tk), a_spec=pl.BlockSpec(a_block(tk), a_index),
               b_spec=pl.BlockSpec(b_block(tk), b_index), o_spec=pl.BlockSpec(o_block, o_index),
               o_shape=_sds(o_shape, BF16), nk=s // tk, acc_shape=acc_shape, colsum=colsum)


def ffn_fwd(u, h, w_up, w_down, layer, name):
    s, d = u.shape
    tm = _row_tile(s, FFN_ROWS)
    nj = 4

    def body(u_ref, h_ref, wup_ref, wd_ref, hn_ref, gu_ref, acc_ref):
        j = pl.program_id(1)
        uv = u_ref[...]
        g = _dot(uv, wup_ref[0], NN)
        p = _dot(uv, wup_ref[1], NN)
        gu_ref[0] = g.astype(BF16)
        gu_ref[1] = p.astype(BF16)
        act = (g * jax.nn.sigmoid(g) * p).astype(BF16)
        part = _dot(act, wd_ref[...], NN)

        @pl.when(j == 0)
        def _():
            acc_ref[...] = part

        @pl.when(j > 0)
        def _():
            acc_ref[...] += part

        @pl.when(j == nj - 1)
        def _():
            hn_ref[...] = h_ref[...] + acc_ref[...]

    row = pl.BlockSpec((tm, d), lambda i, j: (i, 0))
    return _pcall(
        body, name, (_sds((s, d), F32), _sds((2, nj, s, FF_CHUNK), BF16)), grid=(s // tm, nj),
        in_specs=[row, row,
                  pl.BlockSpec((None, 2, None, d, FF_CHUNK), lambda i, j: (layer, 0, j, 0, 0)),
                  pl.BlockSpec((None, None, FF_CHUNK, d), lambda i, j: (layer, j, 0, 0))],
        out_specs=(row, pl.BlockSpec((2, None, tm, FF_CHUNK), lambda i, j: (0, j, i, 0))),
        scratch=[pltpu.VMEM((tm, d), F32)], sem=("parallel", "arbitrary"))(u, h, w_up, w_down)


def ffn_bwd(dh, gu, w_up, w_down, layer, name):
    s, d = dh.shape
    tm = _row_tile(s, FFN_ROWS)
    nj = 4

    def body(dh_ref, gu_ref, wup_ref, wd_ref, du_ref, act_ref, dgu_ref):
        j = pl.program_id(1)
        dact = _dot(dh_ref[...].astype(BF16), wd_ref[...], NT)
        g = gu_ref[0].astype(F32)
        p = gu_ref[1].astype(F32)
        sig = jax.nn.sigmoid(g)
        sl = g * sig
        act_ref[...] = (sl * p).astype(BF16)
        dp = (dact * sl).astype(BF16)
        dg = (dact * p * (sig * (1.0 + g * (1.0 - sig)))).astype(BF16)
        dgu_ref[0] = dg
        dgu_ref[1] = dp
        part = _dot(dg, wup_ref[0], NT) + _dot(dp, wup_ref[1], NT)

        @pl.when(j == 0)
        def _():
            du_ref[...] = part

        @pl.when(j > 0)
        def _():
            du_ref[...] += part

    row = pl.BlockSpec((tm, d), lambda i, j: (i, 0))
    gu_spec = pl.BlockSpec((2, None, tm, FF_CHUNK), lambda i, j: (0, j, i, 0))
    return _pcall(
        body, name, (_sds((s, d), F32), _sds((nj, s, FF_CHUNK), BF16), _sds((2, nj, s, FF_CHUNK), BF16)),
        grid=(s // tm, nj),
        in_specs=[row, gu_spec,
                  pl.BlockSpec((None, 2, None, d, FF_CHUNK), lambda i, j: (layer, 0, j, 0, 0)),
                  pl.BlockSpec((None, None, FF_CHUNK, d), lambda i, j: (layer, j, 0, 0))],
        out_specs=(row, pl.BlockSpec((None, tm, FF_CHUNK), lambda i, j: (j, i, 0)), gu_spec),
        sem=("parallel", "arbitrary"))(dh, gu, w_up, w_down)


def _glu(t):
    t = t.astype(F32)
    return t[:, :D_MODEL] * jax.nn.sigmoid(t[:, D_MODEL:])


def _conv_tile(s):
    return 256 if s % 256 == 0 else s


CONV_ROWS = 32
CONV_LANES = 512
SUBLANES = 8


def _shifted_copies(sh_ref, rows):
    for b in range(1, SUBLANES):
        sh_ref[b, 0:rows - SUBLANES, :] = sh_ref[0, b:b + rows - SUBLANES, :]


def conv_fwd(t, cp, name):
    s = t.shape[0]
    d = D_MODEL
    ts = _conv_tile(s)
    per = ts // HALO
    rows = HALO + ts
    lead = HALO - (CONV_WIDTH - 1)
    rc = CONV_ROWS

    def body(t_ref, tp_ref, cp_ref, z_ref, y_ref, sh_ref):
        i = pl.program_id(0)
        sh_ref[0, 0:HALO, :] = jnp.where(i > 0, _glu(tp_ref[...]), 0.0)
        sh_ref[0, HALO:rows, :] = _glu(t_ref[...])
        _shifted_copies(sh_ref, rows)

        def chunk(c, carry):
            r0 = pl.multiple_of(c * rc, rc)
            for lc in range(d // CONV_LANES):
                ln = slice(lc * CONV_LANES, (lc + 1) * CONV_LANES)
                acc = jnp.zeros((rc, CONV_LANES), F32) + cp_ref[ROW_BDW:ROW_BDW + 1, ln]
                for k in range(CONV_WIDTH):
                    a8, b = divmod(lead + k, SUBLANES)
                    acc = acc + cp_ref[k:k + 1, ln] * sh_ref[b, pl.ds(r0 + SUBLANES * a8, rc), ln]
                y_ref[pl.ds(r0, rc), ln] = acc
            y = y_ref[pl.ds(r0, rc), :]
            mu = jnp.mean(y, axis=-1, keepdims=True)
            yc = y - mu
            rstd = lax.rsqrt(jnp.mean(yc * yc, axis=-1, keepdims=True) + EPS)
            yn = yc * rstd * cp_ref[ROW_LNG:ROW_LNG + 1, :] + cp_ref[ROW_LNB:ROW_LNB + 1, :]
            z_ref[pl.ds(r0, rc), :] = (yn * jax.nn.sigmoid(yn)).astype(BF16)
            return carry

        lax.fori_loop(0, ts // rc, chunk, 0)

    row = pl.BlockSpec((ts, d), lambda i: (i, 0))
    return _pcall(
        body, name, (_sds((s, d), BF16), _sds((s, d), F32)), grid=(s // ts,),
        in_specs=[pl.BlockSpec((ts, 2 * d), lambda i: (i, 0)),
                  pl.BlockSpec((HALO, 2 * d), lambda i: (jnp.maximum(i * per - 1, 0), 0)),
                  pl.BlockSpec((PACK_ROWS, d), lambda i: (0, 0))],
        out_specs=(row, row),
        scratch=[pltpu.VMEM((SUBLANES, rows, d), F32)], sem=("parallel",))(t, t, cp)


def conv_bwd(t, y, dz, cp, name):
    s = t.shape[0]
    d = D_MODEL
    ts = _conv_tile(s)
    per = ts // HALO
    nt = s // ts
    te = ts + HALO
    rc = CONV_ROWS

    def body(t_ref, y_ref, yn_ref, dz_ref, dzn_ref, cp_ref, dt_ref, st_ref, shd_ref, dw_ref):
        i = pl.program_id(0)
        last = i == nt - 1

        @pl.when(i == 0)
        def _():
            st_ref[...] = jnp.zeros_like(st_ref)
            dw_ref[...] = jnp.zeros_like(dw_ref)

        gain = cp_ref[ROW_LNG:ROW_LNG + 1, :]

        def ln_bwd(yv, dzv):
            mu = jnp.mean(yv, axis=-1, keepdims=True)
            yc = yv - mu
            rstd = lax.rsqrt(jnp.mean(yc * yc, axis=-1, keepdims=True) + EPS)
            yh = yc * rstd
            yn = yh * gain + cp_ref[ROW_LNB:ROW_LNB + 1, :]
            sig = jax.nn.sigmoid(yn)
            dyn = dzv * (sig * (1.0 + yn * (1.0 - sig)))
            dyh = dyn * gain
            dy = rstd * (dyh - jnp.mean(dyh, axis=-1, keepdims=True)
                         - yh * jnp.mean(dyh * yh, axis=-1, keepdims=True))
            return dy, dyn, yh

        def norm_chunk(c, carry):
            r0 = pl.multiple_of(c * rc, rc)
            dy, dyn, yh = ln_bwd(y_ref[pl.ds(r0, rc), :], dz_ref[pl.ds(r0, rc), :])
            shd_ref[0, pl.ds(r0, rc), :] = dy
            st_ref[ROW_BDW:ROW_BDW + 1, :] += jnp.sum(dy, axis=0, keepdims=True)
            st_ref[ROW_LNG:ROW_LNG + 1, :] += jnp.sum(dyn * yh, axis=0, keepdims=True)
            st_ref[ROW_LNB:ROW_LNB + 1, :] += jnp.sum(dyn, axis=0, keepdims=True)
            return carry

        lax.fori_loop(0, ts // rc, norm_chunk, 0)
        dy_halo, _, _ = ln_bwd(yn_ref[...], jnp.where(last, 0.0, dzn_ref[...]))
        shd_ref[0, ts:te, :] = dy_halo
        _shifted_copies(shd_ref, te)

        def tap_chunk(c, carry):
            r0 = pl.multiple_of(c * rc, rc)
            for lc in range(d // CONV_LANES):
                ln = slice(lc * CONV_LANES, (lc + 1) * CONV_LANES)
                ln2 = slice(d + lc * CONV_LANES, d + (lc + 1) * CONV_LANES)
                t1 = t_ref[pl.ds(r0, rc), ln].astype(F32)
                sg = jax.nn.sigmoid(t_ref[pl.ds(r0, rc), ln2].astype(F32))
                a = t1 * sg
                da = jnp.zeros((rc, CONV_LANES), F32)
                for k in range(CONV_WIDTH):
                    a8, b = divmod(CONV_WIDTH - 1 - k, SUBLANES)
                    e = shd_ref[b, pl.ds(r0 + SUBLANES * a8, rc), ln]
                    da = da + cp_ref[k:k + 1, ln] * e
                    dw_ref[k, :, ln] += jnp.sum((a * e).reshape(rc // SUBLANES, SUBLANES, CONV_LANES), axis=0)
                dt_ref[pl.ds(r0, rc), ln] = (da * sg).astype(BF16)
                dt_ref[pl.ds(r0, rc), ln2] = (da * t1 * sg * (1.0 - sg)).astype(BF16)
            return carry

        lax.fori_loop(0, ts // rc, tap_chunk, 0)

        @pl.when(last)
        def _():
            for k in range(CONV_WIDTH):
                st_ref[k:k + 1, :] = jnp.sum(dw_ref[k], axis=0, keepdims=True)

    last_halo = s // HALO - 1
    row = pl.BlockSpec((ts, d), lambda i: (i, 0))
    halo = pl.BlockSpec((HALO, d), lambda i: (jnp.minimum((i + 1) * per, last_halo), 0))
    return _pcall(
        body, name, (_sds((s, 2 * d), BF16), _sds((PACK_ROWS, d), F32)), grid=(nt,),
        in_specs=[pl.BlockSpec((ts, 2 * d), lambda i: (i, 0)), row, halo, row, halo,
                  pl.BlockSpec((PACK_ROWS, d), lambda i: (0, 0))],
        out_specs=(pl.BlockSpec((ts, 2 * d), lambda i: (i, 0)), pl.BlockSpec((PACK_ROWS, d), lambda i: (0, 0))),
        scratch=[pltpu.VMEM((SUBLANES, te, d), F32), pltpu.VMEM((CONV_WIDTH, SUBLANES, d), F32)],
        sem=("arbitrary",))(t, y, y, dz, dz, cp)


def _bucket_table():
    qi = np.arange(BLOCK, dtype=np.int64)[:, None]
    kj = np.arange(2 * BLOCK, dtype=np.int64)[None, :]
    dist = qi + BLOCK - kj
    max_exact = N_BUCKETS // 2
    dd = np.maximum(dist, 0)
    ratio = (np.maximum(dd, 1).astype(np.float32) / np.float32(max_exact)).astype(np.float32)
    log_ratio = (np.log(ratio).astype(np.float32) / np.float32(math.log(MAX_DISTANCE / max_exact))).astype(np.float32)
    large = max_exact + (log_ratio * np.float32(N_BUCKETS - max_exact)).astype(np.int32)
    large = np.minimum(large, N_BUCKETS - 1)
    bucket = np.where(dd < max_exact, dd, large)
    return np.where((dist >= 0) & (dist < BLOCK), bucket, -1).astype(np.int32)


def bias_table(rel_bias, bucket, name):
    def body(rb_ref, bk_ref, o_ref):
        bk = bk_ref[...]
        for h in range(N_HEADS):
            acc = jnp.full((BLOCK, 2 * BLOCK), NEG_INF, F32)
            for b in range(N_BUCKETS):
                acc = jnp.where(bk == b, rb_ref[b, h], acc)
            o_ref[h] = acc

    return _pcall(body, name, _sds((N_HEADS, BLOCK, 2 * BLOCK), F32),
                  in_specs=[pl.BlockSpec(memory_space=pltpu.SMEM), pl.BlockSpec(memory_space=pltpu.VMEM)],
                  out_specs=pl.BlockSpec(memory_space=pltpu.VMEM))(rel_bias, bucket)


def bias_grad(dba, dbb, bucket, name):
    def body(a_ref, b_ref, bk_ref, o_ref):
        bk = bk_ref[...]
        for h in range(N_HEADS):
            db = a_ref[h] + b_ref[h]
            for b in range(N_BUCKETS):
                o_ref[b, h] = jnp.sum(jnp.where(bk == b, db, 0.0))

    vm = pl.BlockSpec(memory_space=pltpu.VMEM)
    return _pcall(body, name, _sds((N_BUCKETS, N_HEADS), F32), in_specs=[vm, vm, vm],
                  out_specs=pl.BlockSpec(memory_space=pltpu.SMEM))(dba, dbb, bucket)


def _band_specs():
    cur = pl.BlockSpec((BLOCK, 2 * KV_DIM), lambda n: (n, 0))
    prev = pl.BlockSpec((BLOCK, 2 * KV_DIM), lambda n: (jnp.maximum(n - 1, 0), 0))
    return cur, prev


def _scores(q_h, k_h, bias_h, first_row, sink):
    sc = _dot(q_h, k_h, NT) * (HEAD_DIM ** -0.5) + bias_h + first_row
    m = jnp.maximum(jnp.max(sc, axis=-1, keepdims=True), sink)
    p = jnp.exp(sc - m)
    e_sink = jnp.exp(sink - m)
    den = jnp.sum(p, axis=-1, keepdims=True) + e_sink
    return p, e_sink, den


def _first_block_row(n):
    col = lax.broadcasted_iota(jnp.int32, (1, 2 * BLOCK), 1)
    return jnp.where((col < BLOCK) & (n == 0), NEG_INF, 0.0)


def _head_lanes(hk, g):
    h = hk * GROUP + g
    return slice(h * HEAD_DIM, (h + 1) * HEAD_DIM)


def _group_rows(x_ref, hk):
    return jnp.concatenate([x_ref[:, _head_lanes(hk, g)] for g in range(GROUP)], axis=0)


def _group_bias(bias_ref, hk):
    return bias_ref[hk * GROUP:(hk + 1) * GROUP].reshape(GROUP * BLOCK, 2 * BLOCK)


def _group_sinks(sink_ref, hk):
    head = lax.broadcasted_iota(jnp.int32, (GROUP * BLOCK, 1), 0) // BLOCK
    col = jnp.zeros((GROUP * BLOCK, 1), F32) + sink_ref[0, hk * GROUP]
    for g in range(1, GROUP):
        col = jnp.where(head == g, sink_ref[0, hk * GROUP + g], col)
    return col


def attn_fwd(q, kv, bias, sinks, name):
    s = q.shape[0]
    nb = s // BLOCK

    def body(sink_ref, q_ref, kvc_ref, kvp_ref, bias_ref, o_ref, band_ref):
        n = pl.program_id(0)
        band_ref[0:BLOCK, :] = kvp_ref[...]
        band_ref[BLOCK:2 * BLOCK, :] = kvc_ref[...]
        first_row = _first_block_row(n)
        for hk in range(N_KV_HEADS):
            k_h = band_ref[:, hk * HEAD_DIM:(hk + 1) * HEAD_DIM]
            v_h = band_ref[:, KV_DIM + hk * HEAD_DIM:KV_DIM + (hk + 1) * HEAD_DIM]
            p, _, den = _scores(_group_rows(q_ref, hk), k_h, _group_bias(bias_ref, hk), first_row,
                                _group_sinks(sink_ref, hk))
            o = _dot((p * (1.0 / den)).astype(BF16), v_h, NN).astype(BF16)
            for g in range(GROUP):
                o_ref[:, _head_lanes(hk, g)] = o[g * BLOCK:(g + 1) * BLOCK]

    cur, prev = _band_specs()
    qs = pl.BlockSpec((BLOCK, D_MODEL), lambda n: (n, 0))
    return _pcall(
        body, name, _sds((s, D_MODEL), BF16), grid=(nb,),
        in_specs=[pl.BlockSpec(memory_space=pltpu.SMEM), qs, cur, prev,
                  pl.BlockSpec((N_HEADS, BLOCK, 2 * BLOCK), lambda n: (0, 0, 0))],
        out_specs=qs, scratch=[pltpu.VMEM((2 * BLOCK, 2 * KV_DIM), BF16)],
        sem=("parallel",))(sinks, q, kv, kv, bias)


def attn_bwd(q, kv, do, bias, sinks, name):
    s = q.shape[0]
    nb = s // BLOCK
    scale = HEAD_DIM ** -0.5

    def body(sink_ref, q_ref, do_ref, kvc_ref, kvp_ref, bias_ref, dq_ref, dkv_ref, db_ref, dsink_ref,
             band_ref, dsacc_ref):
        n = pl.program_id(0)
        band_ref[0:BLOCK, :] = kvp_ref[...]
        band_ref[BLOCK:2 * BLOCK, :] = kvc_ref[...]
        first_row = _first_block_row(n)
        lane = lax.broadcasted_iota(jnp.int32, (BLOCK, BLOCK), 1)

        @pl.when(n == 0)
        def _():
            db_ref[...] = jnp.zeros_like(db_ref)
            dsacc_ref[...] = jnp.zeros_like(dsacc_ref)

        for hk in range(N_KV_HEADS):
            k_h = band_ref[:, hk * HEAD_DIM:(hk + 1) * HEAD_DIM]
            v_h = band_ref[:, KV_DIM + hk * HEAD_DIM:KV_DIM + (hk + 1) * HEAD_DIM]
            q_g = _group_rows(q_ref, hk)
            do_g = _group_rows(do_ref, hk)
            p, e_sink, den = _scores(q_g, k_h, _group_bias(bias_ref, hk), first_row, _group_sinks(sink_ref, hk))
            inv = 1.0 / den
            p = p * inv
            dp = _dot(do_g, v_h, NT)
            delta = jnp.sum(p * dp, axis=-1, keepdims=True)
            ds = p * (dp - delta)
            db_ref[hk * GROUP:(hk + 1) * GROUP] += ds.reshape(GROUP, BLOCK, 2 * BLOCK)
            d_sink = -(e_sink * inv) * delta
            for g in range(GROUP):
                dsacc_ref[...] += jnp.where(lane == hk * GROUP + g, d_sink[g * BLOCK:(g + 1) * BLOCK], 0.0)
            dsb = ds.astype(BF16)
            dq = (_dot(dsb, k_h, NN) * scale).astype(BF16)
            for g in range(GROUP):
                dq_ref[:, _head_lanes(hk, g)] = dq[g * BLOCK:(g + 1) * BLOCK]
            dkv_ref[:, hk * HEAD_DIM:(hk + 1) * HEAD_DIM] = _dot(dsb, q_g, TN) * scale
            dkv_ref[:, KV_DIM + hk * HEAD_DIM:KV_DIM + (hk + 1) * HEAD_DIM] = _dot(p.astype(BF16), do_g, TN)

        @pl.when(n == nb - 1)
        def _():
            dsink_ref[...] = jnp.sum(dsacc_ref[...], axis=0, keepdims=True)

    cur, prev = _band_specs()
    qs = pl.BlockSpec((BLOCK, D_MODEL), lambda n: (n, 0))
    full_b = pl.BlockSpec((N_HEADS, BLOCK, 2 * BLOCK), lambda n: (0, 0, 0))
    return _pcall(
        body, name,
        (_sds((s, D_MODEL), BF16), _sds((nb, 2 * BLOCK, 2 * KV_DIM), F32),
         _sds((N_HEADS, BLOCK, 2 * BLOCK), F32), _sds((1, BLOCK), F32)),
        grid=(nb,),
        in_specs=[pl.BlockSpec(memory_space=pltpu.SMEM), qs, qs, cur, prev, full_b],
        out_specs=(qs, pl.BlockSpec((None, 2 * BLOCK, 2 * KV_DIM), lambda n: (n, 0, 0)), full_b,
                   pl.BlockSpec((1, BLOCK), lambda n: (0, 0))),
        scratch=[pltpu.VMEM((2 * BLOCK, 2 * KV_DIM), BF16), pltpu.VMEM((BLOCK, BLOCK), F32)],
        sem=("arbitrary",))(sinks, q, do, kv, kv, bias)


def dkv_combine(pa, pb, name):
    nb = pa.shape[0]
    pa2 = pa.reshape(2 * nb, BLOCK, 2 * KV_DIM)
    pb2 = pb.reshape(2 * nb, BLOCK, 2 * KV_DIM)

    def body(ac_ref, an_ref, bc_ref, bn_ref, o_ref):
        n = pl.program_id(0)
        nxt = jnp.where(n == nb - 1, 0.0, an_ref[...] + bn_ref[...])
        o_ref[...] = (ac_ref[...] + bc_ref[...] + nxt).astype(BF16)

    cur = pl.BlockSpec((None, BLOCK, 2 * KV_DIM), lambda n: (2 * n + 1, 0, 0))
    nxt = pl.BlockSpec((None, BLOCK, 2 * KV_DIM), lambda n: (jnp.minimum(2 * n + 2, 2 * nb - 2), 0, 0))
    return _pcall(body, name, _sds((nb * BLOCK, 2 * KV_DIM), BF16), grid=(nb,),
                  in_specs=[cur, nxt, cur, nxt], out_specs=pl.BlockSpec((BLOCK, 2 * KV_DIM), lambda n: (n, 0)),
                  sem=("parallel",))(pa2, pa2, pb2, pb2)


def loss_head(h, g, target, name):
    s, d = h.shape
    tm = _row_tile(s, 512)

    def body(h_ref, g_ref, t_ref, dh_ref, dg_ref, loss_ref):
        i = pl.program_id(0)
        x = h_ref[...]
        r = lax.rsqrt(jnp.mean(x * x, axis=-1, keepdims=True) + EPS)
        xh = x * r
        gv = g_ref[...]
        err = xh * gv - t_ref[...]
        part_loss = jnp.zeros((1, BLOCK), F32) + 0.5 * jnp.sum(jnp.mean(err * err, axis=-1, keepdims=True))
        dy = err * (1.0 / d)
        dxh = dy * gv
        dh_ref[...] = r * (dxh - xh * jnp.mean(dxh * xh, axis=-1, keepdims=True))
        part_g = jnp.sum(dy * xh, axis=0, keepdims=True)

        @pl.when(i == 0)
        def _():
            dg_ref[...] = part_g
            loss_ref[...] = part_loss

        @pl.when(i > 0)
        def _():
            dg_ref[...] += part_g
            loss_ref[...] += part_loss

    row = pl.BlockSpec((tm, d), lambda i: (i, 0))
    vec = pl.BlockSpec((1, d), lambda i: (0, 0))
    return _pcall(body, name, (_sds((s, d), F32), _sds((1, d), F32), _sds((1, BLOCK), F32)), grid=(s // tm,),
                  in_specs=[row, vec, row], out_specs=(row, vec, pl.BlockSpec((1, BLOCK), lambda i: (0, 0))),
                  sem=("arbitrary",))(h, g, target)


def adamw(w, m, v, parts, name):
    nl, r, c = w.shape
    tr = max(t for t in range(1, min(r, 512) + 1) if r % t == 0 and (t % 16 == 0 or t == r))
    c1 = 1.0 / (1.0 - ADAM_B1 ** ADAM_STEP)
    c2 = 1.0 / (1.0 - ADAM_B2 ** ADAM_STEP)

    def body(w_ref, m_ref, v_ref, p_ref, g_ref, d_ref, nm_ref, nv_ref):
        g = p_ref[0].astype(F32)
        for dev in range(1, N_DEV):
            g = g + p_ref[dev].astype(F32)
        nm = ADAM_B1 * m_ref[...] + (1.0 - ADAM_B1) * g
        nv = ADAM_B2 * v_ref[...] + (1.0 - ADAM_B2) * (g * g)
        g_ref[...] = g
        nm_ref[...] = nm
        nv_ref[...] = nv
        d_ref[...] = -ADAM_LR * ((nm * c1) / (jnp.sqrt(nv * c2) + ADAM_EPS) + ADAM_WD * w_ref[...])

    blk = pl.BlockSpec((None, tr, c), lambda l, i: (l, i, 0))
    out = _sds((nl, r, c), F32)
    return _pcall(body, name, (out, out, out, out), grid=(nl, r // tr),
                  in_specs=[blk, blk, blk, pl.BlockSpec((N_DEV, None, tr, c), lambda l, i: (0, l, i, 0))],
                  out_specs=(blk, blk, blk, blk), sem=("parallel", "parallel"))(w, m, v, parts)


def _place():
    x, y, c = lax.axis_index("x"), lax.axis_index("y"), lax.axis_index("c")
    return x, y, c


def _lin(px, py, pc):
    return 4 * px + 2 * py + pc


HBM = pl.BlockSpec(memory_space=pltpu.HBM)
SEM = pl.BlockSpec(memory_space=pltpu.SEMAPHORE)
EFFECT = pltpu.SideEffectType.DATAFLOW_SIDE_EFFECTING
N_PEERS = N_DEV - 1


def _peers_of(x, y, c):
    return [(x, y, 1 - c), (1 - x, y, c), (x, 1 - y, c), (1 - x, 1 - y, c),
            (1 - x, y, 1 - c), (x, 1 - y, 1 - c), (1 - x, 1 - y, 1 - c)]


def _in_hbm(a):
    return pltpu.with_memory_space_constraint(a, pltpu.HBM)


def send_start(name, srcs, lands, copies, n_groups):
    ns, nl = len(srcs), len(lands)
    per_group = [[i for i, cp in enumerate(copies) if cp[0] == g] for g in range(n_groups)]

    def body(*refs):
        src, land = refs[:ns], refs[ns:ns + nl]
        outs = refs[ns + nl:]
        sems = outs[:2 * n_groups]
        token = outs[2 * n_groups + ns + nl]
        local_sems = refs[-1]
        x, y, c = _place()
        me = _lin(x, y, c)
        peers = _peers_of(x, y, c)
        own = []
        for i, (_, s, src_slab, d, land_slab) in enumerate(copies):
            own.append(pltpu.make_async_copy(src_slab(src[s], me), land_slab(land[d], me), local_sems.at[i]))
            own[-1].start()
        for g in range(n_groups):
            for slot, i in enumerate(per_group[g]):
                _, s, src_slab, d, land_slab = copies[i]
                for k, peer in enumerate(peers):
                    pltpu.make_async_remote_copy(
                        src_ref=src_slab(src[s], _lin(*peer)), dst_ref=land_slab(land[d], me),
                        send_sem=sems[2 * g].at[slot * N_PEERS + k], recv_sem=sems[2 * g + 1].at[slot * N_PEERS + k],
                        device_id=peer, device_id_type=MESH).start()
        for cp in own:
            cp.wait()
        token[...] = jnp.zeros_like(token)

    sem_shapes = []
    for g in range(n_groups):
        sem_shapes += [pltpu.SemaphoreType.DMA((len(per_group[g]) * N_PEERS,))] * 2
    bufs = list(srcs) + list(lands)
    out = pl.pallas_call(
        body, name=name,
        out_shape=tuple(sem_shapes) + tuple(pltpu.HBM(b.shape, b.dtype) for b in bufs) + (_sds((8, 128), F32),),
        in_specs=[HBM] * len(bufs),
        out_specs=tuple([SEM] * len(sem_shapes)) + tuple([HBM] * len(bufs))
        + (pl.BlockSpec(memory_space=pltpu.VMEM),),
        input_output_aliases={i: len(sem_shapes) + i for i in range(len(bufs))},
        scratch_shapes=[pltpu.SemaphoreType.DMA((len(copies),))],
        compiler_params=pltpu.CompilerParams(has_side_effects=EFFECT))(*[_in_hbm(b) for b in bufs])
    sems = [(out[2 * g], out[2 * g + 1]) for g in range(n_groups)]
    thru = out[2 * n_groups:2 * n_groups + ns + nl]
    return sems, list(thru[:ns]), list(thru[ns:])


def send_wait(name, srcs, lands, copies, sems, after):
    ns, nl = len(srcs), len(lands)

    def body(*refs):
        src, land = refs[:ns], refs[ns:ns + nl]
        send_sems, recv_sems = refs[ns + nl], refs[ns + nl + 1]
        x, y, c = _place()
        for slot, (s, src_slab, d, land_slab) in enumerate(copies):
            for k, peer in enumerate(_peers_of(x, y, c)):
                j = _lin(*peer)
                cp = pltpu.make_async_remote_copy(
                    src_ref=src_slab(src[s], j), dst_ref=land_slab(land[d], j),
                    send_sem=send_sems.at[slot * N_PEERS + k], recv_sem=recv_sems.at[slot * N_PEERS + k],
                    device_id=peer, device_id_type=MESH)
                cp.wait_send()
                cp.wait_recv()

    bufs = list(srcs) + list(lands)
    out = pl.pallas_call(
        body, name=name, out_shape=tuple(pltpu.HBM(b.shape, b.dtype) for b in bufs),
        in_specs=[HBM] * len(bufs) + [SEM, SEM, ANY], out_specs=tuple([HBM] * len(bufs)),
        input_output_aliases={i: i for i in range(len(bufs))},
        compiler_params=pltpu.CompilerParams(has_side_effects=EFFECT))(*bufs, sems[0], sems[1], after)
    return list(out[ns:])


def local_step(x, target, weights, rep, emit):
    s = x.shape[0]
    bucket = jnp.asarray(_bucket_table())
    bias = bias_table(rep["rel_bias"], bucket, "bias_table")
    h = x
    saved = []
    kv = None
    h_kv = u_kv = None
    small = None
    for l in range(4):
        g_mix = rep["norm_mix"][l:l + 1]
        g_ffn = rep["norm_ffn"][l:l + 1]
        rec = {"h_in": h}
        u = rms_fwd(h, g_mix, f"rms_mix_fwd{l}")
        rec["u"] = u
        if l < 2:
            w = weights(f"conv{l}", u)
            if l == 0:
                small = w
            cp = small["cp"][l]
            t = mm_nn(f"pw1_fwd{l}", u, w["pw1"], (None, None, D_MODEL, 256), lambda i, j: (0, j, 0, 0),
                      2 * D_MODEL, 256, BF16,
                      bias=(small["b_pw1"], pl.BlockSpec((None, 1, 256), lambda i, j, l=l: (l, 0, j))))
            z, y = conv_fwd(t, cp, f"conv_fwd{l}")
            h = mm_nn(f"pw2_fwd{l}", z, w["pw2"], (None, D_MODEL, 512), lambda i, j: (0, 0, j), D_MODEL, 512,
                      F32, bias=(small["b_pw2"], pl.BlockSpec((None, 1, 512), lambda i, j, l=l: (l, 0, j))), res=h)
            rec.update(t=t, z=z, y=y, cp=cp)
        else:
            a = l - 2
            w = weights(f"attn{a}", u)
            if a == 0:
                h_kv = h
                u_kv = rms_fwd(h, rep["norm_kv"], "rms_kv_fwd")
                w_kv = w["wkv"]
                kv = mm_nn("kv_fwd", u_kv, w_kv, (D_MODEL, 2 * KV_DIM), lambda i, j: (0, 0), 2 * KV_DIM,
                           2 * KV_DIM, BF16)
            q = mm_nn(f"q_fwd{a}", u, w["wq"], (None, D_MODEL, 512), lambda i, j: (0, 0, j), D_MODEL, 512, BF16)
            o = attn_fwd(q, kv, bias, rep["sinks"][a:a + 1], f"attn_fwd{a}")
            h = mm_nn(f"o_fwd{a}", o, w["wo"], (None, D_MODEL, 512), lambda i, j: (0, 0, j), D_MODEL, 512, F32,
                      res=h)
            rec.update(q=q, o=o)
        rec["w"] = w
        rec["h_mid"] = h
        uf = rms_fwd(h, g_ffn, f"rms_ffn_fwd{l}")
        wf = weights(f"ffn{l}", uf)
        h, gu = ffn_fwd(uf, h, wf["up"], wf["down"], 0, f"ffn_fwd{l}")
        rec.update(uf=uf, gu=gu, wf=wf)
        saved.append(rec)

    dh, d_nfin, loss = loss_head(h, rep["norm_final"], target, "loss_head")

    d_mix, d_ffn = [None] * 4, [None] * 4
    cp_grads = [None, None]
    dkv_parts, dbias_parts, dsinks = [], [], [None, None]
    d_nkv = None
    full_rows = lambda tk: (tk, D_MODEL)
    for l in reversed(range(4)):
        rec = saved[l]
        w, wf = rec["w"], rec["wf"]
        grads = {}
        g_mix = rep["norm_mix"][l:l + 1]
        g_ffn = rep["norm_ffn"][l:l + 1]
        du, act, dgu = ffn_bwd(dh, rec["gu"], wf["up"], wf["down"], 0, f"ffn_bwd{l}")
        g_down = mm_tn(
            f"down_grad{l}", act, dh, groups=4, a_block=lambda tk: (None, tk, FF_CHUNK), a_index=lambda j, k: (j, k, 0),
            b_block=full_rows, b_index=lambda j, k: (k, 0), o_block=(None, FF_CHUNK, D_MODEL),
            o_index=lambda j, k: (j, 0, 0), o_shape=(4, FF_CHUNK, D_MODEL), acc_shape=(FF_CHUNK, D_MODEL))
        g_up = mm_tn(
            f"up_grad{l}", rec["uf"], dgu.reshape(8, s, FF_CHUNK), groups=8, a_block=full_rows,
            a_index=lambda j, k: (k, 0), b_block=lambda tk: (None, tk, FF_CHUNK), b_index=lambda j, k: (j, k, 0),
            o_block=(None, D_MODEL, FF_CHUNK), o_index=lambda j, k: (j, 0, 0), o_shape=(8, D_MODEL, FF_CHUNK),
            acc_shape=(D_MODEL, FF_CHUNK))
        emit(f"ffn{l}", {"up": g_up, "down": g_down.reshape(D_FF, D_MODEL)})
        dh, d_ffn[l] = rms_bwd(rec["h_mid"], g_ffn, du, dh, f"rms_ffn_bwd{l}")
        if l < 2:
            dz = mm_nt(f"pw2_bwd{l}", dh, w["pw2"], (None, D_MODEL, D_MODEL), lambda i, k: (0, 0, 0), D_MODEL, F32)
            grads["pw2"], db2 = mm_tn(
                f"pw2_grad{l}", rec["z"], dh, groups=1, a_block=full_rows, a_index=lambda j, k: (k, 0),
                b_block=full_rows, b_index=lambda j, k: (k, 0), o_block=(D_MODEL, D_MODEL), o_index=lambda j, k: (0, 0),
                o_shape=(D_MODEL, D_MODEL), acc_shape=(D_MODEL, D_MODEL),
                colsum=((1, D_MODEL), pl.BlockSpec((1, D_MODEL), lambda j, k: (0, 0))))
            dt, stats = conv_bwd(rec["t"], rec["y"], dz, rec["cp"], f"conv_bwd{l}")
            du = mm_nt(f"pw1_bwd{l}", dt, w["pw1"], (None, None, D_MODEL, 256), lambda i, k: (0, k, 0, 0),
                       D_MODEL, F32, nk=8, tk=256)
            grads["pw1"], db1 = mm_tn(
                f"pw1_grad{l}", rec["u"], dt, groups=8, a_block=full_rows, a_index=lambda j, k: (k, 0),
                b_block=lambda tk: (tk, 256), b_index=lambda j, k: (k, j), o_block=(None, D_MODEL, 256),
                o_index=lambda j, k: (j, 0, 0), o_shape=(8, D_MODEL, 256), acc_shape=(D_MODEL, 256),
                colsum=((8, 1, 256), pl.BlockSpec((None, 1, 256), lambda j, k: (j, 0, 0))))
            cp_grads[l] = (stats, db2, db1)
            emit(f"conv{l}", grads)
        else:
            a = l - 2
            do = mm_nt(f"o_bwd{a}", dh, w["wo"], (None, D_MODEL, D_MODEL), lambda i, k: (0, 0, 0), D_MODEL, BF16)
            grads["wo"] = mm_tn(
                f"wo_grad{a}", rec["o"], dh, groups=1, a_block=full_rows, a_index=lambda j, k: (k, 0),
                b_block=full_rows, b_index=lambda j, k: (k, 0), o_block=(D_MODEL, D_MODEL), o_index=lambda j, k: (0, 0),
                o_shape=(D_MODEL, D_MODEL), acc_shape=(D_MODEL, D_MODEL))
            dq, dkv_p, dbias_p, dsinks[a] = attn_bwd(rec["q"], kv, do, bias, rep["sinks"][a:a + 1], f"attn_bwd{a}")
            dkv_parts.append(dkv_p)
            dbias_parts.append(dbias_p)
            du = mm_nt(f"q_bwd{a}", dq, w["wq"], (None, D_MODEL, D_MODEL), lambda i, k: (0, 0, 0), D_MODEL, F32)
            grads["wq"] = mm_tn(
                f"wq_grad{a}", rec["u"], dq, groups=1, a_block=full_rows, a_index=lambda j, k: (k, 0),
                b_block=full_rows, b_index=lambda j, k: (k, 0), o_block=(D_MODEL, D_MODEL), o_index=lambda j, k: (0, 0),
                o_shape=(D_MODEL, D_MODEL), acc_shape=(D_MODEL, D_MODEL))
            if a == 1:
                emit("attn1", grads)
        dh, d_mix[l] = rms_bwd(rec["h_in"], g_mix, du, dh, f"rms_mix_bwd{l}")
        if l == 2:
            dkv = dkv_combine(dkv_parts[0], dkv_parts[1], "dkv_combine")
            du_kv = mm_nt("kv_bwd", dkv, w_kv, (D_MODEL, 2 * KV_DIM), lambda i, k: (0, 0), D_MODEL, F32)
            grads["wkv"] = mm_tn(
                "wkv_grad", u_kv, dkv, groups=1, a_block=full_rows, a_index=lambda j, k: (k, 0),
                b_block=lambda tk: (tk, 2 * KV_DIM), b_index=lambda j, k: (k, 0), o_block=(D_MODEL, 2 * KV_DIM),
                o_index=lambda j, k: (0, 0), o_shape=(D_MODEL, 2 * KV_DIM), acc_shape=(D_MODEL, 2 * KV_DIM))
            emit("attn0", grads)
            dh, d_nkv = rms_bwd(h_kv, rep["norm_kv"], du_kv, dh, "rms_kv_bwd")

    d_relb = bias_grad(dbias_parts[0], dbias_parts[1], bucket, "bias_grad")
    d_sinks = jnp.concatenate([dsinks[0][0, :N_HEADS], dsinks[1][0, :N_HEADS]])
    tail = jnp.zeros((D_MODEL,), F32)
    rep_grad = jnp.concatenate([
        jnp.concatenate(d_mix, axis=0), jnp.concatenate(d_ffn, axis=0), d_nkv, d_nfin,
        tail.at[:2 * N_HEADS].set(d_sinks)[None], tail.at[:N_BUCKETS * N_HEADS].set(d_relb.reshape(-1))[None],
        tail.at[0].set(loss[0, 0])[None], jnp.zeros((REP_ROWS - ROW_LOSS - 1, D_MODEL), F32)], axis=0)
    return dh, cp_grads, rep_grad


def _pack_conv(w_dw, b_dw, ln_g, ln_b, b_pw2, b_pw1):
    rows = [w_dw, b_dw[:, None], ln_g[:, None], ln_b[:, None], b_pw2[:, None], b_pw1.reshape(2, 2, 128),
            jnp.zeros((2, PACK_ROWS - ROW_BPW1 - 2, 128), F32)]
    return jnp.concatenate(rows, axis=1)


def _unpack_conv(p):
    return (p[:, :CONV_WIDTH], p[:, ROW_BDW], p[:, ROW_LNG], p[:, ROW_LNB], p[:, ROW_BPW2],
            p[:, ROW_BPW1:ROW_BPW1 + 2].reshape(2, 256))


def _pack_rep(norm_mix, norm_ffn, norm_kv, norm_final, sinks, rel_bias):
    tail = jnp.zeros((D_MODEL,), F32)
    return jnp.concatenate([
        norm_mix, norm_ffn, norm_kv[None], norm_final[None], tail.at[:2 * N_HEADS].set(sinks.reshape(-1))[None],
        tail.at[:N_BUCKETS * N_HEADS].set(rel_bias.reshape(-1))[None],
        jnp.zeros((REP_ROWS - ROW_RELB - 1, D_MODEL), F32)], axis=0)


def _unpack_rep(p):
    return (p[0:4], p[4:8], p[ROW_NKV], p[ROW_NFIN], p[ROW_SINK, :2 * N_HEADS].reshape(2, N_HEADS),
            p[ROW_RELB, :N_BUCKETS * N_HEADS].reshape(N_BUCKETS, N_HEADS))


def kernel(x, norm_mix, norm_ffn, conv_w_pw1, conv_b_pw1, conv_w_dw, conv_b_dw, conv_ln_g, conv_ln_b, conv_w_pw2, conv_b_pw2, norm_kv, w_kv, w_q, w_o, sinks, rel_bias, ffn_w_up, ffn_w_down, norm_final, loss_target, m_norm_mix, m_norm_ffn, m_conv_w_pw1, m_conv_b_pw1, m_conv_w_dw, m_conv_b_dw, m_conv_ln_g, m_conv_ln_b, m_conv_w_pw2, m_conv_b_pw2, m_norm_kv, m_w_kv, m_w_q, m_w_o, m_sinks, m_rel_bias, m_ffn_w_up, m_ffn_w_down, m_norm_final, v_norm_mix, v_norm_ffn, v_conv_w_pw1, v_conv_b_pw1, v_conv_w_dw, v_conv_b_dw, v_conv_ln_g, v_conv_ln_b, v_conv_w_pw2, v_conv_b_pw2, v_norm_kv, v_w_kv, v_w_q, v_w_o, v_sinks, v_rel_bias, v_ffn_w_up, v_ffn_w_down, v_norm_final):
    s = x.shape[1]
    d = D_MODEL
    rsh = d // N_DEV
    dsh = D_FF // N_DEV

    whole = lambda ref, j: ref
    lead_slab = lambda ref, j: ref.at[j]
    rows_of = lambda rows: (lambda ref, j: ref.at[pl.ds(j * rows, rows), :])

    conv_pack = _pack_conv(conv_w_dw, conv_b_dw, conv_ln_g, conv_ln_b, conv_b_pw2, conv_b_pw1)
    ag_order = ["conv0", "ffn0", "conv1", "ffn1", "attn0", "ffn2", "attn1", "ffn3"]
    ag_src, ag_land, ag_copies, ag_members = [], [], [], {g: [] for g in ag_order}

    def gather(group, key, shard, land_shape, land_slab):
        i = len(ag_src)
        ag_src.append(shard)
        ag_land.append(lax.empty(land_shape, shard.dtype))
        ag_copies.append((ag_order.index(group), i, whole, i, land_slab))
        ag_members[group].append((key, i, land_slab))

    cols_at0 = lambda ref, j: ref.at[0, j]
    rows_at0 = lambda rows: (lambda ref, j: ref.at[0, pl.ds(j * rows, rows), :])
    for l in range(2):
        gather(f"conv{l}", "pw1", conv_w_pw1[l].astype(BF16), (1, N_DEV, d, 256), cols_at0)
        gather(f"conv{l}", "pw2", conv_w_pw2[l].astype(BF16), (1, d, d), rows_at0(rsh))
    gather("conv0", "pack", conv_pack, (N_DEV, 2, PACK_ROWS, 128), lead_slab)
    gather("attn0", "wkv", w_kv.astype(BF16), (d, 2 * KV_DIM), rows_of(rsh))
    for a in range(2):
        gather(f"attn{a}", "wq", w_q[a].astype(BF16), (1, d, d), rows_at0(rsh))
        gather(f"attn{a}", "wo", w_o[a].astype(BF16), (1, d, d), rows_at0(rsh))
    for l in range(4):
        gather(f"ffn{l}", "up", ffn_w_up[l].astype(BF16), (1, N_DEV, d, FF_CHUNK), cols_at0)
        gather(f"ffn{l}", "down", ffn_w_down[l].astype(BF16), (1, D_FF, d), rows_at0(dsh))
    ag_sems, ag_src_thru, ag_land_thru = send_start("ag_start", ag_src, ag_land, ag_copies, len(ag_order))

    def weights(group, after):
        members = ag_members[group]
        lands = send_wait(f"ag_wait_{group}", [ag_src_thru[i] for _, i, _ in members],
                          [ag_land_thru[i] for _, i, _ in members],
                          [(n, whole, n, slab) for n, (_, _, slab) in enumerate(members)],
                          ag_sems[ag_order.index(group)], after)
        w = {key: land for (key, _, _), land in zip(members, lands)}
        if "up" in w:
            w["up"] = w["up"].reshape(1, 2, 4, d, FF_CHUNK)
            w["down"] = w["down"].reshape(1, 4, FF_CHUNK, d)
        if "pack" in w:
            pack_g = w.pop("pack")
            w["cp"] = jnp.transpose(pack_g, (1, 2, 0, 3)).reshape(2, PACK_ROWS, d)
            w["b_pw1"] = pack_g[:, :, ROW_BPW1:ROW_BPW1 + 2, :].transpose(1, 0, 2, 3).reshape(2, 1, 2 * d)
            w["b_pw2"] = w["cp"][:, ROW_BPW2:ROW_BPW2 + 1, :]
        return w

    shard_shapes = {"pw1": (d, 256), "pw2": (rsh, d), "wkv": (rsh, 2 * KV_DIM), "wq": (rsh, d), "wo": (rsh, d),
                    "up": (d, FF_CHUNK), "down": (dsh, d), "cp": (2, PACK_ROWS, 128), "rep": (REP_ROWS, d)}
    n_layers = {"pw1": 2, "pw2": 2, "wkv": 1, "wq": 2, "wo": 2, "up": 4, "down": 4, "cp": 1, "rep": 1}
    src_slabs = {"pw1": lead_slab, "pw2": rows_of(rsh), "wkv": rows_of(rsh), "wq": rows_of(rsh), "wo": rows_of(rsh),
                 "up": lead_slab, "down": rows_of(dsh), "cp": lead_slab, "rep": whole}
    parts = {}
    pending = {}

    def finish(chain, after):
        keys, srcs, lands, copies, sems, name = pending.pop(chain)
        done = send_wait(f"rs_wait_{name}", srcs, lands, copies, sems, after)
        parts.update(zip(keys, done))

    def exchange(chain, name, layer, grads):
        keys = list(grads)
        if chain in pending:
            finish(chain, grads[keys[0]])
        lands = [parts.pop(k) if k in parts else
                 lax.empty((N_DEV, n_layers[k]) + shard_shapes[k], BF16 if grads[k].dtype == BF16 else F32)
                 for k in keys]
        land_at = lambda ref, i: ref.at[i, layer]
        copies = [(0, n, src_slabs[k], n, land_at) for n, k in enumerate(keys)]
        sems, srcs_thru, lands_thru = send_start(f"rs_start_{name}", [grads[k] for k in keys], lands, copies, 1)
        pending[chain] = (keys, srcs_thru, lands_thru, [c[1:] for c in copies], sems[0], name)

    def emit(group, grads):
        exchange(group[:-1], group, int(group[-1]), grads)

    rep = {"norm_mix": norm_mix, "norm_ffn": norm_ffn, "norm_kv": norm_kv[None], "norm_final": norm_final[None],
           "sinks": sinks, "rel_bias": rel_bias}

    grad_x, cp_grads, rep_grad = local_step(x[0], loss_target[0], weights, rep, emit)

    cp_full = []
    for l in range(2):
        stats, db2, db1 = cp_grads[l]
        cp_full.append(jnp.concatenate([
            stats[:ROW_BPW2], db2, db1.reshape(N_DEV, 2, 128).transpose(1, 0, 2).reshape(2, d),
            jnp.zeros((PACK_ROWS - ROW_BPW1 - 2, d), F32)], axis=0))
    cp_send = jnp.stack(cp_full).reshape(2, PACK_ROWS, N_DEV, 128).transpose(2, 0, 1, 3)
    exchange("tail", "tail", 0, {"cp": cp_send, "rep": rep_grad})
    for chain in ("ffn", "attn", "conv", "tail"):
        finish(chain, grad_x)

    def update(key, w, m, v, name):
        p = parts[key]
        w3 = w.reshape(p.shape[1:])
        outs = adamw(w3, m.reshape(w3.shape), v.reshape(w3.shape), p, name)
        return [o.reshape(w.shape) for o in outs]

    res = {}
    res["conv_w_pw1"] = update("pw1", conv_w_pw1, m_conv_w_pw1, v_conv_w_pw1, "adam_pw1")
    res["conv_w_pw2"] = update("pw2", conv_w_pw2, m_conv_w_pw2, v_conv_w_pw2, "adam_pw2")
    res["w_kv"] = update("wkv", w_kv, m_w_kv, v_w_kv, "adam_wkv")
    res["w_q"] = update("wq", w_q, m_w_q, v_w_q, "adam_wq")
    res["w_o"] = update("wo", w_o, m_w_o, v_w_o, "adam_wo")
    res["ffn_w_up"] = update("up", ffn_w_up, m_ffn_w_up, v_ffn_w_up, "adam_up")
    res["ffn_w_down"] = update("down", ffn_w_down, m_ffn_w_down, v_ffn_w_down, "adam_down")
    m_pack = _pack_conv(m_conv_w_dw, m_conv_b_dw, m_conv_ln_g, m_conv_ln_b, m_conv_b_pw2, m_conv_b_pw1)
    v_pack = _pack_conv(v_conv_w_dw, v_conv_b_dw, v_conv_ln_g, v_conv_ln_b, v_conv_b_pw2, v_conv_b_pw1)
    cp_res = adamw(conv_pack, m_pack, v_pack, parts["cp"].reshape(N_DEV, 2, PACK_ROWS, 128), "adam_conv_pack")
    rep_w = _pack_rep(norm_mix, norm_ffn, norm_kv, norm_final, sinks, rel_bias)
    rep_m = _pack_rep(m_norm_mix, m_norm_ffn, m_norm_kv, m_norm_final, m_sinks, m_rel_bias)
    rep_v = _pack_rep(v_norm_mix, v_norm_ffn, v_norm_kv, v_norm_final, v_sinks, v_rel_bias)
    rep_res = adamw(rep_w[None], rep_m[None], rep_v[None], parts["rep"], "adam_rep")
    loss = rep_res[0][0, ROW_LOSS, 0]

    outs = []
    for kind in range(4):
        cw_dw, cb_dw, cln_g, cln_b, cb_pw2, cb_pw1 = _unpack_conv(cp_res[kind])
        r_mix, r_ffn, r_nkv, r_nfin, r_sinks, r_relb = _unpack_rep(rep_res[kind][0])
        outs += [r_mix, r_ffn, res["conv_w_pw1"][kind], cb_pw1, cw_dw, cb_dw, cln_g, cln_b, res["conv_w_pw2"][kind],
                 cb_pw2, r_nkv, res["w_kv"][kind], res["w_q"][kind], res["w_o"][kind], r_sinks, r_relb,
                 res["ffn_w_up"][kind], res["ffn_w_down"][kind], r_nfin]
    return (loss, grad_x[None], *outs)
```

```python
import functools
import math

import numpy as np
import jax
import jax.numpy as jnp
from jax import lax
from jax.experimental import pallas as pl
from jax.experimental.pallas import tpu as pltpu

F32 = jnp.float32
BF16 = jnp.bfloat16

D_MODEL = 1024
D_FF = 2816
N_HEADS = 16
N_KV_HEADS = 4
GROUP = N_HEADS // N_KV_HEADS
HEAD_DIM = 64
KV_DIM = N_KV_HEADS * HEAD_DIM
BLOCK = 128
CONV_WIDTH = 31
HALO = 32
N_BUCKETS = 32
MAX_DISTANCE = 128
EPS = 1e-6
NEG_INF = -1e30
N_DEV = 8
FF_CHUNK = D_FF // 4
PACK_ROWS = 40
ROW_BDW, ROW_LNG, ROW_LNB, ROW_BPW2, ROW_BPW1 = 31, 32, 33, 34, 35
REP_ROWS = 16
ROW_NKV, ROW_NFIN, ROW_SINK, ROW_RELB, ROW_LOSS = 8, 9, 10, 11, 12

ADAM_LR, ADAM_B1, ADAM_B2, ADAM_EPS, ADAM_WD, ADAM_STEP = 0.001, 0.9, 0.999, 1e-08, 0.01, 10

VMEM_LIMIT_BYTES = 56 * 1024 * 1024
FFN_ROWS = 1024
GRAD_ROWS = 2048
ANY = pl.BlockSpec(memory_space=pl.ANY)
MESH = pl.DeviceIdType.MESH

NN = (((1,), (0,)), ((), ()))
NT = (((1,), (1,)), ((), ()))
TN = (((0,), (0,)), ((), ()))


def _dot(a, b, dims):
    return lax.dot_general(a, b, dims, preferred_element_type=F32)


def _pcall(body, name, out_shape, *, grid=None, in_specs=None, out_specs=None, scratch=(), sem=None, **kw):
    params = pltpu.CompilerParams(dimension_semantics=sem, vmem_limit_bytes=VMEM_LIMIT_BYTES)
    extra = {} if grid is None else {"grid": grid}
    return pl.pallas_call(body, name=name, out_shape=out_shape, in_specs=in_specs, out_specs=out_specs,
                          scratch_shapes=list(scratch), compiler_params=params, **extra, **kw)


def _sds(shape, dtype):
    return jax.ShapeDtypeStruct(tuple(shape), dtype)


def _row_tile(s, want):
    return want if s % want == 0 else s


def rms_fwd(h, g, name):
    s, d = h.shape
    tm = _row_tile(s, 512)

    def body(h_ref, g_ref, u_ref):
        x = h_ref[...]
        r = lax.rsqrt(jnp.mean(x * x, axis=-1, keepdims=True) + EPS)
        u_ref[...] = (x * r * g_ref[...]).astype(BF16)

    return _pcall(body, name, _sds((s, d), BF16), grid=(s // tm,),
                  in_specs=[pl.BlockSpec((tm, d), lambda i: (i, 0)), pl.BlockSpec((1, d), lambda i: (0, 0))],
                  out_specs=pl.BlockSpec((tm, d), lambda i: (i, 0)), sem=("parallel",))(h, g)


def rms_bwd(h, g, du, dh_in, name, token=None):
    s, d = h.shape
    tm = _row_tile(s, 512)

    def body(h_ref, g_ref, du_ref, dhi_ref, *rest):
        dh_ref, dg_ref = rest[-2:]
        i = pl.program_id(0)
        x = h_ref[...]
        r = lax.rsqrt(jnp.mean(x * x, axis=-1, keepdims=True) + EPS)
        xh = x * r
        du_v = du_ref[...]
        dxh = du_v * g_ref[...]
        dx = r * (dxh - xh * jnp.mean(dxh * xh, axis=-1, keepdims=True))
        dh_ref[...] = dhi_ref[...] + dx
        part = jnp.sum(du_v * xh, axis=0, keepdims=True)

        @pl.when(i == 0)
        def _():
            dg_ref[...] = part

        @pl.when(i > 0)
        def _():
            dg_ref[...] += part

    row = pl.BlockSpec((tm, d), lambda i: (i, 0))
    vec = pl.BlockSpec((1, d), lambda i: (0, 0))
    extra = [] if token is None else [token]
    return _pcall(body, name, (_sds((s, d), F32), _sds((1, d), F32)), grid=(s // tm,),
                  in_specs=[row, vec, row, row] + [ANY] * len(extra), out_specs=(row, vec),
                  sem=("arbitrary",))(h, g, du, dh_in, *extra)


def _mm(name, a, b, *, dims, grid, a_spec, b_spec, o_spec, o_shape, nk=1, acc_shape=None,
        bias=None, res=None, colsum=None, sem=None):
    n_axes = len(grid)

    def body(*refs):
        it = iter(refs)
        a_ref, b_ref = next(it), next(it)
        bias_ref = next(it) if bias is not None else None
        res_ref = next(it) if res is not None else None
        o_ref = next(it)
        cs_ref = next(it) if colsum is not None else None
        acc_ref = next(it) if nk > 1 else None
        k = pl.program_id(n_axes - 1)
        p = _dot(a_ref[...].astype(BF16), b_ref[...].astype(BF16), dims)

        def finish(acc):
            if bias_ref is not None:
                acc = acc + bias_ref[...]
            if res_ref is not None:
                acc = acc + res_ref[...]
            o_ref[...] = acc.astype(o_ref.dtype)

        if nk == 1:
            finish(p)
        else:
            @pl.when(k == 0)
            def _():
                acc_ref[...] = p

            @pl.when(k > 0)
            def _():
                acc_ref[...] += p

            @pl.when(k == nk - 1)
            def _():
                finish(acc_ref[...])

        if cs_ref is not None:
            cs = jnp.sum(b_ref[...].astype(F32), axis=0, keepdims=True)

            @pl.when(k == 0)
            def _():
                cs_ref[...] = cs

            @pl.when(k > 0)
            def _():
                cs_ref[...] += cs

    ins, in_specs = [a, b], [a_spec, b_spec]
    for extra in (bias, res):
        if extra is not None:
            ins.append(extra[0])
            in_specs.append(extra[1])
    out_shape, out_specs = o_shape, o_spec
    if colsum is not None:
        out_shape, out_specs = (o_shape, _sds(colsum[0], F32)), (o_spec, colsum[1])
    scratch = [pltpu.VMEM(acc_shape, F32)] if nk > 1 else []
    if sem is None:
        sem = ("parallel",) * (n_axes - 1) + ("arbitrary",)
    return _pcall(body, name, out_shape, grid=grid, in_specs=in_specs, out_specs=out_specs, scratch=scratch,
                  sem=sem)(*ins)


def mm_nn(name, a, w, w_block, w_index, n, tn, out_dtype, bias=None, res=None, tm=512):
    s, k = a.shape
    tm = _row_tile(s, tm)
    col = lambda i, j: (i, j)
    extras = {}
    if bias is not None:
        extras["bias"] = bias
    if res is not None:
        extras["res"] = (res, pl.BlockSpec((tm, tn), col))
    return _mm(name, a, w, dims=NN, grid=(s // tm, n // tn), a_spec=pl.BlockSpec((tm, k), lambda i, j: (i, 0)),
               b_spec=pl.BlockSpec(w_block, w_index), o_spec=pl.BlockSpec((tm, tn), col), o_shape=_sds((s, n), out_dtype),
               sem=("parallel", "arbitrary"), **extras)


def mm_nt(name, a, w, w_block, w_index, kout, out_dtype, nk=1, tk=None, tm=512):
    s, n = a.shape
    tm = _row_tile(s, tm)
    tk = n if tk is None else tk
    return _mm(name, a, w, dims=NT, grid=(s // tm, nk), a_spec=pl.BlockSpec((tm, tk), lambda i, k: (i, k)),
               b_spec=pl.BlockSpec(w_block, w_index), o_spec=pl.BlockSpec((tm, kout), lambda i, k: (i, 0)),
               o_shape=_sds((s, kout), out_dtype), nk=nk, acc_shape=(tm, kout))


def mm_tn(name, a, b, *, groups, a_block, a_index, b_block, b_index, o_block, o_index, o_shape, acc_shape,
          colsum=None, tk=GRAD_ROWS):
    s = a.shape[-2]
    tk = _row_tile(s, tk)
    return _mm(name, a, b, dims=TN, grid=(groups, s // tk), a_spec=pl.BlockSpec(a_block(tk), a_index),
               b_spec=pl.BlockSpec(b_block(tk), b_index), o_spec=pl.BlockSpec(o_block, o_index),
               o_shape=_sds(o_shape, BF16), nk=s // tk, acc_shape=acc_shape, colsum=colsum)


def ffn_fwd(u, h, w_up, w_down, layer, name):
    s, d = u.shape
    tm = _row_tile(s, FFN_ROWS)
    nj = 4

    def body(u_ref, h_ref, wup_ref, wd_ref, hn_ref, gu_ref, acc_ref):
        j = pl.program_id(1)
        uv = u_ref[...]
        g = _dot(uv, wup_ref[0], NN)
        p = _dot(uv, wup_ref[1], NN)
        gu_ref[0] = g.astype(BF16)
        gu_ref[1] = p.astype(BF16)
        act = (g * jax.nn.sigmoid(g) * p).astype(BF16)
        part = _dot(act, wd_ref[...], NN)

        @pl.when(j == 0)
        def _():
            acc_ref[...] = part

        @pl.when(j > 0)
        def _():
            acc_ref[...] += part

        @pl.when(j == nj - 1)
        def _():
            hn_ref[...] = h_ref[...] + acc_ref[...]

    row = pl.BlockSpec((tm, d), lambda i, j: (i, 0))
    return _pcall(
        body, name, (_sds((s, d), F32), _sds((2, nj, s, FF_CHUNK), BF16)), grid=(s // tm, nj),
        in_specs=[row, row,
                  pl.BlockSpec((None, 2, None, d, FF_CHUNK), lambda i, j: (layer, 0, j, 0, 0)),
                  pl.BlockSpec((None, None, FF_CHUNK, d), lambda i, j: (layer, j, 0, 0))],
        out_specs=(row, pl.BlockSpec((2, None, tm, FF_CHUNK), lambda i, j: (0, j, i, 0))),
        scratch=[pltpu.VMEM((tm, d), F32)], sem=("parallel", "arbitrary"))(u, h, w_up, w_down)


def ffn_bwd(dh, gu, w_up, w_down, layer, name):
    s, d = dh.shape
    tm = _row_tile(s, FFN_ROWS)
    nj = 4

    def body(dh_ref, gu_ref, wup_ref, wd_ref, du_ref, act_ref, dgu_ref):
        j = pl.program_id(1)
        dact = _dot(dh_ref[...].astype(BF16), wd_ref[...], NT)
        g = gu_ref[0].astype(F32)
        p = gu_ref[1].astype(F32)
        sig = jax.nn.sigmoid(g)
        sl = g * sig
        act_ref[...] = (sl * p).astype(BF16)
        dp = (dact * sl).astype(BF16)
        dg = (dact * p * (sig * (1.0 + g * (1.0 - sig)))).astype(BF16)
        dgu_ref[0] = dg
        dgu_ref[1] = dp
        part = _dot(dg, wup_ref[0], NT) + _dot(dp, wup_ref[1], NT)

        @pl.when(j == 0)
        def _():
            du_ref[...] = part

        @pl.when(j > 0)
        def _():
            du_ref[...] += part

    row = pl.BlockSpec((tm, d), lambda i, j: (i, 0))
    gu_spec = pl.BlockSpec((2, None, tm, FF_CHUNK), lambda i, j: (0, j, i, 0))
    return _pcall(
        body, name, (_sds((s, d), F32), _sds((nj, s, FF_CHUNK), BF16), _sds((2, nj, s, FF_CHUNK), BF16)),
        grid=(s // tm, nj),
        in_specs=[row, gu_spec,
                  pl.BlockSpec((None, 2, None, d, FF_CHUNK), lambda i, j: (layer, 0, j, 0, 0)),
                  pl.BlockSpec((None, None, FF_CHUNK, d), lambda i, j: (layer, j, 0, 0))],
        out_specs=(row, pl.BlockSpec((None, tm, FF_CHUNK), lambda i, j: (j, i, 0)), gu_spec),
        sem=("parallel", "arbitrary"))(dh, gu, w_up, w_down)


def _glu(t):
    t = t.astype(F32)
    return t[:, :D_MODEL] * jax.nn.sigmoid(t[:, D_MODEL:])


def _conv_tile(s):
    return 256 if s % 256 == 0 else s


CONV_ROWS = 32
CONV_LANES = 512
SUBLANES = 8


def _shifted_copies(sh_ref, rows):
    for b in range(1, SUBLANES):
        sh_ref[b, 0:rows - SUBLANES, :] = sh_ref[0, b:b + rows - SUBLANES, :]


def conv_fwd(t, cp, name):
    s = t.shape[0]
    d = D_MODEL
    ts = _conv_tile(s)
    per = ts // HALO
    rows = HALO + ts
    lead = HALO - (CONV_WIDTH - 1)
    rc = CONV_ROWS

    def body(t_ref, tp_ref, cp_ref, z_ref, y_ref, sh_ref):
        i = pl.program_id(0)
        sh_ref[0, 0:HALO, :] = jnp.where(i > 0, _glu(tp_ref[...]), 0.0)
        sh_ref[0, HALO:rows, :] = _glu(t_ref[...])
        _shifted_copies(sh_ref, rows)

        def chunk(c, carry):
            r0 = pl.multiple_of(c * rc, rc)
            for lc in range(d // CONV_LANES):
                ln = slice(lc * CONV_LANES, (lc + 1) * CONV_LANES)
                acc = jnp.zeros((rc, CONV_LANES), F32) + cp_ref[ROW_BDW:ROW_BDW + 1, ln]
                for k in range(CONV_WIDTH):
                    a8, b = divmod(lead + k, SUBLANES)
                    acc = acc + cp_ref[k:k + 1, ln] * sh_ref[b, pl.ds(r0 + SUBLANES * a8, rc), ln]
                y_ref[pl.ds(r0, rc), ln] = acc
            y = y_ref[pl.ds(r0, rc), :]
            mu = jnp.mean(y, axis=-1, keepdims=True)
            yc = y - mu
            rstd = lax.rsqrt(jnp.mean(yc * yc, axis=-1, keepdims=True) + EPS)
            yn = yc * rstd * cp_ref[ROW_LNG:ROW_LNG + 1, :] + cp_ref[ROW_LNB:ROW_LNB + 1, :]
            z_ref[pl.ds(r0, rc), :] = (yn * jax.nn.sigmoid(yn)).astype(BF16)
            return carry

        lax.fori_loop(0, ts // rc, chunk, 0)

    row = pl.BlockSpec((ts, d), lambda i: (i, 0))
    return _pcall(
        body, name, (_sds((s, d), BF16), _sds((s, d), F32)), grid=(s // ts,),
        in_specs=[pl.BlockSpec((ts, 2 * d), lambda i: (i, 0)),
                  pl.BlockSpec((HALO, 2 * d), lambda i: (jnp.maximum(i * per - 1, 0), 0)),
                  pl.BlockSpec((PACK_ROWS, d), lambda i: (0, 0))],
        out_specs=(row, row),
        scratch=[pltpu.VMEM((SUBLANES, rows, d), F32)], sem=("parallel",))(t, t, cp)


def conv_bwd(t, y, dz, cp, name):
    s = t.shape[0]
    d = D_MODEL
    ts = _conv_tile(s)
    per = ts // HALO
    nt = s // ts
    te = ts + HALO
    rc = CONV_ROWS

    def body(t_ref, y_ref, yn_ref, dz_ref, dzn_ref, cp_ref, dt_ref, st_ref, shd_ref, dw_ref):
        i = pl.program_id(0)
        last = i == nt - 1

        @pl.when(i == 0)
        def _():
            st_ref[...] = jnp.zeros_like(st_ref)
            dw_ref[...] = jnp.zeros_like(dw_ref)

        gain = cp_ref[ROW_LNG:ROW_LNG + 1, :]

        def ln_bwd(yv, dzv):
            mu = jnp.mean(yv, axis=-1, keepdims=True)
            yc = yv - mu
            rstd = lax.rsqrt(jnp.mean(yc * yc, axis=-1, keepdims=True) + EPS)
            yh = yc * rstd
            yn = yh * gain + cp_ref[ROW_LNB:ROW_LNB + 1, :]
            sig = jax.nn.sigmoid(yn)
            dyn = dzv * (sig * (1.0 + yn * (1.0 - sig)))
            dyh = dyn * gain
            dy = rstd * (dyh - jnp.mean(dyh, axis=-1, keepdims=True)
                         - yh * jnp.mean(dyh * yh, axis=-1, keepdims=True))
            return dy, dyn, yh

        def norm_chunk(c, carry):
            r0 = pl.multiple_of(c * rc, rc)
            dy, dyn, yh = ln_bwd(y_ref[pl.ds(r0, rc), :], dz_ref[pl.ds(r0, rc), :])
            shd_ref[0, pl.ds(r0, rc), :] = dy
            st_ref[ROW_BDW:ROW_BDW + 1, :] += jnp.sum(dy, axis=0, keepdims=True)
            st_ref[ROW_LNG:ROW_LNG + 1, :] += jnp.sum(dyn * yh, axis=0, keepdims=True)
            st_ref[ROW_LNB:ROW_LNB + 1, :] += jnp.sum(dyn, axis=0, keepdims=True)
            return carry

        lax.fori_loop(0, ts // rc, norm_chunk, 0)
        dy_halo, _, _ = ln_bwd(yn_ref[...], jnp.where(last, 0.0, dzn_ref[...]))
        shd_ref[0, ts:te, :] = dy_halo
        _shifted_copies(shd_ref, te)

        def tap_chunk(c, carry):
            r0 = pl.multiple_of(c * rc, rc)
            for lc in range(d // CONV_LANES):
                ln = slice(lc * CONV_LANES, (lc + 1) * CONV_LANES)
                ln2 = slice(d + lc * CONV_LANES, d + (lc + 1) * CONV_LANES)
                t1 = t_ref[pl.ds(r0, rc), ln].astype(F32)
                sg = jax.nn.sigmoid(t_ref[pl.ds(r0, rc), ln2].astype(F32))
                a = t1 * sg
                da = jnp.zeros((rc, CONV_LANES), F32)
                for k in range(CONV_WIDTH):
                    a8, b = divmod(CONV_WIDTH - 1 - k, SUBLANES)
                    e = shd_ref[b, pl.ds(r0 + SUBLANES * a8, rc), ln]
                    da = da + cp_ref[k:k + 1, ln] * e
                    dw_ref[k, :, ln] += jnp.sum((a * e).reshape(rc // SUBLANES, SUBLANES, CONV_LANES), axis=0)
                dt_ref[pl.ds(r0, rc), ln] = (da * sg).astype(BF16)
                dt_ref[pl.ds(r0, rc), ln2] = (da * t1 * sg * (1.0 - sg)).astype(BF16)
            return carry

        lax.fori_loop(0, ts // rc, tap_chunk, 0)

        @pl.when(last)
        def _():
            for k in range(CONV_WIDTH):
                st_ref[k:k + 1, :] = jnp.sum(dw_ref[k], axis=0, keepdims=True)

    last_halo = s // HALO - 1
    row = pl.BlockSpec((ts, d), lambda i: (i, 0))
    halo = pl.BlockSpec((HALO, d), lambda i: (jnp.minimum((i + 1) * per, last_halo), 0))
    return _pcall(
        body, name, (_sds((s, 2 * d), BF16), _sds((PACK_ROWS, d), F32)), grid=(nt,),
        in_specs=[pl.BlockSpec((ts, 2 * d), lambda i: (i, 0)), row, halo, row, halo,
                  pl.BlockSpec((PACK_ROWS, d), lambda i: (0, 0))],
        out_specs=(pl.BlockSpec((ts, 2 * d), lambda i: (i, 0)), pl.BlockSpec((PACK_ROWS, d), lambda i: (0, 0))),
        scratch=[pltpu.VMEM((SUBLANES, te, d), F32), pltpu.VMEM((CONV_WIDTH, SUBLANES, d), F32)],
        sem=("arbitrary",))(t, y, y, dz, dz, cp)


def _bucket_table():
    qi = np.arange(BLOCK, dtype=np.int64)[:, None]
    kj = np.arange(2 * BLOCK, dtype=np.int64)[None, :]
    dist = qi + BLOCK - kj
    max_exact = N_BUCKETS // 2
    dd = np.maximum(dist, 0)
    ratio = (np.maximum(dd, 1).astype(np.float32) / np.float32(max_exact)).astype(np.float32)
    log_ratio = (np.log(ratio).astype(np.float32) / np.float32(math.log(MAX_DISTANCE / max_exact))).astype(np.float32)
    large = max_exact + (log_ratio * np.float32(N_BUCKETS - max_exact)).astype(np.int32)
    large = np.minimum(large, N_BUCKETS - 1)
    bucket = np.where(dd < max_exact, dd, large)
    return np.where((dist >= 0) & (dist < BLOCK), bucket, -1).astype(np.int32)


def bias_table(rel_bias, bucket, name):
    def body(rb_ref, bk_ref, o_ref):
        bk = bk_ref[...]
        for h in range(N_HEADS):
            acc = jnp.full((BLOCK, 2 * BLOCK), NEG_INF, F32)
            for b in range(N_BUCKETS):
                acc = jnp.where(bk == b, rb_ref[b, h], acc)
            o_ref[h] = acc

    return _pcall(body, name, _sds((N_HEADS, BLOCK, 2 * BLOCK), F32),
                  in_specs=[pl.BlockSpec(memory_space=pltpu.SMEM), pl.BlockSpec(memory_space=pltpu.VMEM)],
                  out_specs=pl.BlockSpec(memory_space=pltpu.VMEM))(rel_bias, bucket)


def bias_grad(dba, dbb, bucket, name):
    def body(a_ref, b_ref, bk_ref, o_ref):
        bk = bk_ref[...]
        for h in range(N_HEADS):
            db = a_ref[h] + b_ref[h]
            for b in range(N_BUCKETS):
                o_ref[b, h] = jnp.sum(jnp.where(bk == b, db, 0.0))

    vm = pl.BlockSpec(memory_space=pltpu.VMEM)
    return _pcall(body, name, _sds((N_BUCKETS, N_HEADS), F32), in_specs=[vm, vm, vm],
                  out_specs=pl.BlockSpec(memory_space=pltpu.SMEM))(dba, dbb, bucket)


def _band_specs():
    cur = pl.BlockSpec((BLOCK, 2 * KV_DIM), lambda n: (n, 0))
    prev = pl.BlockSpec((BLOCK, 2 * KV_DIM), lambda n: (jnp.maximum(n - 1, 0), 0))
    return cur, prev


def _scores(q_h, k_h, bias_h, first_row, sink):
    sc = _dot(q_h, k_h, NT) * (HEAD_DIM ** -0.5) + bias_h + first_row
    m = jnp.maximum(jnp.max(sc, axis=-1, keepdims=True), sink)
    p = jnp.exp(sc - m)
    e_sink = jnp.exp(sink - m)
    den = jnp.sum(p, axis=-1, keepdims=True) + e_sink
    return p, e_sink, den


def _first_block_row(n):
    col = lax.broadcasted_iota(jnp.int32, (1, 2 * BLOCK), 1)
    return jnp.where((col < BLOCK) & (n == 0), NEG_INF, 0.0)


def _head_lanes(hk, g):
    h = hk * GROUP + g
    return slice(h * HEAD_DIM, (h + 1) * HEAD_DIM)


def _group_rows(x_ref, hk):
    return jnp.concatenate([x_ref[:, _head_lanes(hk, g)] for g in range(GROUP)], axis=0)


def _group_bias(bias_ref, hk):
    return bias_ref[hk * GROUP:(hk + 1) * GROUP].reshape(GROUP * BLOCK, 2 * BLOCK)


def _group_sinks(sink_ref, hk):
    head = lax.broadcasted_iota(jnp.int32, (GROUP * BLOCK, 1), 0) // BLOCK
    col = jnp.zeros((GROUP * BLOCK, 1), F32) + sink_ref[0, hk * GROUP]
    for g in range(1, GROUP):
        col = jnp.where(head == g, sink_ref[0, hk * GROUP + g], col)
    return col


def attn_fwd(q, kv, bias, sinks, name):
    s = q.shape[0]
    nb = s // BLOCK

    def body(sink_ref, q_ref, kvc_ref, kvp_ref, bias_ref, o_ref, band_ref):
        n = pl.program_id(0)
        band_ref[0:BLOCK, :] = kvp_ref[...]
        band_ref[BLOCK:2 * BLOCK, :] = kvc_ref[...]
        first_row = _first_block_row(n)
        for hk in range(N_KV_HEADS):
            k_h = band_ref[:, hk * HEAD_DIM:(hk + 1) * HEAD_DIM]
            v_h = band_ref[:, KV_DIM + hk * HEAD_DIM:KV_DIM + (hk + 1) * HEAD_DIM]
            p, _, den = _scores(_group_rows(q_ref, hk), k_h, _group_bias(bias_ref, hk), first_row,
                                _group_sinks(sink_ref, hk))
            o = _dot((p * (1.0 / den)).astype(BF16), v_h, NN).astype(BF16)
            for g in range(GROUP):
                o_ref[:, _head_lanes(hk, g)] = o[g * BLOCK:(g + 1) * BLOCK]

    cur, prev = _band_specs()
    qs = pl.BlockSpec((BLOCK, D_MODEL), lambda n: (n, 0))
    return _pcall(
        body, name, _sds((s, D_MODEL), BF16), grid=(nb,),
        in_specs=[pl.BlockSpec(memory_space=pltpu.SMEM), qs, cur, prev,
                  pl.BlockSpec((N_HEADS, BLOCK, 2 * BLOCK), lambda n: (0, 0, 0))],
        out_specs=qs, scratch=[pltpu.VMEM((2 * BLOCK, 2 * KV_DIM), BF16)],
        sem=("parallel",))(sinks, q, kv, kv, bias)


def attn_bwd(q, kv, do, bias, sinks, name):
    s = q.shape[0]
    nb = s // BLOCK
    scale = HEAD_DIM ** -0.5

    def body(sink_ref, q_ref, do_ref, kvc_ref, kvp_ref, bias_ref, dq_ref, dkv_ref, db_ref, dsink_ref,
             band_ref, dsacc_ref):
        n = pl.program_id(0)
        band_ref[0:BLOCK, :] = kvp_ref[...]
        band_ref[BLOCK:2 * BLOCK, :] = kvc_ref[...]
        first_row = _first_block_row(n)
        lane = lax.broadcasted_iota(jnp.int32, (BLOCK, BLOCK), 1)

        @pl.when(n == 0)
        def _():
            db_ref[...] = jnp.zeros_like(db_ref)
            dsacc_ref[...] = jnp.zeros_like(dsacc_ref)

        for hk in range(N_KV_HEADS):
            k_h = band_ref[:, hk * HEAD_DIM:(hk + 1) * HEAD_DIM]
            v_h = band_ref[:, KV_DIM + hk * HEAD_DIM:KV_DIM + (hk + 1) * HEAD_DIM]
            q_g = _group_rows(q_ref, hk)
            do_g = _group_rows(do_ref, hk)
            p, e_sink, den = _scores(q_g, k_h, _group_bias(bias_ref, hk), first_row, _group_sinks(sink_ref, hk))
            inv = 1.0 / den
            p = p * inv
            dp = _dot(do_g, v_h, NT)
            delta = jnp.sum(p * dp, axis=-1, keepdims=True)
            ds = p * (dp - delta)
            db_ref[hk * GROUP:(hk + 1) * GROUP] += ds.reshape(GROUP, BLOCK, 2 * BLOCK)
            d_sink = -(e_sink * inv) * delta
            for g in range(GROUP):
                dsacc_ref[...] += jnp.where(lane == hk * GROUP + g, d_sink[g * BLOCK:(g + 1) * BLOCK], 0.0)
            dsb = ds.astype(BF16)
            dq = (_dot(dsb, k_h, NN) * scale).astype(BF16)
            for g in range(GROUP):
                dq_ref[:, _head_lanes(hk, g)] = dq[g * BLOCK:(g + 1) * BLOCK]
            dkv_ref[:, hk * HEAD_DIM:(hk + 1) * HEAD_DIM] = _dot(dsb, q_g, TN) * scale
            dkv_ref[:, KV_DIM + hk * HEAD_DIM:KV_DIM + (hk + 1) * HEAD_DIM] = _dot(p.astype(BF16), do_g, TN)

        @pl.when(n == nb - 1)
        def _():
            dsink_ref[...] = jnp.sum(dsacc_ref[...], axis=0, keepdims=True)

    cur, prev = _band_specs()
    qs = pl.BlockSpec((BLOCK, D_MODEL), lambda n: (n, 0))
    full_b = pl.BlockSpec((N_HEADS, BLOCK, 2 * BLOCK), lambda n: (0, 0, 0))
    return _pcall(
        body, name,
        (_sds((s, D_MODEL), BF16), _sds((nb, 2 * BLOCK, 2 * KV_DIM), F32),
         _sds((N_HEADS, BLOCK, 2 * BLOCK), F32), _sds((1, BLOCK), F32)),
        grid=(nb,),
        in_specs=[pl.BlockSpec(memory_space=pltpu.SMEM), qs, qs, cur, prev, full_b],
        out_specs=(qs, pl.BlockSpec((None, 2 * BLOCK, 2 * KV_DIM), lambda n: (n, 0, 0)), full_b,
                   pl.BlockSpec((1, BLOCK), lambda n: (0, 0))),
        scratch=[pltpu.VMEM((2 * BLOCK, 2 * KV_DIM), BF16), pltpu.VMEM((BLOCK, BLOCK), F32)],
        sem=("arbitrary",))(sinks, q, do, kv, kv, bias)


def dkv_combine(pa, pb, name):
    nb = pa.shape[0]
    pa2 = pa.reshape(2 * nb, BLOCK, 2 * KV_DIM)
    pb2 = pb.reshape(2 * nb, BLOCK, 2 * KV_DIM)

    def body(ac_ref, an_ref, bc_ref, bn_ref, o_ref):
        n = pl.program_id(0)
        nxt = jnp.where(n == nb - 1, 0.0, an_ref[...] + bn_ref[...])
        o_ref[...] = (ac_ref[...] + bc_ref[...] + nxt).astype(BF16)

    cur = pl.BlockSpec((None, BLOCK, 2 * KV_DIM), lambda n: (2 * n + 1, 0, 0))
    nxt = pl.BlockSpec((None, BLOCK, 2 * KV_DIM), lambda n: (jnp.minimum(2 * n + 2, 2 * nb - 2), 0, 0))
    return _pcall(body, name, _sds((nb * BLOCK, 2 * KV_DIM), BF16), grid=(nb,),
                  in_specs=[cur, nxt, cur, nxt], out_specs=pl.BlockSpec((BLOCK, 2 * KV_DIM), lambda n: (n, 0)),
                  sem=("parallel",))(pa2, pa2, pb2, pb2)


def loss_head(h, g, target, name):
    s, d = h.shape
    tm = _row_tile(s, 512)

    def body(h_ref, g_ref, t_ref, dh_ref, dg_ref, loss_ref):
        i = pl.program_id(0)
        x = h_ref[...]
        r = lax.rsqrt(jnp.mean(x * x, axis=-1, keepdims=True) + EPS)
        xh = x * r
        gv = g_ref[...]
        err = xh * gv - t_ref[...]
        part_loss = jnp.zeros((1, BLOCK), F32) + 0.5 * jnp.sum(jnp.mean(err * err, axis=-1, keepdims=True))
        dy = err * (1.0 / d)
        dxh = dy * gv
        dh_ref[...] = r * (dxh - xh * jnp.mean(dxh * xh, axis=-1, keepdims=True))
        part_g = jnp.sum(dy * xh, axis=0, keepdims=True)

        @pl.when(i == 0)
        def _():
            dg_ref[...] = part_g
            loss_ref[...] = part_loss

        @pl.when(i > 0)
        def _():
            dg_ref[...] += part_g
            loss_ref[...] += part_loss

    row = pl.BlockSpec((tm, d), lambda i: (i, 0))
    vec = pl.BlockSpec((1, d), lambda i: (0, 0))
    return _pcall(body, name, (_sds((s, d), F32), _sds((1, d), F32), _sds((1, BLOCK), F32)), grid=(s // tm,),
                  in_specs=[row, vec, row], out_specs=(row, vec, pl.BlockSpec((1, BLOCK), lambda i: (0, 0))),
                  sem=("arbitrary",))(h, g, target)


def adamw(w, m, v, parts, name):
    nl, r, c = w.shape
    tr = max(t for t in range(1, min(r, 512) + 1) if r % t == 0 and (t % 16 == 0 or t == r))
    c1 = 1.0 / (1.0 - ADAM_B1 ** ADAM_STEP)
    c2 = 1.0 / (1.0 - ADAM_B2 ** ADAM_STEP)

    def body(w_ref, m_ref, v_ref, p_ref, g_ref, d_ref, nm_ref, nv_ref):
        g = p_ref[0].astype(F32)
        for dev in range(1, N_DEV):
            g = g + p_ref[dev].astype(F32)
        nm = ADAM_B1 * m_ref[...] + (1.0 - ADAM_B1) * g
        nv = ADAM_B2 * v_ref[...] + (1.0 - ADAM_B2) * (g * g)
        g_ref[...] = g
        nm_ref[...] = nm
        nv_ref[...] = nv
        d_ref[...] = -ADAM_LR * ((nm * c1) / (jnp.sqrt(nv * c2) + ADAM_EPS) + ADAM_WD * w_ref[...])

    blk = pl.BlockSpec((None, tr, c), lambda l, i: (l, i, 0))
    out = _sds((nl, r, c), F32)
    return _pcall(body, name, (out, out, out, out), grid=(nl, r // tr),
                  in_specs=[blk, blk, blk, pl.BlockSpec((N_DEV, None, tr, c), lambda l, i: (0, l, i, 0))],
                  out_specs=(blk, blk, blk, blk), sem=("parallel", "parallel"))(w, m, v, parts)


def _place():
    x, y, c = lax.axis_index("x"), lax.axis_index("y"), lax.axis_index("c")
    return x, y, c


def _lin(px, py, pc):
    return 4 * px + 2 * py + pc


HBM = pl.BlockSpec(memory_space=pltpu.HBM)
SEM = pl.BlockSpec(memory_space=pltpu.SEMAPHORE)
EFFECT = pltpu.SideEffectType.DATAFLOW_SIDE_EFFECTING
N_PEERS = N_DEV - 1


def _peers_of(x, y, c):
    return [(x, y, 1 - c), (1 - x, y, c), (x, 1 - y, c), (1 - x, 1 - y, c),
            (1 - x, y, 1 - c), (x, 1 - y, 1 - c), (1 - x, 1 - y, 1 - c)]


def _in_hbm(a):
    return pltpu.with_memory_space_constraint(a, pltpu.HBM)


def send_start(name, bufs, copies, n_groups):
    nb = len(bufs)
    per_group = [[i for i, cp in enumerate(copies) if cp[0] == g] for g in range(n_groups)]

    def body(*refs):
        buf = refs[:nb]
        sems = refs[nb:nb + 2 * n_groups]
        token = refs[2 * nb + 2 * n_groups]
        x, y, c = _place()
        me = _lin(x, y, c)
        for g in range(n_groups):
            for slot, i in enumerate(per_group[g]):
                _, s, src_slab, d, land_slab = copies[i]
                for k, peer in enumerate(_peers_of(x, y, c)):
                    pltpu.make_async_remote_copy(
                        src_ref=src_slab(buf[s], _lin(*peer), me), dst_ref=land_slab(buf[d], me),
                        send_sem=sems[2 * g].at[slot * N_PEERS + k], recv_sem=sems[2 * g + 1].at[slot * N_PEERS + k],
                        device_id=peer, device_id_type=MESH).start()
        token[...] = jnp.zeros_like(token)

    sem_shapes = []
    for g in range(n_groups):
        sem_shapes += [pltpu.SemaphoreType.DMA((len(per_group[g]) * N_PEERS,))] * 2
    out = pl.pallas_call(
        body, name=name,
        out_shape=tuple(sem_shapes) + tuple(pltpu.HBM(b.shape, b.dtype) for b in bufs) + (_sds((8, 128), F32),),
        in_specs=[HBM] * nb,
        out_specs=tuple([SEM] * len(sem_shapes)) + tuple([HBM] * nb) + (pl.BlockSpec(memory_space=pltpu.VMEM),),
        input_output_aliases={i: len(sem_shapes) + i for i in range(nb)},
        compiler_params=pltpu.CompilerParams(has_side_effects=EFFECT))(*[_in_hbm(b) for b in bufs])
    sems = [(out[2 * g], out[2 * g + 1]) for g in range(n_groups)]
    return sems, list(out[2 * n_groups:2 * n_groups + nb]), out[2 * n_groups + nb]


def send_wait(name, bufs, copies, sems, after):
    nb = len(bufs)

    def body(*refs):
        buf = refs[:nb]
        send_sems, recv_sems = refs[nb], refs[nb + 1]
        x, y, c = _place()
        me = _lin(x, y, c)
        for slot, (s, src_slab, d, land_slab) in enumerate(copies):
            for k, peer in enumerate(_peers_of(x, y, c)):
                j = _lin(*peer)
                cp = pltpu.make_async_remote_copy(
                    src_ref=src_slab(buf[s], j, me), dst_ref=land_slab(buf[d], j),
                    send_sem=send_sems.at[slot * N_PEERS + k], recv_sem=recv_sems.at[slot * N_PEERS + k],
                    device_id=peer, device_id_type=MESH)
                cp.wait_send()
                cp.wait_recv()

    out = pl.pallas_call(
        body, name=name, out_shape=tuple(pltpu.HBM(b.shape, b.dtype) for b in bufs),
        in_specs=[HBM] * nb + [SEM, SEM, ANY], out_specs=tuple([HBM] * nb),
        input_output_aliases={i: i for i in range(nb)},
        compiler_params=pltpu.CompilerParams(has_side_effects=EFFECT))(*bufs, sems[0], sems[1], after)
    return list(out)


def local_step(x, target, weights, rep, emit):
    s = x.shape[0]
    bucket = jnp.asarray(_bucket_table())
    bias = bias_table(rep["rel_bias"], bucket, "bias_table")
    h = x
    saved = []
    kv = None
    h_kv = u_kv = None
    small = None
    for l in range(4):
        g_mix = rep["norm_mix"][l:l + 1]
        g_ffn = rep["norm_ffn"][l:l + 1]
        rec = {"h_in": h}
        u = rms_fwd(h, g_mix, f"rms_mix_fwd{l}")
        rec["u"] = u
        if l < 2:
            w = weights(f"conv{l}", u)
            if l == 0:
                small = w
            cp = small["cp"][l]
            t = mm_nn(f"pw1_fwd{l}", u, w["pw1"], (None, None, D_MODEL, 256), lambda i, j: (0, j, 0, 0),
                      2 * D_MODEL, 256, BF16,
                      bias=(small["b_pw1"], pl.BlockSpec((None, 1, 256), lambda i, j, l=l: (l, 0, j))))
            z, y = conv_fwd(t, cp, f"conv_fwd{l}")
            h = mm_nn(f"pw2_fwd{l}", z, w["pw2"], (None, D_MODEL, 512), lambda i, j: (0, 0, j), D_MODEL, 512,
                      F32, bias=(small["b_pw2"], pl.BlockSpec((None, 1, 512), lambda i, j, l=l: (l, 0, j))), res=h)
            rec.update(t=t, z=z, y=y, cp=cp)
        else:
            a = l - 2
            w = weights(f"attn{a}", u)
            if a == 0:
                h_kv = h
                u_kv = rms_fwd(h, rep["norm_kv"], "rms_kv_fwd")
                w_kv = w["wkv"]
                kv = mm_nn("kv_fwd", u_kv, w_kv, (D_MODEL, 2 * KV_DIM), lambda i, j: (0, 0), 2 * KV_DIM,
                           2 * KV_DIM, BF16)
            q = mm_nn(f"q_fwd{a}", u, w["wq"], (None, D_MODEL, 512), lambda i, j: (0, 0, j), D_MODEL, 512, BF16)
            o = attn_fwd(q, kv, bias, rep["sinks"][a:a + 1], f"attn_fwd{a}")
            h = mm_nn(f"o_fwd{a}", o, w["wo"], (None, D_MODEL, 512), lambda i, j: (0, 0, j), D_MODEL, 512, F32,
                      res=h)
            rec.update(q=q, o=o)
        rec["w"] = w
        rec["h_mid"] = h
        uf = rms_fwd(h, g_ffn, f"rms_ffn_fwd{l}")
        wf = weights(f"ffn{l}", uf)
        h, gu = ffn_fwd(uf, h, wf["up"], wf["down"], 0, f"ffn_fwd{l}")
        rec.update(uf=uf, gu=gu, wf=wf)
        saved.append(rec)

    dh, d_nfin, loss = loss_head(h, rep["norm_final"], target, "loss_head")

    d_mix, d_ffn = [None] * 4, [None] * 4
    cp_grads = [None, None]
    dkv_parts, dbias_parts, dsinks = [], [], [None, None]
    d_nkv = None
    full_rows = lambda tk: (tk, D_MODEL)
    for l in reversed(range(4)):
        rec = saved[l]
        w, wf = rec["w"], rec["wf"]
        grads = {}
        g_mix = rep["norm_mix"][l:l + 1]
        g_ffn = rep["norm_ffn"][l:l + 1]
        du, act, dgu = ffn_bwd(dh, rec["gu"], wf["up"], wf["down"], 0, f"ffn_bwd{l}")
        g_down = mm_tn(
            f"down_grad{l}", act, dh, groups=4, a_block=lambda tk: (None, tk, FF_CHUNK), a_index=lambda j, k: (j, k, 0),
            b_block=full_rows, b_index=lambda j, k: (k, 0), o_block=(None, FF_CHUNK, D_MODEL),
            o_index=lambda j, k: (j, 0, 0), o_shape=(4, FF_CHUNK, D_MODEL), acc_shape=(FF_CHUNK, D_MODEL))
        g_up = mm_tn(
            f"up_grad{l}", rec["uf"], dgu.reshape(8, s, FF_CHUNK), groups=8, a_block=full_rows,
            a_index=lambda j, k: (k, 0), b_block=lambda tk: (None, tk, FF_CHUNK), b_index=lambda j, k: (j, k, 0),
            o_block=(None, D_MODEL, FF_CHUNK), o_index=lambda j, k: (j, 0, 0), o_shape=(8, D_MODEL, FF_CHUNK),
            acc_shape=(D_MODEL, FF_CHUNK))
        tok = emit(f"ffn{l}", {"up": g_up, "down": g_down.reshape(D_FF, D_MODEL)})
        dh, d_ffn[l] = rms_bwd(rec["h_mid"], g_ffn, du, dh, f"rms_ffn_bwd{l}", token=tok)
        tok = None
        if l < 2:
            dz = mm_nt(f"pw2_bwd{l}", dh, w["pw2"], (None, D_MODEL, D_MODEL), lambda i, k: (0, 0, 0), D_MODEL, F32)
            grads["pw2"], db2 = mm_tn(
                f"pw2_grad{l}", rec["z"], dh, groups=1, a_block=full_rows, a_index=lambda j, k: (k, 0),
                b_block=full_rows, b_index=lambda j, k: (k, 0), o_block=(D_MODEL, D_MODEL), o_index=lambda j, k: (0, 0),
                o_shape=(D_MODEL, D_MODEL), acc_shape=(D_MODEL, D_MODEL),
                colsum=((1, D_MODEL), pl.BlockSpec((1, D_MODEL), lambda j, k: (0, 0))))
            dt, stats = conv_bwd(rec["t"], rec["y"], dz, rec["cp"], f"conv_bwd{l}")
            du = mm_nt(f"pw1_bwd{l}", dt, w["pw1"], (None, None, D_MODEL, 256), lambda i, k: (0, k, 0, 0),
                       D_MODEL, F32, nk=8, tk=256)
            grads["pw1"], db1 = mm_tn(
                f"pw1_grad{l}", rec["u"], dt, groups=8, a_block=full_rows, a_index=lambda j, k: (k, 0),
                b_block=lambda tk: (tk, 256), b_index=lambda j, k: (k, j), o_block=(None, D_MODEL, 256),
                o_index=lambda j, k: (j, 0, 0), o_shape=(8, D_MODEL, 256), acc_shape=(D_MODEL, 256),
                colsum=((8, 1, 256), pl.BlockSpec((None, 1, 256), lambda j, k: (j, 0, 0))))
            cp_grads[l] = (stats, db2, db1)
            tok = emit(f"conv{l}", grads)
        else:
            a = l - 2
            do = mm_nt(f"o_bwd{a}", dh, w["wo"], (None, D_MODEL, D_MODEL), lambda i, k: (0, 0, 0), D_MODEL, BF16)
            grads["wo"] = mm_tn(
                f"wo_grad{a}", rec["o"], dh, groups=1, a_block=full_rows, a_index=lambda j, k: (k, 0),
                b_block=full_rows, b_index=lambda j, k: (k, 0), o_block=(D_MODEL, D_MODEL), o_index=lambda j, k: (0, 0),
                o_shape=(D_MODEL, D_MODEL), acc_shape=(D_MODEL, D_MODEL))
            dq, dkv_p, dbias_p, dsinks[a] = attn_bwd(rec["q"], kv, do, bias, rep["sinks"][a:a + 1], f"attn_bwd{a}")
            dkv_parts.append(dkv_p)
            dbias_parts.append(dbias_p)
            du = mm_nt(f"q_bwd{a}", dq, w["wq"], (None, D_MODEL, D_MODEL), lambda i, k: (0, 0, 0), D_MODEL, F32)
            grads["wq"] = mm_tn(
                f"wq_grad{a}", rec["u"], dq, groups=1, a_block=full_rows, a_index=lambda j, k: (k, 0),
                b_block=full_rows, b_index=lambda j, k: (k, 0), o_block=(D_MODEL, D_MODEL), o_index=lambda j, k: (0, 0),
                o_shape=(D_MODEL, D_MODEL), acc_shape=(D_MODEL, D_MODEL))
            if a == 1:
                tok = emit("attn1", grads)
        dh, d_mix[l] = rms_bwd(rec["h_in"], g_mix, du, dh, f"rms_mix_bwd{l}", token=tok)
        if l == 2:
            dkv = dkv_combine(dkv_parts[0], dkv_parts[1], "dkv_combine")
            du_kv = mm_nt("kv_bwd", dkv, w_kv, (D_MODEL, 2 * KV_DIM), lambda i, k: (0, 0), D_MODEL, F32)
            grads["wkv"] = mm_tn(
                "wkv_grad", u_kv, dkv, groups=1, a_block=full_rows, a_index=lambda j, k: (k, 0),
                b_block=lambda tk: (tk, 2 * KV_DIM), b_index=lambda j, k: (k, 0), o_block=(D_MODEL, 2 * KV_DIM),
                o_index=lambda j, k: (0, 0), o_shape=(D_MODEL, 2 * KV_DIM), acc_shape=(D_MODEL, 2 * KV_DIM))
            tok = emit("attn0", grads)
            dh, d_nkv = rms_bwd(h_kv, rep["norm_kv"], du_kv, dh, "rms_kv_bwd", token=tok)

    d_relb = bias_grad(dbias_parts[0], dbias_parts[1], bucket, "bias_grad")
    d_sinks = jnp.concatenate([dsinks[0][0, :N_HEADS], dsinks[1][0, :N_HEADS]])
    tail = jnp.zeros((D_MODEL,), F32)
    rep_grad = jnp.concatenate([
        jnp.concatenate(d_mix, axis=0), jnp.concatenate(d_ffn, axis=0), d_nkv, d_nfin,
        tail.at[:2 * N_HEADS].set(d_sinks)[None], tail.at[:N_BUCKETS * N_HEADS].set(d_relb.reshape(-1))[None],
        tail.at[0].set(loss[0, 0])[None], jnp.zeros((REP_ROWS - ROW_LOSS - 1, D_MODEL), F32)], axis=0)
    return dh, cp_grads, rep_grad


def _pack_conv(w_dw, b_dw, ln_g, ln_b, b_pw2, b_pw1):
    rows = [w_dw, b_dw[:, None], ln_g[:, None], ln_b[:, None], b_pw2[:, None], b_pw1.reshape(2, 2, 128),
            jnp.zeros((2, PACK_ROWS - ROW_BPW1 - 2, 128), F32)]
    return jnp.concatenate(rows, axis=1)


def _unpack_conv(p):
    return (p[:, :CONV_WIDTH], p[:, ROW_BDW], p[:, ROW_LNG], p[:, ROW_LNB], p[:, ROW_BPW2],
            p[:, ROW_BPW1:ROW_BPW1 + 2].reshape(2, 256))


def _pack_rep(norm_mix, norm_ffn, norm_kv, norm_final, sinks, rel_bias):
    tail = jnp.zeros((D_MODEL,), F32)
    return jnp.concatenate([
        norm_mix, norm_ffn, norm_kv[None], norm_final[None], tail.at[:2 * N_HEADS].set(sinks.reshape(-1))[None],
        tail.at[:N_BUCKETS * N_HEADS].set(rel_bias.reshape(-1))[None],
        jnp.zeros((REP_ROWS - ROW_RELB - 1, D_MODEL), F32)], axis=0)


def _unpack_rep(p):
    return (p[0:4], p[4:8], p[ROW_NKV], p[ROW_NFIN], p[ROW_SINK, :2 * N_HEADS].reshape(2, N_HEADS),
            p[ROW_RELB, :N_BUCKETS * N_HEADS].reshape(N_BUCKETS, N_HEADS))


def kernel(x, norm_mix, norm_ffn, conv_w_pw1, conv_b_pw1, conv_w_dw, conv_b_dw, conv_ln_g, conv_ln_b, conv_w_pw2, conv_b_pw2, norm_kv, w_kv, w_q, w_o, sinks, rel_bias, ffn_w_up, ffn_w_down, norm_final, loss_target, m_norm_mix, m_norm_ffn, m_conv_w_pw1, m_conv_b_pw1, m_conv_w_dw, m_conv_b_dw, m_conv_ln_g, m_conv_ln_b, m_conv_w_pw2, m_conv_b_pw2, m_norm_kv, m_w_kv, m_w_q, m_w_o, m_sinks, m_rel_bias, m_ffn_w_up, m_ffn_w_down, m_norm_final, v_norm_mix, v_norm_ffn, v_conv_w_pw1, v_conv_b_pw1, v_conv_w_dw, v_conv_b_dw, v_conv_ln_g, v_conv_ln_b, v_conv_w_pw2, v_conv_b_pw2, v_norm_kv, v_w_kv, v_w_q, v_w_o, v_sinks, v_rel_bias, v_ffn_w_up, v_ffn_w_down, v_norm_final):
    s = x.shape[1]
    d = D_MODEL
    rsh = d // N_DEV
    dsh = D_FF // N_DEV

    me = _lin(*_place())
    lead_slab = lambda ref, j: ref.at[j]
    rows_of = lambda rows: (lambda ref, j: ref.at[pl.ds(j * rows, rows), :])

    conv_pack = _pack_conv(conv_w_dw, conv_b_dw, conv_ln_g, conv_ln_b, conv_b_pw2, conv_b_pw1)
    ag_order = ["conv0", "ffn0", "conv1", "ffn1", "attn0", "ffn2", "attn1", "ffn3"]
    ag_land, ag_copies, ag_members = [], [], {g: [] for g in ag_order}

    def gather(group, key, shard, land_shape, at, land_slab):
        i = len(ag_land)
        ag_land.append(lax.dynamic_update_slice(lax.empty(land_shape, shard.dtype), shard, at(me)))
        own = lambda ref, j, me_, slab=land_slab: slab(ref, me_)
        ag_copies.append((ag_order.index(group), i, own, i, land_slab))
        ag_members[group].append((key, i, own, land_slab))

    cols_at0 = lambda ref, j: ref.at[0, j]
    rows_at0 = lambda rows: (lambda ref, j: ref.at[0, pl.ds(j * rows, rows), :])
    col_at = lambda m: (0, m, 0, 0)
    row_at = lambda rows: (lambda m: (0, m * rows, 0))
    for l in range(2):
        gather(f"conv{l}", "pw1", conv_w_pw1[l].astype(BF16)[None, None], (1, N_DEV, d, 256), col_at, cols_at0)
        gather(f"conv{l}", "pw2", conv_w_pw2[l].astype(BF16)[None], (1, d, d), row_at(rsh), rows_at0(rsh))
    gather("conv0", "pack", conv_pack[None], (N_DEV, 2, PACK_ROWS, 128), lambda m: (m, 0, 0, 0), lead_slab)
    gather("attn0", "wkv", w_kv.astype(BF16), (d, 2 * KV_DIM), lambda m: (m * rsh, 0), rows_of(rsh))
    for a in range(2):
        gather(f"attn{a}", "wq", w_q[a].astype(BF16)[None], (1, d, d), row_at(rsh), rows_at0(rsh))
        gather(f"attn{a}", "wo", w_o[a].astype(BF16)[None], (1, d, d), row_at(rsh), rows_at0(rsh))
    for l in range(4):
        gather(f"ffn{l}", "up", ffn_w_up[l].astype(BF16)[None, None], (1, N_DEV, d, FF_CHUNK), col_at, cols_at0)
        gather(f"ffn{l}", "down", ffn_w_down[l].astype(BF16)[None], (1, D_FF, d), row_at(dsh), rows_at0(dsh))
    ag_sems, ag_land_thru, _ = send_start("ag_start", ag_land, ag_copies, len(ag_order))

    def weights(group, after):
        members = ag_members[group]
        lands = send_wait(f"ag_wait_{group}", [ag_land_thru[i] for _, i, _, _ in members],
                          [(n, own, n, slab) for n, (_, _, own, slab) in enumerate(members)],
                          ag_sems[ag_order.index(group)], after)
        w = {key: land for (key, _, _, _), land in zip(members, lands)}
        if "up" in w:
            w["up"] = w["up"].reshape(1, 2, 4, d, FF_CHUNK)
            w["down"] = w["down"].reshape(1, 4, FF_CHUNK, d)
        if "pack" in w:
            pack_g = w.pop("pack")
            w["cp"] = jnp.transpose(pack_g, (1, 2, 0, 3)).reshape(2, PACK_ROWS, d)
            w["b_pw1"] = pack_g[:, :, ROW_BPW1:ROW_BPW1 + 2, :].transpose(1, 0, 2, 3).reshape(2, 1, 2 * d)
            w["b_pw2"] = w["cp"][:, ROW_BPW2:ROW_BPW2 + 1, :]
        return w

    shard_shapes = {"pw1": (d, 256), "pw2": (rsh, d), "wkv": (rsh, 2 * KV_DIM), "wq": (rsh, d), "wo": (rsh, d),
                    "up": (d, FF_CHUNK), "down": (dsh, d), "cp": (2, PACK_ROWS, 128), "rep": (REP_ROWS, d)}
    n_layers = {"pw1": 2, "pw2": 2, "wkv": 1, "wq": 2, "wo": 2, "up": 4, "down": 4, "cp": 1, "rep": 1}
    by_lead = (lambda ref, j, me_: ref.at[j], lambda g: lax.dynamic_index_in_dim(g, me, 0, keepdims=False))
    by_rows = lambda rows: (lambda ref, j, me_: ref.at[pl.ds(j * rows, rows), :],
                            lambda g: lax.dynamic_slice_in_dim(g, me * rows, rows, 0))
    all_of = (lambda ref, j, me_: ref, lambda g: g)
    owned = {"pw1": by_lead, "pw2": by_rows(rsh), "wkv": by_rows(rsh), "wq": by_rows(rsh), "wo": by_rows(rsh),
             "up": by_lead, "down": by_rows(dsh), "cp": by_lead, "rep": all_of}
    parts = {}
    pending = {}

    def finish(chain, after):
        keys, bufs, copies, sems, name = pending.pop(chain)
        done = send_wait(f"rs_wait_{name}", bufs, copies, sems, after)
        parts.update(zip(keys, done[len(keys):]))

    def exchange(chain, name, layer, grads):
        keys = list(grads)
        if chain in pending:
            finish(chain, grads[keys[0]])
        lands = []
        for k in keys:
            land = parts.pop(k) if k in parts else lax.empty((N_DEV, n_layers[k]) + shard_shapes[k], grads[k].dtype)
            mine = owned[k][1](grads[k])[None, None]
            lands.append(lax.dynamic_update_slice(land, mine, (me, layer) + (0,) * len(shard_shapes[k])))
        land_at = lambda ref, i: ref.at[i, layer]
        copies = [(0, n, owned[k][0], len(keys) + n, land_at) for n, k in enumerate(keys)]
        sems, thru, token = send_start(f"rs_start_{name}", [grads[k] for k in keys] + lands, copies, 1)
        pending[chain] = (keys, thru, [c[1:] for c in copies], sems[0], name)
        return token

    def emit(group, grads):
        return exchange(group[:-1], group, int(group[-1]), grads)

    rep = {"norm_mix": norm_mix, "norm_ffn": norm_ffn, "norm_kv": norm_kv[None], "norm_final": norm_final[None],
           "sinks": sinks, "rel_bias": rel_bias}

    grad_x, cp_grads, rep_grad = local_step(x[0], loss_target[0], weights, rep, emit)

    cp_full = []
    for l in range(2):
        stats, db2, db1 = cp_grads[l]
        cp_full.append(jnp.concatenate([
            stats[:ROW_BPW2], db2, db1.reshape(N_DEV, 2, 128).transpose(1, 0, 2).reshape(2, d),
            jnp.zeros((PACK_ROWS - ROW_BPW1 - 2, d), F32)], axis=0))
    cp_send = jnp.stack(cp_full).reshape(2, PACK_ROWS, N_DEV, 128).transpose(2, 0, 1, 3)
    exchange("tail", "tail", 0, {"cp": cp_send, "rep": rep_grad})
    for chain in ("ffn", "attn", "conv", "tail"):
        finish(chain, grad_x)

    def update(key, w, m, v, name):
        p = parts[key]
        w3 = w.reshape(p.shape[1:])
        outs = adamw(w3, m.reshape(w3.shape), v.reshape(w3.shape), p, name)
        return [o.reshape(w.shape) for o in outs]

    res = {}
    res["conv_w_pw1"] = update("pw1", conv_w_pw1, m_conv_w_pw1, v_conv_w_pw1, "adam_pw1")
    res["conv_w_pw2"] = update("pw2", conv_w_pw2, m_conv_w_pw2, v_conv_w_pw2, "adam_pw2")
    res["w_kv"] = update("wkv", w_kv, m_w_kv, v_w_kv, "adam_wkv")
    res["w_q"] = update("wq", w_q, m_w_q, v_w_q, "adam_wq")
    res["w_o"] = update("wo", w_o, m_w_o, v_w_o, "adam_wo")
    res["ffn_w_up"] = update("up", ffn_w_up, m_ffn_w_up, v_ffn_w_up, "adam_up")
    res["ffn_w_down"] = update("down", ffn_w_down, m_ffn_w_down, v_ffn_w_down, "adam_down")
    m_pack = _pack_conv(m_conv_w_dw, m_conv_b_dw, m_conv_ln_g, m_conv_ln_b, m_conv_b_pw2, m_conv_b_pw1)
    v_pack = _pack_conv(v_conv_w_dw, v_conv_b_dw, v_conv_ln_g, v_conv_ln_b, v_conv_b_pw2, v_conv_b_pw1)
    cp_res = adamw(conv_pack, m_pack, v_pack, parts["cp"].reshape(N_DEV, 2, PACK_ROWS, 128), "adam_conv_pack")
    rep_w = _pack_rep(norm_mix, norm_ffn, norm_kv, norm_final, sinks, rel_bias)
    rep_m = _pack_rep(m_norm_mix, m_norm_ffn, m_norm_kv, m_norm_final, m_sinks, m_rel_bias)
    rep_v = _pack_rep(v_norm_mix, v_norm_ffn, v_norm_kv, v_norm_final, v_sinks, v_rel_bias)
    rep_res = adamw(rep_w[None], rep_m[None], rep_v[None], parts["rep"], "adam_rep")
    loss = rep_res[0][0, ROW_LOSS, 0]

    outs = []
    for kind in range(4):
        cw_dw, cb_dw, cln_g, cln_b, cb_pw2, cb_pw1 = _unpack_conv(cp_res[kind])
        r_mix, r_ffn, r_nkv, r_nfin, r_sinks, r_relb = _unpack_rep(rep_res[kind][0])
        outs += [r_mix, r_ffn, res["conv_w_pw1"][kind], cb_pw1, cw_dw, cb_dw, cln_g, cln_b, res["conv_w_pw2"][kind],
                 cb_pw2, r_nkv, res["w_kv"][kind], res["w_q"][kind], res["w_o"][kind], r_sinks, r_relb,
                 res["ffn_w_up"][kind], res["ffn_w_down"][kind], r_nfin]
    return (loss, grad_x[None], *outs)
```

```python
import functools
import math

import numpy as np
import jax
import jax.numpy as jnp
from jax import lax
from jax.experimental import pallas as pl
from jax.experimental.pallas import tpu as pltpu

F32 = jnp.float32
BF16 = jnp.bfloat16

D_MODEL = 1024
D_FF = 2816
N_HEADS = 16
N_KV_HEADS = 4
GROUP = N_HEADS // N_KV_HEADS
HEAD_DIM = 64
KV_DIM = N_KV_HEADS * HEAD_DIM
BLOCK = 128
CONV_WIDTH = 31
HALO = 32
N_BUCKETS = 32
MAX_DISTANCE = 128
EPS = 1e-6
NEG_INF = -1e30
N_DEV = 8
FF_CHUNK = D_FF // 4
PACK_ROWS = 40
ROW_BDW, ROW_LNG, ROW_LNB, ROW_BPW2, ROW_BPW1 = 31, 32, 33, 34, 35
REP_ROWS = 16
ROW_NKV, ROW_NFIN, ROW_SINK, ROW_RELB, ROW_LOSS = 8, 9, 10, 11, 12

ADAM_LR, ADAM_B1, ADAM_B2, ADAM_EPS, ADAM_WD, ADAM_STEP = 0.001, 0.9, 0.999, 1e-08, 0.01, 10

VMEM_LIMIT_BYTES = 56 * 1024 * 1024
FFN_ROWS = 1024
FFN_BWD_ROWS = 512
GRAD_ROWS = 2048
ANY = pl.BlockSpec(memory_space=pl.ANY)
MESH = pl.DeviceIdType.MESH

NN = (((1,), (0,)), ((), ()))
NT = (((1,), (1,)), ((), ()))
TN = (((0,), (0,)), ((), ()))


def _dot(a, b, dims):
    return lax.dot_general(a, b, dims, preferred_element_type=F32)


def _pcall(body, name, out_shape, *, grid=None, in_specs=None, out_specs=None, scratch=(), sem=None, **kw):
    params = pltpu.CompilerParams(dimension_semantics=sem, vmem_limit_bytes=VMEM_LIMIT_BYTES)
    extra = {} if grid is None else {"grid": grid}
    return pl.pallas_call(body, name=name, out_shape=out_shape, in_specs=in_specs, out_specs=out_specs,
                          scratch_shapes=list(scratch), compiler_params=params, **extra, **kw)


def _sds(shape, dtype):
    return jax.ShapeDtypeStruct(tuple(shape), dtype)


def _row_tile(s, want):
    return want if s % want == 0 else s


def rms_fwd(h, g, name):
    s, d = h.shape
    tm = _row_tile(s, 512)

    def body(h_ref, g_ref, u_ref):
        x = h_ref[...]
        r = lax.rsqrt(jnp.mean(x * x, axis=-1, keepdims=True) + EPS)
        u_ref[...] = (x * r * g_ref[...]).astype(BF16)

    return _pcall(body, name, _sds((s, d), BF16), grid=(s // tm,),
                  in_specs=[pl.BlockSpec((tm, d), lambda i: (i, 0)), pl.BlockSpec((1, d), lambda i: (0, 0))],
                  out_specs=pl.BlockSpec((tm, d), lambda i: (i, 0)), sem=("parallel",))(h, g)


def _mm(name, a, b, *, dims, grid, a_spec, b_spec, o_spec, o_shape, nk=1, acc_shape=None,
        bias=None, res=None, colsum=None, sem=None, token=None):
    n_axes = len(grid)

    def body(*refs):
        it = iter(refs)
        a_ref, b_ref = next(it), next(it)
        bias_ref = next(it) if bias is not None else None
        res_ref = next(it) if res is not None else None
        if token is not None:
            next(it)
        o_ref = next(it)
        cs_ref = next(it) if colsum is not None else None
        acc_ref = next(it) if nk > 1 else None
        k = pl.program_id(n_axes - 1)
        p = _dot(a_ref[...].astype(BF16), b_ref[...].astype(BF16), dims)

        def finish(acc):
            if bias_ref is not None:
                acc = acc + bias_ref[...]
            if res_ref is not None:
                acc = acc + res_ref[...]
            o_ref[...] = acc.astype(o_ref.dtype)

        if nk == 1:
            finish(p)
        else:
            @pl.when(k == 0)
            def _():
                acc_ref[...] = p

            @pl.when(k > 0)
            def _():
                acc_ref[...] += p

            @pl.when(k == nk - 1)
            def _():
                finish(acc_ref[...])

        if cs_ref is not None:
            cs = jnp.sum(b_ref[...].astype(F32), axis=0, keepdims=True)

            @pl.when(k == 0)
            def _():
                cs_ref[...] = cs

            @pl.when(k > 0)
            def _():
                cs_ref[...] += cs

    ins, in_specs = [a, b], [a_spec, b_spec]
    for extra in (bias, res, None if token is None else (token, ANY)):
        if extra is not None:
            ins.append(extra[0])
            in_specs.append(extra[1])
    out_shape, out_specs = o_shape, o_spec
    if colsum is not None:
        out_shape, out_specs = (o_shape, _sds(colsum[0], F32)), (o_spec, colsum[1])
    scratch = [pltpu.VMEM(acc_shape, F32)] if nk > 1 else []
    if sem is None:
        sem = ("parallel",) * (n_axes - 1) + ("arbitrary",)
    return _pcall(body, name, out_shape, grid=grid, in_specs=in_specs, out_specs=out_specs, scratch=scratch,
                  sem=sem)(*ins)


def mm_nn(name, a, w, w_block, w_index, n, tn, out_dtype, bias=None, res=None, tm=512):
    s, k = a.shape
    tm = _row_tile(s, tm)
    col = lambda i, j: (i, j)
    extras = {}
    if bias is not None:
        extras["bias"] = bias
    if res is not None:
        extras["res"] = (res, pl.BlockSpec((tm, tn), col))
    return _mm(name, a, w, dims=NN, grid=(s // tm, n // tn), a_spec=pl.BlockSpec((tm, k), lambda i, j: (i, 0)),
               b_spec=pl.BlockSpec(w_block, w_index), o_spec=pl.BlockSpec((tm, tn), col), o_shape=_sds((s, n), out_dtype),
               sem=("parallel", "arbitrary"), **extras)


def mm_nt(name, a, w, w_block, w_index, kout, out_dtype, nk=1, tk=None, tm=512, token=None):
    s, n = a.shape
    tm = _row_tile(s, tm)
    tk = n if tk is None else tk
    return _mm(name, a, w, dims=NT, grid=(s // tm, nk), a_spec=pl.BlockSpec((tm, tk), lambda i, k: (i, k)),
               b_spec=pl.BlockSpec(w_block, w_index), o_spec=pl.BlockSpec((tm, kout), lambda i, k: (i, 0)),
               o_shape=_sds((s, kout), out_dtype), nk=nk, acc_shape=(tm, kout), token=token)


def mm_tn(name, a, b, *, groups, a_block, a_index, b_block, b_index, o_block, o_index, o_shape, acc_shape,
          colsum=None, tk=GRAD_ROWS):
    s = a.shape[-2]
    tk = _row_tile(s, tk)
    return _mm(name, a, b, dims=TN, grid=(groups, s // tk), a_spec=pl.BlockSpec(a_block(tk), a_index),
               b_spec=pl.BlockSpec(b_block(tk), b_index), o_spec=pl.BlockSpec(o_block, o_index),
               o_shape=_sds(o_shape, BF16), nk=s // tk, acc_shape=acc_shape, colsum=colsum)


FFN_SUB = 256


def _sub_rows(tm):
    sub = FFN_SUB if tm % FFN_SUB == 0 else tm
    return [slice(r * sub, (r + 1) * sub) for r in range(tm // sub)]


def ffn_fwd(u, h, w_up_t, w_down, layer, name):
    s, d = u.shape
    tm = _row_tile(s, FFN_ROWS)
    nj = 4

    def body(u_ref, h_ref, wup_ref, wd_ref, hn_ref, gu_ref):
        j = pl.program_id(1)

        @pl.when(j == 0)
        def _():
            hn_ref[...] = h_ref[...]

        for rows in _sub_rows(tm):
            uv = u_ref[rows, :]
            g = _dot(uv, wup_ref[0], NT)
            p = _dot(uv, wup_ref[1], NT)
            gu_ref[0, rows, :] = g.astype(BF16)
            gu_ref[1, rows, :] = p.astype(BF16)
            act = (g * jax.nn.sigmoid(g) * p).astype(BF16)
            hn_ref[rows, :] += _dot(act, wd_ref[...], NN)

    row = pl.BlockSpec((tm, d), lambda i, j: (i, 0))
    return _pcall(
        body, name, (_sds((s, d), F32), _sds((2, nj, s, FF_CHUNK), BF16)), grid=(s // tm, nj),
        in_specs=[row, row,
                  pl.BlockSpec((None, 2, None, FF_CHUNK, d), lambda i, j: (layer, 0, j, 0, 0)),
                  pl.BlockSpec((None, None, FF_CHUNK, d), lambda i, j: (layer, j, 0, 0))],
        out_specs=(row, pl.BlockSpec((2, None, tm, FF_CHUNK), lambda i, j: (0, j, i, 0))),
        sem=("parallel", "arbitrary"))(u, h, w_up_t, w_down)


def _rms_bwd_rows(x, gain, du, dh_in):
    r = lax.rsqrt(jnp.mean(x * x, axis=-1, keepdims=True) + EPS)
    xh = x * r
    dxh = du * gain
    dx = r * (dxh - xh * jnp.mean(dxh * xh, axis=-1, keepdims=True))
    return dh_in + dx, jnp.sum(du * xh, axis=0, keepdims=True)


def ffn_bwd(dh, gu, w_up_t, w_down, layer, h_mid, gain, name, token=None):
    s, d = dh.shape
    tm = _row_tile(s, FFN_BWD_ROWS)
    nj = 4

    def body(dh_ref, gu_ref, wup_ref, wd_ref, h_ref, gain_ref, *rest):
        dho_ref, dgain_ref, act_ref, dgu_ref, du_ref = rest[-5:]
        i, j = pl.program_id(0), pl.program_id(1)

        @pl.when(j == 0)
        def _():
            du_ref[...] = jnp.zeros_like(du_ref)

        @pl.when((i == 0) & (j == 0))
        def _():
            dgain_ref[...] = jnp.zeros_like(dgain_ref)

        for rows in _sub_rows(tm):
            dact = _dot(dh_ref[rows, :].astype(BF16), wd_ref[...], NT)
            g = gu_ref[0, rows, :].astype(F32)
            p = gu_ref[1, rows, :].astype(F32)
            sig = jax.nn.sigmoid(g)
            sl = g * sig
            act_ref[rows, :] = (sl * p).astype(BF16)
            dp = (dact * sl).astype(BF16)
            dg = (dact * p * (sig * (1.0 + g * (1.0 - sig)))).astype(BF16)
            dgu_ref[0, rows, :] = dg
            dgu_ref[1, rows, :] = dp
            du_ref[rows, :] += _dot(dg, wup_ref[0], NN) + _dot(dp, wup_ref[1], NN)

        @pl.when(j == nj - 1)
        def _():
            for rows in _sub_rows(tm):
                dho, part = _rms_bwd_rows(h_ref[rows, :], gain_ref[...], du_ref[rows, :], dh_ref[rows, :])
                dho_ref[rows, :] = dho
                dgain_ref[...] += part

    row = pl.BlockSpec((tm, d), lambda i, j: (i, 0))
    vec = pl.BlockSpec((1, d), lambda i, j: (0, 0))
    gu_spec = pl.BlockSpec((2, None, tm, FF_CHUNK), lambda i, j: (0, j, i, 0))
    extra = [] if token is None else [token]
    return _pcall(
        body, name,
        (_sds((s, d), F32), _sds((1, d), F32), _sds((nj, s, FF_CHUNK), BF16), _sds((2, nj, s, FF_CHUNK), BF16)),
        grid=(s // tm, nj),
        in_specs=[row, gu_spec,
                  pl.BlockSpec((None, 2, None, FF_CHUNK, d), lambda i, j: (layer, 0, j, 0, 0)),
                  pl.BlockSpec((None, None, FF_CHUNK, d), lambda i, j: (layer, j, 0, 0)), row, vec]
        + [ANY] * len(extra),
        out_specs=(row, vec, pl.BlockSpec((None, tm, FF_CHUNK), lambda i, j: (j, i, 0)), gu_spec),
        scratch=[pltpu.VMEM((tm, d), F32)],
        sem=("arbitrary", "arbitrary"))(dh, gu, w_up_t, w_down, h_mid, gain, *extra)


def nt_rms_bwd(name, a, w, w_block, w_index, h, gain, dh_in):
    s, n = a.shape
    d = h.shape[1]
    tm = _row_tile(s, 512)

    def body(a_ref, w_ref, h_ref, gain_ref, dhi_ref, dho_ref, dgain_ref):
        i = pl.program_id(0)

        @pl.when(i == 0)
        def _():
            dgain_ref[...] = jnp.zeros_like(dgain_ref)

        du = _dot(a_ref[...].astype(BF16), w_ref[...], NT)
        dho, part = _rms_bwd_rows(h_ref[...], gain_ref[...], du, dhi_ref[...])
        dho_ref[...] = dho
        dgain_ref[...] += part

    row = pl.BlockSpec((tm, d), lambda i: (i, 0))
    vec = pl.BlockSpec((1, d), lambda i: (0, 0))
    return _pcall(body, name, (_sds((s, d), F32), _sds((1, d), F32)), grid=(s // tm,),
                  in_specs=[pl.BlockSpec((tm, n), lambda i: (i, 0)), pl.BlockSpec(w_block, w_index), row, vec, row],
                  out_specs=(row, vec), sem=("arbitrary",))(a, w, h, gain, dh_in)


def pw1_fwd(u, w, b, layer, name):
    s, d = u.shape
    tm = _row_tile(s, FFN_ROWS)
    nb, wb = w.shape[1], w.shape[3]

    def body(u_ref, w_ref, b_ref, t_ref):
        for rows in _sub_rows(tm):
            uv = u_ref[rows, :]
            for j in range(nb):
                cols = slice(j * wb, (j + 1) * wb)
                t_ref[rows, cols] = (_dot(uv, w_ref[j], NN) + b_ref[:, cols]).astype(BF16)

    return _pcall(
        body, name, _sds((s, nb * wb), BF16), grid=(s // tm,),
        in_specs=[pl.BlockSpec((tm, d), lambda i: (i, 0)), pl.BlockSpec((None, nb, d, wb), lambda i: (0, 0, 0, 0)),
                  pl.BlockSpec((None, 1, nb * wb), lambda i: (layer, 0, 0))],
        out_specs=pl.BlockSpec((tm, nb * wb), lambda i: (i, 0)), sem=("parallel",))(u, w, b)


def pw1_bwd(dt, w, h, gain, dh_in, name):
    s = dt.shape[0]
    nb, d, wb = w.shape[1], w.shape[2], w.shape[3]
    tm = _row_tile(s, 512)

    def body(dt_ref, w_ref, h_ref, gain_ref, dhi_ref, dho_ref, dgain_ref):
        i = pl.program_id(0)

        @pl.when(i == 0)
        def _():
            dgain_ref[...] = jnp.zeros_like(dgain_ref)

        for rows in _sub_rows(tm):
            du = _dot(dt_ref[rows, 0:wb], w_ref[0], NT)
            for j in range(1, nb):
                du = du + _dot(dt_ref[rows, j * wb:(j + 1) * wb], w_ref[j], NT)
            dho, part = _rms_bwd_rows(h_ref[rows, :], gain_ref[...], du, dhi_ref[rows, :])
            dho_ref[rows, :] = dho
            dgain_ref[...] += part

    row = pl.BlockSpec((tm, d), lambda i: (i, 0))
    vec = pl.BlockSpec((1, d), lambda i: (0, 0))
    return _pcall(
        body, name, (_sds((s, d), F32), _sds((1, d), F32)), grid=(s // tm,),
        in_specs=[pl.BlockSpec((tm, nb * wb), lambda i: (i, 0)),
                  pl.BlockSpec((None, nb, d, wb), lambda i: (0, 0, 0, 0)), row, vec, row],
        out_specs=(row, vec), sem=("arbitrary",))(dt, w, h, gain, dh_in)


def pw1_grad(u, dt, name):
    s, d = u.shape
    n = dt.shape[1]
    nb = N_DEV
    wb = n // nb
    tk = _row_tile(s, 1024)
    nk = s // tk

    def body(u_ref, dt_ref, g_ref, db_ref, acc_ref):
        k = pl.program_id(0)
        p = _dot(u_ref[...], dt_ref[...], TN)
        cs = jnp.sum(dt_ref[...].astype(F32), axis=0, keepdims=True)

        @pl.when(k == 0)
        def _():
            acc_ref[...] = p
            db_ref[...] = cs

        @pl.when(k > 0)
        def _():
            acc_ref[...] += p
            db_ref[...] += cs

        @pl.when(k == nk - 1)
        def _():
            for j in range(nb):
                g_ref[j] = acc_ref[:, j * wb:(j + 1) * wb].astype(BF16)

    return _pcall(
        body, name, (_sds((nb, d, wb), BF16), _sds((1, n), F32)), grid=(nk,),
        in_specs=[pl.BlockSpec((tk, d), lambda k: (k, 0)), pl.BlockSpec((tk, n), lambda k: (k, 0))],
        out_specs=(pl.BlockSpec((nb, d, wb), lambda k: (0, 0, 0)), pl.BlockSpec((1, n), lambda k: (0, 0))),
        scratch=[pltpu.VMEM((d, n), F32)], sem=("arbitrary",))(u, dt)


def _glu(t):
    t = t.astype(F32)
    return t[:, :D_MODEL] * jax.nn.sigmoid(t[:, D_MODEL:])


def _conv_tile(s):
    return 256 if s % 256 == 0 else s


CONV_ROWS = 32
CONV_LANES = 512
SUBLANES = 8


def _shifted_copies(sh_ref, rows):
    for b in range(1, SUBLANES):
        sh_ref[b, 0:rows - SUBLANES, :] = sh_ref[0, b:b + rows - SUBLANES, :]


def conv_fwd(t, cp, name):
    s = t.shape[0]
    d = D_MODEL
    ts = _conv_tile(s)
    per = ts // HALO
    rows = HALO + ts
    lead = HALO - (CONV_WIDTH - 1)
    rc = CONV_ROWS

    def body(t_ref, tp_ref, cp_ref, z_ref, y_ref, sh_ref):
        i = pl.program_id(0)
        sh_ref[0, 0:HALO, :] = jnp.where(i > 0, _glu(tp_ref[...]), 0.0)
        sh_ref[0, HALO:rows, :] = _glu(t_ref[...])
        _shifted_copies(sh_ref, rows)

        def chunk(c, carry):
            r0 = pl.multiple_of(c * rc, rc)
            for lc in range(d // CONV_LANES):
                ln = slice(lc * CONV_LANES, (lc + 1) * CONV_LANES)
                acc = jnp.zeros((rc, CONV_LANES), F32) + cp_ref[ROW_BDW:ROW_BDW + 1, ln]
                for k in range(CONV_WIDTH):
                    a8, b = divmod(lead + k, SUBLANES)
                    acc = acc + cp_ref[k:k + 1, ln] * sh_ref[b, pl.ds(r0 + SUBLANES * a8, rc), ln]
                y_ref[pl.ds(r0, rc), ln] = acc
            y = y_ref[pl.ds(r0, rc), :]
            mu = jnp.mean(y, axis=-1, keepdims=True)
            yc = y - mu
            rstd = lax.rsqrt(jnp.mean(yc * yc, axis=-1, keepdims=True) + EPS)
            yn = yc * rstd * cp_ref[ROW_LNG:ROW_LNG + 1, :] + cp_ref[ROW_LNB:ROW_LNB + 1, :]
            z_ref[pl.ds(r0, rc), :] = (yn * jax.nn.sigmoid(yn)).astype(BF16)
            return carry

        lax.fori_loop(0, ts // rc, chunk, 0)

    row = pl.BlockSpec((ts, d), lambda i: (i, 0))
    return _pcall(
        body, name, (_sds((s, d), BF16), _sds((s, d), F32)), grid=(s // ts,),
        in_specs=[pl.BlockSpec((ts, 2 * d), lambda i: (i, 0)),
                  pl.BlockSpec((HALO, 2 * d), lambda i: (jnp.maximum(i * per - 1, 0), 0)),
                  pl.BlockSpec((PACK_ROWS, d), lambda i: (0, 0))],
        out_specs=(row, row),
        scratch=[pltpu.VMEM((SUBLANES, rows, d), F32)], sem=("parallel",))(t, t, cp)


def conv_bwd(t, y, dz, cp, name):
    s = t.shape[0]
    d = D_MODEL
    ts = _conv_tile(s)
    per = ts // HALO
    nt = s // ts
    te = ts + HALO
    rc = CONV_ROWS

    def body(t_ref, y_ref, yn_ref, dz_ref, dzn_ref, cp_ref, dt_ref, st_ref, shd_ref, dw_ref):
        i = pl.program_id(0)
        last = i == nt - 1

        @pl.when(i == 0)
        def _():
            st_ref[...] = jnp.zeros_like(st_ref)
            dw_ref[...] = jnp.zeros_like(dw_ref)

        gain = cp_ref[ROW_LNG:ROW_LNG + 1, :]

        def ln_bwd(yv, dzv):
            mu = jnp.mean(yv, axis=-1, keepdims=True)
            yc = yv - mu
            rstd = lax.rsqrt(jnp.mean(yc * yc, axis=-1, keepdims=True) + EPS)
            yh = yc * rstd
            yn = yh * gain + cp_ref[ROW_LNB:ROW_LNB + 1, :]
            sig = jax.nn.sigmoid(yn)
            dyn = dzv * (sig * (1.0 + yn * (1.0 - sig)))
            dyh = dyn * gain
            dy = rstd * (dyh - jnp.mean(dyh, axis=-1, keepdims=True)
                         - yh * jnp.mean(dyh * yh, axis=-1, keepdims=True))
            return dy, dyn, yh

        def norm_chunk(c, carry):
            r0 = pl.multiple_of(c * rc, rc)
            dy, dyn, yh = ln_bwd(y_ref[pl.ds(r0, rc), :], dz_ref[pl.ds(r0, rc), :])
            shd_ref[0, pl.ds(r0, rc), :] = dy
            st_ref[ROW_BDW:ROW_BDW + 1, :] += jnp.sum(dy, axis=0, keepdims=True)
            st_ref[ROW_LNG:ROW_LNG + 1, :] += jnp.sum(dyn * yh, axis=0, keepdims=True)
            st_ref[ROW_LNB:ROW_LNB + 1, :] += jnp.sum(dyn, axis=0, keepdims=True)
            return carry

        lax.fori_loop(0, ts // rc, norm_chunk, 0)
        dy_halo, _, _ = ln_bwd(yn_ref[...], jnp.where(last, 0.0, dzn_ref[...]))
        shd_ref[0, ts:te, :] = dy_halo
        _shifted_copies(shd_ref, te)

        def tap_chunk(c, carry):
            r0 = pl.multiple_of(c * rc, rc)
            for lc in range(d // CONV_LANES):
                ln = slice(lc * CONV_LANES, (lc + 1) * CONV_LANES)
                ln2 = slice(d + lc * CONV_LANES, d + (lc + 1) * CONV_LANES)
                t1 = t_ref[pl.ds(r0, rc), ln].astype(F32)
                sg = jax.nn.sigmoid(t_ref[pl.ds(r0, rc), ln2].astype(F32))
                a = t1 * sg
                da = jnp.zeros((rc, CONV_LANES), F32)
                for k in range(CONV_WIDTH):
                    a8, b = divmod(CONV_WIDTH - 1 - k, SUBLANES)
                    e = shd_ref[b, pl.ds(r0 + SUBLANES * a8, rc), ln]
                    da = da + cp_ref[k:k + 1, ln] * e
                    dw_ref[k, :, ln] += jnp.sum((a * e).reshape(rc // SUBLANES, SUBLANES, CONV_LANES), axis=0)
                dt_ref[pl.ds(r0, rc), ln] = (da * sg).astype(BF16)
                dt_ref[pl.ds(r0, rc), ln2] = (da * t1 * sg * (1.0 - sg)).astype(BF16)
            return carry

        lax.fori_loop(0, ts // rc, tap_chunk, 0)

        @pl.when(last)
        def _():
            for k in range(CONV_WIDTH):
                st_ref[k:k + 1, :] = jnp.sum(dw_ref[k], axis=0, keepdims=True)

    last_halo = s // HALO - 1
    row = pl.BlockSpec((ts, d), lambda i: (i, 0))
    halo = pl.BlockSpec((HALO, d), lambda i: (jnp.minimum((i + 1) * per, last_halo), 0))
    return _pcall(
        body, name, (_sds((s, 2 * d), BF16), _sds((PACK_ROWS, d), F32)), grid=(nt,),
        in_specs=[pl.BlockSpec((ts, 2 * d), lambda i: (i, 0)), row, halo, row, halo,
                  pl.BlockSpec((PACK_ROWS, d), lambda i: (0, 0))],
        out_specs=(pl.BlockSpec((ts, 2 * d), lambda i: (i, 0)), pl.BlockSpec((PACK_ROWS, d), lambda i: (0, 0))),
        scratch=[pltpu.VMEM((SUBLANES, te, d), F32), pltpu.VMEM((CONV_WIDTH, SUBLANES, d), F32)],
        sem=("arbitrary",))(t, y, y, dz, dz, cp)


def _bucket_table():
    qi = np.arange(BLOCK, dtype=np.int64)[:, None]
    kj = np.arange(2 * BLOCK, dtype=np.int64)[None, :]
    dist = qi + BLOCK - kj
    max_exact = N_BUCKETS // 2
    dd = np.maximum(dist, 0)
    ratio = (np.maximum(dd, 1).astype(np.float32) / np.float32(max_exact)).astype(np.float32)
    log_ratio = (np.log(ratio).astype(np.float32) / np.float32(math.log(MAX_DISTANCE / max_exact))).astype(np.float32)
    large = max_exact + (log_ratio * np.float32(N_BUCKETS - max_exact)).astype(np.int32)
    large = np.minimum(large, N_BUCKETS - 1)
    bucket = np.where(dd < max_exact, dd, large)
    return np.where((dist >= 0) & (dist < BLOCK), bucket, -1).astype(np.int32)


def bias_table(rel_bias, bucket, name):
    def body(rb_ref, bk_ref, o_ref):
        bk = bk_ref[...]
        for h in range(N_HEADS):
            acc = jnp.full((BLOCK, 2 * BLOCK), NEG_INF, F32)
            for b in range(N_BUCKETS):
                acc = jnp.where(bk == b, rb_ref[b, h], acc)
            o_ref[h] = acc

    return _pcall(body, name, _sds((N_HEADS, BLOCK, 2 * BLOCK), F32),
                  in_specs=[pl.BlockSpec(memory_space=pltpu.SMEM), pl.BlockSpec(memory_space=pltpu.VMEM)],
                  out_specs=pl.BlockSpec(memory_space=pltpu.VMEM))(rel_bias, bucket)


def bias_grad(dba, dbb, bucket, name):
    def body(a_ref, b_ref, bk_ref, o_ref):
        bk = bk_ref[...]
        for h in range(N_HEADS):
            db = a_ref[h] + b_ref[h]
            for b in range(N_BUCKETS):
                o_ref[b, h] = jnp.sum(jnp.where(bk == b, db, 0.0))

    vm = pl.BlockSpec(memory_space=pltpu.VMEM)
    return _pcall(body, name, _sds((N_BUCKETS, N_HEADS), F32), in_specs=[vm, vm, vm],
                  out_specs=pl.BlockSpec(memory_space=pltpu.SMEM))(dba, dbb, bucket)


def _band_specs():
    cur = pl.BlockSpec((BLOCK, 2 * KV_DIM), lambda n: (n, 0))
    prev = pl.BlockSpec((BLOCK, 2 * KV_DIM), lambda n: (jnp.maximum(n - 1, 0), 0))
    return cur, prev


def _scores(q_h, k_h, bias_h, first_row, sink):
    sc = _dot(q_h, k_h, NT) * (HEAD_DIM ** -0.5) + bias_h + first_row
    m = jnp.maximum(jnp.max(sc, axis=-1, keepdims=True), sink)
    p = jnp.exp(sc - m)
    e_sink = jnp.exp(sink - m)
    den = jnp.sum(p, axis=-1, keepdims=True) + e_sink
    return p, e_sink, den


def _first_block_row(n):
    col = lax.broadcasted_iota(jnp.int32, (1, 2 * BLOCK), 1)
    return jnp.where((col < BLOCK) & (n == 0), NEG_INF, 0.0)


def _head_lanes(hk, g):
    h = hk * GROUP + g
    return slice(h * HEAD_DIM, (h + 1) * HEAD_DIM)


def _group_rows(x_ref, hk):
    return jnp.concatenate([x_ref[:, _head_lanes(hk, g)] for g in range(GROUP)], axis=0)


def _group_bias(bias_ref, hk):
    return bias_ref[hk * GROUP:(hk + 1) * GROUP].reshape(GROUP * BLOCK, 2 * BLOCK)


def _group_sinks(sink_ref, hk):
    head = lax.broadcasted_iota(jnp.int32, (GROUP * BLOCK, 1), 0) // BLOCK
    col = jnp.zeros((GROUP * BLOCK, 1), F32) + sink_ref[0, hk * GROUP]
    for g in range(1, GROUP):
        col = jnp.where(head == g, sink_ref[0, hk * GROUP + g], col)
    return col


def attn_fwd(q, kv, bias, sinks, name):
    s = q.shape[0]
    nb = s // BLOCK

    def body(sink_ref, q_ref, kvc_ref, kvp_ref, bias_ref, o_ref, band_ref):
        n = pl.program_id(0)
        band_ref[0:BLOCK, :] = kvp_ref[...]
        band_ref[BLOCK:2 * BLOCK, :] = kvc_ref[...]
        first_row = _first_block_row(n)
        for hk in range(N_KV_HEADS):
            k_h = band_ref[:, hk * HEAD_DIM:(hk + 1) * HEAD_DIM]
            v_h = band_ref[:, KV_DIM + hk * HEAD_DIM:KV_DIM + (hk + 1) * HEAD_DIM]
            p, _, den = _scores(_group_rows(q_ref, hk), k_h, _group_bias(bias_ref, hk), first_row,
                                _group_sinks(sink_ref, hk))
            o = _dot((p * (1.0 / den)).astype(BF16), v_h, NN).astype(BF16)
            for g in range(GROUP):
                o_ref[:, _head_lanes(hk, g)] = o[g * BLOCK:(g + 1) * BLOCK]

    cur, prev = _band_specs()
    qs = pl.BlockSpec((BLOCK, D_MODEL), lambda n: (n, 0))
    return _pcall(
        body, name, _sds((s, D_MODEL), BF16), grid=(nb,),
        in_specs=[pl.BlockSpec(memory_space=pltpu.SMEM), qs, cur, prev,
                  pl.BlockSpec((N_HEADS, BLOCK, 2 * BLOCK), lambda n: (0, 0, 0))],
        out_specs=qs, scratch=[pltpu.VMEM((2 * BLOCK, 2 * KV_DIM), BF16)],
        sem=("parallel",))(sinks, q, kv, kv, bias)


def attn_bwd(q, kv, do, bias, sinks, name):
    s = q.shape[0]
    nb = s // BLOCK
    scale = HEAD_DIM ** -0.5

    def body(sink_ref, q_ref, do_ref, kvc_ref, kvp_ref, bias_ref, dq_ref, dkv_ref, db_ref, dsink_ref,
             band_ref, dsacc_ref):
        n = pl.program_id(0)
        band_ref[0:BLOCK, :] = kvp_ref[...]
        band_ref[BLOCK:2 * BLOCK, :] = kvc_ref[...]
        first_row = _first_block_row(n)
        lane = lax.broadcasted_iota(jnp.int32, (BLOCK, BLOCK), 1)

        @pl.when(n == 0)
        def _():
            db_ref[...] = jnp.zeros_like(db_ref)
            dsacc_ref[...] = jnp.zeros_like(dsacc_ref)

        for hk in range(N_KV_HEADS):
            k_h = band_ref[:, hk * HEAD_DIM:(hk + 1) * HEAD_DIM]
            v_h = band_ref[:, KV_DIM + hk * HEAD_DIM:KV_DIM + (hk + 1) * HEAD_DIM]
            q_g = _group_rows(q_ref, hk)
            do_g = _group_rows(do_ref, hk)
            p, e_sink, den = _scores(q_g, k_h, _group_bias(bias_ref, hk), first_row, _group_sinks(sink_ref, hk))
            inv = 1.0 / den
            p = p * inv
            dp = _dot(do_g, v_h, NT)
            delta = jnp.sum(p * dp, axis=-1, keepdims=True)
            ds = p * (dp - delta)
            db_ref[hk * GROUP:(hk + 1) * GROUP] += ds.reshape(GROUP, BLOCK, 2 * BLOCK)
            d_sink = -(e_sink * inv) * delta
            for g in range(GROUP):
                dsacc_ref[...] += jnp.where(lane == hk * GROUP + g, d_sink[g * BLOCK:(g + 1) * BLOCK], 0.0)
            dsb = ds.astype(BF16)
            dq = (_dot(dsb, k_h, NN) * scale).astype(BF16)
            for g in range(GROUP):
                dq_ref[:, _head_lanes(hk, g)] = dq[g * BLOCK:(g + 1) * BLOCK]
            dkv_ref[:, hk * HEAD_DIM:(hk + 1) * HEAD_DIM] = _dot(dsb, q_g, TN) * scale
            dkv_ref[:, KV_DIM + hk * HEAD_DIM:KV_DIM + (hk + 1) * HEAD_DIM] = _dot(p.astype(BF16), do_g, TN)

        @pl.when(n == nb - 1)
        def _():
            dsink_ref[...] = jnp.sum(dsacc_ref[...], axis=0, keepdims=True)

    cur, prev = _band_specs()
    qs = pl.BlockSpec((BLOCK, D_MODEL), lambda n: (n, 0))
    full_b = pl.BlockSpec((N_HEADS, BLOCK, 2 * BLOCK), lambda n: (0, 0, 0))
    return _pcall(
        body, name,
        (_sds((s, D_MODEL), BF16), _sds((nb, 2 * BLOCK, 2 * KV_DIM), F32),
         _sds((N_HEADS, BLOCK, 2 * BLOCK), F32), _sds((1, BLOCK), F32)),
        grid=(nb,),
        in_specs=[pl.BlockSpec(memory_space=pltpu.SMEM), qs, qs, cur, prev, full_b],
        out_specs=(qs, pl.BlockSpec((None, 2 * BLOCK, 2 * KV_DIM), lambda n: (n, 0, 0)), full_b,
                   pl.BlockSpec((1, BLOCK), lambda n: (0, 0))),
        scratch=[pltpu.VMEM((2 * BLOCK, 2 * KV_DIM), BF16), pltpu.VMEM((BLOCK, BLOCK), F32)],
        sem=("arbitrary",))(sinks, q, do, kv, kv, bias)


def dkv_combine(pa, pb, name):
    nb = pa.shape[0]
    pa2 = pa.reshape(2 * nb, BLOCK, 2 * KV_DIM)
    pb2 = pb.reshape(2 * nb, BLOCK, 2 * KV_DIM)

    def body(ac_ref, an_ref, bc_ref, bn_ref, o_ref):
        n = pl.program_id(0)
        nxt = jnp.where(n == nb - 1, 0.0, an_ref[...] + bn_ref[...])
        o_ref[...] = (ac_ref[...] + bc_ref[...] + nxt).astype(BF16)

    cur = pl.BlockSpec((None, BLOCK, 2 * KV_DIM), lambda n: (2 * n + 1, 0, 0))
    nxt = pl.BlockSpec((None, BLOCK, 2 * KV_DIM), lambda n: (jnp.minimum(2 * n + 2, 2 * nb - 2), 0, 0))
    return _pcall(body, name, _sds((nb * BLOCK, 2 * KV_DIM), BF16), grid=(nb,),
                  in_specs=[cur, nxt, cur, nxt], out_specs=pl.BlockSpec((BLOCK, 2 * KV_DIM), lambda n: (n, 0)),
                  sem=("parallel",))(pa2, pa2, pb2, pb2)


def loss_head(h, g, target, name):
    s, d = h.shape
    tm = _row_tile(s, 512)

    def body(h_ref, g_ref, t_ref, dh_ref, dg_ref, loss_ref):
        i = pl.program_id(0)
        x = h_ref[...]
        r = lax.rsqrt(jnp.mean(x * x, axis=-1, keepdims=True) + EPS)
        xh = x * r
        gv = g_ref[...]
        err = xh * gv - t_ref[...]
        part_loss = jnp.zeros((1, BLOCK), F32) + 0.5 * jnp.sum(jnp.mean(err * err, axis=-1, keepdims=True))
        dy = err * (1.0 / d)
        dxh = dy * gv
        dh_ref[...] = r * (dxh - xh * jnp.mean(dxh * xh, axis=-1, keepdims=True))
        part_g = jnp.sum(dy * xh, axis=0, keepdims=True)

        @pl.when(i == 0)
        def _():
            dg_ref[...] = part_g
            loss_ref[...] = part_loss

        @pl.when(i > 0)
        def _():
            dg_ref[...] += part_g
            loss_ref[...] += part_loss

    row = pl.BlockSpec((tm, d), lambda i: (i, 0))
    vec = pl.BlockSpec((1, d), lambda i: (0, 0))
    return _pcall(body, name, (_sds((s, d), F32), _sds((1, d), F32), _sds((1, BLOCK), F32)), grid=(s // tm,),
                  in_specs=[row, vec, row], out_specs=(row, vec, pl.BlockSpec((1, BLOCK), lambda i: (0, 0))),
                  sem=("arbitrary",))(h, g, target)


def adamw(w, m, v, parts, name):
    nl, r, c = w.shape
    tr = max(t for t in range(1, min(r, 512) + 1) if r % t == 0 and (t % 16 == 0 or t == r))
    c1 = 1.0 / (1.0 - ADAM_B1 ** ADAM_STEP)
    c2 = 1.0 / (1.0 - ADAM_B2 ** ADAM_STEP)

    def body(w_ref, m_ref, v_ref, p_ref, g_ref, d_ref, nm_ref, nv_ref):
        g = p_ref[0].astype(F32)
        for dev in range(1, N_DEV):
            g = g + p_ref[dev].astype(F32)
        nm = ADAM_B1 * m_ref[...] + (1.0 - ADAM_B1) * g
        nv = ADAM_B2 * v_ref[...] + (1.0 - ADAM_B2) * (g * g)
        g_ref[...] = g
        nm_ref[...] = nm
        nv_ref[...] = nv
        d_ref[...] = -ADAM_LR * ((nm * c1) / (jnp.sqrt(nv * c2) + ADAM_EPS) + ADAM_WD * w_ref[...])

    blk = pl.BlockSpec((None, tr, c), lambda l, i: (l, i, 0))
    out = _sds((nl, r, c), F32)
    return _pcall(body, name, (out, out, out, out), grid=(nl, r // tr),
                  in_specs=[blk, blk, blk, pl.BlockSpec((N_DEV, None, tr, c), lambda l, i: (0, l, i, 0))],
                  out_specs=(blk, blk, blk, blk), sem=("parallel", "parallel"))(w, m, v, parts)


def _place():
    x, y, c = lax.axis_index("x"), lax.axis_index("y"), lax.axis_index("c")
    return x, y, c


def _lin(px, py, pc):
    return 4 * px + 2 * py + pc


HBM = pl.BlockSpec(memory_space=pltpu.HBM)
SEM = pl.BlockSpec(memory_space=pltpu.SEMAPHORE)
EFFECT = pltpu.SideEffectType.DATAFLOW_SIDE_EFFECTING
N_PEERS = N_DEV - 1


def _peers_of(x, y, c):
    return [(x, y, 1 - c), (1 - x, y, c), (x, 1 - y, c), (1 - x, 1 - y, c),
            (1 - x, y, 1 - c), (x, 1 - y, 1 - c), (1 - x, 1 - y, 1 - c)]


def _in_hbm(a):
    return pltpu.with_memory_space_constraint(a, pltpu.HBM)


def send_start(name, bufs, copies, n_groups):
    nb = len(bufs)
    per_group = [[i for i, cp in enumerate(copies) if cp[0] == g] for g in range(n_groups)]

    def body(*refs):
        buf = refs[:nb]
        sems = refs[nb:nb + 2 * n_groups]
        token = refs[2 * nb + 2 * n_groups]
        x, y, c = _place()
        me = _lin(x, y, c)
        for g in range(n_groups):
            for slot, i in enumerate(per_group[g]):
                _, s, src_slab, d, land_slab = copies[i]
                for k, peer in enumerate(_peers_of(x, y, c)):
                    pltpu.make_async_remote_copy(
                        src_ref=src_slab(buf[s], _lin(*peer), me), dst_ref=land_slab(buf[d], me),
                        send_sem=sems[2 * g].at[slot * N_PEERS + k], recv_sem=sems[2 * g + 1].at[slot * N_PEERS + k],
                        device_id=peer, device_id_type=MESH).start()
        token[...] = jnp.zeros_like(token)

    sem_shapes = []
    for g in range(n_groups):
        sem_shapes += [pltpu.SemaphoreType.DMA((len(per_group[g]) * N_PEERS,))] * 2
    out = pl.pallas_call(
        body, name=name,
        out_shape=tuple(sem_shapes) + tuple(pltpu.HBM(b.shape, b.dtype) for b in bufs) + (_sds((8, 128), F32),),
        in_specs=[HBM] * nb,
        out_specs=tuple([SEM] * len(sem_shapes)) + tuple([HBM] * nb) + (pl.BlockSpec(memory_space=pltpu.VMEM),),
        input_output_aliases={i: len(sem_shapes) + i for i in range(nb)},
        compiler_params=pltpu.CompilerParams(has_side_effects=EFFECT))(*[_in_hbm(b) for b in bufs])
    sems = [(out[2 * g], out[2 * g + 1]) for g in range(n_groups)]
    return sems, list(out[2 * n_groups:2 * n_groups + nb]), out[2 * n_groups + nb]


def send_wait(name, bufs, copies, sems, after):
    nb = len(bufs)

    def body(*refs):
        buf = refs[:nb]
        send_sems, recv_sems = refs[nb], refs[nb + 1]
        x, y, c = _place()
        me = _lin(x, y, c)
        for slot, (s, src_slab, d, land_slab) in enumerate(copies):
            for k, peer in enumerate(_peers_of(x, y, c)):
                j = _lin(*peer)
                cp = pltpu.make_async_remote_copy(
                    src_ref=src_slab(buf[s], j, me), dst_ref=land_slab(buf[d], j),
                    send_sem=send_sems.at[slot * N_PEERS + k], recv_sem=recv_sems.at[slot * N_PEERS + k],
                    device_id=peer, device_id_type=MESH)
                cp.wait_send()
                cp.wait_recv()

    out = pl.pallas_call(
        body, name=name, out_shape=tuple(pltpu.HBM(b.shape, b.dtype) for b in bufs),
        in_specs=[HBM] * nb + [SEM, SEM, ANY], out_specs=tuple([HBM] * nb),
        input_output_aliases={i: i for i in range(nb)},
        compiler_params=pltpu.CompilerParams(has_side_effects=EFFECT))(*bufs, sems[0], sems[1], after)
    return list(out)


def local_step(x, target, weights, rep, emit):
    s = x.shape[0]
    bucket = jnp.asarray(_bucket_table())
    bias = bias_table(rep["rel_bias"], bucket, "bias_table")
    h = x
    saved = []
    kv = None
    h_kv = u_kv = None
    small = None
    for l in range(4):
        g_mix = rep["norm_mix"][l:l + 1]
        g_ffn = rep["norm_ffn"][l:l + 1]
        rec = {"h_in": h}
        u = rms_fwd(h, g_mix, f"rms_mix_fwd{l}")
        rec["u"] = u
        if l < 2:
            w = weights(f"conv{l}", u)
            if l == 0:
                small = w
            cp = small["cp"][l]
            t = pw1_fwd(u, w["pw1"], small["b_pw1"], l, f"pw1_fwd{l}")
            z, y = conv_fwd(t, cp, f"conv_fwd{l}")
            h = mm_nn(f"pw2_fwd{l}", z, w["pw2"], (None, D_MODEL, 512), lambda i, j: (0, 0, j), D_MODEL, 512,
                      F32, bias=(small["b_pw2"], pl.BlockSpec((None, 1, 512), lambda i, j, l=l: (l, 0, j))), res=h)
            rec.update(t=t, z=z, y=y, cp=cp)
        else:
            a = l - 2
            w = weights(f"attn{a}", u)
            if a == 0:
                h_kv = h
                u_kv = rms_fwd(h, rep["norm_kv"], "rms_kv_fwd")
                w_kv = w["wkv"]
                kv = mm_nn("kv_fwd", u_kv, w_kv, (D_MODEL, 2 * KV_DIM), lambda i, j: (0, 0), 2 * KV_DIM,
                           2 * KV_DIM, BF16)
            q = mm_nn(f"q_fwd{a}", u, w["wq"], (None, D_MODEL, 512), lambda i, j: (0, 0, j), D_MODEL, 512, BF16)
            o = attn_fwd(q, kv, bias, rep["sinks"][a:a + 1], f"attn_fwd{a}")
            h = mm_nn(f"o_fwd{a}", o, w["wo"], (None, D_MODEL, 512), lambda i, j: (0, 0, j), D_MODEL, 512, F32,
                      res=h)
            rec.update(q=q, o=o)
        rec["w"] = w
        rec["h_mid"] = h
        uf = rms_fwd(h, g_ffn, f"rms_ffn_fwd{l}")
        wf = weights(f"ffn{l}", uf)
        h, gu = ffn_fwd(uf, h, wf["up"], wf["down"], 0, f"ffn_fwd{l}")
        rec.update(uf=uf, gu=gu, wf=wf)
        saved.append(rec)

    dh, d_nfin, loss = loss_head(h, rep["norm_final"], target, "loss_head")

    d_mix, d_ffn = [None] * 4, [None] * 4
    cp_grads = [None, None]
    dkv_parts, dbias_parts, dsinks = [], [], [None, None]
    d_nkv = None
    full_rows = lambda tk: (tk, D_MODEL)
    tok = None
    for l in reversed(range(4)):
        rec = saved[l]
        w, wf = rec["w"], rec["wf"]
        grads = {}
        g_mix = rep["norm_mix"][l:l + 1]
        g_ffn = rep["norm_ffn"][l:l + 1]
        dh_mid, d_ffn[l], act, dgu = ffn_bwd(dh, rec["gu"], wf["up"], wf["down"], 0, rec["h_mid"], g_ffn,
                                             f"ffn_bwd{l}", token=tok)
        g_down = mm_tn(
            f"down_grad{l}", act, dh, groups=4, a_block=lambda tk: (None, tk, FF_CHUNK), a_index=lambda j, k: (j, k, 0),
            b_block=full_rows, b_index=lambda j, k: (k, 0), o_block=(None, FF_CHUNK, D_MODEL),
            o_index=lambda j, k: (j, 0, 0), o_shape=(4, FF_CHUNK, D_MODEL), acc_shape=(FF_CHUNK, D_MODEL))
        g_up = mm_tn(
            f"up_grad{l}", dgu.reshape(8, s, FF_CHUNK), rec["uf"], groups=8, a_block=lambda tk: (None, tk, FF_CHUNK),
            a_index=lambda j, k: (j, k, 0), b_block=full_rows, b_index=lambda j, k: (k, 0),
            o_block=(None, FF_CHUNK, D_MODEL), o_index=lambda j, k: (j, 0, 0), o_shape=(8, FF_CHUNK, D_MODEL),
            acc_shape=(FF_CHUNK, D_MODEL))
        tok = emit(f"ffn{l}", {"up": g_up, "down": g_down.reshape(D_FF, D_MODEL)})
        dh = dh_mid
        if l < 2:
            dz = mm_nt(f"pw2_bwd{l}", dh, w["pw2"], (None, D_MODEL, D_MODEL), lambda i, k: (0, 0, 0), D_MODEL, F32,
                       token=tok)
            grads["pw2"], db2 = mm_tn(
                f"pw2_grad{l}", rec["z"], dh, groups=1, a_block=full_rows, a_index=lambda j, k: (k, 0),
                b_block=full_rows, b_index=lambda j, k: (k, 0), o_block=(D_MODEL, D_MODEL), o_index=lambda j, k: (0, 0),
                o_shape=(D_MODEL, D_MODEL), acc_shape=(D_MODEL, D_MODEL),
                colsum=((1, D_MODEL), pl.BlockSpec((1, D_MODEL), lambda j, k: (0, 0))))
            dt, stats = conv_bwd(rec["t"], rec["y"], dz, rec["cp"], f"conv_bwd{l}")
            grads["pw1"], db1 = pw1_grad(rec["u"], dt, f"pw1_grad{l}")
            cp_grads[l] = (stats, db2, db1)
            tok = emit(f"conv{l}", grads)
            dh, d_mix[l] = pw1_bwd(dt, w["pw1"], rec["h_in"], g_mix, dh, f"pw1_bwd{l}")
        else:
            a = l - 2
            do = mm_nt(f"o_bwd{a}", dh, w["wo"], (None, D_MODEL, D_MODEL), lambda i, k: (0, 0, 0), D_MODEL, BF16,
                       token=tok)
            tok = None
            grads["wo"] = mm_tn(
                f"wo_grad{a}", rec["o"], dh, groups=1, a_block=full_rows, a_index=lambda j, k: (k, 0),
                b_block=full_rows, b_index=lambda j, k: (k, 0), o_block=(D_MODEL, D_MODEL), o_index=lambda j, k: (0, 0),
                o_shape=(D_MODEL, D_MODEL), acc_shape=(D_MODEL, D_MODEL))
            dq, dkv_p, dbias_p, dsinks[a] = attn_bwd(rec["q"], kv, do, bias, rep["sinks"][a:a + 1], f"attn_bwd{a}")
            dkv_parts.append(dkv_p)
            dbias_parts.append(dbias_p)
            grads["wq"] = mm_tn(
                f"wq_grad{a}", rec["u"], dq, groups=1, a_block=full_rows, a_index=lambda j, k: (k, 0),
                b_block=full_rows, b_index=lambda j, k: (k, 0), o_block=(D_MODEL, D_MODEL), o_index=lambda j, k: (0, 0),
                o_shape=(D_MODEL, D_MODEL), acc_shape=(D_MODEL, D_MODEL))
            if a == 1:
                tok = emit("attn1", grads)
            dh, d_mix[l] = nt_rms_bwd(f"q_bwd{a}", dq, w["wq"], (None, D_MODEL, D_MODEL), lambda i: (0, 0, 0),
                                      rec["h_in"], g_mix, dh)
        if l == 2:
            dkv = dkv_combine(dkv_parts[0], dkv_parts[1], "dkv_combine")
            grads["wkv"] = mm_tn(
                "wkv_grad", u_kv, dkv, groups=1, a_block=full_rows, a_index=lambda j, k: (k, 0),
                b_block=lambda tk: (tk, 2 * KV_DIM), b_index=lambda j, k: (k, 0), o_block=(D_MODEL, 2 * KV_DIM),
                o_index=lambda j, k: (0, 0), o_shape=(D_MODEL, 2 * KV_DIM), acc_shape=(D_MODEL, 2 * KV_DIM))
            tok = emit("attn0", grads)
            dh, d_nkv = nt_rms_bwd("kv_bwd", dkv, w_kv, (D_MODEL, 2 * KV_DIM), lambda i: (0, 0), h_kv,
                                   rep["norm_kv"], dh)

    d_relb = bias_grad(dbias_parts[0], dbias_parts[1], bucket, "bias_grad")
    d_sinks = jnp.concatenate([dsinks[0][0, :N_HEADS], dsinks[1][0, :N_HEADS]])
    tail = jnp.zeros((D_MODEL,), F32)
    rep_grad = jnp.concatenate([
        jnp.concatenate(d_mix, axis=0), jnp.concatenate(d_ffn, axis=0), d_nkv, d_nfin,
        tail.at[:2 * N_HEADS].set(d_sinks)[None], tail.at[:N_BUCKETS * N_HEADS].set(d_relb.reshape(-1))[None],
        tail.at[0].set(loss[0, 0])[None], jnp.zeros((REP_ROWS - ROW_LOSS - 1, D_MODEL), F32)], axis=0)
    return dh, cp_grads, rep_grad


def _pack_conv(w_dw, b_dw, ln_g, ln_b, b_pw2, b_pw1):
    rows = [w_dw, b_dw[:, None], ln_g[:, None], ln_b[:, None], b_pw2[:, None], b_pw1.reshape(2, 2, 128),
            jnp.zeros((2, PACK_ROWS - ROW_BPW1 - 2, 128), F32)]
    return jnp.concatenate(rows, axis=1)


def _unpack_conv(p):
    return (p[:, :CONV_WIDTH], p[:, ROW_BDW], p[:, ROW_LNG], p[:, ROW_LNB], p[:, ROW_BPW2],
            p[:, ROW_BPW1:ROW_BPW1 + 2].reshape(2, 256))


def _pack_rep(norm_mix, norm_ffn, norm_kv, norm_final, sinks, rel_bias):
    tail = jnp.zeros((D_MODEL,), F32)
    return jnp.concatenate([
        norm_mix, norm_ffn, norm_kv[None], norm_final[None], tail.at[:2 * N_HEADS].set(sinks.reshape(-1))[None],
        tail.at[:N_BUCKETS * N_HEADS].set(rel_bias.reshape(-1))[None],
        jnp.zeros((REP_ROWS - ROW_RELB - 1, D_MODEL), F32)], axis=0)


def _unpack_rep(p):
    return (p[0:4], p[4:8], p[ROW_NKV], p[ROW_NFIN], p[ROW_SINK, :2 * N_HEADS].reshape(2, N_HEADS),
            p[ROW_RELB, :N_BUCKETS * N_HEADS].reshape(N_BUCKETS, N_HEADS))


def kernel(x, norm_mix, norm_ffn, conv_w_pw1, conv_b_pw1, conv_w_dw, conv_b_dw, conv_ln_g, conv_ln_b, conv_w_pw2, conv_b_pw2, norm_kv, w_kv, w_q, w_o, sinks, rel_bias, ffn_w_up, ffn_w_down, norm_final, loss_target, m_norm_mix, m_norm_ffn, m_conv_w_pw1, m_conv_b_pw1, m_conv_w_dw, m_conv_b_dw, m_conv_ln_g, m_conv_ln_b, m_conv_w_pw2, m_conv_b_pw2, m_norm_kv, m_w_kv, m_w_q, m_w_o, m_sinks, m_rel_bias, m_ffn_w_up, m_ffn_w_down, m_norm_final, v_norm_mix, v_norm_ffn, v_conv_w_pw1, v_conv_b_pw1, v_conv_w_dw, v_conv_b_dw, v_conv_ln_g, v_conv_ln_b, v_conv_w_pw2, v_conv_b_pw2, v_norm_kv, v_w_kv, v_w_q, v_w_o, v_sinks, v_rel_bias, v_ffn_w_up, v_ffn_w_down, v_norm_final):
    s = x.shape[1]
    d = D_MODEL
    rsh = d // N_DEV
    dsh = D_FF // N_DEV

    me = _lin(*_place())
    lead_slab = lambda ref, j: ref.at[j]
    rows_of = lambda rows: (lambda ref, j: ref.at[pl.ds(j * rows, rows), :])

    conv_pack = _pack_conv(conv_w_dw, conv_b_dw, conv_ln_g, conv_ln_b, conv_b_pw2, conv_b_pw1)
    ag_order = ["conv0", "ffn0", "conv1", "ffn1", "attn0", "ffn2", "attn1", "ffn3"]
    ag_land, ag_copies, ag_members = [], [], {g: [] for g in ag_order}

    def gather(group, key, shard, land_shape, at, land_slab):
        i = len(ag_land)
        ag_land.append(lax.dynamic_update_slice(lax.empty(land_shape, shard.dtype), shard, at(me)))
        own = lambda ref, j, me_, slab=land_slab: slab(ref, me_)
        ag_copies.append((ag_order.index(group), i, own, i, land_slab))
        ag_members[group].append((key, i, own, land_slab))

    cols_at0 = lambda ref, j: ref.at[0, j]
    rows_at0 = lambda rows: (lambda ref, j: ref.at[0, pl.ds(j * rows, rows), :])
    col_at = lambda m: (0, m, 0, 0)
    row_at = lambda rows: (lambda m: (0, m * rows, 0))
    for l in range(2):
        gather(f"conv{l}", "pw1", conv_w_pw1[l].astype(BF16)[None, None], (1, N_DEV, d, 256), col_at, cols_at0)
        gather(f"conv{l}", "pw2", conv_w_pw2[l].astype(BF16)[None], (1, d, d), row_at(rsh), rows_at0(rsh))
    gather("conv0", "pack", conv_pack[None], (N_DEV, 2, PACK_ROWS, 128), lambda m: (m, 0, 0, 0), lead_slab)
    gather("attn0", "wkv", w_kv.astype(BF16), (d, 2 * KV_DIM), lambda m: (m * rsh, 0), rows_of(rsh))
    for a in range(2):
        gather(f"attn{a}", "wq", w_q[a].astype(BF16)[None], (1, d, d), row_at(rsh), rows_at0(rsh))
        gather(f"attn{a}", "wo", w_o[a].astype(BF16)[None], (1, d, d), row_at(rsh), rows_at0(rsh))
    up_t, m_up_t, v_up_t = (jnp.swapaxes(a, 1, 2) for a in (ffn_w_up, m_ffn_w_up, v_ffn_w_up))
    for l in range(4):
        gather(f"ffn{l}", "up", up_t[l].astype(BF16)[None, None], (1, N_DEV, FF_CHUNK, d), col_at, cols_at0)
        gather(f"ffn{l}", "down", ffn_w_down[l].astype(BF16)[None], (1, D_FF, d), row_at(dsh), rows_at0(dsh))
    ag_sems, ag_land_thru, _ = send_start("ag_start", ag_land, ag_copies, len(ag_order))

    def weights(group, after):
        members = ag_members[group]
        lands = send_wait(f"ag_wait_{group}", [ag_land_thru[i] for _, i, _, _ in members],
                          [(n, own, n, slab) for n, (_, _, own, slab) in enumerate(members)],
                          ag_sems[ag_order.index(group)], after)
        w = {key: land for (key, _, _, _), land in zip(members, lands)}
        if "up" in w:
            w["up"] = w["up"].reshape(1, 2, 4, FF_CHUNK, d)
            w["down"] = w["down"].reshape(1, 4, FF_CHUNK, d)
        if "pack" in w:
            pack_g = w.pop("pack")
            w["cp"] = jnp.transpose(pack_g, (1, 2, 0, 3)).reshape(2, PACK_ROWS, d)
            w["b_pw1"] = pack_g[:, :, ROW_BPW1:ROW_BPW1 + 2, :].transpose(1, 0, 2, 3).reshape(2, 1, 2 * d)
            w["b_pw2"] = w["cp"][:, ROW_BPW2:ROW_BPW2 + 1, :]
        return w

    shard_shapes = {"pw1": (d, 256), "pw2": (rsh, d), "wkv": (rsh, 2 * KV_DIM), "wq": (rsh, d), "wo": (rsh, d),
                    "up": (FF_CHUNK, d), "down": (dsh, d), "cp": (2, PACK_ROWS, 128), "rep": (REP_ROWS, d)}
    n_layers = {"pw1": 2, "pw2": 2, "wkv": 1, "wq": 2, "wo": 2, "up": 4, "down": 4, "cp": 1, "rep": 1}
    by_lead = (lambda ref, j, me_: ref.at[j], lambda g: lax.dynamic_index_in_dim(g, me, 0, keepdims=False))
    by_rows = lambda rows: (lambda ref, j, me_: ref.at[pl.ds(j * rows, rows), :],
                            lambda g: lax.dynamic_slice_in_dim(g, me * rows, rows, 0))
    all_of = (lambda ref, j, me_: ref, lambda g: g)
    owned = {"pw1": by_lead, "pw2": by_rows(rsh), "wkv": by_rows(rsh), "wq": by_rows(rsh), "wo": by_rows(rsh),
             "up": by_lead, "down": by_rows(dsh), "cp": by_lead, "rep": all_of}
    parts = {}
    pending = {}

    def finish(chain, after):
        keys, bufs, copies, sems, name = pending.pop(chain)
        done = send_wait(f"rs_wait_{name}", bufs, copies, sems, after)
        parts.update(zip(keys, done[len(keys):]))

    def exchange(chain, name, layer, grads):
        keys = list(grads)
        if chain in pending:
            finish(chain, grads[keys[0]])
        lands = []
        for k in keys:
            land = parts.pop(k) if k in parts else lax.empty((N_DEV, n_layers[k]) + shard_shapes[k], grads[k].dtype)
            mine = owned[k][1](grads[k])[None, None]
            lands.append(lax.dynamic_update_slice(land, mine, (me, layer) + (0,) * len(shard_shapes[k])))
        land_at = lambda ref, i: ref.at[i, layer]
        copies = [(0, n, owned[k][0], len(keys) + n, land_at) for n, k in enumerate(keys)]
        sems, thru, token = send_start(f"rs_start_{name}", [grads[k] for k in keys] + lands, copies, 1)
        pending[chain] = (keys, thru, [c[1:] for c in copies], sems[0], name)
        return token

    def emit(group, grads):
        return exchange(group[:-1], group, int(group[-1]), grads)

    rep = {"norm_mix": norm_mix, "norm_ffn": norm_ffn, "norm_kv": norm_kv[None], "norm_final": norm_final[None],
           "sinks": sinks, "rel_bias": rel_bias}

    grad_x, cp_grads, rep_grad = local_step(x[0], loss_target[0], weights, rep, emit)

    cp_full = []
    for l in range(2):
        stats, db2, db1 = cp_grads[l]
        cp_full.append(jnp.concatenate([
            stats[:ROW_BPW2], db2, db1.reshape(N_DEV, 2, 128).transpose(1, 0, 2).reshape(2, d),
            jnp.zeros((PACK_ROWS - ROW_BPW1 - 2, d), F32)], axis=0))
    cp_send = jnp.stack(cp_full).reshape(2, PACK_ROWS, N_DEV, 128).transpose(2, 0, 1, 3)
    exchange("tail", "tail", 0, {"cp": cp_send, "rep": rep_grad})
    for chain in ("ffn", "attn", "conv", "tail"):
        finish(chain, grad_x)

    def update(key, w, m, v, name):
        p = parts[key]
        w3 = w.reshape(p.shape[1:])
        outs = adamw(w3, m.reshape(w3.shape), v.reshape(w3.shape), p, name)
        return [o.reshape(w.shape) for o in outs]

    res = {}
    res["conv_w_pw1"] = update("pw1", conv_w_pw1, m_conv_w_pw1, v_conv_w_pw1, "adam_pw1")
    res["conv_w_pw2"] = update("pw2", conv_w_pw2, m_conv_w_pw2, v_conv_w_pw2, "adam_pw2")
    res["w_kv"] = update("wkv", w_kv, m_w_kv, v_w_kv, "adam_wkv")
    res["w_q"] = update("wq", w_q, m_w_q, v_w_q, "adam_wq")
    res["w_o"] = update("wo", w_o, m_w_o, v_w_o, "adam_wo")
    res["ffn_w_up"] = [jnp.swapaxes(o, 1, 2) for o in update("up", up_t, m_up_t, v_up_t, "adam_up")]
    res["ffn_w_down"] = update("down", ffn_w_down, m_ffn_w_down, v_ffn_w_down, "adam_down")
    m_pack = _pack_conv(m_conv_w_dw, m_conv_b_dw, m_conv_ln_g, m_conv_ln_b, m_conv_b_pw2, m_conv_b_pw1)
    v_pack = _pack_conv(v_conv_w_dw, v_conv_b_dw, v_conv_ln_g, v_conv_ln_b, v_conv_b_pw2, v_conv_b_pw1)
    cp_res = adamw(conv_pack, m_pack, v_pack, parts["cp"].reshape(N_DEV, 2, PACK_ROWS, 128), "adam_conv_pack")
    rep_w = _pack_rep(norm_mix, norm_ffn, norm_kv, norm_final, sinks, rel_bias)
    rep_m = _pack_rep(m_norm_mix, m_norm_ffn, m_norm_kv, m_norm_final, m_sinks, m_rel_bias)
    rep_v = _pack_rep(v_norm_mix, v_norm_ffn, v_norm_kv, v_norm_final, v_sinks, v_rel_bias)
    rep_res = adamw(rep_w[None], rep_m[None], rep_v[None], parts["rep"], "adam_rep")
    loss = rep_res[0][0, ROW_LOSS, 0]

    outs = []
    for kind in range(4):
        cw_dw, cb_dw, cln_g, cln_b, cb_pw2, cb_pw1 = _unpack_conv(cp_res[kind])
        r_mix, r_ffn, r_nkv, r_nfin, r_sinks, r_relb = _unpack_rep(rep_res[kind][0])
        outs += [r_mix, r_ffn, res["conv_w_pw1"][kind], cb_pw1, cw_dw, cb_dw, cln_g, cln_b, res["conv_w_pw2"][kind],
                 cb_pw2, r_nkv, res["w_kv"][kind], res["w_q"][kind], res["w_o"][kind], r_sinks, r_relb,
                 res["ffn_w_up"][kind], res["ffn_w_down"][kind], r_nfin]
    return (loss, grad_x[None], *outs)
```

```python
import functools
import math

import numpy as np
import jax
import jax.numpy as jnp
from jax import lax
from jax.experimental import pallas as pl
from jax.experimental.pallas import tpu as pltpu

F32 = jnp.float32
BF16 = jnp.bfloat16

D_MODEL = 1024
D_FF = 2816
N_HEADS = 16
N_KV_HEADS = 4
GROUP = N_HEADS // N_KV_HEADS
HEAD_DIM = 64
KV_DIM = N_KV_HEADS * HEAD_DIM
BLOCK = 128
CONV_WIDTH = 31
HALO = 32
N_BUCKETS = 32
MAX_DISTANCE = 128
EPS = 1e-6
NEG_INF = -1e30
N_DEV = 8
FF_CHUNK = D_FF // 4
PACK_ROWS = 40
ROW_BDW, ROW_LNG, ROW_LNB, ROW_BPW2, ROW_BPW1 = 31, 32, 33, 34, 35
REP_ROWS = 16
ROW_NKV, ROW_NFIN, ROW_SINK, ROW_RELB, ROW_LOSS = 8, 9, 10, 11, 12

ADAM_LR, ADAM_B1, ADAM_B2, ADAM_EPS, ADAM_WD, ADAM_STEP = 0.001, 0.9, 0.999, 1e-08, 0.01, 10

VMEM_LIMIT_BYTES = 56 * 1024 * 1024
FFN_ROWS = 1024
FFN_BWD_ROWS = 1024
GRAD_ROWS = 2048
ANY = pl.BlockSpec(memory_space=pl.ANY)
MESH = pl.DeviceIdType.MESH

NN = (((1,), (0,)), ((), ()))
NT = (((1,), (1,)), ((), ()))
TN = (((0,), (0,)), ((), ()))


def _dot(a, b, dims):
    return lax.dot_general(a, b, dims, preferred_element_type=F32)


def _pcall(body, name, out_shape, *, grid=None, in_specs=None, out_specs=None, scratch=(), sem=None, **kw):
    params = pltpu.CompilerParams(dimension_semantics=sem, vmem_limit_bytes=VMEM_LIMIT_BYTES)
    extra = {} if grid is None else {"grid": grid}
    return pl.pallas_call(body, name=name, out_shape=out_shape, in_specs=in_specs, out_specs=out_specs,
                          scratch_shapes=list(scratch), compiler_params=params, **extra, **kw)


def _sds(shape, dtype):
    return jax.ShapeDtypeStruct(tuple(shape), dtype)


def _row_tile(s, want):
    return want if s % want == 0 else s


def rms_fwd(h, g, name):
    s, d = h.shape
    tm = _row_tile(s, 512)

    def body(h_ref, g_ref, u_ref):
        x = h_ref[...]
        r = lax.rsqrt(jnp.mean(x * x, axis=-1, keepdims=True) + EPS)
        u_ref[...] = (x * r * g_ref[...]).astype(BF16)

    return _pcall(body, name, _sds((s, d), BF16), grid=(s // tm,),
                  in_specs=[pl.BlockSpec((tm, d), lambda i: (i, 0)), pl.BlockSpec((1, d), lambda i: (0, 0))],
                  out_specs=pl.BlockSpec((tm, d), lambda i: (i, 0)), sem=("parallel",))(h, g)


def _rms_rows(x, gain):
    return (x * lax.rsqrt(jnp.mean(x * x, axis=-1, keepdims=True) + EPS) * gain).astype(BF16)


def _mm(name, a, b, *, dims, grid, a_spec, b_spec, o_spec, o_shape, nk=1, acc_shape=None,
        bias=None, res=None, colsum=None, sem=None, token=None, norm=None):
    n_axes = len(grid)

    def body(*refs):
        it = iter(refs)
        a_ref, b_ref = next(it), next(it)
        bias_ref = next(it) if bias is not None else None
        res_ref = next(it) if res is not None else None
        gain_ref = next(it) if norm is not None else None
        if token is not None:
            next(it)
        o_ref = next(it)
        un_ref = next(it) if norm is not None else None
        cs_ref = next(it) if colsum is not None else None
        acc_ref = next(it) if nk > 1 else None
        k = pl.program_id(n_axes - 1)
        p = _dot(a_ref[...].astype(BF16), b_ref[...].astype(BF16), dims)

        def finish(acc):
            if bias_ref is not None:
                acc = acc + bias_ref[...]
            if res_ref is not None:
                acc = acc + res_ref[...]
            o_ref[...] = acc.astype(o_ref.dtype)
            if un_ref is not None:
                un_ref[...] = _rms_rows(acc, gain_ref[...])

        if nk == 1:
            finish(p)
        else:
            @pl.when(k == 0)
            def _():
                acc_ref[...] = p

            @pl.when(k > 0)
            def _():
                acc_ref[...] += p

            @pl.when(k == nk - 1)
            def _():
                finish(acc_ref[...])

        if cs_ref is not None:
            cs = jnp.sum(b_ref[...].astype(F32), axis=0, keepdims=True)

            @pl.when(k == 0)
            def _():
                cs_ref[...] = cs

            @pl.when(k > 0)
            def _():
                cs_ref[...] += cs

    ins, in_specs = [a, b], [a_spec, b_spec]
    gain = None if norm is None else (norm, pl.BlockSpec(norm.shape, lambda *_: (0,) * norm.ndim))
    for extra in (bias, res, gain, None if token is None else (token, ANY)):
        if extra is not None:
            ins.append(extra[0])
            in_specs.append(extra[1])
    out_shape, out_specs = o_shape, o_spec
    if norm is not None:
        out_shape, out_specs = (o_shape, _sds(o_shape.shape, BF16)), (o_spec, o_spec)
    if colsum is not None:
        out_shape, out_specs = (o_shape, _sds(colsum[0], F32)), (o_spec, colsum[1])
    scratch = [pltpu.VMEM(acc_shape, F32)] if nk > 1 else []
    if sem is None:
        sem = ("parallel",) * (n_axes - 1) + ("arbitrary",)
    return _pcall(body, name, out_shape, grid=grid, in_specs=in_specs, out_specs=out_specs, scratch=scratch,
                  sem=sem)(*ins)


def mm_nn(name, a, w, w_block, w_index, n, tn, out_dtype, bias=None, res=None, tm=1024, norm=None):
    s, k = a.shape
    tm = _row_tile(s, tm)
    col = lambda i, j: (i, j)
    extras = {}
    if bias is not None:
        extras["bias"] = bias
    if res is not None:
        extras["res"] = (res, pl.BlockSpec((tm, tn), col))
    if norm is not None:
        assert tn == n, "a fused norm needs whole rows"
        extras["norm"] = norm
    return _mm(name, a, w, dims=NN, grid=(s // tm, n // tn), a_spec=pl.BlockSpec((tm, k), lambda i, j: (i, 0)),
               b_spec=pl.BlockSpec(w_block, w_index), o_spec=pl.BlockSpec((tm, tn), col), o_shape=_sds((s, n), out_dtype),
               sem=("parallel", "arbitrary"), **extras)


def mm_nt(name, a, w, w_block, w_index, kout, out_dtype, nk=1, tk=None, tm=1024, token=None):
    s, n = a.shape
    tm = _row_tile(s, tm)
    tk = n if tk is None else tk
    return _mm(name, a, w, dims=NT, grid=(s // tm, nk), a_spec=pl.BlockSpec((tm, tk), lambda i, k: (i, k)),
               b_spec=pl.BlockSpec(w_block, w_index), o_spec=pl.BlockSpec((tm, kout), lambda i, k: (i, 0)),
               o_shape=_sds((s, kout), out_dtype), nk=nk, acc_shape=(tm, kout), token=token)


def mm_tn(name, a, b, *, groups, a_block, a_index, b_block, b_index, o_block, o_index, o_shape, acc_shape,
          colsum=None, tk=GRAD_ROWS):
    s = a.shape[-2]
    tk = _row_tile(s, tk)
    return _mm(name, a, b, dims=TN, grid=(groups, s // tk), a_spec=pl.BlockSpec(a_block(tk), a_index),
               b_spec=pl.BlockSpec(b_block(tk), b_index), o_spec=pl.BlockSpec(o_block, o_index),
               o_shape=_sds(o_shape, BF16), nk=s // tk, acc_shape=acc_shape, colsum=colsum)


FFN_SUB = 512


def _sub_rows(tm):
    sub = FFN_SUB if tm % FFN_SUB == 0 else tm
    return [slice(r * sub, (r + 1) * sub) for r in range(tm // sub)]


def ffn_fwd(u, h, w_up_t, w_down, layer, name, norms=()):
    s, d = u.shape
    tm = _row_tile(s, FFN_ROWS)
    nj = 4
    nn = len(norms)

    def body(u_ref, h_ref, wup_ref, wd_ref, *rest):
        gain_refs, (hn_ref, gu_ref), un_refs = rest[:nn], rest[nn:nn + 2], rest[nn + 2:]
        j = pl.program_id(1)

        @pl.when(j == 0)
        def _():
            hn_ref[...] = h_ref[...]

        for rows in _sub_rows(tm):
            uv = u_ref[rows, :]
            g = _dot(uv, wup_ref[0], NT)
            p = _dot(uv, wup_ref[1], NT)
            gu_ref[0, rows, :] = g.astype(BF16)
            gu_ref[1, rows, :] = p.astype(BF16)
            act = (g * jax.nn.sigmoid(g) * p).astype(BF16)
            hn_ref[rows, :] += _dot(act, wd_ref[...], NN)

        if nn:
            @pl.when(j == nj - 1)
            def _():
                for rows in _sub_rows(tm):
                    for gain_ref, un_ref in zip(gain_refs, un_refs):
                        un_ref[rows, :] = _rms_rows(hn_ref[rows, :], gain_ref[...])

    row = pl.BlockSpec((tm, d), lambda i, j: (i, 0))
    vec = pl.BlockSpec((1, d), lambda i, j: (0, 0))
    return _pcall(
        body, name, (_sds((s, d), F32), _sds((2, nj, s, FF_CHUNK), BF16)) + (_sds((s, d), BF16),) * nn,
        grid=(s // tm, nj),
        in_specs=[row, row,
                  pl.BlockSpec((None, 2, None, FF_CHUNK, d), lambda i, j: (layer, 0, j, 0, 0)),
                  pl.BlockSpec((None, None, FF_CHUNK, d), lambda i, j: (layer, j, 0, 0))] + [vec] * nn,
        out_specs=(row, pl.BlockSpec((2, None, tm, FF_CHUNK), lambda i, j: (0, j, i, 0))) + (row,) * nn,
        sem=("parallel", "arbitrary"))(u, h, w_up_t, w_down, *norms)


def _rms_bwd_rows(x, gain, du, dh_in):
    r = lax.rsqrt(jnp.mean(x * x, axis=-1, keepdims=True) + EPS)
    xh = x * r
    dxh = du * gain
    dx = r * (dxh - xh * jnp.mean(dxh * xh, axis=-1, keepdims=True))
    return dh_in + dx, jnp.sum(du * xh, axis=0, keepdims=True)


def ffn_bwd(dh, gu, w_up_t, w_down, layer, h_mid, gain, name, token=None):
    s, d = dh.shape
    tm = _row_tile(s, FFN_BWD_ROWS)
    nj = 4

    def body(dh_ref, gu_ref, wup_ref, wd_ref, h_ref, gain_ref, *rest):
        dho_ref, dgain_ref, act_ref, dgu_ref, du_ref = rest[-5:]
        i, j = pl.program_id(0), pl.program_id(1)

        @pl.when(j == 0)
        def _():
            du_ref[...] = jnp.zeros_like(du_ref)

        @pl.when((i == 0) & (j == 0))
        def _():
            dgain_ref[...] = jnp.zeros_like(dgain_ref)

        for rows in _sub_rows(tm):
            dact = _dot(dh_ref[rows, :].astype(BF16), wd_ref[...], NT)
            g = gu_ref[0, rows, :].astype(F32)
            p = gu_ref[1, rows, :].astype(F32)
            sig = jax.nn.sigmoid(g)
            sl = g * sig
            act_ref[rows, :] = (sl * p).astype(BF16)
            dp = (dact * sl).astype(BF16)
            dg = (dact * p * (sig * (1.0 + g * (1.0 - sig)))).astype(BF16)
            dgu_ref[0, rows, :] = dg
            dgu_ref[1, rows, :] = dp
            du_ref[rows, :] += _dot(dg, wup_ref[0], NN) + _dot(dp, wup_ref[1], NN)

        @pl.when(j == nj - 1)
        def _():
            for rows in _sub_rows(tm):
                dho, part = _rms_bwd_rows(h_ref[rows, :], gain_ref[...], du_ref[rows, :], dh_ref[rows, :])
                dho_ref[rows, :] = dho
                dgain_ref[...] += part

    row = pl.BlockSpec((tm, d), lambda i, j: (i, 0))
    row_once = pl.BlockSpec((tm, d), lambda i, j: (i, 0), pipeline_mode=pl.Buffered(1))
    vec = pl.BlockSpec((1, d), lambda i, j: (0, 0))
    gu_spec = pl.BlockSpec((2, None, tm, FF_CHUNK), lambda i, j: (0, j, i, 0))
    extra = [] if token is None else [token]
    return _pcall(
        body, name,
        (_sds((s, d), F32), _sds((1, d), F32), _sds((nj, s, FF_CHUNK), BF16), _sds((2, nj, s, FF_CHUNK), BF16)),
        grid=(s // tm, nj),
        in_specs=[row_once, gu_spec,
                  pl.BlockSpec((None, 2, None, FF_CHUNK, d), lambda i, j: (layer, 0, j, 0, 0)),
                  pl.BlockSpec((None, None, FF_CHUNK, d), lambda i, j: (layer, j, 0, 0)), row_once, vec]
        + [ANY] * len(extra),
        out_specs=(row, vec, pl.BlockSpec((None, tm, FF_CHUNK), lambda i, j: (j, i, 0)), gu_spec),
        scratch=[pltpu.VMEM((tm, d), F32)],
        sem=("arbitrary", "arbitrary"))(dh, gu, w_up_t, w_down, h_mid, gain, *extra)


def nt_rms_bwd(name, a, w, w_block, w_index, h, gain, dh_in):
    s, n = a.shape
    d = h.shape[1]
    tm = _row_tile(s, 512)

    def body(a_ref, w_ref, h_ref, gain_ref, dhi_ref, dho_ref, dgain_ref):
        i = pl.program_id(0)

        @pl.when(i == 0)
        def _():
            dgain_ref[...] = jnp.zeros_like(dgain_ref)

        du = _dot(a_ref[...].astype(BF16), w_ref[...], NT)
        dho, part = _rms_bwd_rows(h_ref[...], gain_ref[...], du, dhi_ref[...])
        dho_ref[...] = dho
        dgain_ref[...] += part

    row = pl.BlockSpec((tm, d), lambda i: (i, 0))
    vec = pl.BlockSpec((1, d), lambda i: (0, 0))
    return _pcall(body, name, (_sds((s, d), F32), _sds((1, d), F32)), grid=(s // tm,),
                  in_specs=[pl.BlockSpec((tm, n), lambda i: (i, 0)), pl.BlockSpec(w_block, w_index), row, vec, row],
                  out_specs=(row, vec), sem=("arbitrary",))(a, w, h, gain, dh_in)


def pw1_fwd(u, w, b, layer, name):
    s, d = u.shape
    tm = _row_tile(s, FFN_ROWS)
    nb, wb = w.shape[1], w.shape[3]

    def body(u_ref, w_ref, b_ref, t_ref):
        for rows in _sub_rows(tm):
            uv = u_ref[rows, :]
            for j in range(nb):
                cols = slice(j * wb, (j + 1) * wb)
                t_ref[rows, cols] = (_dot(uv, w_ref[j], NN) + b_ref[:, cols]).astype(BF16)

    return _pcall(
        body, name, _sds((s, nb * wb), BF16), grid=(s // tm,),
        in_specs=[pl.BlockSpec((tm, d), lambda i: (i, 0)), pl.BlockSpec((None, nb, d, wb), lambda i: (0, 0, 0, 0)),
                  pl.BlockSpec((None, 1, nb * wb), lambda i: (layer, 0, 0))],
        out_specs=pl.BlockSpec((tm, nb * wb), lambda i: (i, 0)), sem=("parallel",))(u, w, b)


def pw1_bwd(dt, w, h, gain, dh_in, name):
    s = dt.shape[0]
    nb, d, wb = w.shape[1], w.shape[2], w.shape[3]
    tm = _row_tile(s, 512)

    def body(dt_ref, w_ref, h_ref, gain_ref, dhi_ref, dho_ref, dgain_ref):
        i = pl.program_id(0)

        @pl.when(i == 0)
        def _():
            dgain_ref[...] = jnp.zeros_like(dgain_ref)

        for rows in _sub_rows(tm):
            du = _dot(dt_ref[rows, 0:wb], w_ref[0], NT)
            for j in range(1, nb):
                du = du + _dot(dt_ref[rows, j * wb:(j + 1) * wb], w_ref[j], NT)
            dho, part = _rms_bwd_rows(h_ref[rows, :], gain_ref[...], du, dhi_ref[rows, :])
            dho_ref[rows, :] = dho
            dgain_ref[...] += part

    row = pl.BlockSpec((tm, d), lambda i: (i, 0))
    vec = pl.BlockSpec((1, d), lambda i: (0, 0))
    return _pcall(
        body, name, (_sds((s, d), F32), _sds((1, d), F32)), grid=(s // tm,),
        in_specs=[pl.BlockSpec((tm, nb * wb), lambda i: (i, 0)),
                  pl.BlockSpec((None, nb, d, wb), lambda i: (0, 0, 0, 0)), row, vec, row],
        out_specs=(row, vec), sem=("arbitrary",))(dt, w, h, gain, dh_in)


def pw1_grad(u, dt, name):
    s, d = u.shape
    n = dt.shape[1]
    nb = N_DEV
    wb = n // nb
    tk = _row_tile(s, 1024)
    nk = s // tk

    def body(u_ref, dt_ref, g_ref, db_ref, acc_ref):
        k = pl.program_id(0)
        p = _dot(u_ref[...], dt_ref[...], TN)
        cs = jnp.sum(dt_ref[...].astype(F32), axis=0, keepdims=True)

        @pl.when(k == 0)
        def _():
            acc_ref[...] = p
            db_ref[...] = cs

        @pl.when(k > 0)
        def _():
            acc_ref[...] += p
            db_ref[...] += cs

        @pl.when(k == nk - 1)
        def _():
            for j in range(nb):
                g_ref[j] = acc_ref[:, j * wb:(j + 1) * wb].astype(BF16)

    return _pcall(
        body, name, (_sds((nb, d, wb), BF16), _sds((1, n), F32)), grid=(nk,),
        in_specs=[pl.BlockSpec((tk, d), lambda k: (k, 0)), pl.BlockSpec((tk, n), lambda k: (k, 0))],
        out_specs=(pl.BlockSpec((nb, d, wb), lambda k: (0, 0, 0)), pl.BlockSpec((1, n), lambda k: (0, 0))),
        scratch=[pltpu.VMEM((d, n), F32)], sem=("arbitrary",))(u, dt)


def _glu(t):
    t = t.astype(F32)
    return t[:, :D_MODEL] * jax.nn.sigmoid(t[:, D_MODEL:])


def _conv_tile(s):
    return 256 if s % 256 == 0 else s


CONV_ROWS = 32
CONV_LANES = 512
SUBLANES = 8


def _shifted_copies(sh_ref, rows):
    for b in range(1, SUBLANES):
        sh_ref[b, 0:rows - SUBLANES, :] = sh_ref[0, b:b + rows - SUBLANES, :]


def conv_fwd(t, cp, name):
    s = t.shape[0]
    d = D_MODEL
    ts = _conv_tile(s)
    per = ts // HALO
    rows = HALO + ts
    lead = HALO - (CONV_WIDTH - 1)
    rc = CONV_ROWS

    def body(t_ref, tp_ref, cp_ref, z_ref, y_ref, sh_ref):
        i = pl.program_id(0)
        sh_ref[0, 0:HALO, :] = jnp.where(i > 0, _glu(tp_ref[...]), 0.0)
        sh_ref[0, HALO:rows, :] = _glu(t_ref[...])
        _shifted_copies(sh_ref, rows)

        def chunk(c, carry):
            r0 = pl.multiple_of(c * rc, rc)
            for lc in range(d // CONV_LANES):
                ln = slice(lc * CONV_LANES, (lc + 1) * CONV_LANES)
                acc = jnp.zeros((rc, CONV_LANES), F32) + cp_ref[ROW_BDW:ROW_BDW + 1, ln]
                for k in range(CONV_WIDTH):
                    a8, b = divmod(lead + k, SUBLANES)
                    acc = acc + cp_ref[k:k + 1, ln] * sh_ref[b, pl.ds(r0 + SUBLANES * a8, rc), ln]
                y_ref[pl.ds(r0, rc), ln] = acc
            y = y_ref[pl.ds(r0, rc), :]
            mu = jnp.mean(y, axis=-1, keepdims=True)
            yc = y - mu
            rstd = lax.rsqrt(jnp.mean(yc * yc, axis=-1, keepdims=True) + EPS)
            yn = yc * rstd * cp_ref[ROW_LNG:ROW_LNG + 1, :] + cp_ref[ROW_LNB:ROW_LNB + 1, :]
            z_ref[pl.ds(r0, rc), :] = (yn * jax.nn.sigmoid(yn)).astype(BF16)
            return carry

        lax.fori_loop(0, ts // rc, chunk, 0)

    row = pl.BlockSpec((ts, d), lambda i: (i, 0))
    return _pcall(
        body, name, (_sds((s, d), BF16), _sds((s, d), F32)), grid=(s // ts,),
        in_specs=[pl.BlockSpec((ts, 2 * d), lambda i: (i, 0)),
                  pl.BlockSpec((HALO, 2 * d), lambda i: (jnp.maximum(i * per - 1, 0), 0)),
                  pl.BlockSpec((PACK_ROWS, d), lambda i: (0, 0))],
        out_specs=(row, row),
        scratch=[pltpu.VMEM((SUBLANES, rows, d), F32)], sem=("parallel",))(t, t, cp)


def conv_bwd(t, y, dz, cp, name):
    s = t.shape[0]
    d = D_MODEL
    ts = _conv_tile(s)
    per = ts // HALO
    nt = s // ts
    te = ts + HALO
    rc = CONV_ROWS

    def body(t_ref, y_ref, yn_ref, dz_ref, dzn_ref, cp_ref, dt_ref, st_ref, shd_ref, dw_ref):
        i = pl.program_id(0)
        last = i == nt - 1

        @pl.when(i == 0)
        def _():
            st_ref[...] = jnp.zeros_like(st_ref)
            dw_ref[...] = jnp.zeros_like(dw_ref)

        gain = cp_ref[ROW_LNG:ROW_LNG + 1, :]

        def ln_bwd(yv, dzv):
            mu = jnp.mean(yv, axis=-1, keepdims=True)
            yc = yv - mu
            rstd = lax.rsqrt(jnp.mean(yc * yc, axis=-1, keepdims=True) + EPS)
            yh = yc * rstd
            yn = yh * gain + cp_ref[ROW_LNB:ROW_LNB + 1, :]
            sig = jax.nn.sigmoid(yn)
            dyn = dzv * (sig * (1.0 + yn * (1.0 - sig)))
            dyh = dyn * gain
            dy = rstd * (dyh - jnp.mean(dyh, axis=-1, keepdims=True)
                         - yh * jnp.mean(dyh * yh, axis=-1, keepdims=True))
            return dy, dyn, yh

        def norm_chunk(c, carry):
            r0 = pl.multiple_of(c * rc, rc)
            dy, dyn, yh = ln_bwd(y_ref[pl.ds(r0, rc), :], dz_ref[pl.ds(r0, rc), :])
            shd_ref[0, pl.ds(r0, rc), :] = dy
            st_ref[ROW_BDW:ROW_BDW + 1, :] += jnp.sum(dy, axis=0, keepdims=True)
            st_ref[ROW_LNG:ROW_LNG + 1, :] += jnp.sum(dyn * yh, axis=0, keepdims=True)
            st_ref[ROW_LNB:ROW_LNB + 1, :] += jnp.sum(dyn, axis=0, keepdims=True)
            return carry

        lax.fori_loop(0, ts // rc, norm_chunk, 0)
        dy_halo, _, _ = ln_bwd(yn_ref[...], jnp.where(last, 0.0, dzn_ref[...]))
        shd_ref[0, ts:te, :] = dy_halo
        _shifted_copies(shd_ref, te)

        def tap_chunk(c, carry):
            r0 = pl.multiple_of(c * rc, rc)
            for lc in range(d // CONV_LANES):
                ln = slice(lc * CONV_LANES, (lc + 1) * CONV_LANES)
                ln2 = slice(d + lc * CONV_LANES, d + (lc + 1) * CONV_LANES)
                t1 = t_ref[pl.ds(r0, rc), ln].astype(F32)
                sg = jax.nn.sigmoid(t_ref[pl.ds(r0, rc), ln2].astype(F32))
                a = t1 * sg
                da = jnp.zeros((rc, CONV_LANES), F32)
                for k in range(CONV_WIDTH):
                    a8, b = divmod(CONV_WIDTH - 1 - k, SUBLANES)
                    e = shd_ref[b, pl.ds(r0 + SUBLANES * a8, rc), ln]
                    da = da + cp_ref[k:k + 1, ln] * e
                    dw_ref[k, :, ln] += jnp.sum((a * e).reshape(rc // SUBLANES, SUBLANES, CONV_LANES), axis=0)
                dt_ref[pl.ds(r0, rc), ln] = (da * sg).astype(BF16)
                dt_ref[pl.ds(r0, rc), ln2] = (da * t1 * sg * (1.0 - sg)).astype(BF16)
            return carry

        lax.fori_loop(0, ts // rc, tap_chunk, 0)

        @pl.when(last)
        def _():
            for k in range(CONV_WIDTH):
                st_ref[k:k + 1, :] = jnp.sum(dw_ref[k], axis=0, keepdims=True)

    last_halo = s // HALO - 1
    row = pl.BlockSpec((ts, d), lambda i: (i, 0))
    halo = pl.BlockSpec((HALO, d), lambda i: (jnp.minimum((i + 1) * per, last_halo), 0))
    return _pcall(
        body, name, (_sds((s, 2 * d), BF16), _sds((PACK_ROWS, d), F32)), grid=(nt,),
        in_specs=[pl.BlockSpec((ts, 2 * d), lambda i: (i, 0)), row, halo, row, halo,
                  pl.BlockSpec((PACK_ROWS, d), lambda i: (0, 0))],
        out_specs=(pl.BlockSpec((ts, 2 * d), lambda i: (i, 0)), pl.BlockSpec((PACK_ROWS, d), lambda i: (0, 0))),
        scratch=[pltpu.VMEM((SUBLANES, te, d), F32), pltpu.VMEM((CONV_WIDTH, SUBLANES, d), F32)],
        sem=("arbitrary",))(t, y, y, dz, dz, cp)


def _bucket_table():
    qi = np.arange(BLOCK, dtype=np.int64)[:, None]
    kj = np.arange(2 * BLOCK, dtype=np.int64)[None, :]
    dist = qi + BLOCK - kj
    max_exact = N_BUCKETS // 2
    dd = np.maximum(dist, 0)
    ratio = (np.maximum(dd, 1).astype(np.float32) / np.float32(max_exact)).astype(np.float32)
    log_ratio = (np.log(ratio).astype(np.float32) / np.float32(math.log(MAX_DISTANCE / max_exact))).astype(np.float32)
    large = max_exact + (log_ratio * np.float32(N_BUCKETS - max_exact)).astype(np.int32)
    large = np.minimum(large, N_BUCKETS - 1)
    bucket = np.where(dd < max_exact, dd, large)
    return np.where((dist >= 0) & (dist < BLOCK), bucket, -1).astype(np.int32)


def bias_table(rel_bias, bucket, name):
    def body(rb_ref, bk_ref, o_ref):
        bk = bk_ref[...]
        for h in range(N_HEADS):
            acc = jnp.full((BLOCK, 2 * BLOCK), NEG_INF, F32)
            for b in range(N_BUCKETS):
                acc = jnp.where(bk == b, rb_ref[b, h], acc)
            o_ref[h] = acc

    return _pcall(body, name, _sds((N_HEADS, BLOCK, 2 * BLOCK), F32),
                  in_specs=[pl.BlockSpec(memory_space=pltpu.SMEM), pl.BlockSpec(memory_space=pltpu.VMEM)],
                  out_specs=pl.BlockSpec(memory_space=pltpu.VMEM))(rel_bias, bucket)


def bias_grad(dba, dbb, bucket, name):
    def body(a_ref, b_ref, bk_ref, o_ref):
        bk = bk_ref[...]
        for h in range(N_HEADS):
            db = a_ref[h] + b_ref[h]
            for b in range(N_BUCKETS):
                o_ref[b, h] = jnp.sum(jnp.where(bk == b, db, 0.0))

    vm = pl.BlockSpec(memory_space=pltpu.VMEM)
    return _pcall(body, name, _sds((N_BUCKETS, N_HEADS), F32), in_specs=[vm, vm, vm],
                  out_specs=pl.BlockSpec(memory_space=pltpu.SMEM))(dba, dbb, bucket)


def _band_specs():
    cur = pl.BlockSpec((BLOCK, 2 * KV_DIM), lambda n: (n, 0))
    prev = pl.BlockSpec((BLOCK, 2 * KV_DIM), lambda n: (jnp.maximum(n - 1, 0), 0))
    return cur, prev


def _scores(q_h, k_h, bias_h, first_row, sink):
    sc = _dot(q_h, k_h, NT) * (HEAD_DIM ** -0.5) + bias_h + first_row
    m = jnp.maximum(jnp.max(sc, axis=-1, keepdims=True), sink)
    p = jnp.exp(sc - m)
    e_sink = jnp.exp(sink - m)
    den = jnp.sum(p, axis=-1, keepdims=True) + e_sink
    return p, e_sink, den


def _first_block_row(n):
    col = lax.broadcasted_iota(jnp.int32, (1, 2 * BLOCK), 1)
    return jnp.where((col < BLOCK) & (n == 0), NEG_INF, 0.0)


def _head_lanes(hk, g):
    h = hk * GROUP + g
    return slice(h * HEAD_DIM, (h + 1) * HEAD_DIM)


def _group_rows(x_ref, hk):
    return jnp.concatenate([x_ref[:, _head_lanes(hk, g)] for g in range(GROUP)], axis=0)


def _group_bias(bias_ref, hk):
    return bias_ref[hk * GROUP:(hk + 1) * GROUP].reshape(GROUP * BLOCK, 2 * BLOCK)


def _group_sinks(sink_ref, hk):
    head = lax.broadcasted_iota(jnp.int32, (GROUP * BLOCK, 1), 0) // BLOCK
    col = jnp.zeros((GROUP * BLOCK, 1), F32) + sink_ref[0, hk * GROUP]
    for g in range(1, GROUP):
        col = jnp.where(head == g, sink_ref[0, hk * GROUP + g], col)
    return col


def attn_fwd(q, kv, bias, sinks, name):
    s = q.shape[0]
    nb = s // BLOCK

    def body(sink_ref, q_ref, kvc_ref, kvp_ref, bias_ref, o_ref, band_ref):
        n = pl.program_id(0)
        band_ref[0:BLOCK, :] = kvp_ref[...]
        band_ref[BLOCK:2 * BLOCK, :] = kvc_ref[...]
        first_row = _first_block_row(n)
        for hk in range(N_KV_HEADS):
            k_h = band_ref[:, hk * HEAD_DIM:(hk + 1) * HEAD_DIM]
            v_h = band_ref[:, KV_DIM + hk * HEAD_DIM:KV_DIM + (hk + 1) * HEAD_DIM]
            p, _, den = _scores(_group_rows(q_ref, hk), k_h, _group_bias(bias_ref, hk), first_row,
                                _group_sinks(sink_ref, hk))
            o = _dot((p * (1.0 / den)).astype(BF16), v_h, NN).astype(BF16)
            for g in range(GROUP):
                o_ref[:, _head_lanes(hk, g)] = o[g * BLOCK:(g + 1) * BLOCK]

    cur, prev = _band_specs()
    qs = pl.BlockSpec((BLOCK, D_MODEL), lambda n: (n, 0))
    return _pcall(
        body, name, _sds((s, D_MODEL), BF16), grid=(nb,),
        in_specs=[pl.BlockSpec(memory_space=pltpu.SMEM), qs, cur, prev,
                  pl.BlockSpec((N_HEADS, BLOCK, 2 * BLOCK), lambda n: (0, 0, 0))],
        out_specs=qs, scratch=[pltpu.VMEM((2 * BLOCK, 2 * KV_DIM), BF16)],
        sem=("parallel",))(sinks, q, kv, kv, bias)


def attn_bwd(q, kv, do, bias, sinks, name):
    s = q.shape[0]
    nb = s // BLOCK
    scale = HEAD_DIM ** -0.5

    def body(sink_ref, q_ref, do_ref, kvc_ref, kvp_ref, bias_ref, dq_ref, dkv_ref, db_ref, dsink_ref,
             band_ref, dsacc_ref):
        n = pl.program_id(0)
        band_ref[0:BLOCK, :] = kvp_ref[...]
        band_ref[BLOCK:2 * BLOCK, :] = kvc_ref[...]
        first_row = _first_block_row(n)
        lane = lax.broadcasted_iota(jnp.int32, (BLOCK, BLOCK), 1)

        @pl.when(n == 0)
        def _():
            db_ref[...] = jnp.zeros_like(db_ref)
            dsacc_ref[...] = jnp.zeros_like(dsacc_ref)

        for hk in range(N_KV_HEADS):
            k_h = band_ref[:, hk * HEAD_DIM:(hk + 1) * HEAD_DIM]
            v_h = band_ref[:, KV_DIM + hk * HEAD_DIM:KV_DIM + (hk + 1) * HEAD_DIM]
            q_g = _group_rows(q_ref, hk)
            do_g = _group_rows(do_ref, hk)
            p, e_sink, den = _scores(q_g, k_h, _group_bias(bias_ref, hk), first_row, _group_sinks(sink_ref, hk))
            inv = 1.0 / den
            p = p * inv
            dp = _dot(do_g, v_h, NT)
            delta = jnp.sum(p * dp, axis=-1, keepdims=True)
            ds = p * (dp - delta)
            db_ref[hk * GROUP:(hk + 1) * GROUP] += ds.reshape(GROUP, BLOCK, 2 * BLOCK)
            d_sink = -(e_sink * inv) * delta
            for g in range(GROUP):
                dsacc_ref[...] += jnp.where(lane == hk * GROUP + g, d_sink[g * BLOCK:(g + 1) * BLOCK], 0.0)
            dsb = ds.astype(BF16)
            dq = (_dot(dsb, k_h, NN) * scale).astype(BF16)
            for g in range(GROUP):
                dq_ref[:, _head_lanes(hk, g)] = dq[g * BLOCK:(g + 1) * BLOCK]
            dkv_ref[:, hk * HEAD_DIM:(hk + 1) * HEAD_DIM] = _dot(dsb, q_g, TN) * scale
            dkv_ref[:, KV_DIM + hk * HEAD_DIM:KV_DIM + (hk + 1) * HEAD_DIM] = _dot(p.astype(BF16), do_g, TN)

        @pl.when(n == nb - 1)
        def _():
            dsink_ref[...] = jnp.sum(dsacc_ref[...], axis=0, keepdims=True)

    cur, prev = _band_specs()
    qs = pl.BlockSpec((BLOCK, D_MODEL), lambda n: (n, 0))
    full_b = pl.BlockSpec((N_HEADS, BLOCK, 2 * BLOCK), lambda n: (0, 0, 0))
    return _pcall(
        body, name,
        (_sds((s, D_MODEL), BF16), _sds((nb, 2 * BLOCK, 2 * KV_DIM), F32),
         _sds((N_HEADS, BLOCK, 2 * BLOCK), F32), _sds((1, BLOCK), F32)),
        grid=(nb,),
        in_specs=[pl.BlockSpec(memory_space=pltpu.SMEM), qs, qs, cur, prev, full_b],
        out_specs=(qs, pl.BlockSpec((None, 2 * BLOCK, 2 * KV_DIM), lambda n: (n, 0, 0)), full_b,
                   pl.BlockSpec((1, BLOCK), lambda n: (0, 0))),
        scratch=[pltpu.VMEM((2 * BLOCK, 2 * KV_DIM), BF16), pltpu.VMEM((BLOCK, BLOCK), F32)],
        sem=("arbitrary",))(sinks, q, do, kv, kv, bias)


def dkv_combine(pa, pb, name):
    nb = pa.shape[0]
    pa2 = pa.reshape(2 * nb, BLOCK, 2 * KV_DIM)
    pb2 = pb.reshape(2 * nb, BLOCK, 2 * KV_DIM)

    def body(ac_ref, an_ref, bc_ref, bn_ref, o_ref):
        n = pl.program_id(0)
        nxt = jnp.where(n == nb - 1, 0.0, an_ref[...] + bn_ref[...])
        o_ref[...] = (ac_ref[...] + bc_ref[...] + nxt).astype(BF16)

    cur = pl.BlockSpec((None, BLOCK, 2 * KV_DIM), lambda n: (2 * n + 1, 0, 0))
    nxt = pl.BlockSpec((None, BLOCK, 2 * KV_DIM), lambda n: (jnp.minimum(2 * n + 2, 2 * nb - 2), 0, 0))
    return _pcall(body, name, _sds((nb * BLOCK, 2 * KV_DIM), BF16), grid=(nb,),
                  in_specs=[cur, nxt, cur, nxt], out_specs=pl.BlockSpec((BLOCK, 2 * KV_DIM), lambda n: (n, 0)),
                  sem=("parallel",))(pa2, pa2, pb2, pb2)


def loss_head(h, g, target, name):
    s, d = h.shape
    tm = _row_tile(s, 512)

    def body(h_ref, g_ref, t_ref, dh_ref, dg_ref, loss_ref):
        i = pl.program_id(0)
        x = h_ref[...]
        r = lax.rsqrt(jnp.mean(x * x, axis=-1, keepdims=True) + EPS)
        xh = x * r
        gv = g_ref[...]
        err = xh * gv - t_ref[...]
        part_loss = jnp.zeros((1, BLOCK), F32) + 0.5 * jnp.sum(jnp.mean(err * err, axis=-1, keepdims=True))
        dy = err * (1.0 / d)
        dxh = dy * gv
        dh_ref[...] = r * (dxh - xh * jnp.mean(dxh * xh, axis=-1, keepdims=True))
        part_g = jnp.sum(dy * xh, axis=0, keepdims=True)

        @pl.when(i == 0)
        def _():
            dg_ref[...] = part_g
            loss_ref[...] = part_loss

        @pl.when(i > 0)
        def _():
            dg_ref[...] += part_g
            loss_ref[...] += part_loss

    row = pl.BlockSpec((tm, d), lambda i: (i, 0))
    vec = pl.BlockSpec((1, d), lambda i: (0, 0))
    return _pcall(body, name, (_sds((s, d), F32), _sds((1, d), F32), _sds((1, BLOCK), F32)), grid=(s // tm,),
                  in_specs=[row, vec, row], out_specs=(row, vec, pl.BlockSpec((1, BLOCK), lambda i: (0, 0))),
                  sem=("arbitrary",))(h, g, target)


def adamw(w, m, v, parts, name):
    nl, r, c = w.shape
    tr = max(t for t in range(1, min(r, 512) + 1) if r % t == 0 and (t % 16 == 0 or t == r))
    c1 = 1.0 / (1.0 - ADAM_B1 ** ADAM_STEP)
    c2 = 1.0 / (1.0 - ADAM_B2 ** ADAM_STEP)

    def body(w_ref, m_ref, v_ref, p_ref, g_ref, d_ref, nm_ref, nv_ref):
        g = p_ref[0].astype(F32)
        for dev in range(1, N_DEV):
            g = g + p_ref[dev].astype(F32)
        nm = ADAM_B1 * m_ref[...] + (1.0 - ADAM_B1) * g
        nv = ADAM_B2 * v_ref[...] + (1.0 - ADAM_B2) * (g * g)
        g_ref[...] = g
        nm_ref[...] = nm
        nv_ref[...] = nv
        d_ref[...] = -ADAM_LR * ((nm * c1) / (jnp.sqrt(nv * c2) + ADAM_EPS) + ADAM_WD * w_ref[...])

    blk = pl.BlockSpec((None, tr, c), lambda l, i: (l, i, 0))
    out = _sds((nl, r, c), F32)
    return _pcall(body, name, (out, out, out, out), grid=(nl, r // tr),
                  in_specs=[blk, blk, blk, pl.BlockSpec((N_DEV, None, tr, c), lambda l, i: (0, l, i, 0))],
                  out_specs=(blk, blk, blk, blk), sem=("parallel", "parallel"))(w, m, v, parts)


def _place():
    x, y, c = lax.axis_index("x"), lax.axis_index("y"), lax.axis_index("c")
    return x, y, c


def _lin(px, py, pc):
    return 4 * px + 2 * py + pc


HBM = pl.BlockSpec(memory_space=pltpu.HBM)
SEM = pl.BlockSpec(memory_space=pltpu.SEMAPHORE)
EFFECT = pltpu.SideEffectType.DATAFLOW_SIDE_EFFECTING
N_PEERS = N_DEV - 1


def _peers_of(x, y, c):
    return [(x, y, 1 - c), (1 - x, y, c), (x, 1 - y, c), (1 - x, 1 - y, c),
            (1 - x, y, 1 - c), (x, 1 - y, 1 - c), (1 - x, 1 - y, 1 - c)]


def _in_hbm(a):
    return pltpu.with_memory_space_constraint(a, pltpu.HBM)


def send_start(name, bufs, copies, n_groups):
    nb = len(bufs)
    per_group = [[i for i, cp in enumerate(copies) if cp[0] == g] for g in range(n_groups)]

    def body(*refs):
        buf = refs[:nb]
        sems = refs[nb:nb + 2 * n_groups]
        token = refs[2 * nb + 2 * n_groups]
        x, y, c = _place()
        me = _lin(x, y, c)
        for g in range(n_groups):
            for slot, i in enumerate(per_group[g]):
                _, s, src_slab, d, land_slab = copies[i]
                for k, peer in enumerate(_peers_of(x, y, c)):
                    pltpu.make_async_remote_copy(
                        src_ref=src_slab(buf[s], _lin(*peer), me), dst_ref=land_slab(buf[d], me),
                        send_sem=sems[2 * g].at[slot * N_PEERS + k], recv_sem=sems[2 * g + 1].at[slot * N_PEERS + k],
                        device_id=peer, device_id_type=MESH).start()
        token[...] = jnp.zeros_like(token)

    sem_shapes = []
    for g in range(n_groups):
        sem_shapes += [pltpu.SemaphoreType.DMA((len(per_group[g]) * N_PEERS,))] * 2
    out = pl.pallas_call(
        body, name=name,
        out_shape=tuple(sem_shapes) + tuple(pltpu.HBM(b.shape, b.dtype) for b in bufs) + (_sds((8, 128), F32),),
        in_specs=[HBM] * nb,
        out_specs=tuple([SEM] * len(sem_shapes)) + tuple([HBM] * nb) + (pl.BlockSpec(memory_space=pltpu.VMEM),),
        input_output_aliases={i: len(sem_shapes) + i for i in range(nb)},
        compiler_params=pltpu.CompilerParams(has_side_effects=EFFECT))(*[_in_hbm(b) for b in bufs])
    sems = [(out[2 * g], out[2 * g + 1]) for g in range(n_groups)]
    return sems, list(out[2 * n_groups:2 * n_groups + nb]), out[2 * n_groups + nb]


def send_wait(name, bufs, copies, sems, after):
    nb = len(bufs)

    def body(*refs):
        buf = refs[:nb]
        send_sems, recv_sems = refs[nb], refs[nb + 1]
        x, y, c = _place()
        me = _lin(x, y, c)
        for slot, (s, src_slab, d, land_slab) in enumerate(copies):
            for k, peer in enumerate(_peers_of(x, y, c)):
                j = _lin(*peer)
                cp = pltpu.make_async_remote_copy(
                    src_ref=src_slab(buf[s], j, me), dst_ref=land_slab(buf[d], j),
                    send_sem=send_sems.at[slot * N_PEERS + k], recv_sem=recv_sems.at[slot * N_PEERS + k],
                    device_id=peer, device_id_type=MESH)
                cp.wait_send()
                cp.wait_recv()

    out = pl.pallas_call(
        body, name=name, out_shape=tuple(pltpu.HBM(b.shape, b.dtype) for b in bufs),
        in_specs=[HBM] * nb + [SEM, SEM, ANY], out_specs=tuple([HBM] * nb),
        input_output_aliases={i: i for i in range(nb)},
        compiler_params=pltpu.CompilerParams(has_side_effects=EFFECT))(*bufs, sems[0], sems[1], after)
    return list(out)


def local_step(x, target, weights, rep, emit):
    s = x.shape[0]
    bucket = jnp.asarray(_bucket_table())
    bias = bias_table(rep["rel_bias"], bucket, "bias_table")
    h = x
    saved = []
    kv = None
    h_kv = u_kv = None
    small = None
    u = rms_fwd(h, rep["norm_mix"][0:1], "rms_mix_fwd0")
    for l in range(4):
        g_ffn = rep["norm_ffn"][l:l + 1]
        rec = {"h_in": h, "u": u}
        if l < 2:
            w = weights(f"conv{l}", u)
            if l == 0:
                small = w
            cp = small["cp"][l]
            t = pw1_fwd(u, w["pw1"], small["b_pw1"], l, f"pw1_fwd{l}")
            z, y = conv_fwd(t, cp, f"conv_fwd{l}")
            h, uf = mm_nn(f"pw2_fwd{l}", z, w["pw2"], (None, D_MODEL, D_MODEL), lambda i, j: (0, 0, j), D_MODEL,
                          D_MODEL, F32, res=h, norm=g_ffn,
                          bias=(small["b_pw2"], pl.BlockSpec((None, 1, D_MODEL), lambda i, j, l=l: (l, 0, j))))
            rec.update(t=t, z=z, y=y, cp=cp)
        else:
            a = l - 2
            w = weights(f"attn{a}", u)
            if a == 0:
                h_kv = h
                w_kv = w["wkv"]
                kv = mm_nn("kv_fwd", u_kv, w_kv, (D_MODEL, 2 * KV_DIM), lambda i, j: (0, 0), 2 * KV_DIM,
                           2 * KV_DIM, BF16)
            q = mm_nn(f"q_fwd{a}", u, w["wq"], (None, D_MODEL, D_MODEL), lambda i, j: (0, 0, j), D_MODEL, D_MODEL,
                      BF16)
            o = attn_fwd(q, kv, bias, rep["sinks"][a:a + 1], f"attn_fwd{a}")
            h, uf = mm_nn(f"o_fwd{a}", o, w["wo"], (None, D_MODEL, D_MODEL), lambda i, j: (0, 0, j), D_MODEL,
                          D_MODEL, F32, res=h, norm=g_ffn)
            rec.update(q=q, o=o)
        rec["w"] = w
        rec["h_mid"] = h
        wf = weights(f"ffn{l}", uf)
        nxt = [] if l == 3 else [rep["norm_mix"][l + 1:l + 2]] + ([rep["norm_kv"]] if l == 1 else [])
        h, gu, *normed = ffn_fwd(uf, h, wf["up"], wf["down"], 0, f"ffn_fwd{l}", norms=nxt)
        if normed:
            u = normed[0]
        if l == 1:
            u_kv = normed[1]
        rec.update(uf=uf, gu=gu, wf=wf)
        saved.append(rec)

    dh, d_nfin, loss = loss_head(h, rep["norm_final"], target, "loss_head")

    d_mix, d_ffn = [None] * 4, [None] * 4
    cp_grads = [None, None]
    dkv_parts, dbias_parts, dsinks = [], [], [None, None]
    d_nkv = None
    full_rows = lambda tk: (tk, D_MODEL)
    tok = None
    for l in reversed(range(4)):
        rec = saved[l]
        w, wf = rec["w"], rec["wf"]
        grads = {}
        g_mix = rep["norm_mix"][l:l + 1]
        g_ffn = rep["norm_ffn"][l:l + 1]
        dh_mid, d_ffn[l], act, dgu = ffn_bwd(dh, rec["gu"], wf["up"], wf["down"], 0, rec["h_mid"], g_ffn,
                                             f"ffn_bwd{l}", token=tok)
        g_down = mm_tn(
            f"down_grad{l}", act, dh, groups=4, a_block=lambda tk: (None, tk, FF_CHUNK), a_index=lambda j, k: (j, k, 0),
            b_block=full_rows, b_index=lambda j, k: (k, 0), o_block=(None, FF_CHUNK, D_MODEL),
            o_index=lambda j, k: (j, 0, 0), o_shape=(4, FF_CHUNK, D_MODEL), acc_shape=(FF_CHUNK, D_MODEL))
        g_up = mm_tn(
            f"up_grad{l}", dgu.reshape(8, s, FF_CHUNK), rec["uf"], groups=8, a_block=lambda tk: (None, tk, FF_CHUNK),
            a_index=lambda j, k: (j, k, 0), b_block=full_rows, b_index=lambda j, k: (k, 0),
            o_block=(None, FF_CHUNK, D_MODEL), o_index=lambda j, k: (j, 0, 0), o_shape=(8, FF_CHUNK, D_MODEL),
            acc_shape=(FF_CHUNK, D_MODEL))
        tok = emit(f"ffn{l}", {"up": g_up, "down": g_down.reshape(D_FF, D_MODEL)})
        dh = dh_mid
        if l < 2:
            dz = mm_nt(f"pw2_bwd{l}", dh, w["pw2"], (None, D_MODEL, D_MODEL), lambda i, k: (0, 0, 0), D_MODEL, F32,
                       token=tok)
            grads["pw2"], db2 = mm_tn(
                f"pw2_grad{l}", rec["z"], dh, groups=1, a_block=full_rows, a_index=lambda j, k: (k, 0),
                b_block=full_rows, b_index=lambda j, k: (k, 0), o_block=(D_MODEL, D_MODEL), o_index=lambda j, k: (0, 0),
                o_shape=(D_MODEL, D_MODEL), acc_shape=(D_MODEL, D_MODEL),
                colsum=((1, D_MODEL), pl.BlockSpec((1, D_MODEL), lambda j, k: (0, 0))))
            dt, stats = conv_bwd(rec["t"], rec["y"], dz, rec["cp"], f"conv_bwd{l}")
            grads["pw1"], db1 = pw1_grad(rec["u"], dt, f"pw1_grad{l}")
            cp_grads[l] = (stats, db2, db1)
            tok = emit(f"conv{l}", grads)
            dh, d_mix[l] = pw1_bwd(dt, w["pw1"], rec["h_in"], g_mix, dh, f"pw1_bwd{l}")
        else:
            a = l - 2
            do = mm_nt(f"o_bwd{a}", dh, w["wo"], (None, D_MODEL, D_MODEL), lambda i, k: (0, 0, 0), D_MODEL, BF16,
                       token=tok)
            tok = None
            grads["wo"] = mm_tn(
                f"wo_grad{a}", rec["o"], dh, groups=1, a_block=full_rows, a_index=lambda j, k: (k, 0),
                b_block=full_rows, b_index=lambda j, k: (k, 0), o_block=(D_MODEL, D_MODEL), o_index=lambda j, k: (0, 0),
                o_shape=(D_MODEL, D_MODEL), acc_shape=(D_MODEL, D_MODEL))
            dq, dkv_p, dbias_p, dsinks[a] = attn_bwd(rec["q"], kv, do, bias, rep["sinks"][a:a + 1], f"attn_bwd{a}")
            dkv_parts.append(dkv_p)
            dbias_parts.append(dbias_p)
            grads["wq"] = mm_tn(
                f"wq_grad{a}", rec["u"], dq, groups=1, a_block=full_rows, a_index=lambda j, k: (k, 0),
                b_block=full_rows, b_index=lambda j, k: (k, 0), o_block=(D_MODEL, D_MODEL), o_index=lambda j, k: (0, 0),
                o_shape=(D_MODEL, D_MODEL), acc_shape=(D_MODEL, D_MODEL))
            if a == 1:
                tok = emit("attn1", grads)
            dh, d_mix[l] = nt_rms_bwd(f"q_bwd{a}", dq, w["wq"], (None, D_MODEL, D_MODEL), lambda i: (0, 0, 0),
                                      rec["h_in"], g_mix, dh)
        if l == 2:
            dkv = dkv_combine(dkv_parts[0], dkv_parts[1], "dkv_combine")
            grads["wkv"] = mm_tn(
                "wkv_grad", u_kv, dkv, groups=1, a_block=full_rows, a_index=lambda j, k: (k, 0),
                b_block=lambda tk: (tk, 2 * KV_DIM), b_index=lambda j, k: (k, 0), o_block=(D_MODEL, 2 * KV_DIM),
                o_index=lambda j, k: (0, 0), o_shape=(D_MODEL, 2 * KV_DIM), acc_shape=(D_MODEL, 2 * KV_DIM))
            tok = emit("attn0", grads)
            dh, d_nkv = nt_rms_bwd("kv_bwd", dkv, w_kv, (D_MODEL, 2 * KV_DIM), lambda i: (0, 0), h_kv,
                                   rep["norm_kv"], dh)

    d_relb = bias_grad(dbias_parts[0], dbias_parts[1], bucket, "bias_grad")
    d_sinks = jnp.concatenate([dsinks[0][0, :N_HEADS], dsinks[1][0, :N_HEADS]])
    tail = jnp.zeros((D_MODEL,), F32)
    rep_grad = jnp.concatenate([
        jnp.concatenate(d_mix, axis=0), jnp.concatenate(d_ffn, axis=0), d_nkv, d_nfin,
        tail.at[:2 * N_HEADS].set(d_sinks)[None], tail.at[:N_BUCKETS * N_HEADS].set(d_relb.reshape(-1))[None],
        tail.at[0].set(loss[0, 0])[None], jnp.zeros((REP_ROWS - ROW_LOSS - 1, D_MODEL), F32)], axis=0)
    return dh, cp_grads, rep_grad


def _pack_conv(w_dw, b_dw, ln_g, ln_b, b_pw2, b_pw1):
    rows = [w_dw, b_dw[:, None], ln_g[:, None], ln_b[:, None], b_pw2[:, None], b_pw1.reshape(2, 2, 128),
            jnp.zeros((2, PACK_ROWS - ROW_BPW1 - 2, 128), F32)]
    return jnp.concatenate(rows, axis=1)


def _unpack_conv(p):
    return (p[:, :CONV_WIDTH], p[:, ROW_BDW], p[:, ROW_LNG], p[:, ROW_LNB], p[:, ROW_BPW2],
            p[:, ROW_BPW1:ROW_BPW1 + 2].reshape(2, 256))


def _pack_rep(norm_mix, norm_ffn, norm_kv, norm_final, sinks, rel_bias):
    tail = jnp.zeros((D_MODEL,), F32)
    return jnp.concatenate([
        norm_mix, norm_ffn, norm_kv[None], norm_final[None], tail.at[:2 * N_HEADS].set(sinks.reshape(-1))[None],
        tail.at[:N_BUCKETS * N_HEADS].set(rel_bias.reshape(-1))[None],
        jnp.zeros((REP_ROWS - ROW_RELB - 1, D_MODEL), F32)], axis=0)


def _unpack_rep(p):
    return (p[0:4], p[4:8], p[ROW_NKV], p[ROW_NFIN], p[ROW_SINK, :2 * N_HEADS].reshape(2, N_HEADS),
            p[ROW_RELB, :N_BUCKETS * N_HEADS].reshape(N_BUCKETS, N_HEADS))


def kernel(x, norm_mix, norm_ffn, conv_w_pw1, conv_b_pw1, conv_w_dw, conv_b_dw, conv_ln_g, conv_ln_b, conv_w_pw2, conv_b_pw2, norm_kv, w_kv, w_q, w_o, sinks, rel_bias, ffn_w_up, ffn_w_down, norm_final, loss_target, m_norm_mix, m_norm_ffn, m_conv_w_pw1, m_conv_b_pw1, m_conv_w_dw, m_conv_b_dw, m_conv_ln_g, m_conv_ln_b, m_conv_w_pw2, m_conv_b_pw2, m_norm_kv, m_w_kv, m_w_q, m_w_o, m_sinks, m_rel_bias, m_ffn_w_up, m_ffn_w_down, m_norm_final, v_norm_mix, v_norm_ffn, v_conv_w_pw1, v_conv_b_pw1, v_conv_w_dw, v_conv_b_dw, v_conv_ln_g, v_conv_ln_b, v_conv_w_pw2, v_conv_b_pw2, v_norm_kv, v_w_kv, v_w_q, v_w_o, v_sinks, v_rel_bias, v_ffn_w_up, v_ffn_w_down, v_norm_final):
    s = x.shape[1]
    d = D_MODEL
    rsh = d // N_DEV
    dsh = D_FF // N_DEV

    me = _lin(*_place())
    lead_slab = lambda ref, j: ref.at[j]
    rows_of = lambda rows: (lambda ref, j: ref.at[pl.ds(j * rows, rows), :])

    conv_pack = _pack_conv(conv_w_dw, conv_b_dw, conv_ln_g, conv_ln_b, conv_b_pw2, conv_b_pw1)
    ag_order = ["conv0", "ffn0", "conv1", "ffn1", "attn0", "ffn2", "attn1", "ffn3"]
    ag_land, ag_copies, ag_members = [], [], {g: [] for g in ag_order}

    def gather(group, key, shard, land_shape, at, land_slab):
        i = len(ag_land)
        ag_land.append(lax.dynamic_update_slice(lax.empty(land_shape, shard.dtype), shard, at(me)))
        own = lambda ref, j, me_, slab=land_slab: slab(ref, me_)
        ag_copies.append((ag_order.index(group), i, own, i, land_slab))
        ag_members[group].append((key, i, own, land_slab))

    cols_at0 = lambda ref, j: ref.at[0, j]
    rows_at0 = lambda rows: (lambda ref, j: ref.at[0, pl.ds(j * rows, rows), :])
    col_at = lambda m: (0, m, 0, 0)
    row_at = lambda rows: (lambda m: (0, m * rows, 0))
    for l in range(2):
        gather(f"conv{l}", "pw1", conv_w_pw1[l].astype(BF16)[None, None], (1, N_DEV, d, 256), col_at, cols_at0)
        gather(f"conv{l}", "pw2", conv_w_pw2[l].astype(BF16)[None], (1, d, d), row_at(rsh), rows_at0(rsh))
    gather("conv0", "pack", conv_pack[None], (N_DEV, 2, PACK_ROWS, 128), lambda m: (m, 0, 0, 0), lead_slab)
    gather("attn0", "wkv", w_kv.astype(BF16), (d, 2 * KV_DIM), lambda m: (m * rsh, 0), rows_of(rsh))
    for a in range(2):
        gather(f"attn{a}", "wq", w_q[a].astype(BF16)[None], (1, d, d), row_at(rsh), rows_at0(rsh))
        gather(f"attn{a}", "wo", w_o[a].astype(BF16)[None], (1, d, d), row_at(rsh), rows_at0(rsh))
    up_t, m_up_t, v_up_t = (jnp.swapaxes(a, 1, 2) for a in (ffn_w_up, m_ffn_w_up, v_ffn_w_up))
    for l in range(4):
        gather(f"ffn{l}", "up", up_t[l].astype(BF16)[None, None], (1, N_DEV, FF_CHUNK, d), col_at, cols_at0)
        gather(f"ffn{l}", "down", ffn_w_down[l].astype(BF16)[None], (1, D_FF, d), row_at(dsh), rows_at0(dsh))
    ag_sems, ag_land_thru, _ = send_start("ag_start", ag_land, ag_copies, len(ag_order))

    def weights(group, after):
        members = ag_members[group]
        lands = send_wait(f"ag_wait_{group}", [ag_land_thru[i] for _, i, _, _ in members],
                          [(n, own, n, slab) for n, (_, _, own, slab) in enumerate(members)],
                          ag_sems[ag_order.index(group)], after)
        w = {key: land for (key, _, _, _), land in zip(members, lands)}
        if "up" in w:
            w["up"] = w["up"].reshape(1, 2, 4, FF_CHUNK, d)
            w["down"] = w["down"].reshape(1, 4, FF_CHUNK, d)
        if "pack" in w:
            pack_g = w.pop("pack")
            w["cp"] = jnp.transpose(pack_g, (1, 2, 0, 3)).reshape(2, PACK_ROWS, d)
            w["b_pw1"] = pack_g[:, :, ROW_BPW1:ROW_BPW1 + 2, :].transpose(1, 0, 2, 3).reshape(2, 1, 2 * d)
            w["b_pw2"] = w["cp"][:, ROW_BPW2:ROW_BPW2 + 1, :]
        return w

    shard_shapes = {"pw1": (d, 256), "pw2": (rsh, d), "wkv": (rsh, 2 * KV_DIM), "wq": (rsh, d), "wo": (rsh, d),
                    "up": (FF_CHUNK, d), "down": (dsh, d), "cp": (2, PACK_ROWS, 128), "rep": (REP_ROWS, d)}
    n_layers = {"pw1": 2, "pw2": 2, "wkv": 1, "wq": 2, "wo": 2, "up": 4, "down": 4, "cp": 1, "rep": 1}
    by_lead = (lambda ref, j, me_: ref.at[j], lambda g: lax.dynamic_index_in_dim(g, me, 0, keepdims=False))
    by_rows = lambda rows: (lambda ref, j, me_: ref.at[pl.ds(j * rows, rows), :],
                            lambda g: lax.dynamic_slice_in_dim(g, me * rows, rows, 0))
    all_of = (lambda ref, j, me_: ref, lambda g: g)
    owned = {"pw1": by_lead, "pw2": by_rows(rsh), "wkv": by_rows(rsh), "wq": by_rows(rsh), "wo": by_rows(rsh),
             "up": by_lead, "down": by_rows(dsh), "cp": by_lead, "rep": all_of}
    parts = {}
    pending = {}

    def finish(chain, after):
        keys, bufs, copies, sems, name = pending.pop(chain)
        done = send_wait(f"rs_wait_{name}", bufs, copies, sems, after)
        parts.update(zip(keys, done[len(keys):]))

    def exchange(chain, name, layer, grads):
        keys = list(grads)
        if chain in pending:
            finish(chain, grads[keys[0]])
        lands = []
        for k in keys:
            land = parts.pop(k) if k in parts else lax.empty((N_DEV, n_layers[k]) + shard_shapes[k], grads[k].dtype)
            mine = owned[k][1](grads[k])[None, None]
            lands.append(lax.dynamic_update_slice(land, mine, (me, layer) + (0,) * len(shard_shapes[k])))
        land_at = lambda ref, i: ref.at[i, layer]
        copies = [(0, n, owned[k][0], len(keys) + n, land_at) for n, k in enumerate(keys)]
        sems, thru, token = send_start(f"rs_start_{name}", [grads[k] for k in keys] + lands, copies, 1)
        pending[chain] = (keys, thru, [c[1:] for c in copies], sems[0], name)
        return token

    def emit(group, grads):
        return exchange(group[:-1], group, int(group[-1]), grads)

    rep = {"norm_mix": norm_mix, "norm_ffn": norm_ffn, "norm_kv": norm_kv[None], "norm_final": norm_final[None],
           "sinks": sinks, "rel_bias": rel_bias}

    grad_x, cp_grads, rep_grad = local_step(x[0], loss_target[0], weights, rep, emit)

    cp_full = []
    for l in range(2):
        stats, db2, db1 = cp_grads[l]
        cp_full.append(jnp.concatenate([
            stats[:ROW_BPW2], db2, db1.reshape(N_DEV, 2, 128).transpose(1, 0, 2).reshape(2, d),
            jnp.zeros((PACK_ROWS - ROW_BPW1 - 2, d), F32)], axis=0))
    cp_send = jnp.stack(cp_full).reshape(2, PACK_ROWS, N_DEV, 128).transpose(2, 0, 1, 3)
    exchange("tail", "tail", 0, {"cp": cp_send, "rep": rep_grad})

    def update(key, w, m, v, name):
        p = parts[key]
        w3 = w.reshape(p.shape[1:])
        outs = adamw(w3, m.reshape(w3.shape), v.reshape(w3.shape), p, name)
        return [o.reshape(w.shape) for o in outs]

    res = {}
    finish("ffn", grad_x)
    res["ffn_w_up"] = [jnp.swapaxes(o, 1, 2) for o in update("up", up_t, m_up_t, v_up_t, "adam_up")]
    res["ffn_w_down"] = update("down", ffn_w_down, m_ffn_w_down, v_ffn_w_down, "adam_down")
    finish("attn", res["ffn_w_down"][0])
    res["w_kv"] = update("wkv", w_kv, m_w_kv, v_w_kv, "adam_wkv")
    res["w_q"] = update("wq", w_q, m_w_q, v_w_q, "adam_wq")
    res["w_o"] = update("wo", w_o, m_w_o, v_w_o, "adam_wo")
    finish("conv", res["w_o"][0])
    res["conv_w_pw1"] = update("pw1", conv_w_pw1, m_conv_w_pw1, v_conv_w_pw1, "adam_pw1")
    res["conv_w_pw2"] = update("pw2", conv_w_pw2, m_conv_w_pw2, v_conv_w_pw2, "adam_pw2")
    finish("tail", res["conv_w_pw2"][0])
    m_pack = _pack_conv(m_conv_w_dw, m_conv_b_dw, m_conv_ln_g, m_conv_ln_b, m_conv_b_pw2, m_conv_b_pw1)
    v_pack = _pack_conv(v_conv_w_dw, v_conv_b_dw, v_conv_ln_g, v_conv_ln_b, v_conv_b_pw2, v_conv_b_pw1)
    cp_res = adamw(conv_pack, m_pack, v_pack, parts["cp"].reshape(N_DEV, 2, PACK_ROWS, 128), "adam_conv_pack")
    rep_w = _pack_rep(norm_mix, norm_ffn, norm_kv, norm_final, sinks, rel_bias)
    rep_m = _pack_rep(m_norm_mix, m_norm_ffn, m_norm_kv, m_norm_final, m_sinks, m_rel_bias)
    rep_v = _pack_rep(v_norm_mix, v_norm_ffn, v_norm_kv, v_norm_final, v_sinks, v_rel_bias)
    rep_res = adamw(rep_w[None], rep_m[None], rep_v[None], parts["rep"], "adam_rep")
    loss = rep_res[0][0, ROW_LOSS, 0]

    outs = []
    for kind in range(4):
        cw_dw, cb_dw, cln_g, cln_b, cb_pw2, cb_pw1 = _unpack_conv(cp_res[kind])
        r_mix, r_ffn, r_nkv, r_nfin, r_sinks, r_relb = _unpack_rep(rep_res[kind][0])
        outs += [r_mix, r_ffn, res["conv_w_pw1"][kind], cb_pw1, cw_dw, cb_dw, cln_g, cln_b, res["conv_w_pw2"][kind],
                 cb_pw2, r_nkv, res["w_kv"][kind], res["w_q"][kind], res["w_o"][kind], r_sinks, r_relb,
                 res["ffn_w_up"][kind], res["ffn_w_down"][kind], r_nfin]
    return (loss, grad_x[None], *outs)
```

```python
import functools
import math

import numpy as np
import jax
import jax.numpy as jnp
from jax import lax
from jax.experimental import pallas as pl
from jax.experimental.pallas import tpu as pltpu

F32 = jnp.float32
BF16 = jnp.bfloat16

D_MODEL = 1024
D_FF = 2816
N_HEADS = 16
N_KV_HEADS = 4
GROUP = N_HEADS // N_KV_HEADS
HEAD_DIM = 64
KV_DIM = N_KV_HEADS * HEAD_DIM
BLOCK = 128
CONV_WIDTH = 31
HALO = 32
N_BUCKETS = 32
MAX_DISTANCE = 128
EPS = 1e-6
NEG_INF = -1e30
N_DEV = 8
FF_CHUNK = D_FF // 4
PACK_ROWS = 40
ROW_BDW, ROW_LNG, ROW_LNB, ROW_BPW2, ROW_BPW1 = 31, 32, 33, 34, 35
REP_ROWS = 16
ROW_NKV, ROW_NFIN, ROW_SINK, ROW_RELB, ROW_LOSS = 8, 9, 10, 11, 12

ADAM_LR, ADAM_B1, ADAM_B2, ADAM_EPS, ADAM_WD, ADAM_STEP = 0.001, 0.9, 0.999, 1e-08, 0.01, 10

VMEM_LIMIT_BYTES = 56 * 1024 * 1024
FFN_ROWS = 1024
FFN_BWD_ROWS = 512
GRAD_ROWS = 2048
ANY = pl.BlockSpec(memory_space=pl.ANY)
MESH = pl.DeviceIdType.MESH

NN = (((1,), (0,)), ((), ()))
NT = (((1,), (1,)), ((), ()))
TN = (((0,), (0,)), ((), ()))


def _dot(a, b, dims):
    return lax.dot_general(a, b, dims, preferred_element_type=F32)


def _pcall(body, name, out_shape, *, grid=None, in_specs=None, out_specs=None, scratch=(), sem=None, **kw):
    params = pltpu.CompilerParams(dimension_semantics=sem, vmem_limit_bytes=VMEM_LIMIT_BYTES)
    extra = {} if grid is None else {"grid": grid}
    return pl.pallas_call(body, name=name, out_shape=out_shape, in_specs=in_specs, out_specs=out_specs,
                          scratch_shapes=list(scratch), compiler_params=params, **extra, **kw)


def _sds(shape, dtype):
    return jax.ShapeDtypeStruct(tuple(shape), dtype)


def _row_tile(s, want):
    return want if s % want == 0 else s


def rms_fwd(h, g, name):
    s, d = h.shape
    tm = _row_tile(s, 512)

    def body(h_ref, g_ref, u_ref):
        x = h_ref[...]
        r = lax.rsqrt(jnp.mean(x * x, axis=-1, keepdims=True) + EPS)
        u_ref[...] = (x * r * g_ref[...]).astype(BF16)

    return _pcall(body, name, _sds((s, d), BF16), grid=(s // tm,),
                  in_specs=[pl.BlockSpec((tm, d), lambda i: (i, 0)), pl.BlockSpec((1, d), lambda i: (0, 0))],
                  out_specs=pl.BlockSpec((tm, d), lambda i: (i, 0)), sem=("parallel",))(h, g)


def _rms_rows(x, gain):
    return (x * lax.rsqrt(jnp.mean(x * x, axis=-1, keepdims=True) + EPS) * gain).astype(BF16)


def _mm(name, a, b, *, dims, grid, a_spec, b_spec, o_spec, o_shape, nk=1, acc_shape=None,
        bias=None, res=None, colsum=None, sem=None, token=None, norm=None):
    n_axes = len(grid)

    def body(*refs):
        it = iter(refs)
        a_ref, b_ref = next(it), next(it)
        bias_ref = next(it) if bias is not None else None
        res_ref = next(it) if res is not None else None
        gain_ref = next(it) if norm is not None else None
        if token is not None:
            next(it)
        o_ref = next(it)
        un_ref = next(it) if norm is not None else None
        cs_ref = next(it) if colsum is not None else None
        acc_ref = next(it) if nk > 1 else None
        k = pl.program_id(n_axes - 1)
        p = _dot(a_ref[...].astype(BF16), b_ref[...].astype(BF16), dims)

        def finish(acc):
            if bias_ref is not None:
                acc = acc + bias_ref[...]
            if res_ref is not None:
                acc = acc + res_ref[...]
            o_ref[...] = acc.astype(o_ref.dtype)
            if un_ref is not None:
                un_ref[...] = _rms_rows(acc, gain_ref[...])

        if nk == 1:
            finish(p)
        else:
            @pl.when(k == 0)
            def _():
                acc_ref[...] = p

            @pl.when(k > 0)
            def _():
                acc_ref[...] += p

            @pl.when(k == nk - 1)
            def _():
                finish(acc_ref[...])

        if cs_ref is not None:
            cs = jnp.sum(b_ref[...].astype(F32), axis=0, keepdims=True)

            @pl.when(k == 0)
            def _():
                cs_ref[...] = cs

            @pl.when(k > 0)
            def _():
                cs_ref[...] += cs

    ins, in_specs = [a, b], [a_spec, b_spec]
    gain = None if norm is None else (norm, pl.BlockSpec(norm.shape, lambda *_: (0,) * norm.ndim))
    for extra in (bias, res, gain, None if token is None else (token, ANY)):
        if extra is not None:
            ins.append(extra[0])
            in_specs.append(extra[1])
    out_shape, out_specs = o_shape, o_spec
    if norm is not None:
        out_shape, out_specs = (o_shape, _sds(o_shape.shape, BF16)), (o_spec, o_spec)
    if colsum is not None:
        out_shape, out_specs = (o_shape, _sds(colsum[0], F32)), (o_spec, colsum[1])
    scratch = [pltpu.VMEM(acc_shape, F32)] if nk > 1 else []
    if sem is None:
        sem = ("parallel",) * (n_axes - 1) + ("arbitrary",)
    return _pcall(body, name, out_shape, grid=grid, in_specs=in_specs, out_specs=out_specs, scratch=scratch,
                  sem=sem)(*ins)


def mm_nn(name, a, w, w_block, w_index, n, tn, out_dtype, bias=None, res=None, tm=1024, norm=None):
    s, k = a.shape
    tm = _row_tile(s, tm)
    col = lambda i, j: (i, j)
    extras = {}
    if bias is not None:
        extras["bias"] = bias
    if res is not None:
        extras["res"] = (res, pl.BlockSpec((tm, tn), col))
    if norm is not None:
        assert tn == n, "a fused norm needs whole rows"
        extras["norm"] = norm
    return _mm(name, a, w, dims=NN, grid=(s // tm, n // tn), a_spec=pl.BlockSpec((tm, k), lambda i, j: (i, 0)),
               b_spec=pl.BlockSpec(w_block, w_index), o_spec=pl.BlockSpec((tm, tn), col), o_shape=_sds((s, n), out_dtype),
               sem=("parallel", "arbitrary"), **extras)


def mm_nt(name, a, w, w_block, w_index, kout, out_dtype, nk=1, tk=None, tm=1024, token=None):
    s, n = a.shape
    tm = _row_tile(s, tm)
    tk = n if tk is None else tk
    return _mm(name, a, w, dims=NT, grid=(s // tm, nk), a_spec=pl.BlockSpec((tm, tk), lambda i, k: (i, k)),
               b_spec=pl.BlockSpec(w_block, w_index), o_spec=pl.BlockSpec((tm, kout), lambda i, k: (i, 0)),
               o_shape=_sds((s, kout), out_dtype), nk=nk, acc_shape=(tm, kout), token=token)


def mm_tn(name, a, b, *, groups, a_block, a_index, b_block, b_index, o_block, o_index, o_shape, acc_shape,
          colsum=None, tk=GRAD_ROWS):
    s = a.shape[-2]
    tk = _row_tile(s, tk)
    return _mm(name, a, b, dims=TN, grid=(groups, s // tk), a_spec=pl.BlockSpec(a_block(tk), a_index),
               b_spec=pl.BlockSpec(b_block(tk), b_index), o_spec=pl.BlockSpec(o_block, o_index),
               o_shape=_sds(o_shape, BF16), nk=s // tk, acc_shape=acc_shape, colsum=colsum)


FFN_SUB = 512


def _sub_rows(tm):
    sub = FFN_SUB if tm % FFN_SUB == 0 else tm
    return [slice(r * sub, (r + 1) * sub) for r in range(tm // sub)]


def ffn_fwd(u, h, w_up_t, w_down, layer, name, norms=()):
    s, d = u.shape
    tm = _row_tile(s, FFN_ROWS)
    nj = 4
    nn = len(norms)

    def body(u_ref, h_ref, wup_ref, wd_ref, *rest):
        gain_refs, (hn_ref, gu_ref), un_refs = rest[:nn], rest[nn:nn + 2], rest[nn + 2:]
        j = pl.program_id(1)

        @pl.when(j == 0)
        def _():
            hn_ref[...] = h_ref[...]

        for rows in _sub_rows(tm):
            uv = u_ref[rows, :]
            g = _dot(uv, wup_ref[0], NT)
            p = _dot(uv, wup_ref[1], NT)
            gu_ref[0, rows, :] = g.astype(BF16)
            gu_ref[1, rows, :] = p.astype(BF16)
            act = (g * jax.nn.sigmoid(g) * p).astype(BF16)
            hn_ref[rows, :] += _dot(act, wd_ref[...], NN)

        if nn:
            @pl.when(j == nj - 1)
            def _():
                for rows in _sub_rows(tm):
                    for gain_ref, un_ref in zip(gain_refs, un_refs):
                        un_ref[rows, :] = _rms_rows(hn_ref[rows, :], gain_ref[...])

    row = pl.BlockSpec((tm, d), lambda i, j: (i, 0))
    vec = pl.BlockSpec((1, d), lambda i, j: (0, 0))
    return _pcall(
        body, name, (_sds((s, d), F32), _sds((2, nj, s, FF_CHUNK), BF16)) + (_sds((s, d), BF16),) * nn,
        grid=(s // tm, nj),
        in_specs=[row, row,
                  pl.BlockSpec((None, 2, None, FF_CHUNK, d), lambda i, j: (layer, 0, j, 0, 0)),
                  pl.BlockSpec((None, None, FF_CHUNK, d), lambda i, j: (layer, j, 0, 0))] + [vec] * nn,
        out_specs=(row, pl.BlockSpec((2, None, tm, FF_CHUNK), lambda i, j: (0, j, i, 0))) + (row,) * nn,
        sem=("parallel", "arbitrary"))(u, h, w_up_t, w_down, *norms)


def _rms_bwd_rows(x, gain, du, dh_in):
    r = lax.rsqrt(jnp.mean(x * x, axis=-1, keepdims=True) + EPS)
    xh = x * r
    dxh = du * gain
    dx = r * (dxh - xh * jnp.mean(dxh * xh, axis=-1, keepdims=True))
    return dh_in + dx, jnp.sum(du * xh, axis=0, keepdims=True)


def ffn_bwd(dh, gu, w_up_t, w_down, layer, h_mid, gain, name, token=None):
    s, d = dh.shape
    tm = _row_tile(s, FFN_BWD_ROWS)
    nj = 4

    def body(dh_ref, gu_ref, wup_ref, wd_ref, h_ref, gain_ref, *rest):
        dho_ref, dgain_ref, act_ref, dgu_ref, du_ref = rest[-5:]
        i, j = pl.program_id(0), pl.program_id(1)

        @pl.when(j == 0)
        def _():
            du_ref[...] = jnp.zeros_like(du_ref)

        @pl.when((i == 0) & (j == 0))
        def _():
            dgain_ref[...] = jnp.zeros_like(dgain_ref)

        for rows in _sub_rows(tm):
            dact = _dot(dh_ref[rows, :].astype(BF16), wd_ref[...], NT)
            g = gu_ref[0, rows, :].astype(F32)
            p = gu_ref[1, rows, :].astype(F32)
            sig = jax.nn.sigmoid(g)
            sl = g * sig
            act_ref[rows, :] = (sl * p).astype(BF16)
            dp = (dact * sl).astype(BF16)
            dg = (dact * p * (sig * (1.0 + g * (1.0 - sig)))).astype(BF16)
            dgu_ref[0, rows, :] = dg
            dgu_ref[1, rows, :] = dp
            du_ref[rows, :] += _dot(dg, wup_ref[0], NN) + _dot(dp, wup_ref[1], NN)

        @pl.when(j == nj - 1)
        def _():
            for rows in _sub_rows(tm):
                dho, part = _rms_bwd_rows(h_ref[rows, :], gain_ref[...], du_ref[rows, :], dh_ref[rows, :])
                dho_ref[rows, :] = dho
                dgain_ref[...] += part

    row = pl.BlockSpec((tm, d), lambda i, j: (i, 0))
    vec = pl.BlockSpec((1, d), lambda i, j: (0, 0))
    gu_spec = pl.BlockSpec((2, None, tm, FF_CHUNK), lambda i, j: (0, j, i, 0))
    extra = [] if token is None else [token]
    return _pcall(
        body, name,
        (_sds((s, d), F32), _sds((1, d), F32), _sds((nj, s, FF_CHUNK), BF16), _sds((2, nj, s, FF_CHUNK), BF16)),
        grid=(s // tm, nj),
        in_specs=[row, gu_spec,
                  pl.BlockSpec((None, 2, None, FF_CHUNK, d), lambda i, j: (layer, 0, j, 0, 0)),
                  pl.BlockSpec((None, None, FF_CHUNK, d), lambda i, j: (layer, j, 0, 0)), row, vec]
        + [ANY] * len(extra),
        out_specs=(row, vec, pl.BlockSpec((None, tm, FF_CHUNK), lambda i, j: (j, i, 0)), gu_spec),
        scratch=[pltpu.VMEM((tm, d), F32)],
        sem=("arbitrary", "arbitrary"))(dh, gu, w_up_t, w_down, h_mid, gain, *extra)


def nt_rms_bwd(name, a, w, w_block, w_index, h, gain, dh_in):
    s, n = a.shape
    d = h.shape[1]
    tm = _row_tile(s, 512)

    def body(a_ref, w_ref, h_ref, gain_ref, dhi_ref, dho_ref, dgain_ref):
        i = pl.program_id(0)

        @pl.when(i == 0)
        def _():
            dgain_ref[...] = jnp.zeros_like(dgain_ref)

        du = _dot(a_ref[...].astype(BF16), w_ref[...], NT)
        dho, part = _rms_bwd_rows(h_ref[...], gain_ref[...], du, dhi_ref[...])
        dho_ref[...] = dho
        dgain_ref[...] += part

    row = pl.BlockSpec((tm, d), lambda i: (i, 0))
    vec = pl.BlockSpec((1, d), lambda i: (0, 0))
    return _pcall(body, name, (_sds((s, d), F32), _sds((1, d), F32)), grid=(s // tm,),
                  in_specs=[pl.BlockSpec((tm, n), lambda i: (i, 0)), pl.BlockSpec(w_block, w_index), row, vec, row],
                  out_specs=(row, vec), sem=("arbitrary",))(a, w, h, gain, dh_in)


def pw1_fwd(u, w, b, layer, name):
    s, d = u.shape
    tm = _row_tile(s, FFN_ROWS)
    nb, wb = w.shape[1], w.shape[3]

    def body(u_ref, w_ref, b_ref, t_ref):
        for rows in _sub_rows(tm):
            uv = u_ref[rows, :]
            for j in range(nb):
                cols = slice(j * wb, (j + 1) * wb)
                t_ref[rows, cols] = (_dot(uv, w_ref[j], NN) + b_ref[:, cols]).astype(BF16)

    return _pcall(
        body, name, _sds((s, nb * wb), BF16), grid=(s // tm,),
        in_specs=[pl.BlockSpec((tm, d), lambda i: (i, 0)), pl.BlockSpec((None, nb, d, wb), lambda i: (0, 0, 0, 0)),
                  pl.BlockSpec((None, 1, nb * wb), lambda i: (layer, 0, 0))],
        out_specs=pl.BlockSpec((tm, nb * wb), lambda i: (i, 0)), sem=("parallel",))(u, w, b)


def pw1_bwd(dt, w, h, gain, dh_in, name):
    s = dt.shape[0]
    nb, d, wb = w.shape[1], w.shape[2], w.shape[3]
    tm = _row_tile(s, 512)

    def body(dt_ref, w_ref, h_ref, gain_ref, dhi_ref, dho_ref, dgain_ref):
        i = pl.program_id(0)

        @pl.when(i == 0)
        def _():
            dgain_ref[...] = jnp.zeros_like(dgain_ref)

        for rows in _sub_rows(tm):
            du = _dot(dt_ref[rows, 0:wb], w_ref[0], NT)
            for j in range(1, nb):
                du = du + _dot(dt_ref[rows, j * wb:(j + 1) * wb], w_ref[j], NT)
            dho, part = _rms_bwd_rows(h_ref[rows, :], gain_ref[...], du, dhi_ref[rows, :])
            dho_ref[rows, :] = dho
            dgain_ref[...] += part

    row = pl.BlockSpec((tm, d), lambda i: (i, 0))
    vec = pl.BlockSpec((1, d), lambda i: (0, 0))
    return _pcall(
        body, name, (_sds((s, d), F32), _sds((1, d), F32)), grid=(s // tm,),
        in_specs=[pl.BlockSpec((tm, nb * wb), lambda i: (i, 0)),
                  pl.BlockSpec((None, nb, d, wb), lambda i: (0, 0, 0, 0)), row, vec, row],
        out_specs=(row, vec), sem=("arbitrary",))(dt, w, h, gain, dh_in)


def pw1_grad(u, dt, name):
    s, d = u.shape
    n = dt.shape[1]
    nb = N_DEV
    wb = n // nb
    tk = _row_tile(s, 1024)
    nk = s // tk

    def body(u_ref, dt_ref, g_ref, db_ref, acc_ref):
        k = pl.program_id(0)
        p = _dot(u_ref[...], dt_ref[...], TN)
        cs = jnp.sum(dt_ref[...].astype(F32), axis=0, keepdims=True)

        @pl.when(k == 0)
        def _():
            acc_ref[...] = p
            db_ref[...] = cs

        @pl.when(k > 0)
        def _():
            acc_ref[...] += p
            db_ref[...] += cs

        @pl.when(k == nk - 1)
        def _():
            for j in range(nb):
                g_ref[j] = acc_ref[:, j * wb:(j + 1) * wb].astype(BF16)

    return _pcall(
        body, name, (_sds((nb, d, wb), BF16), _sds((1, n), F32)), grid=(nk,),
        in_specs=[pl.BlockSpec((tk, d), lambda k: (k, 0)), pl.BlockSpec((tk, n), lambda k: (k, 0))],
        out_specs=(pl.BlockSpec((nb, d, wb), lambda k: (0, 0, 0)), pl.BlockSpec((1, n), lambda k: (0, 0))),
        scratch=[pltpu.VMEM((d, n), F32)], sem=("arbitrary",))(u, dt)


def _glu(t):
    t = t.astype(F32)
    return t[:, :D_MODEL] * jax.nn.sigmoid(t[:, D_MODEL:])


def _conv_tile(s):
    return 256 if s % 256 == 0 else s


CONV_ROWS = 32
CONV_LANES = 512
SUBLANES = 8


def _shifted_copies(sh_ref, rows):
    for b in range(1, SUBLANES):
        sh_ref[b, 0:rows - SUBLANES, :] = sh_ref[0, b:b + rows - SUBLANES, :]


def conv_fwd(t, cp, name):
    s = t.shape[0]
    d = D_MODEL
    ts = _conv_tile(s)
    per = ts // HALO
    rows = HALO + ts
    lead = HALO - (CONV_WIDTH - 1)
    rc = CONV_ROWS

    def body(t_ref, tp_ref, cp_ref, z_ref, y_ref, sh_ref):
        i = pl.program_id(0)
        sh_ref[0, 0:HALO, :] = jnp.where(i > 0, _glu(tp_ref[...]), 0.0)
        sh_ref[0, HALO:rows, :] = _glu(t_ref[...])
        _shifted_copies(sh_ref, rows)

        def chunk(c, carry):
            r0 = pl.multiple_of(c * rc, rc)
            for lc in range(d // CONV_LANES):
                ln = slice(lc * CONV_LANES, (lc + 1) * CONV_LANES)
                acc = jnp.zeros((rc, CONV_LANES), F32) + cp_ref[ROW_BDW:ROW_BDW + 1, ln]
                for k in range(CONV_WIDTH):
                    a8, b = divmod(lead + k, SUBLANES)
                    acc = acc + cp_ref[k:k + 1, ln] * sh_ref[b, pl.ds(r0 + SUBLANES * a8, rc), ln]
                y_ref[pl.ds(r0, rc), ln] = acc
            y = y_ref[pl.ds(r0, rc), :]
            mu = jnp.mean(y, axis=-1, keepdims=True)
            yc = y - mu
            rstd = lax.rsqrt(jnp.mean(yc * yc, axis=-1, keepdims=True) + EPS)
            yn = yc * rstd * cp_ref[ROW_LNG:ROW_LNG + 1, :] + cp_ref[ROW_LNB:ROW_LNB + 1, :]
            z_ref[pl.ds(r0, rc), :] = (yn * jax.nn.sigmoid(yn)).astype(BF16)
            return carry

        lax.fori_loop(0, ts // rc, chunk, 0)

    row = pl.BlockSpec((ts, d), lambda i: (i, 0))
    return _pcall(
        body, name, (_sds((s, d), BF16), _sds((s, d), F32)), grid=(s // ts,),
        in_specs=[pl.BlockSpec((ts, 2 * d), lambda i: (i, 0)),
                  pl.BlockSpec((HALO, 2 * d), lambda i: (jnp.maximum(i * per - 1, 0), 0)),
                  pl.BlockSpec((PACK_ROWS, d), lambda i: (0, 0))],
        out_specs=(row, row),
        scratch=[pltpu.VMEM((SUBLANES, rows, d), F32)], sem=("parallel",))(t, t, cp)


def conv_bwd(t, y, dz, cp, name):
    s = t.shape[0]
    d = D_MODEL
    ts = _conv_tile(s)
    per = ts // HALO
    nt = s // ts
    te = ts + HALO
    rc = CONV_ROWS

    def body(t_ref, y_ref, yn_ref, dz_ref, dzn_ref, cp_ref, dt_ref, st_ref, shd_ref, dw_ref):
        i = pl.program_id(0)
        last = i == nt - 1

        @pl.when(i == 0)
        def _():
            st_ref[...] = jnp.zeros_like(st_ref)
            dw_ref[...] = jnp.zeros_like(dw_ref)

        gain = cp_ref[ROW_LNG:ROW_LNG + 1, :]

        def ln_bwd(yv, dzv):
            mu = jnp.mean(yv, axis=-1, keepdims=True)
            yc = yv - mu
            rstd = lax.rsqrt(jnp.mean(yc * yc, axis=-1, keepdims=True) + EPS)
            yh = yc * rstd
            yn = yh * gain + cp_ref[ROW_LNB:ROW_LNB + 1, :]
            sig = jax.nn.sigmoid(yn)
            dyn = dzv * (sig * (1.0 + yn * (1.0 - sig)))
            dyh = dyn * gain
            dy = rstd * (dyh - jnp.mean(dyh, axis=-1, keepdims=True)
                         - yh * jnp.mean(dyh * yh, axis=-1, keepdims=True))
            return dy, dyn, yh

        def norm_chunk(c, carry):
            r0 = pl.multiple_of(c * rc, rc)
            dy, dyn, yh = ln_bwd(y_ref[pl.ds(r0, rc), :], dz_ref[pl.ds(r0, rc), :])
            shd_ref[0, pl.ds(r0, rc), :] = dy
            st_ref[ROW_BDW:ROW_BDW + 1, :] += jnp.sum(dy, axis=0, keepdims=True)
            st_ref[ROW_LNG:ROW_LNG + 1, :] += jnp.sum(dyn * yh, axis=0, keepdims=True)
            st_ref[ROW_LNB:ROW_LNB + 1, :] += jnp.sum(dyn, axis=0, keepdims=True)
            return carry

        lax.fori_loop(0, ts // rc, norm_chunk, 0)
        dy_halo, _, _ = ln_bwd(yn_ref[...], jnp.where(last, 0.0, dzn_ref[...]))
        shd_ref[0, ts:te, :] = dy_halo
        _shifted_copies(shd_ref, te)

        def tap_chunk(c, carry):
            r0 = pl.multiple_of(c * rc, rc)
            for lc in range(d // CONV_LANES):
                ln = slice(lc * CONV_LANES, (lc + 1) * CONV_LANES)
                ln2 = slice(d + lc * CONV_LANES, d + (lc + 1) * CONV_LANES)
                t1 = t_ref[pl.ds(r0, rc), ln].astype(F32)
                sg = jax.nn.sigmoid(t_ref[pl.ds(r0, rc), ln2].astype(F32))
                a = t1 * sg
                da = jnp.zeros((rc, CONV_LANES), F32)
                for k in range(CONV_WIDTH):
                    a8, b = divmod(CONV_WIDTH - 1 - k, SUBLANES)
                    e = shd_ref[b, pl.ds(r0 + SUBLANES * a8, rc), ln]
                    da = da + cp_ref[k:k + 1, ln] * e
                    dw_ref[k, :, ln] += jnp.sum((a * e).reshape(rc // SUBLANES, SUBLANES, CONV_LANES), axis=0)
                dt_ref[pl.ds(r0, rc), ln] = (da * sg).astype(BF16)
                dt_ref[pl.ds(r0, rc), ln2] = (da * t1 * sg * (1.0 - sg)).astype(BF16)
            return carry

        lax.fori_loop(0, ts // rc, tap_chunk, 0)

        @pl.when(last)
        def _():
            for k in range(CONV_WIDTH):
                st_ref[k:k + 1, :] = jnp.sum(dw_ref[k], axis=0, keepdims=True)

    last_halo = s // HALO - 1
    row = pl.BlockSpec((ts, d), lambda i: (i, 0))
    halo = pl.BlockSpec((HALO, d), lambda i: (jnp.minimum((i + 1) * per, last_halo), 0))
    return _pcall(
        body, name, (_sds((s, 2 * d), BF16), _sds((PACK_ROWS, d), F32)), grid=(nt,),
        in_specs=[pl.BlockSpec((ts, 2 * d), lambda i: (i, 0)), row, halo, row, halo,
                  pl.BlockSpec((PACK_ROWS, d), lambda i: (0, 0))],
        out_specs=(pl.BlockSpec((ts, 2 * d), lambda i: (i, 0)), pl.BlockSpec((PACK_ROWS, d), lambda i: (0, 0))),
        scratch=[pltpu.VMEM((SUBLANES, te, d), F32), pltpu.VMEM((CONV_WIDTH, SUBLANES, d), F32)],
        sem=("arbitrary",))(t, y, y, dz, dz, cp)


def _bucket_table():
    qi = np.arange(BLOCK, dtype=np.int64)[:, None]
    kj = np.arange(2 * BLOCK, dtype=np.int64)[None, :]
    dist = qi + BLOCK - kj
    max_exact = N_BUCKETS // 2
    dd = np.maximum(dist, 0)
    ratio = (np.maximum(dd, 1).astype(np.float32) / np.float32(max_exact)).astype(np.float32)
    log_ratio = (np.log(ratio).astype(np.float32) / np.float32(math.log(MAX_DISTANCE / max_exact))).astype(np.float32)
    large = max_exact + (log_ratio * np.float32(N_BUCKETS - max_exact)).astype(np.int32)
    large = np.minimum(large, N_BUCKETS - 1)
    bucket = np.where(dd < max_exact, dd, large)
    return np.where((dist >= 0) & (dist < BLOCK), bucket, -1).astype(np.int32)


def bias_table(rel_bias, bucket, name):
    def body(rb_ref, bk_ref, o_ref):
        bk = bk_ref[...]
        for h in range(N_HEADS):
            acc = jnp.full((BLOCK, 2 * BLOCK), NEG_INF, F32)
            for b in range(N_BUCKETS):
                acc = jnp.where(bk == b, rb_ref[b, h], acc)
            o_ref[h] = acc

    return _pcall(body, name, _sds((N_HEADS, BLOCK, 2 * BLOCK), F32),
                  in_specs=[pl.BlockSpec(memory_space=pltpu.SMEM), pl.BlockSpec(memory_space=pltpu.VMEM)],
                  out_specs=pl.BlockSpec(memory_space=pltpu.VMEM))(rel_bias, bucket)


def bias_grad(dba, dbb, bucket, name):
    def body(a_ref, b_ref, bk_ref, o_ref):
        bk = bk_ref[...]
        for h in range(N_HEADS):
            db = a_ref[h] + b_ref[h]
            for b in range(N_BUCKETS):
                o_ref[b, h] = jnp.sum(jnp.where(bk == b, db, 0.0))

    vm = pl.BlockSpec(memory_space=pltpu.VMEM)
    return _pcall(body, name, _sds((N_BUCKETS, N_HEADS), F32), in_specs=[vm, vm, vm],
                  out_specs=pl.BlockSpec(memory_space=pltpu.SMEM))(dba, dbb, bucket)


def _band_specs():
    cur = pl.BlockSpec((BLOCK, 2 * KV_DIM), lambda n: (n, 0))
    prev = pl.BlockSpec((BLOCK, 2 * KV_DIM), lambda n: (jnp.maximum(n - 1, 0), 0))
    return cur, prev


def _scores(q_h, k_h, bias_h, first_row, sink):
    sc = _dot(q_h, k_h, NT) * (HEAD_DIM ** -0.5) + bias_h + first_row
    m = jnp.maximum(jnp.max(sc, axis=-1, keepdims=True), sink)
    p = jnp.exp(sc - m)
    e_sink = jnp.exp(sink - m)
    den = jnp.sum(p, axis=-1, keepdims=True) + e_sink
    return p, e_sink, den


def _first_block_row(n):
    col = lax.broadcasted_iota(jnp.int32, (1, 2 * BLOCK), 1)
    return jnp.where((col < BLOCK) & (n == 0), NEG_INF, 0.0)


def _head_lanes(hk, g):
    h = hk * GROUP + g
    return slice(h * HEAD_DIM, (h + 1) * HEAD_DIM)


def _group_rows(x_ref, hk):
    return jnp.concatenate([x_ref[:, _head_lanes(hk, g)] for g in range(GROUP)], axis=0)


def _group_bias(bias_ref, hk):
    return bias_ref[hk * GROUP:(hk + 1) * GROUP].reshape(GROUP * BLOCK, 2 * BLOCK)


def _group_sinks(sink_ref, hk):
    head = lax.broadcasted_iota(jnp.int32, (GROUP * BLOCK, 1), 0) // BLOCK
    col = jnp.zeros((GROUP * BLOCK, 1), F32) + sink_ref[0, hk * GROUP]
    for g in range(1, GROUP):
        col = jnp.where(head == g, sink_ref[0, hk * GROUP + g], col)
    return col


def attn_fwd(q, kv, bias, sinks, name):
    s = q.shape[0]
    nb = s // BLOCK

    def body(sink_ref, q_ref, kvc_ref, kvp_ref, bias_ref, o_ref, band_ref):
        n = pl.program_id(0)
        band_ref[0:BLOCK, :] = kvp_ref[...]
        band_ref[BLOCK:2 * BLOCK, :] = kvc_ref[...]
        first_row = _first_block_row(n)
        for hk in range(N_KV_HEADS):
            k_h = band_ref[:, hk * HEAD_DIM:(hk + 1) * HEAD_DIM]
            v_h = band_ref[:, KV_DIM + hk * HEAD_DIM:KV_DIM + (hk + 1) * HEAD_DIM]
            p, _, den = _scores(_group_rows(q_ref, hk), k_h, _group_bias(bias_ref, hk), first_row,
                                _group_sinks(sink_ref, hk))
            o = _dot((p * (1.0 / den)).astype(BF16), v_h, NN).astype(BF16)
            for g in range(GROUP):
                o_ref[:, _head_lanes(hk, g)] = o[g * BLOCK:(g + 1) * BLOCK]

    cur, prev = _band_specs()
    qs = pl.BlockSpec((BLOCK, D_MODEL), lambda n: (n, 0))
    return _pcall(
        body, name, _sds((s, D_MODEL), BF16), grid=(nb,),
        in_specs=[pl.BlockSpec(memory_space=pltpu.SMEM), qs, cur, prev,
                  pl.BlockSpec((N_HEADS, BLOCK, 2 * BLOCK), lambda n: (0, 0, 0))],
        out_specs=qs, scratch=[pltpu.VMEM((2 * BLOCK, 2 * KV_DIM), BF16)],
        sem=("parallel",))(sinks, q, kv, kv, bias)


def attn_bwd(q, kv, do, bias, sinks, name):
    s = q.shape[0]
    nb = s // BLOCK
    scale = HEAD_DIM ** -0.5

    def body(sink_ref, q_ref, do_ref, kvc_ref, kvp_ref, bias_ref, dq_ref, dkv_ref, db_ref, dsink_ref,
             band_ref, dsacc_ref):
        n = pl.program_id(0)
        band_ref[0:BLOCK, :] = kvp_ref[...]
        band_ref[BLOCK:2 * BLOCK, :] = kvc_ref[...]
        first_row = _first_block_row(n)
        lane = lax.broadcasted_iota(jnp.int32, (BLOCK, BLOCK), 1)

        @pl.when(n == 0)
        def _():
            db_ref[...] = jnp.zeros_like(db_ref)
            dsacc_ref[...] = jnp.zeros_like(dsacc_ref)

        for hk in range(N_KV_HEADS):
            k_h = band_ref[:, hk * HEAD_DIM:(hk + 1) * HEAD_DIM]
            v_h = band_ref[:, KV_DIM + hk * HEAD_DIM:KV_DIM + (hk + 1) * HEAD_DIM]
            q_g = _group_rows(q_ref, hk)
            do_g = _group_rows(do_ref, hk)
            p, e_sink, den = _scores(q_g, k_h, _group_bias(bias_ref, hk), first_row, _group_sinks(sink_ref, hk))
            inv = 1.0 / den
            p = p * inv
            dp = _dot(do_g, v_h, NT)
            delta = jnp.sum(p * dp, axis=-1, keepdims=True)
            ds = p * (dp - delta)
            db_ref[hk * GROUP:(hk + 1) * GROUP] += ds.reshape(GROUP, BLOCK, 2 * BLOCK)
            d_sink = -(e_sink * inv) * delta
            for g in range(GROUP):
                dsacc_ref[...] += jnp.where(lane == hk * GROUP + g, d_sink[g * BLOCK:(g + 1) * BLOCK], 0.0)
            dsb = ds.astype(BF16)
            dq = (_dot(dsb, k_h, NN) * scale).astype(BF16)
            for g in range(GROUP):
                dq_ref[:, _head_lanes(hk, g)] = dq[g * BLOCK:(g + 1) * BLOCK]
            dkv_ref[:, hk * HEAD_DIM:(hk + 1) * HEAD_DIM] = _dot(dsb, q_g, TN) * scale
            dkv_ref[:, KV_DIM + hk * HEAD_DIM:KV_DIM + (hk + 1) * HEAD_DIM] = _dot(p.astype(BF16), do_g, TN)

        @pl.when(n == nb - 1)
        def _():
            dsink_ref[...] = jnp.sum(dsacc_ref[...], axis=0, keepdims=True)

    cur, prev = _band_specs()
    qs = pl.BlockSpec((BLOCK, D_MODEL), lambda n: (n, 0))
    full_b = pl.BlockSpec((N_HEADS, BLOCK, 2 * BLOCK), lambda n: (0, 0, 0))
    return _pcall(
        body, name,
        (_sds((s, D_MODEL), BF16), _sds((nb, 2 * BLOCK, 2 * KV_DIM), F32),
         _sds((N_HEADS, BLOCK, 2 * BLOCK), F32), _sds((1, BLOCK), F32)),
        grid=(nb,),
        in_specs=[pl.BlockSpec(memory_space=pltpu.SMEM), qs, qs, cur, prev, full_b],
        out_specs=(qs, pl.BlockSpec((None, 2 * BLOCK, 2 * KV_DIM), lambda n: (n, 0, 0)), full_b,
                   pl.BlockSpec((1, BLOCK), lambda n: (0, 0))),
        scratch=[pltpu.VMEM((2 * BLOCK, 2 * KV_DIM), BF16), pltpu.VMEM((BLOCK, BLOCK), F32)],
        sem=("arbitrary",))(sinks, q, do, kv, kv, bias)


def dkv_combine(pa, pb, name):
    nb = pa.shape[0]
    pa2 = pa.reshape(2 * nb, BLOCK, 2 * KV_DIM)
    pb2 = pb.reshape(2 * nb, BLOCK, 2 * KV_DIM)

    def body(ac_ref, an_ref, bc_ref, bn_ref, o_ref):
        n = pl.program_id(0)
        nxt = jnp.where(n == nb - 1, 0.0, an_ref[...] + bn_ref[...])
        o_ref[...] = (ac_ref[...] + bc_ref[...] + nxt).astype(BF16)

    cur = pl.BlockSpec((None, BLOCK, 2 * KV_DIM), lambda n: (2 * n + 1, 0, 0))
    nxt = pl.BlockSpec((None, BLOCK, 2 * KV_DIM), lambda n: (jnp.minimum(2 * n + 2, 2 * nb - 2), 0, 0))
    return _pcall(body, name, _sds((nb * BLOCK, 2 * KV_DIM), BF16), grid=(nb,),
                  in_specs=[cur, nxt, cur, nxt], out_specs=pl.BlockSpec((BLOCK, 2 * KV_DIM), lambda n: (n, 0)),
                  sem=("parallel",))(pa2, pa2, pb2, pb2)


def loss_head(h, g, target, name):
    s, d = h.shape
    tm = _row_tile(s, 512)

    def body(h_ref, g_ref, t_ref, dh_ref, dg_ref, loss_ref):
        i = pl.program_id(0)
        x = h_ref[...]
        r = lax.rsqrt(jnp.mean(x * x, axis=-1, keepdims=True) + EPS)
        xh = x * r
        gv = g_ref[...]
        err = xh * gv - t_ref[...]
        part_loss = jnp.zeros((1, BLOCK), F32) + 0.5 * jnp.sum(jnp.mean(err * err, axis=-1, keepdims=True))
        dy = err * (1.0 / d)
        dxh = dy * gv
        dh_ref[...] = r * (dxh - xh * jnp.mean(dxh * xh, axis=-1, keepdims=True))
        part_g = jnp.sum(dy * xh, axis=0, keepdims=True)

        @pl.when(i == 0)
        def _():
            dg_ref[...] = part_g
            loss_ref[...] = part_loss

        @pl.when(i > 0)
        def _():
            dg_ref[...] += part_g
            loss_ref[...] += part_loss

    row = pl.BlockSpec((tm, d), lambda i: (i, 0))
    vec = pl.BlockSpec((1, d), lambda i: (0, 0))
    return _pcall(body, name, (_sds((s, d), F32), _sds((1, d), F32), _sds((1, BLOCK), F32)), grid=(s // tm,),
                  in_specs=[row, vec, row], out_specs=(row, vec, pl.BlockSpec((1, BLOCK), lambda i: (0, 0))),
                  sem=("arbitrary",))(h, g, target)


def adamw(w, m, v, parts, name, token=None):
    nl, r, c = w.shape
    tr = max(t for t in range(1, min(r, 512) + 1) if r % t == 0 and (t % 16 == 0 or t == r))
    c1 = 1.0 / (1.0 - ADAM_B1 ** ADAM_STEP)
    c2 = 1.0 / (1.0 - ADAM_B2 ** ADAM_STEP)

    def body(w_ref, m_ref, v_ref, p_ref, *rest):
        g_ref, d_ref, nm_ref, nv_ref = rest[-4:]
        g = p_ref[0].astype(F32)
        for dev in range(1, N_DEV):
            g = g + p_ref[dev].astype(F32)
        nm = ADAM_B1 * m_ref[...] + (1.0 - ADAM_B1) * g
        nv = ADAM_B2 * v_ref[...] + (1.0 - ADAM_B2) * (g * g)
        g_ref[...] = g
        nm_ref[...] = nm
        nv_ref[...] = nv
        d_ref[...] = -ADAM_LR * ((nm * c1) / (jnp.sqrt(nv * c2) + ADAM_EPS) + ADAM_WD * w_ref[...])

    blk = pl.BlockSpec((None, tr, c), lambda l, i: (l, i, 0))
    out = _sds((nl, r, c), F32)
    extra = [] if token is None else [token]
    return _pcall(body, name, (out, out, out, out), grid=(nl, r // tr),
                  in_specs=[blk, blk, blk, pl.BlockSpec((N_DEV, None, tr, c), lambda l, i: (0, l, i, 0))]
                  + [ANY] * len(extra),
                  out_specs=(blk, blk, blk, blk), sem=("parallel", "parallel"))(w, m, v, parts, *extra)


def _place():
    x, y, c = lax.axis_index("x"), lax.axis_index("y"), lax.axis_index("c")
    return x, y, c


def _lin(px, py, pc):
    return 4 * px + 2 * py + pc


HBM = pl.BlockSpec(memory_space=pltpu.HBM)
SEM = pl.BlockSpec(memory_space=pltpu.SEMAPHORE)
EFFECT = pltpu.SideEffectType.DATAFLOW_SIDE_EFFECTING
N_PEERS = N_DEV - 1


def _peers_of(x, y, c):
    return [(x, y, 1 - c), (1 - x, y, c), (x, 1 - y, c), (1 - x, 1 - y, c),
            (1 - x, y, 1 - c), (x, 1 - y, 1 - c), (1 - x, 1 - y, 1 - c)]


def _in_hbm(a):
    return pltpu.with_memory_space_constraint(a, pltpu.HBM)


def send_start(name, bufs, copies, n_groups):
    nb = len(bufs)
    per_group = [[i for i, cp in enumerate(copies) if cp[0] == g] for g in range(n_groups)]

    def body(*refs):
        buf = refs[:nb]
        sems = refs[nb:nb + 2 * n_groups]
        token = refs[2 * nb + 2 * n_groups]
        x, y, c = _place()
        me = _lin(x, y, c)
        for g in range(n_groups):
            for slot, i in enumerate(per_group[g]):
                _, s, src_slab, d, land_slab = copies[i]
                for k, peer in enumerate(_peers_of(x, y, c)):
                    pltpu.make_async_remote_copy(
                        src_ref=src_slab(buf[s], _lin(*peer), me), dst_ref=land_slab(buf[d], me),
                        send_sem=sems[2 * g].at[slot * N_PEERS + k], recv_sem=sems[2 * g + 1].at[slot * N_PEERS + k],
                        device_id=peer, device_id_type=MESH).start()
        token[...] = jnp.zeros_like(token)

    sem_shapes = []
    for g in range(n_groups):
        sem_shapes += [pltpu.SemaphoreType.DMA((len(per_group[g]) * N_PEERS,))] * 2
    out = pl.pallas_call(
        body, name=name,
        out_shape=tuple(sem_shapes) + tuple(pltpu.HBM(b.shape, b.dtype) for b in bufs) + (_sds((8, 128), F32),),
        in_specs=[HBM] * nb,
        out_specs=tuple([SEM] * len(sem_shapes)) + tuple([HBM] * nb) + (pl.BlockSpec(memory_space=pltpu.VMEM),),
        input_output_aliases={i: len(sem_shapes) + i for i in range(nb)},
        compiler_params=pltpu.CompilerParams(has_side_effects=EFFECT))(*[_in_hbm(b) for b in bufs])
    sems = [(out[2 * g], out[2 * g + 1]) for g in range(n_groups)]
    return sems, list(out[2 * n_groups:2 * n_groups + nb]), out[2 * n_groups + nb]


N_FIRST = 4
N_RELAY = 3


def _gather_peers(x, y, c):
    first = [(x, y, 1 - c), (1 - x, y, c), (x, 1 - y, c), (1 - x, 1 - y, c)]
    return first, first[1:]


def gather_start(name, bufs, copies, n_groups):
    nb = len(bufs)
    per_group = [[i for i, cp in enumerate(copies) if cp[0] == g] for g in range(n_groups)]

    def body(*refs):
        buf = refs[:nb]
        sems = refs[nb:nb + 2 * n_groups]
        token = refs[2 * nb + 2 * n_groups]
        x, y, c = _place()
        me = _lin(x, y, c)
        first, _ = _gather_peers(x, y, c)
        for g in range(n_groups):
            for slot, i in enumerate(per_group[g]):
                _, d, slab = copies[i]
                for k, peer in enumerate(first):
                    pltpu.make_async_remote_copy(
                        src_ref=slab(buf[d], me), dst_ref=slab(buf[d], me),
                        send_sem=sems[2 * g].at[slot * N_FIRST + k], recv_sem=sems[2 * g + 1].at[slot * N_FIRST + k],
                        device_id=peer, device_id_type=MESH).start()
        token[...] = jnp.zeros_like(token)

    sem_shapes = []
    for g in range(n_groups):
        sem_shapes += [pltpu.SemaphoreType.DMA((len(per_group[g]) * N_FIRST,))] * 2
    out = pl.pallas_call(
        body, name=name,
        out_shape=tuple(sem_shapes) + tuple(pltpu.HBM(b.shape, b.dtype) for b in bufs) + (_sds((8, 128), F32),),
        in_specs=[HBM] * nb,
        out_specs=tuple([SEM] * len(sem_shapes)) + tuple([HBM] * nb) + (pl.BlockSpec(memory_space=pltpu.VMEM),),
        input_output_aliases={i: len(sem_shapes) + i for i in range(nb)},
        compiler_params=pltpu.CompilerParams(has_side_effects=EFFECT))(*[_in_hbm(b) for b in bufs])
    return [(out[2 * g], out[2 * g + 1]) for g in range(n_groups)], list(out[2 * n_groups:2 * n_groups + nb])


def gather_relay(name, bufs, slabs, first_sems, after):
    nb = len(bufs)

    def body(*refs):
        buf = refs[:nb]
        send_a, recv_a = refs[nb], refs[nb + 1]
        send_b, recv_b = refs[nb + 3], refs[nb + 4]
        x, y, c = _place()
        first, origins = _gather_peers(x, y, c)
        for n, slab in enumerate(slabs):
            for j, origin in enumerate(origins):
                block = slab(buf[n], _lin(*origin))
                pltpu.make_async_remote_copy(
                    src_ref=block, dst_ref=block, send_sem=send_a.at[n * N_FIRST + 1 + j],
                    recv_sem=recv_a.at[n * N_FIRST + 1 + j], device_id=origin, device_id_type=MESH).wait_recv()
                pltpu.make_async_remote_copy(
                    src_ref=block, dst_ref=block, send_sem=send_b.at[n * N_RELAY + j],
                    recv_sem=recv_b.at[n * N_RELAY + j], device_id=first[0], device_id_type=MESH).start()

    sem_shape = pltpu.SemaphoreType.DMA((nb * N_RELAY,))
    out = pl.pallas_call(
        body, name=name, out_shape=(sem_shape, sem_shape) + tuple(pltpu.HBM(b.shape, b.dtype) for b in bufs),
        in_specs=[HBM] * nb + [SEM, SEM, ANY], out_specs=(SEM, SEM) + tuple([HBM] * nb),
        input_output_aliases={i: 2 + i for i in range(nb)},
        compiler_params=pltpu.CompilerParams(has_side_effects=EFFECT))(*bufs, first_sems[0], first_sems[1], after)
    return (out[0], out[1]), list(out[2:])


def gather_wait(name, bufs, slabs, first_sems, relay_sems, after):
    nb = len(bufs)

    def body(*refs):
        buf = refs[:nb]
        send_a, recv_a, send_b, recv_b = refs[nb:nb + 4]
        x, y, c = _place()
        me = _lin(x, y, c)
        first, origins = _gather_peers(x, y, c)
        sibling = first[0]
        for n, slab in enumerate(slabs):
            mine = slab(buf[n], me)
            for k, peer in enumerate(first):
                pltpu.make_async_remote_copy(
                    src_ref=mine, dst_ref=mine, send_sem=send_a.at[n * N_FIRST + k],
                    recv_sem=recv_a.at[n * N_FIRST + k], device_id=peer, device_id_type=MESH).wait_send()
            theirs = slab(buf[n], _lin(*sibling))
            pltpu.make_async_remote_copy(
                src_ref=theirs, dst_ref=theirs, send_sem=send_a.at[n * N_FIRST], recv_sem=recv_a.at[n * N_FIRST],
                device_id=sibling, device_id_type=MESH).wait_recv()
            for j, (ox, oy, oc) in enumerate(origins):
                sent = slab(buf[n], _lin(ox, oy, oc))
                got = slab(buf[n], _lin(ox, oy, 1 - oc))
                pltpu.make_async_remote_copy(
                    src_ref=sent, dst_ref=got, send_sem=send_b.at[n * N_RELAY + j],
                    recv_sem=recv_b.at[n * N_RELAY + j], device_id=sibling, device_id_type=MESH).wait()

    out = pl.pallas_call(
        body, name=name, out_shape=tuple(pltpu.HBM(b.shape, b.dtype) for b in bufs),
        in_specs=[HBM] * nb + [SEM] * 4 + [ANY], out_specs=tuple([HBM] * nb),
        input_output_aliases={i: i for i in range(nb)},
        compiler_params=pltpu.CompilerParams(has_side_effects=EFFECT))(
            *bufs, first_sems[0], first_sems[1], relay_sems[0], relay_sems[1], after)
    return list(out)


def send_wait(name, bufs, copies, sems, after):
    nb = len(bufs)

    def body(*refs):
        buf = refs[:nb]
        send_sems, recv_sems = refs[nb], refs[nb + 1]
        x, y, c = _place()
        me = _lin(x, y, c)
        for slot, (s, src_slab, d, land_slab) in enumerate(copies):
            for k, peer in enumerate(_peers_of(x, y, c)):
                j = _lin(*peer)
                cp = pltpu.make_async_remote_copy(
                    src_ref=src_slab(buf[s], j, me), dst_ref=land_slab(buf[d], j),
                    send_sem=send_sems.at[slot * N_PEERS + k], recv_sem=recv_sems.at[slot * N_PEERS + k],
                    device_id=peer, device_id_type=MESH)
                cp.wait_send()
                cp.wait_recv()

    out = pl.pallas_call(
        body, name=name, out_shape=tuple(pltpu.HBM(b.shape, b.dtype) for b in bufs),
        in_specs=[HBM] * nb + [SEM, SEM, ANY], out_specs=tuple([HBM] * nb),
        input_output_aliases={i: i for i in range(nb)},
        compiler_params=pltpu.CompilerParams(has_side_effects=EFFECT))(*bufs, sems[0], sems[1], after)
    return list(out)


def local_step(x, target, weights, rep, emit):
    s = x.shape[0]
    bucket = jnp.asarray(_bucket_table())
    bias = bias_table(rep["rel_bias"], bucket, "bias_table")
    h = x
    saved = []
    kv = None
    h_kv = u_kv = None
    small = None
    u = rms_fwd(h, rep["norm_mix"][0:1], "rms_mix_fwd0")
    for l in range(4):
        g_ffn = rep["norm_ffn"][l:l + 1]
        rec = {"h_in": h, "u": u}
        if l < 2:
            w = weights(f"conv{l}", u)
            if l == 0:
                small = w
            cp = small["cp"][l]
            t = pw1_fwd(u, w["pw1"], small["b_pw1"], l, f"pw1_fwd{l}")
            z, y = conv_fwd(t, cp, f"conv_fwd{l}")
            weights(f"ffn{l}", z, relay=True)
            h, uf = mm_nn(f"pw2_fwd{l}", z, w["pw2"], (None, D_MODEL, D_MODEL), lambda i, j: (0, 0, j), D_MODEL,
                          D_MODEL, F32, res=h, norm=g_ffn,
                          bias=(small["b_pw2"], pl.BlockSpec((None, 1, D_MODEL), lambda i, j, l=l: (l, 0, j))))
            rec.update(t=t, z=z, y=y, cp=cp)
        else:
            a = l - 2
            w = weights(f"attn{a}", u)
            if a == 0:
                h_kv = h
                w_kv = w["wkv"]
                kv = mm_nn("kv_fwd", u_kv, w_kv, (D_MODEL, 2 * KV_DIM), lambda i, j: (0, 0), 2 * KV_DIM,
                           2 * KV_DIM, BF16)
            q = mm_nn(f"q_fwd{a}", u, w["wq"], (None, D_MODEL, D_MODEL), lambda i, j: (0, 0, j), D_MODEL, D_MODEL,
                      BF16)
            o = attn_fwd(q, kv, bias, rep["sinks"][a:a + 1], f"attn_fwd{a}")
            weights(f"ffn{l}", o, relay=True)
            h, uf = mm_nn(f"o_fwd{a}", o, w["wo"], (None, D_MODEL, D_MODEL), lambda i, j: (0, 0, j), D_MODEL,
                          D_MODEL, F32, res=h, norm=g_ffn)
            rec.update(q=q, o=o)
        rec["w"] = w
        rec["h_mid"] = h
        wf = weights(f"ffn{l}", uf)
        if l < 3:
            weights(("conv1", "attn0", "attn1")[l], uf, relay=True)
        nxt = [] if l == 3 else [rep["norm_mix"][l + 1:l + 2]] + ([rep["norm_kv"]] if l == 1 else [])
        h, gu, *normed = ffn_fwd(uf, h, wf["up"], wf["down"], 0, f"ffn_fwd{l}", norms=nxt)
        if normed:
            u = normed[0]
        if l == 1:
            u_kv = normed[1]
        rec.update(uf=uf, gu=gu, wf=wf)
        saved.append(rec)

    dh, d_nfin, loss = loss_head(h, rep["norm_final"], target, "loss_head")

    d_mix, d_ffn = [None] * 4, [None] * 4
    cp_grads = [None, None]
    dkv_parts, dbias_parts, dsinks = [], [], [None, None]
    d_nkv = None
    full_rows = lambda tk: (tk, D_MODEL)
    tok = None
    for l in reversed(range(4)):
        rec = saved[l]
        w, wf = rec["w"], rec["wf"]
        grads = {}
        g_mix = rep["norm_mix"][l:l + 1]
        g_ffn = rep["norm_ffn"][l:l + 1]
        dh_mid, d_ffn[l], act, dgu = ffn_bwd(dh, rec["gu"], wf["up"], wf["down"], 0, rec["h_mid"], g_ffn,
                                             f"ffn_bwd{l}", token=tok)
        g_down = mm_tn(
            f"down_grad{l}", act, dh, groups=4, a_block=lambda tk: (None, tk, FF_CHUNK), a_index=lambda j, k: (j, k, 0),
            b_block=full_rows, b_index=lambda j, k: (k, 0), o_block=(None, FF_CHUNK, D_MODEL),
            o_index=lambda j, k: (j, 0, 0), o_shape=(4, FF_CHUNK, D_MODEL), acc_shape=(FF_CHUNK, D_MODEL))
        g_up = mm_tn(
            f"up_grad{l}", dgu.reshape(8, s, FF_CHUNK), rec["uf"], groups=8, a_block=lambda tk: (None, tk, FF_CHUNK),
            a_index=lambda j, k: (j, k, 0), b_block=full_rows, b_index=lambda j, k: (k, 0),
            o_block=(None, FF_CHUNK, D_MODEL), o_index=lambda j, k: (j, 0, 0), o_shape=(8, FF_CHUNK, D_MODEL),
            acc_shape=(FF_CHUNK, D_MODEL))
        tok = emit(f"ffn{l}", {"up": g_up, "down": g_down.reshape(D_FF, D_MODEL)})
        dh = dh_mid
        if l < 2:
            dz = mm_nt(f"pw2_bwd{l}", dh, w["pw2"], (None, D_MODEL, D_MODEL), lambda i, k: (0, 0, 0), D_MODEL, F32,
                       token=tok)
            grads["pw2"], db2 = mm_tn(
                f"pw2_grad{l}", rec["z"], dh, groups=1, a_block=full_rows, a_index=lambda j, k: (k, 0),
                b_block=full_rows, b_index=lambda j, k: (k, 0), o_block=(D_MODEL, D_MODEL), o_index=lambda j, k: (0, 0),
                o_shape=(D_MODEL, D_MODEL), acc_shape=(D_MODEL, D_MODEL),
                colsum=((1, D_MODEL), pl.BlockSpec((1, D_MODEL), lambda j, k: (0, 0))))
            dt, stats = conv_bwd(rec["t"], rec["y"], dz, rec["cp"], f"conv_bwd{l}")
            grads["pw1"], db1 = pw1_grad(rec["u"], dt, f"pw1_grad{l}")
            cp_grads[l] = (stats, db2, db1)
            tok = emit(f"conv{l}", grads)
            dh, d_mix[l] = pw1_bwd(dt, w["pw1"], rec["h_in"], g_mix, dh, f"pw1_bwd{l}")
        else:
            a = l - 2
            do = mm_nt(f"o_bwd{a}", dh, w["wo"], (None, D_MODEL, D_MODEL), lambda i, k: (0, 0, 0), D_MODEL, BF16,
                       token=tok)
            tok = None
            grads["wo"] = mm_tn(
                f"wo_grad{a}", rec["o"], dh, groups=1, a_block=full_rows, a_index=lambda j, k: (k, 0),
                b_block=full_rows, b_index=lambda j, k: (k, 0), o_block=(D_MODEL, D_MODEL), o_index=lambda j, k: (0, 0),
                o_shape=(D_MODEL, D_MODEL), acc_shape=(D_MODEL, D_MODEL))
            dq, dkv_p, dbias_p, dsinks[a] = attn_bwd(rec["q"], kv, do, bias, rep["sinks"][a:a + 1], f"attn_bwd{a}")
            dkv_parts.append(dkv_p)
            dbias_parts.append(dbias_p)
            grads["wq"] = mm_tn(
                f"wq_grad{a}", rec["u"], dq, groups=1, a_block=full_rows, a_index=lambda j, k: (k, 0),
                b_block=full_rows, b_index=lambda j, k: (k, 0), o_block=(D_MODEL, D_MODEL), o_index=lambda j, k: (0, 0),
                o_shape=(D_MODEL, D_MODEL), acc_shape=(D_MODEL, D_MODEL))
            if a == 1:
                tok = emit("attn1", grads)
            dh, d_mix[l] = nt_rms_bwd(f"q_bwd{a}", dq, w["wq"], (None, D_MODEL, D_MODEL), lambda i: (0, 0, 0),
                                      rec["h_in"], g_mix, dh)
        if l == 2:
            dkv = dkv_combine(dkv_parts[0], dkv_parts[1], "dkv_combine")
            grads["wkv"] = mm_tn(
                "wkv_grad", u_kv, dkv, groups=1, a_block=full_rows, a_index=lambda j, k: (k, 0),
                b_block=lambda tk: (tk, 2 * KV_DIM), b_index=lambda j, k: (k, 0), o_block=(D_MODEL, 2 * KV_DIM),
                o_index=lambda j, k: (0, 0), o_shape=(D_MODEL, 2 * KV_DIM), acc_shape=(D_MODEL, 2 * KV_DIM))
            tok = emit("attn0", grads)
            dh, d_nkv = nt_rms_bwd("kv_bwd", dkv, w_kv, (D_MODEL, 2 * KV_DIM), lambda i: (0, 0), h_kv,
                                   rep["norm_kv"], dh)

    d_relb = bias_grad(dbias_parts[0], dbias_parts[1], bucket, "bias_grad")
    d_sinks = jnp.concatenate([dsinks[0][0, :N_HEADS], dsinks[1][0, :N_HEADS]])
    tail = jnp.zeros((D_MODEL,), F32)
    rep_grad = jnp.concatenate([
        jnp.concatenate(d_mix, axis=0), jnp.concatenate(d_ffn, axis=0), d_nkv, d_nfin,
        tail.at[:2 * N_HEADS].set(d_sinks)[None], tail.at[:N_BUCKETS * N_HEADS].set(d_relb.reshape(-1))[None],
        tail.at[0].set(loss[0, 0])[None], jnp.zeros((REP_ROWS - ROW_LOSS - 1, D_MODEL), F32)], axis=0)
    return dh, cp_grads, rep_grad


def _pack_conv(w_dw, b_dw, ln_g, ln_b, b_pw2, b_pw1):
    rows = [w_dw, b_dw[:, None], ln_g[:, None], ln_b[:, None], b_pw2[:, None], b_pw1.reshape(2, 2, 128),
            jnp.zeros((2, PACK_ROWS - ROW_BPW1 - 2, 128), F32)]
    return jnp.concatenate(rows, axis=1)


def _unpack_conv(p):
    return (p[:, :CONV_WIDTH], p[:, ROW_BDW], p[:, ROW_LNG], p[:, ROW_LNB], p[:, ROW_BPW2],
            p[:, ROW_BPW1:ROW_BPW1 + 2].reshape(2, 256))


def _pack_rep(norm_mix, norm_ffn, norm_kv, norm_final, sinks, rel_bias):
    tail = jnp.zeros((D_MODEL,), F32)
    return jnp.concatenate([
        norm_mix, norm_ffn, norm_kv[None], norm_final[None], tail.at[:2 * N_HEADS].set(sinks.reshape(-1))[None],
        tail.at[:N_BUCKETS * N_HEADS].set(rel_bias.reshape(-1))[None],
        jnp.zeros((REP_ROWS - ROW_RELB - 1, D_MODEL), F32)], axis=0)


def _unpack_rep(p):
    return (p[0:4], p[4:8], p[ROW_NKV], p[ROW_NFIN], p[ROW_SINK, :2 * N_HEADS].reshape(2, N_HEADS),
            p[ROW_RELB, :N_BUCKETS * N_HEADS].reshape(N_BUCKETS, N_HEADS))


def kernel(x, norm_mix, norm_ffn, conv_w_pw1, conv_b_pw1, conv_w_dw, conv_b_dw, conv_ln_g, conv_ln_b, conv_w_pw2, conv_b_pw2, norm_kv, w_kv, w_q, w_o, sinks, rel_bias, ffn_w_up, ffn_w_down, norm_final, loss_target, m_norm_mix, m_norm_ffn, m_conv_w_pw1, m_conv_b_pw1, m_conv_w_dw, m_conv_b_dw, m_conv_ln_g, m_conv_ln_b, m_conv_w_pw2, m_conv_b_pw2, m_norm_kv, m_w_kv, m_w_q, m_w_o, m_sinks, m_rel_bias, m_ffn_w_up, m_ffn_w_down, m_norm_final, v_norm_mix, v_norm_ffn, v_conv_w_pw1, v_conv_b_pw1, v_conv_w_dw, v_conv_b_dw, v_conv_ln_g, v_conv_ln_b, v_conv_w_pw2, v_conv_b_pw2, v_norm_kv, v_w_kv, v_w_q, v_w_o, v_sinks, v_rel_bias, v_ffn_w_up, v_ffn_w_down, v_norm_final):
    s = x.shape[1]
    d = D_MODEL
    rsh = d // N_DEV
    dsh = D_FF // N_DEV

    me = _lin(*_place())
    lead_slab = lambda ref, j: ref.at[j]
    rows_of = lambda rows: (lambda ref, j: ref.at[pl.ds(j * rows, rows), :])

    conv_pack = _pack_conv(conv_w_dw, conv_b_dw, conv_ln_g, conv_ln_b, conv_b_pw2, conv_b_pw1)
    ag_order = ["conv0", "ffn0", "conv1", "ffn1", "attn0", "ffn2", "attn1", "ffn3"]
    ag_land, ag_copies, ag_members = [], [], {g: [] for g in ag_order}

    def gather(group, key, shard, land_shape, at, land_slab):
        i = len(ag_land)
        ag_land.append(lax.dynamic_update_slice(lax.empty(land_shape, shard.dtype), shard, at(me)))
        ag_copies.append((ag_order.index(group), i, land_slab))
        ag_members[group].append((key, i, land_slab))

    cols_at0 = lambda ref, j: ref.at[0, j]
    rows_at0 = lambda rows: (lambda ref, j: ref.at[0, pl.ds(j * rows, rows), :])
    col_at = lambda m: (0, m, 0, 0)
    row_at = lambda rows: (lambda m: (0, m * rows, 0))
    for l in range(2):
        gather(f"conv{l}", "pw1", conv_w_pw1[l].astype(BF16)[None, None], (1, N_DEV, d, 256), col_at, cols_at0)
        gather(f"conv{l}", "pw2", conv_w_pw2[l].astype(BF16)[None], (1, d, d), row_at(rsh), rows_at0(rsh))
    gather("conv0", "pack", conv_pack[None], (N_DEV, 2, PACK_ROWS, 128), lambda m: (m, 0, 0, 0), lead_slab)
    gather("attn0", "wkv", w_kv.astype(BF16), (d, 2 * KV_DIM), lambda m: (m * rsh, 0), rows_of(rsh))
    for a in range(2):
        gather(f"attn{a}", "wq", w_q[a].astype(BF16)[None], (1, d, d), row_at(rsh), rows_at0(rsh))
        gather(f"attn{a}", "wo", w_o[a].astype(BF16)[None], (1, d, d), row_at(rsh), rows_at0(rsh))
    up_t, m_up_t, v_up_t = (jnp.swapaxes(a, 1, 2) for a in (ffn_w_up, m_ffn_w_up, v_ffn_w_up))
    for l in range(4):
        gather(f"ffn{l}", "up", up_t[l].astype(BF16)[None, None], (1, N_DEV, FF_CHUNK, d), col_at, cols_at0)
        gather(f"ffn{l}", "down", ffn_w_down[l].astype(BF16)[None], (1, D_FF, d), row_at(dsh), rows_at0(dsh))
    ag_sems, ag_land_thru = gather_start("ag_start", ag_land, ag_copies, len(ag_order))
    relayed = {}

    def weights(group, after, relay=False):
        members = ag_members[group]
        slabs = [slab for _, _, slab in members]
        first_sems = ag_sems[ag_order.index(group)]
        if group not in relayed:
            relayed[group] = gather_relay(f"ag_relay_{group}", [ag_land_thru[i] for _, i, _ in members], slabs,
                                          first_sems, after)
        if relay:
            return None
        relay_sems, bufs = relayed[group]
        lands = gather_wait(f"ag_wait_{group}", bufs, slabs, first_sems, relay_sems, after)
        w = {key: land for (key, _, _), land in zip(members, lands)}
        if "up" in w:
            w["up"] = w["up"].reshape(1, 2, 4, FF_CHUNK, d)
            w["down"] = w["down"].reshape(1, 4, FF_CHUNK, d)
        if "pack" in w:
            pack_g = w.pop("pack")
            w["cp"] = jnp.transpose(pack_g, (1, 2, 0, 3)).reshape(2, PACK_ROWS, d)
            w["b_pw1"] = pack_g[:, :, ROW_BPW1:ROW_BPW1 + 2, :].transpose(1, 0, 2, 3).reshape(2, 1, 2 * d)
            w["b_pw2"] = w["cp"][:, ROW_BPW2:ROW_BPW2 + 1, :]
        return w

    shard_shapes = {"pw1": (d, 256), "pw2": (rsh, d), "wkv": (rsh, 2 * KV_DIM), "wq": (rsh, d), "wo": (rsh, d),
                    "up": (FF_CHUNK, d), "down": (dsh, d), "cp": (2, PACK_ROWS, 128), "rep": (REP_ROWS, d)}
    n_layers = {"pw1": 2, "pw2": 2, "wkv": 1, "wq": 2, "wo": 2, "up": 4, "down": 4, "cp": 1, "rep": 1}
    by_lead = (lambda ref, j, me_: ref.at[j], lambda g: lax.dynamic_index_in_dim(g, me, 0, keepdims=False))
    by_rows = lambda rows: (lambda ref, j, me_: ref.at[pl.ds(j * rows, rows), :],
                            lambda g: lax.dynamic_slice_in_dim(g, me * rows, rows, 0))
    all_of = (lambda ref, j, me_: ref, lambda g: g)
    owned = {"pw1": by_lead, "pw2": by_rows(rsh), "wkv": by_rows(rsh), "wq": by_rows(rsh), "wo": by_rows(rsh),
             "up": by_lead, "down": by_rows(dsh), "cp": by_lead, "rep": all_of}
    parts = {}
    pending = {}

    def finish(chain, after):
        keys, bufs, copies, sems, name = pending.pop(chain)
        done = send_wait(f"rs_wait_{name}", bufs, copies, sems, after)
        parts.update(zip(keys, done[len(keys):]))

    def exchange(chain, name, layer, grads):
        keys = list(grads)
        if chain in pending:
            finish(chain, grads[keys[0]])
        lands = []
        for k in keys:
            land = parts.pop(k) if k in parts else lax.empty((N_DEV, n_layers[k]) + shard_shapes[k], grads[k].dtype)
            mine = owned[k][1](grads[k])[None, None]
            lands.append(lax.dynamic_update_slice(land, mine, (me, layer) + (0,) * len(shard_shapes[k])))
        land_at = lambda ref, i: ref.at[i, layer]
        copies = [(0, n, owned[k][0], len(keys) + n, land_at) for n, k in enumerate(keys)]
        sems, thru, token = send_start(f"rs_start_{name}", [grads[k] for k in keys] + lands, copies, 1)
        pending[chain] = (keys, thru, [c[1:] for c in copies], sems[0], name)
        return token

    def emit(group, grads):
        return exchange(group[:-1], group, int(group[-1]), grads)

    rep = {"norm_mix": norm_mix, "norm_ffn": norm_ffn, "norm_kv": norm_kv[None], "norm_final": norm_final[None],
           "sinks": sinks, "rel_bias": rel_bias}

    grad_x, cp_grads, rep_grad = local_step(x[0], loss_target[0], weights, rep, emit)

    cp_full = []
    for l in range(2):
        stats, db2, db1 = cp_grads[l]
        cp_full.append(jnp.concatenate([
            stats[:ROW_BPW2], db2, db1.reshape(N_DEV, 2, 128).transpose(1, 0, 2).reshape(2, d),
            jnp.zeros((PACK_ROWS - ROW_BPW1 - 2, d), F32)], axis=0))
    cp_send = jnp.stack(cp_full).reshape(2, PACK_ROWS, N_DEV, 128).transpose(2, 0, 1, 3)
    tail_token = exchange("tail", "tail", 0, {"cp": cp_send, "rep": rep_grad})

    def update(key, w, m, v, name, token=None):
        p = parts[key]
        w3 = w.reshape(p.shape[1:])
        outs = adamw(w3, m.reshape(w3.shape), v.reshape(w3.shape), p, name, token=token)
        return [o.reshape(w.shape) for o in outs]

    res = {}
    finish("ffn", grad_x)
    res["ffn_w_up"] = [jnp.swapaxes(o, 1, 2) for o in update("up", up_t, m_up_t, v_up_t, "adam_up", tail_token)]
    res["ffn_w_down"] = update("down", ffn_w_down, m_ffn_w_down, v_ffn_w_down, "adam_down", res["ffn_w_up"][0])
    finish("attn", res["ffn_w_down"][0])
    res["w_kv"] = update("wkv", w_kv, m_w_kv, v_w_kv, "adam_wkv")
    res["w_q"] = update("wq", w_q, m_w_q, v_w_q, "adam_wq")
    res["w_o"] = update("wo", w_o, m_w_o, v_w_o, "adam_wo")
    finish("conv", res["w_o"][0])
    res["conv_w_pw1"] = update("pw1", conv_w_pw1, m_conv_w_pw1, v_conv_w_pw1, "adam_pw1")
    res["conv_w_pw2"] = update("pw2", conv_w_pw2, m_conv_w_pw2, v_conv_w_pw2, "adam_pw2")
    finish("tail", res["conv_w_pw2"][0])
    m_pack = _pack_conv(m_conv_w_dw, m_conv_b_dw, m_conv_ln_g, m_conv_ln_b, m_conv_b_pw2, m_conv_b_pw1)
    v_pack = _pack_conv(v_conv_w_dw, v_conv_b_dw, v_conv_ln_g, v_conv_ln_b, v_conv_b_pw2, v_conv_b_pw1)
    cp_res = adamw(conv_pack, m_pack, v_pack, parts["cp"].reshape(N_DEV, 2, PACK_ROWS, 128), "adam_conv_pack")
    rep_w = _pack_rep(norm_mix, norm_ffn, norm_kv, norm_final, sinks, rel_bias)
    rep_m = _pack_rep(m_norm_mix, m_norm_ffn, m_norm_kv, m_norm_final, m_sinks, m_rel_bias)
    rep_v = _pack_rep(v_norm_mix, v_norm_ffn, v_norm_kv, v_norm_final, v_sinks, v_rel_bias)
    rep_res = adamw(rep_w[None], rep_m[None], rep_v[None], parts["rep"], "adam_rep")
    loss = rep_res[0][0, ROW_LOSS, 0]

    outs = []
    for kind in range(4):
        cw_dw, cb_dw, cln_g, cln_b, cb_pw2, cb_pw1 = _unpack_conv(cp_res[kind])
        r_mix, r_ffn, r_nkv, r_nfin, r_sinks, r_relb = _unpack_rep(rep_res[kind][0])
        outs += [r_mix, r_ffn, res["conv_w_pw1"][kind], cb_pw1, cw_dw, cb_dw, cln_g, cln_b, res["conv_w_pw2"][kind],
                 cb_pw2, r_nkv, res["w_kv"][kind], res["w_q"][kind], res["w_o"][kind], r_sinks, r_relb,
                 res["ffn_w_up"][kind], res["ffn_w_down"][kind], r_nfin]
    return (loss, grad_x[None], *outs)
```

```python
import functools
import math

import numpy as np
import jax
import jax.numpy as jnp
from jax import lax
from jax.experimental import pallas as pl
from jax.experimental.pallas import tpu as pltpu

F32 = jnp.float32
BF16 = jnp.bfloat16

D_MODEL = 1024
D_FF = 2816
N_HEADS = 16
N_KV_HEADS = 4
GROUP = N_HEADS // N_KV_HEADS
HEAD_DIM = 64
KV_DIM = N_KV_HEADS * HEAD_DIM
BLOCK = 128
CONV_WIDTH = 31
HALO = 32
N_BUCKETS = 32
MAX_DISTANCE = 128
EPS = 1e-6
NEG_INF = -1e30
N_DEV = 8
FF_CHUNK = D_FF // 4
PACK_ROWS = 40
ROW_BDW, ROW_LNG, ROW_LNB, ROW_BPW2, ROW_BPW1 = 31, 32, 33, 34, 35
REP_ROWS = 16
ROW_NKV, ROW_NFIN, ROW_SINK, ROW_RELB, ROW_LOSS = 8, 9, 10, 11, 12

ADAM_LR, ADAM_B1, ADAM_B2, ADAM_EPS, ADAM_WD, ADAM_STEP = 0.001, 0.9, 0.999, 1e-08, 0.01, 10

VMEM_LIMIT_BYTES = 56 * 1024 * 1024
FFN_ROWS = 1024
FFN_BWD_ROWS = 512
GRAD_ROWS = 2048
ANY = pl.BlockSpec(memory_space=pl.ANY)
MESH = pl.DeviceIdType.MESH

NN = (((1,), (0,)), ((), ()))
NT = (((1,), (1,)), ((), ()))
TN = (((0,), (0,)), ((), ()))


def _dot(a, b, dims):
    return lax.dot_general(a, b, dims, preferred_element_type=F32)


def _pcall(body, name, out_shape, *, grid=None, in_specs=None, out_specs=None, scratch=(), sem=None, **kw):
    params = pltpu.CompilerParams(dimension_semantics=sem, vmem_limit_bytes=VMEM_LIMIT_BYTES)
    extra = {} if grid is None else {"grid": grid}
    return pl.pallas_call(body, name=name, out_shape=out_shape, in_specs=in_specs, out_specs=out_specs,
                          scratch_shapes=list(scratch), compiler_params=params, **extra, **kw)


def _sds(shape, dtype):
    return jax.ShapeDtypeStruct(tuple(shape), dtype)


def _row_tile(s, want):
    return want if s % want == 0 else s


def rms_fwd(h, g, name):
    s, d = h.shape
    tm = _row_tile(s, 512)

    def body(h_ref, g_ref, u_ref):
        x = h_ref[...]
        r = lax.rsqrt(jnp.mean(x * x, axis=-1, keepdims=True) + EPS)
        u_ref[...] = (x * r * g_ref[...]).astype(BF16)

    return _pcall(body, name, _sds((s, d), BF16), grid=(s // tm,),
                  in_specs=[pl.BlockSpec((tm, d), lambda i: (i, 0)), pl.BlockSpec((1, d), lambda i: (0, 0))],
                  out_specs=pl.BlockSpec((tm, d), lambda i: (i, 0)), sem=("parallel",))(h, g)


def _rms_rows(x, gain):
    return (x * lax.rsqrt(jnp.mean(x * x, axis=-1, keepdims=True) + EPS) * gain).astype(BF16)


def _mm(name, a, b, *, dims, grid, a_spec, b_spec, o_spec, o_shape, nk=1, acc_shape=None,
        bias=None, res=None, colsum=None, sem=None, token=None, norm=None):
    n_axes = len(grid)

    def body(*refs):
        it = iter(refs)
        a_ref, b_ref = next(it), next(it)
        bias_ref = next(it) if bias is not None else None
        res_ref = next(it) if res is not None else None
        gain_ref = next(it) if norm is not None else None
        if token is not None:
            next(it)
        o_ref = next(it)
        un_ref = next(it) if norm is not None else None
        cs_ref = next(it) if colsum is not None else None
        acc_ref = next(it) if nk > 1 else None
        k = pl.program_id(n_axes - 1)
        p = _dot(a_ref[...].astype(BF16), b_ref[...].astype(BF16), dims)

        def finish(acc):
            if bias_ref is not None:
                acc = acc + bias_ref[...]
            if res_ref is not None:
                acc = acc + res_ref[...]
            o_ref[...] = acc.astype(o_ref.dtype)
            if un_ref is not None:
                un_ref[...] = _rms_rows(acc, gain_ref[...])

        if nk == 1:
            finish(p)
        else:
            @pl.when(k == 0)
            def _():
                acc_ref[...] = p

            @pl.when(k > 0)
            def _():
                acc_ref[...] += p

            @pl.when(k == nk - 1)
            def _():
                finish(acc_ref[...])

        if cs_ref is not None:
            cs = jnp.sum(b_ref[...].astype(F32), axis=0, keepdims=True)

            @pl.when(k == 0)
            def _():
                cs_ref[...] = cs

            @pl.when(k > 0)
            def _():
                cs_ref[...] += cs

    ins, in_specs = [a, b], [a_spec, b_spec]
    gain = None if norm is None else (norm, pl.BlockSpec(norm.shape, lambda *_: (0,) * norm.ndim))
    for extra in (bias, res, gain, None if token is None else (token, ANY)):
        if extra is not None:
            ins.append(extra[0])
            in_specs.append(extra[1])
    out_shape, out_specs = o_shape, o_spec
    if norm is not None:
        out_shape, out_specs = (o_shape, _sds(o_shape.shape, BF16)), (o_spec, o_spec)
    if colsum is not None:
        out_shape, out_specs = (o_shape, _sds(colsum[0], F32)), (o_spec, colsum[1])
    scratch = [pltpu.VMEM(acc_shape, F32)] if nk > 1 else []
    if sem is None:
        sem = ("parallel",) * (n_axes - 1) + ("arbitrary",)
    return _pcall(body, name, out_shape, grid=grid, in_specs=in_specs, out_specs=out_specs, scratch=scratch,
                  sem=sem)(*ins)


def mm_nn(name, a, w, w_block, w_index, n, tn, out_dtype, bias=None, res=None, tm=1024, norm=None, token=None):
    s, k = a.shape
    tm = _row_tile(s, tm)
    col = lambda i, j: (i, j)
    extras = {"token": token}
    if bias is not None:
        extras["bias"] = bias
    if res is not None:
        extras["res"] = (res, pl.BlockSpec((tm, tn), col))
    if norm is not None:
        assert tn == n, "a fused norm needs whole rows"
        extras["norm"] = norm
    return _mm(name, a, w, dims=NN, grid=(s // tm, n // tn), a_spec=pl.BlockSpec((tm, k), lambda i, j: (i, 0)),
               b_spec=pl.BlockSpec(w_block, w_index), o_spec=pl.BlockSpec((tm, tn), col), o_shape=_sds((s, n), out_dtype),
               sem=("parallel", "arbitrary"), **extras)


def mm_nt(name, a, w, w_block, w_index, kout, out_dtype, nk=1, tk=None, tm=1024, token=None):
    s, n = a.shape
    tm = _row_tile(s, tm)
    tk = n if tk is None else tk
    return _mm(name, a, w, dims=NT, grid=(s // tm, nk), a_spec=pl.BlockSpec((tm, tk), lambda i, k: (i, k)),
               b_spec=pl.BlockSpec(w_block, w_index), o_spec=pl.BlockSpec((tm, kout), lambda i, k: (i, 0)),
               o_shape=_sds((s, kout), out_dtype), nk=nk, acc_shape=(tm, kout), token=token)


def mm_tn(name, a, b, *, groups, a_block, a_index, b_block, b_index, o_block, o_index, o_shape, acc_shape,
          colsum=None, tk=GRAD_ROWS):
    s = a.shape[-2]
    tk = _row_tile(s, tk)
    return _mm(name, a, b, dims=TN, grid=(groups, s // tk), a_spec=pl.BlockSpec(a_block(tk), a_index),
               b_spec=pl.BlockSpec(b_block(tk), b_index), o_spec=pl.BlockSpec(o_block, o_index),
               o_shape=_sds(o_shape, BF16), nk=s // tk, acc_shape=acc_shape, colsum=colsum)


FFN_SUB = 512


def _sub_rows(tm):
    sub = FFN_SUB if tm % FFN_SUB == 0 else tm
    return [slice(r * sub, (r + 1) * sub) for r in range(tm // sub)]


def ffn_fwd(u, h, w_up_t, w_down, layer, name, norms=(), token=None):
    s, d = u.shape
    tm = _row_tile(s, FFN_ROWS)
    nj = 4
    nn = len(norms)

    extra = [] if token is None else [token]
    nx = len(extra)

    def body(u_ref, h_ref, wup_ref, wd_ref, *rest):
        gain_refs, (hn_ref, gu_ref), un_refs = rest[:nn], rest[nn + nx:nn + nx + 2], rest[nn + nx + 2:]
        j = pl.program_id(1)

        @pl.when(j == 0)
        def _():
            hn_ref[...] = h_ref[...]

        for rows in _sub_rows(tm):
            uv = u_ref[rows, :]
            g = _dot(uv, wup_ref[0], NT)
            p = _dot(uv, wup_ref[1], NT)
            gu_ref[0, rows, :] = g.astype(BF16)
            gu_ref[1, rows, :] = p.astype(BF16)
            act = (g * jax.nn.sigmoid(g) * p).astype(BF16)
            hn_ref[rows, :] += _dot(act, wd_ref[...], NN)

        if nn:
            @pl.when(j == nj - 1)
            def _():
                for rows in _sub_rows(tm):
                    for gain_ref, un_ref in zip(gain_refs, un_refs):
                        un_ref[rows, :] = _rms_rows(hn_ref[rows, :], gain_ref[...])

    row = pl.BlockSpec((tm, d), lambda i, j: (i, 0))
    vec = pl.BlockSpec((1, d), lambda i, j: (0, 0))
    return _pcall(
        body, name, (_sds((s, d), F32), _sds((2, nj, s, FF_CHUNK), BF16)) + (_sds((s, d), BF16),) * nn,
        grid=(s // tm, nj),
        in_specs=[row, row,
                  pl.BlockSpec((None, 2, None, FF_CHUNK, d), lambda i, j: (layer, 0, j, 0, 0)),
                  pl.BlockSpec((None, None, FF_CHUNK, d), lambda i, j: (layer, j, 0, 0))] + [vec] * nn + [ANY] * nx,
        out_specs=(row, pl.BlockSpec((2, None, tm, FF_CHUNK), lambda i, j: (0, j, i, 0))) + (row,) * nn,
        sem=("parallel", "arbitrary"))(u, h, w_up_t, w_down, *norms, *extra)


def _rms_bwd_rows(x, gain, du, dh_in):
    r = lax.rsqrt(jnp.mean(x * x, axis=-1, keepdims=True) + EPS)
    xh = x * r
    dxh = du * gain
    dx = r * (dxh - xh * jnp.mean(dxh * xh, axis=-1, keepdims=True))
    return dh_in + dx, jnp.sum(du * xh, axis=0, keepdims=True)


def ffn_bwd(dh, gu, w_up_t, w_down, layer, h_mid, gain, name, token=None):
    s, d = dh.shape
    tm = _row_tile(s, FFN_BWD_ROWS)
    nj = 4

    def body(dh_ref, gu_ref, wup_ref, wd_ref, h_ref, gain_ref, *rest):
        dho_ref, dgain_ref, act_ref, dgu_ref, du_ref = rest[-5:]
        i, j = pl.program_id(0), pl.program_id(1)

        @pl.when(j == 0)
        def _():
            du_ref[...] = jnp.zeros_like(du_ref)

        @pl.when((i == 0) & (j == 0))
        def _():
            dgain_ref[...] = jnp.zeros_like(dgain_ref)

        for rows in _sub_rows(tm):
            dact = _dot(dh_ref[rows, :].astype(BF16), wd_ref[...], NT)
            g = gu_ref[0, rows, :].astype(F32)
            p = gu_ref[1, rows, :].astype(F32)
            sig = jax.nn.sigmoid(g)
            sl = g * sig
            act_ref[rows, :] = (sl * p).astype(BF16)
            dp = (dact * sl).astype(BF16)
            dg = (dact * p * (sig * (1.0 + g * (1.0 - sig)))).astype(BF16)
            dgu_ref[0, rows, :] = dg
            dgu_ref[1, rows, :] = dp
            du_ref[rows, :] += _dot(dg, wup_ref[0], NN) + _dot(dp, wup_ref[1], NN)

        @pl.when(j == nj - 1)
        def _():
            for rows in _sub_rows(tm):
                dho, part = _rms_bwd_rows(h_ref[rows, :], gain_ref[...], du_ref[rows, :], dh_ref[rows, :])
                dho_ref[rows, :] = dho
                dgain_ref[...] += part

    row = pl.BlockSpec((tm, d), lambda i, j: (i, 0))
    vec = pl.BlockSpec((1, d), lambda i, j: (0, 0))
    gu_spec = pl.BlockSpec((2, None, tm, FF_CHUNK), lambda i, j: (0, j, i, 0))
    extra = [] if token is None else [token]
    return _pcall(
        body, name,
        (_sds((s, d), F32), _sds((1, d), F32), _sds((nj, s, FF_CHUNK), BF16), _sds((2, nj, s, FF_CHUNK), BF16)),
        grid=(s // tm, nj),
        in_specs=[row, gu_spec,
                  pl.BlockSpec((None, 2, None, FF_CHUNK, d), lambda i, j: (layer, 0, j, 0, 0)),
                  pl.BlockSpec((None, None, FF_CHUNK, d), lambda i, j: (layer, j, 0, 0)), row, vec]
        + [ANY] * len(extra),
        out_specs=(row, vec, pl.BlockSpec((None, tm, FF_CHUNK), lambda i, j: (j, i, 0)), gu_spec),
        scratch=[pltpu.VMEM((tm, d), F32)],
        sem=("arbitrary", "arbitrary"))(dh, gu, w_up_t, w_down, h_mid, gain, *extra)


def nt_rms_bwd(name, a, w, w_block, w_index, h, gain, dh_in):
    s, n = a.shape
    d = h.shape[1]
    tm = _row_tile(s, 512)

    def body(a_ref, w_ref, h_ref, gain_ref, dhi_ref, dho_ref, dgain_ref):
        i = pl.program_id(0)

        @pl.when(i == 0)
        def _():
            dgain_ref[...] = jnp.zeros_like(dgain_ref)

        du = _dot(a_ref[...].astype(BF16), w_ref[...], NT)
        dho, part = _rms_bwd_rows(h_ref[...], gain_ref[...], du, dhi_ref[...])
        dho_ref[...] = dho
        dgain_ref[...] += part

    row = pl.BlockSpec((tm, d), lambda i: (i, 0))
    vec = pl.BlockSpec((1, d), lambda i: (0, 0))
    return _pcall(body, name, (_sds((s, d), F32), _sds((1, d), F32)), grid=(s // tm,),
                  in_specs=[pl.BlockSpec((tm, n), lambda i: (i, 0)), pl.BlockSpec(w_block, w_index), row, vec, row],
                  out_specs=(row, vec), sem=("arbitrary",))(a, w, h, gain, dh_in)


def pw1_fwd(u, w, b, layer, name):
    s, d = u.shape
    tm = _row_tile(s, FFN_ROWS)
    nb, wb = w.shape[1], w.shape[3]

    def body(u_ref, w_ref, b_ref, t_ref):
        for rows in _sub_rows(tm):
            uv = u_ref[rows, :]
            for j in range(nb):
                cols = slice(j * wb, (j + 1) * wb)
                t_ref[rows, cols] = (_dot(uv, w_ref[j], NN) + b_ref[:, cols]).astype(BF16)

    return _pcall(
        body, name, _sds((s, nb * wb), BF16), grid=(s // tm,),
        in_specs=[pl.BlockSpec((tm, d), lambda i: (i, 0)), pl.BlockSpec((None, nb, d, wb), lambda i: (0, 0, 0, 0)),
                  pl.BlockSpec((None, 1, nb * wb), lambda i: (layer, 0, 0))],
        out_specs=pl.BlockSpec((tm, nb * wb), lambda i: (i, 0)), sem=("parallel",))(u, w, b)


def pw1_bwd(dt, w, h, gain, dh_in, name, token=None):
    s = dt.shape[0]
    nb, d, wb = w.shape[1], w.shape[2], w.shape[3]
    tm = _row_tile(s, 512)

    def body(dt_ref, w_ref, h_ref, gain_ref, dhi_ref, *rest):
        dho_ref, dgain_ref = rest[-2:]
        i = pl.program_id(0)

        @pl.when(i == 0)
        def _():
            dgain_ref[...] = jnp.zeros_like(dgain_ref)

        for rows in _sub_rows(tm):
            du = _dot(dt_ref[rows, 0:wb], w_ref[0], NT)
            for j in range(1, nb):
                du = du + _dot(dt_ref[rows, j * wb:(j + 1) * wb], w_ref[j], NT)
            dho, part = _rms_bwd_rows(h_ref[rows, :], gain_ref[...], du, dhi_ref[rows, :])
            dho_ref[rows, :] = dho
            dgain_ref[...] += part

    row = pl.BlockSpec((tm, d), lambda i: (i, 0))
    vec = pl.BlockSpec((1, d), lambda i: (0, 0))
    extra = [] if token is None else [token]
    return _pcall(
        body, name, (_sds((s, d), F32), _sds((1, d), F32)), grid=(s // tm,),
        in_specs=[pl.BlockSpec((tm, nb * wb), lambda i: (i, 0)),
                  pl.BlockSpec((None, nb, d, wb), lambda i: (0, 0, 0, 0)), row, vec, row] + [ANY] * len(extra),
        out_specs=(row, vec), sem=("arbitrary",))(dt, w, h, gain, dh_in, *extra)


def pw1_grad(u, dt, name):
    s, d = u.shape
    n = dt.shape[1]
    nb = N_DEV
    wb = n // nb
    tk = _row_tile(s, 1024)
    nk = s // tk

    def body(u_ref, dt_ref, g_ref, db_ref, acc_ref):
        k = pl.program_id(0)
        p = _dot(u_ref[...], dt_ref[...], TN)
        cs = jnp.sum(dt_ref[...].astype(F32), axis=0, keepdims=True)

        @pl.when(k == 0)
        def _():
            acc_ref[...] = p
            db_ref[...] = cs

        @pl.when(k > 0)
        def _():
            acc_ref[...] += p
            db_ref[...] += cs

        @pl.when(k == nk - 1)
        def _():
            for j in range(nb):
                g_ref[j] = acc_ref[:, j * wb:(j + 1) * wb].astype(BF16)

    return _pcall(
        body, name, (_sds((nb, d, wb), BF16), _sds((1, n), F32)), grid=(nk,),
        in_specs=[pl.BlockSpec((tk, d), lambda k: (k, 0)), pl.BlockSpec((tk, n), lambda k: (k, 0))],
        out_specs=(pl.BlockSpec((nb, d, wb), lambda k: (0, 0, 0)), pl.BlockSpec((1, n), lambda k: (0, 0))),
        scratch=[pltpu.VMEM((d, n), F32)], sem=("arbitrary",))(u, dt)


def _glu(t):
    t = t.astype(F32)
    return t[:, :D_MODEL] * jax.nn.sigmoid(t[:, D_MODEL:])


def _conv_tile(s):
    return 256 if s % 256 == 0 else s


CONV_ROWS = 32
CONV_LANES = 512
SUBLANES = 8


def _shifted_copies(sh_ref, rows):
    for b in range(1, SUBLANES):
        sh_ref[b, 0:rows - SUBLANES, :] = sh_ref[0, b:b + rows - SUBLANES, :]


def conv_fwd(t, cp, name):
    s = t.shape[0]
    d = D_MODEL
    ts = _conv_tile(s)
    per = ts // HALO
    rows = HALO + ts
    lead = HALO - (CONV_WIDTH - 1)
    rc = CONV_ROWS

    def body(t_ref, tp_ref, cp_ref, z_ref, y_ref, sh_ref):
        i = pl.program_id(0)
        sh_ref[0, 0:HALO, :] = jnp.where(i > 0, _glu(tp_ref[...]), 0.0)
        sh_ref[0, HALO:rows, :] = _glu(t_ref[...])
        _shifted_copies(sh_ref, rows)

        def chunk(c, carry):
            r0 = pl.multiple_of(c * rc, rc)
            for lc in range(d // CONV_LANES):
                ln = slice(lc * CONV_LANES, (lc + 1) * CONV_LANES)
                acc = jnp.zeros((rc, CONV_LANES), F32) + cp_ref[ROW_BDW:ROW_BDW + 1, ln]
                for k in range(CONV_WIDTH):
                    a8, b = divmod(lead + k, SUBLANES)
                    acc = acc + cp_ref[k:k + 1, ln] * sh_ref[b, pl.ds(r0 + SUBLANES * a8, rc), ln]
                y_ref[pl.ds(r0, rc), ln] = acc
            y = y_ref[pl.ds(r0, rc), :]
            mu = jnp.mean(y, axis=-1, keepdims=True)
            yc = y - mu
            rstd = lax.rsqrt(jnp.mean(yc * yc, axis=-1, keepdims=True) + EPS)
            yn = yc * rstd * cp_ref[ROW_LNG:ROW_LNG + 1, :] + cp_ref[ROW_LNB:ROW_LNB + 1, :]
            z_ref[pl.ds(r0, rc), :] = (yn * jax.nn.sigmoid(yn)).astype(BF16)
            return carry

        lax.fori_loop(0, ts // rc, chunk, 0)

    row = pl.BlockSpec((ts, d), lambda i: (i, 0))
    return _pcall(
        body, name, (_sds((s, d), BF16), _sds((s, d), F32)), grid=(s // ts,),
        in_specs=[pl.BlockSpec((ts, 2 * d), lambda i: (i, 0)),
                  pl.BlockSpec((HALO, 2 * d), lambda i: (jnp.maximum(i * per - 1, 0), 0)),
                  pl.BlockSpec((PACK_ROWS, d), lambda i: (0, 0))],
        out_specs=(row, row),
        scratch=[pltpu.VMEM((SUBLANES, rows, d), F32)], sem=("parallel",))(t, t, cp)


def conv_bwd(t, y, dz, cp, name):
    s = t.shape[0]
    d = D_MODEL
    ts = _conv_tile(s)
    per = ts // HALO
    nt = s // ts
    te = ts + HALO
    rc = CONV_ROWS

    def body(t_ref, y_ref, yn_ref, dz_ref, dzn_ref, cp_ref, dt_ref, st_ref, shd_ref, dw_ref):
        i = pl.program_id(0)
        last = i == nt - 1

        @pl.when(i == 0)
        def _():
            st_ref[...] = jnp.zeros_like(st_ref)
            dw_ref[...] = jnp.zeros_like(dw_ref)

        gain = cp_ref[ROW_LNG:ROW_LNG + 1, :]

        def ln_bwd(yv, dzv):
            mu = jnp.mean(yv, axis=-1, keepdims=True)
            yc = yv - mu
            rstd = lax.rsqrt(jnp.mean(yc * yc, axis=-1, keepdims=True) + EPS)
            yh = yc * rstd
            yn = yh * gain + cp_ref[ROW_LNB:ROW_LNB + 1, :]
            sig = jax.nn.sigmoid(yn)
            dyn = dzv * (sig * (1.0 + yn * (1.0 - sig)))
            dyh = dyn * gain
            dy = rstd * (dyh - jnp.mean(dyh, axis=-1, keepdims=True)
                         - yh * jnp.mean(dyh * yh, axis=-1, keepdims=True))
            return dy, dyn, yh

        def norm_chunk(c, carry):
            r0 = pl.multiple_of(c * rc, rc)
            dy, dyn, yh = ln_bwd(y_ref[pl.ds(r0, rc), :], dz_ref[pl.ds(r0, rc), :])
            shd_ref[0, pl.ds(r0, rc), :] = dy
            st_ref[ROW_BDW:ROW_BDW + 1, :] += jnp.sum(dy, axis=0, keepdims=True)
            st_ref[ROW_LNG:ROW_LNG + 1, :] += jnp.sum(dyn * yh, axis=0, keepdims=True)
            st_ref[ROW_LNB:ROW_LNB + 1, :] += jnp.sum(dyn, axis=0, keepdims=True)
            return carry

        lax.fori_loop(0, ts // rc, norm_chunk, 0)
        dy_halo, _, _ = ln_bwd(yn_ref[...], jnp.where(last, 0.0, dzn_ref[...]))
        shd_ref[0, ts:te, :] = dy_halo
        _shifted_copies(shd_ref, te)

        def tap_chunk(c, carry):
            r0 = pl.multiple_of(c * rc, rc)
            for lc in range(d // CONV_LANES):
                ln = slice(lc * CONV_LANES, (lc + 1) * CONV_LANES)
                ln2 = slice(d + lc * CONV_LANES, d + (lc + 1) * CONV_LANES)
                t1 = t_ref[pl.ds(r0, rc), ln].astype(F32)
                sg = jax.nn.sigmoid(t_ref[pl.ds(r0, rc), ln2].astype(F32))
                a = t1 * sg
                da = jnp.zeros((rc, CONV_LANES), F32)
                for k in range(CONV_WIDTH):
                    a8, b = divmod(CONV_WIDTH - 1 - k, SUBLANES)
                    e = shd_ref[b, pl.ds(r0 + SUBLANES * a8, rc), ln]
                    da = da + cp_ref[k:k + 1, ln] * e
                    dw_ref[k, :, ln] += jnp.sum((a * e).reshape(rc // SUBLANES, SUBLANES, CONV_LANES), axis=0)
                dt_ref[pl.ds(r0, rc), ln] = (da * sg).astype(BF16)
                dt_ref[pl.ds(r0, rc), ln2] = (da * t1 * sg * (1.0 - sg)).astype(BF16)
            return carry

        lax.fori_loop(0, ts // rc, tap_chunk, 0)

        @pl.when(last)
        def _():
            for k in range(CONV_WIDTH):
                st_ref[k:k + 1, :] = jnp.sum(dw_ref[k], axis=0, keepdims=True)

    last_halo = s // HALO - 1
    row = pl.BlockSpec((ts, d), lambda i: (i, 0))
    halo = pl.BlockSpec((HALO, d), lambda i: (jnp.minimum((i + 1) * per, last_halo), 0))
    return _pcall(
        body, name, (_sds((s, 2 * d), BF16), _sds((PACK_ROWS, d), F32)), grid=(nt,),
        in_specs=[pl.BlockSpec((ts, 2 * d), lambda i: (i, 0)), row, halo, row, halo,
                  pl.BlockSpec((PACK_ROWS, d), lambda i: (0, 0))],
        out_specs=(pl.BlockSpec((ts, 2 * d), lambda i: (i, 0)), pl.BlockSpec((PACK_ROWS, d), lambda i: (0, 0))),
        scratch=[pltpu.VMEM((SUBLANES, te, d), F32), pltpu.VMEM((CONV_WIDTH, SUBLANES, d), F32)],
        sem=("arbitrary",))(t, y, y, dz, dz, cp)


def _bucket_table():
    qi = np.arange(BLOCK, dtype=np.int64)[:, None]
    kj = np.arange(2 * BLOCK, dtype=np.int64)[None, :]
    dist = qi + BLOCK - kj
    max_exact = N_BUCKETS // 2
    dd = np.maximum(dist, 0)
    ratio = (np.maximum(dd, 1).astype(np.float32) / np.float32(max_exact)).astype(np.float32)
    log_ratio = (np.log(ratio).astype(np.float32) / np.float32(math.log(MAX_DISTANCE / max_exact))).astype(np.float32)
    large = max_exact + (log_ratio * np.float32(N_BUCKETS - max_exact)).astype(np.int32)
    large = np.minimum(large, N_BUCKETS - 1)
    bucket = np.where(dd < max_exact, dd, large)
    return np.where((dist >= 0) & (dist < BLOCK), bucket, -1).astype(np.int32)


def bias_table(rel_bias, bucket, name):
    def body(rb_ref, bk_ref, o_ref):
        bk = bk_ref[...]
        for h in range(N_HEADS):
            acc = jnp.full((BLOCK, 2 * BLOCK), NEG_INF, F32)
            for b in range(N_BUCKETS):
                acc = jnp.where(bk == b, rb_ref[b, h], acc)
            o_ref[h] = acc

    return _pcall(body, name, _sds((N_HEADS, BLOCK, 2 * BLOCK), F32),
                  in_specs=[pl.BlockSpec(memory_space=pltpu.SMEM), pl.BlockSpec(memory_space=pltpu.VMEM)],
                  out_specs=pl.BlockSpec(memory_space=pltpu.VMEM))(rel_bias, bucket)


def bias_grad(dba, dbb, bucket, name):
    def body(a_ref, b_ref, bk_ref, o_ref):
        bk = bk_ref[...]
        for h in range(N_HEADS):
            db = a_ref[h] + b_ref[h]
            for b in range(N_BUCKETS):
                o_ref[b, h] = jnp.sum(jnp.where(bk == b, db, 0.0))

    vm = pl.BlockSpec(memory_space=pltpu.VMEM)
    return _pcall(body, name, _sds((N_BUCKETS, N_HEADS), F32), in_specs=[vm, vm, vm],
                  out_specs=pl.BlockSpec(memory_space=pltpu.SMEM))(dba, dbb, bucket)


def _band_specs():
    cur = pl.BlockSpec((BLOCK, 2 * KV_DIM), lambda n: (n, 0))
    prev = pl.BlockSpec((BLOCK, 2 * KV_DIM), lambda n: (jnp.maximum(n - 1, 0), 0))
    return cur, prev


def _scores(q_h, k_h, bias_h, first_row, sink):
    sc = _dot(q_h, k_h, NT) * (HEAD_DIM ** -0.5) + bias_h + first_row
    m = jnp.maximum(jnp.max(sc, axis=-1, keepdims=True), sink)
    p = jnp.exp(sc - m)
    e_sink = jnp.exp(sink - m)
    den = jnp.sum(p, axis=-1, keepdims=True) + e_sink
    return p, e_sink, den


def _first_block_row(n):
    col = lax.broadcasted_iota(jnp.int32, (1, 2 * BLOCK), 1)
    return jnp.where((col < BLOCK) & (n == 0), NEG_INF, 0.0)


def _head_lanes(hk, g):
    h = hk * GROUP + g
    return slice(h * HEAD_DIM, (h + 1) * HEAD_DIM)


def _group_rows(x_ref, hk):
    return jnp.concatenate([x_ref[:, _head_lanes(hk, g)] for g in range(GROUP)], axis=0)


def _group_bias(bias_ref, hk):
    return bias_ref[hk * GROUP:(hk + 1) * GROUP].reshape(GROUP * BLOCK, 2 * BLOCK)


def _group_sinks(sink_ref, hk):
    head = lax.broadcasted_iota(jnp.int32, (GROUP * BLOCK, 1), 0) // BLOCK
    col = jnp.zeros((GROUP * BLOCK, 1), F32) + sink_ref[0, hk * GROUP]
    for g in range(1, GROUP):
        col = jnp.where(head == g, sink_ref[0, hk * GROUP + g], col)
    return col


def attn_fwd(q, kv, bias, sinks, name):
    s = q.shape[0]
    nb = s // BLOCK

    def body(sink_ref, q_ref, kvc_ref, kvp_ref, bias_ref, o_ref, band_ref):
        n = pl.program_id(0)
        band_ref[0:BLOCK, :] = kvp_ref[...]
        band_ref[BLOCK:2 * BLOCK, :] = kvc_ref[...]
        first_row = _first_block_row(n)
        for hk in range(N_KV_HEADS):
            k_h = band_ref[:, hk * HEAD_DIM:(hk + 1) * HEAD_DIM]
            v_h = band_ref[:, KV_DIM + hk * HEAD_DIM:KV_DIM + (hk + 1) * HEAD_DIM]
            p, _, den = _scores(_group_rows(q_ref, hk), k_h, _group_bias(bias_ref, hk), first_row,
                                _group_sinks(sink_ref, hk))
            o = _dot((p * (1.0 / den)).astype(BF16), v_h, NN).astype(BF16)
            for g in range(GROUP):
                o_ref[:, _head_lanes(hk, g)] = o[g * BLOCK:(g + 1) * BLOCK]

    cur, prev = _band_specs()
    qs = pl.BlockSpec((BLOCK, D_MODEL), lambda n: (n, 0))
    return _pcall(
        body, name, _sds((s, D_MODEL), BF16), grid=(nb,),
        in_specs=[pl.BlockSpec(memory_space=pltpu.SMEM), qs, cur, prev,
                  pl.BlockSpec((N_HEADS, BLOCK, 2 * BLOCK), lambda n: (0, 0, 0))],
        out_specs=qs, scratch=[pltpu.VMEM((2 * BLOCK, 2 * KV_DIM), BF16)],
        sem=("parallel",))(sinks, q, kv, kv, bias)


def attn_bwd(q, kv, do, bias, sinks, name):
    s = q.shape[0]
    nb = s // BLOCK
    scale = HEAD_DIM ** -0.5

    def body(sink_ref, q_ref, do_ref, kvc_ref, kvp_ref, bias_ref, dq_ref, dkv_ref, db_ref, dsink_ref,
             band_ref, dsacc_ref):
        n = pl.program_id(0)
        band_ref[0:BLOCK, :] = kvp_ref[...]
        band_ref[BLOCK:2 * BLOCK, :] = kvc_ref[...]
        first_row = _first_block_row(n)
        lane = lax.broadcasted_iota(jnp.int32, (BLOCK, BLOCK), 1)

        @pl.when(n == 0)
        def _():
            db_ref[...] = jnp.zeros_like(db_ref)
            dsacc_ref[...] = jnp.zeros_like(dsacc_ref)

        for hk in range(N_KV_HEADS):
            k_h = band_ref[:, hk * HEAD_DIM:(hk + 1) * HEAD_DIM]
            v_h = band_ref[:, KV_DIM + hk * HEAD_DIM:KV_DIM + (hk + 1) * HEAD_DIM]
            q_g = _group_rows(q_ref, hk)
            do_g = _group_rows(do_ref, hk)
            p, e_sink, den = _scores(q_g, k_h, _group_bias(bias_ref, hk), first_row, _group_sinks(sink_ref, hk))
            inv = 1.0 / den
            p = p * inv
            dp = _dot(do_g, v_h, NT)
            delta = jnp.sum(p * dp, axis=-1, keepdims=True)
            ds = p * (dp - delta)
            db_ref[hk * GROUP:(hk + 1) * GROUP] += ds.reshape(GROUP, BLOCK, 2 * BLOCK)
            d_sink = -(e_sink * inv) * delta
            for g in range(GROUP):
                dsacc_ref[...] += jnp.where(lane == hk * GROUP + g, d_sink[g * BLOCK:(g + 1) * BLOCK], 0.0)
            dsb = ds.astype(BF16)
            dq = (_dot(dsb, k_h, NN) * scale).astype(BF16)
            for g in range(GROUP):
                dq_ref[:, _head_lanes(hk, g)] = dq[g * BLOCK:(g + 1) * BLOCK]
            dkv_ref[:, hk * HEAD_DIM:(hk + 1) * HEAD_DIM] = _dot(dsb, q_g, TN) * scale
            dkv_ref[:, KV_DIM + hk * HEAD_DIM:KV_DIM + (hk + 1) * HEAD_DIM] = _dot(p.astype(BF16), do_g, TN)

        @pl.when(n == nb - 1)
        def _():
            dsink_ref[...] = jnp.sum(dsacc_ref[...], axis=0, keepdims=True)

    cur, prev = _band_specs()
    qs = pl.BlockSpec((BLOCK, D_MODEL), lambda n: (n, 0))
    full_b = pl.BlockSpec((N_HEADS, BLOCK, 2 * BLOCK), lambda n: (0, 0, 0))
    return _pcall(
        body, name,
        (_sds((s, D_MODEL), BF16), _sds((nb, 2 * BLOCK, 2 * KV_DIM), F32),
         _sds((N_HEADS, BLOCK, 2 * BLOCK), F32), _sds((1, BLOCK), F32)),
        grid=(nb,),
        in_specs=[pl.BlockSpec(memory_space=pltpu.SMEM), qs, qs, cur, prev, full_b],
        out_specs=(qs, pl.BlockSpec((None, 2 * BLOCK, 2 * KV_DIM), lambda n: (n, 0, 0)), full_b,
                   pl.BlockSpec((1, BLOCK), lambda n: (0, 0))),
        scratch=[pltpu.VMEM((2 * BLOCK, 2 * KV_DIM), BF16), pltpu.VMEM((BLOCK, BLOCK), F32)],
        sem=("arbitrary",))(sinks, q, do, kv, kv, bias)


def dkv_combine(pa, pb, name):
    nb = pa.shape[0]
    pa2 = pa.reshape(2 * nb, BLOCK, 2 * KV_DIM)
    pb2 = pb.reshape(2 * nb, BLOCK, 2 * KV_DIM)

    def body(ac_ref, an_ref, bc_ref, bn_ref, o_ref):
        n = pl.program_id(0)
        nxt = jnp.where(n == nb - 1, 0.0, an_ref[...] + bn_ref[...])
        o_ref[...] = (ac_ref[...] + bc_ref[...] + nxt).astype(BF16)

    cur = pl.BlockSpec((None, BLOCK, 2 * KV_DIM), lambda n: (2 * n + 1, 0, 0))
    nxt = pl.BlockSpec((None, BLOCK, 2 * KV_DIM), lambda n: (jnp.minimum(2 * n + 2, 2 * nb - 2), 0, 0))
    return _pcall(body, name, _sds((nb * BLOCK, 2 * KV_DIM), BF16), grid=(nb,),
                  in_specs=[cur, nxt, cur, nxt], out_specs=pl.BlockSpec((BLOCK, 2 * KV_DIM), lambda n: (n, 0)),
                  sem=("parallel",))(pa2, pa2, pb2, pb2)


def loss_head(h, g, target, name):
    s, d = h.shape
    tm = _row_tile(s, 512)

    def body(h_ref, g_ref, t_ref, dh_ref, dg_ref, loss_ref):
        i = pl.program_id(0)
        x = h_ref[...]
        r = lax.rsqrt(jnp.mean(x * x, axis=-1, keepdims=True) + EPS)
        xh = x * r
        gv = g_ref[...]
        err = xh * gv - t_ref[...]
        part_loss = jnp.zeros((1, BLOCK), F32) + 0.5 * jnp.sum(jnp.mean(err * err, axis=-1, keepdims=True))
        dy = err * (1.0 / d)
        dxh = dy * gv
        dh_ref[...] = r * (dxh - xh * jnp.mean(dxh * xh, axis=-1, keepdims=True))
        part_g = jnp.sum(dy * xh, axis=0, keepdims=True)

        @pl.when(i == 0)
        def _():
            dg_ref[...] = part_g
            loss_ref[...] = part_loss

        @pl.when(i > 0)
        def _():
            dg_ref[...] += part_g
            loss_ref[...] += part_loss

    row = pl.BlockSpec((tm, d), lambda i: (i, 0))
    vec = pl.BlockSpec((1, d), lambda i: (0, 0))
    return _pcall(body, name, (_sds((s, d), F32), _sds((1, d), F32), _sds((1, BLOCK), F32)), grid=(s // tm,),
                  in_specs=[row, vec, row], out_specs=(row, vec, pl.BlockSpec((1, BLOCK), lambda i: (0, 0))),
                  sem=("arbitrary",))(h, g, target)


def adamw(w, m, v, parts, name, token=None):
    nl, r, c = w.shape
    tr = max(t for t in range(1, min(r, 512) + 1) if r % t == 0 and (t % 16 == 0 or t == r))
    c1 = 1.0 / (1.0 - ADAM_B1 ** ADAM_STEP)
    c2 = 1.0 / (1.0 - ADAM_B2 ** ADAM_STEP)

    def body(w_ref, m_ref, v_ref, p_ref, *rest):
        g_ref, d_ref, nm_ref, nv_ref = rest[-4:]
        g = p_ref[0].astype(F32)
        for dev in range(1, N_DEV):
            g = g + p_ref[dev].astype(F32)
        nm = ADAM_B1 * m_ref[...] + (1.0 - ADAM_B1) * g
        nv = ADAM_B2 * v_ref[...] + (1.0 - ADAM_B2) * (g * g)
        g_ref[...] = g
        nm_ref[...] = nm
        nv_ref[...] = nv
        d_ref[...] = -ADAM_LR * ((nm * c1) / (jnp.sqrt(nv * c2) + ADAM_EPS) + ADAM_WD * w_ref[...])

    blk = pl.BlockSpec((None, tr, c), lambda l, i: (l, i, 0))
    out = _sds((nl, r, c), F32)
    extra = [] if token is None else [token]
    return _pcall(body, name, (out, out, out, out), grid=(nl, r // tr),
                  in_specs=[blk, blk, blk, pl.BlockSpec((N_DEV, None, tr, c), lambda l, i: (0, l, i, 0))]
                  + [ANY] * len(extra),
                  out_specs=(blk, blk, blk, blk), sem=("parallel", "parallel"))(w, m, v, parts, *extra)


def _place():
    x, y, c = lax.axis_index("x"), lax.axis_index("y"), lax.axis_index("c")
    return x, y, c


def _lin(px, py, pc):
    return 4 * px + 2 * py + pc


HBM = pl.BlockSpec(memory_space=pltpu.HBM)
SEM = pl.BlockSpec(memory_space=pltpu.SEMAPHORE)
EFFECT = pltpu.SideEffectType.DATAFLOW_SIDE_EFFECTING
N_PEERS = N_DEV - 1


def _peers_of(x, y, c):
    return [(x, y, 1 - c), (1 - x, y, c), (x, 1 - y, c), (1 - x, 1 - y, c),
            (1 - x, y, 1 - c), (x, 1 - y, 1 - c), (1 - x, 1 - y, 1 - c)]


def _in_hbm(a):
    return pltpu.with_memory_space_constraint(a, pltpu.HBM)


def send_start(name, bufs, copies, n_groups):
    nb = len(bufs)
    per_group = [[i for i, cp in enumerate(copies) if cp[0] == g] for g in range(n_groups)]

    def body(*refs):
        buf = refs[:nb]
        sems = refs[nb:nb + 2 * n_groups]
        token = refs[2 * nb + 2 * n_groups]
        x, y, c = _place()
        me = _lin(x, y, c)
        for g in range(n_groups):
            for slot, i in enumerate(per_group[g]):
                _, s, src_slab, d, land_slab = copies[i]
                for k, peer in enumerate(_peers_of(x, y, c)):
                    pltpu.make_async_remote_copy(
                        src_ref=src_slab(buf[s], _lin(*peer), me), dst_ref=land_slab(buf[d], me),
                        send_sem=sems[2 * g].at[slot * N_PEERS + k], recv_sem=sems[2 * g + 1].at[slot * N_PEERS + k],
                        device_id=peer, device_id_type=MESH).start()
        token[...] = jnp.zeros_like(token)

    sem_shapes = []
    for g in range(n_groups):
        sem_shapes += [pltpu.SemaphoreType.DMA((len(per_group[g]) * N_PEERS,))] * 2
    out = pl.pallas_call(
        body, name=name,
        out_shape=tuple(sem_shapes) + tuple(pltpu.HBM(b.shape, b.dtype) for b in bufs) + (_sds((8, 128), F32),),
        in_specs=[HBM] * nb,
        out_specs=tuple([SEM] * len(sem_shapes)) + tuple([HBM] * nb) + (pl.BlockSpec(memory_space=pltpu.VMEM),),
        input_output_aliases={i: len(sem_shapes) + i for i in range(nb)},
        compiler_params=pltpu.CompilerParams(has_side_effects=EFFECT))(*[_in_hbm(b) for b in bufs])
    sems = [(out[2 * g], out[2 * g + 1]) for g in range(n_groups)]
    return sems, list(out[2 * n_groups:2 * n_groups + nb]), out[2 * n_groups + nb]


N_FIRST = 4
N_RELAY = 3


def _gather_peers(x, y, c):
    first = [(x, y, 1 - c), (1 - x, y, c), (x, 1 - y, c), (1 - x, 1 - y, c)]
    return first, first[1:]


def gather_start(name, bufs, copies, n_groups):
    nb = len(bufs)
    per_group = [[i for i, cp in enumerate(copies) if cp[0] == g] for g in range(n_groups)]

    def body(*refs):
        buf = refs[:nb]
        sems = refs[nb:nb + 2 * n_groups]
        token = refs[2 * nb + 2 * n_groups]
        x, y, c = _place()
        me = _lin(x, y, c)
        first, _ = _gather_peers(x, y, c)
        for g in range(n_groups):
            for slot, i in enumerate(per_group[g]):
                _, d, slab = copies[i]
                for k, peer in enumerate(first):
                    pltpu.make_async_remote_copy(
                        src_ref=slab(buf[d], me), dst_ref=slab(buf[d], me),
                        send_sem=sems[2 * g].at[slot * N_FIRST + k], recv_sem=sems[2 * g + 1].at[slot * N_FIRST + k],
                        device_id=peer, device_id_type=MESH).start()
        token[...] = jnp.zeros_like(token)

    sem_shapes = []
    for g in range(n_groups):
        sem_shapes += [pltpu.SemaphoreType.DMA((len(per_group[g]) * N_FIRST,))] * 2
    out = pl.pallas_call(
        body, name=name,
        out_shape=tuple(sem_shapes) + tuple(pltpu.HBM(b.shape, b.dtype) for b in bufs) + (_sds((8, 128), F32),),
        in_specs=[HBM] * nb,
        out_specs=tuple([SEM] * len(sem_shapes)) + tuple([HBM] * nb) + (pl.BlockSpec(memory_space=pltpu.VMEM),),
        input_output_aliases={i: len(sem_shapes) + i for i in range(nb)},
        compiler_params=pltpu.CompilerParams(has_side_effects=EFFECT))(*[_in_hbm(b) for b in bufs])
    return [(out[2 * g], out[2 * g + 1]) for g in range(n_groups)], list(out[2 * n_groups:2 * n_groups + nb])


def gather_relay(name, bufs, slabs, first_sems, after):
    nb = len(bufs)

    def body(*refs):
        buf = refs[:nb]
        send_a, recv_a = refs[nb], refs[nb + 1]
        send_b, recv_b = refs[nb + 3], refs[nb + 4]
        token = refs[2 * nb + 5]
        x, y, c = _place()
        first, origins = _gather_peers(x, y, c)
        for n, slab in enumerate(slabs):
            for j, origin in enumerate(origins):
                block = slab(buf[n], _lin(*origin))
                pltpu.make_async_remote_copy(
                    src_ref=block, dst_ref=block, send_sem=send_a.at[n * N_FIRST + 1 + j],
                    recv_sem=recv_a.at[n * N_FIRST + 1 + j], device_id=origin, device_id_type=MESH).wait_recv()
                pltpu.make_async_remote_copy(
                    src_ref=block, dst_ref=block, send_sem=send_b.at[n * N_RELAY + j],
                    recv_sem=recv_b.at[n * N_RELAY + j], device_id=first[0], device_id_type=MESH).start()
        token[...] = jnp.zeros_like(token)

    sem_shape = pltpu.SemaphoreType.DMA((nb * N_RELAY,))
    out = pl.pallas_call(
        body, name=name,
        out_shape=(sem_shape, sem_shape) + tuple(pltpu.HBM(b.shape, b.dtype) for b in bufs) + (_sds((8, 128), F32),),
        in_specs=[HBM] * nb + [SEM, SEM, ANY],
        out_specs=(SEM, SEM) + tuple([HBM] * nb) + (pl.BlockSpec(memory_space=pltpu.VMEM),),
        input_output_aliases={i: 2 + i for i in range(nb)},
        compiler_params=pltpu.CompilerParams(has_side_effects=EFFECT))(*bufs, first_sems[0], first_sems[1], after)
    return (out[0], out[1]), list(out[2:2 + nb]), out[2 + nb]


def gather_wait(name, bufs, slabs, first_sems, relay_sems, after):
    nb = len(bufs)

    def body(*refs):
        buf = refs[:nb]
        send_a, recv_a, send_b, recv_b = refs[nb:nb + 4]
        x, y, c = _place()
        me = _lin(x, y, c)
        first, origins = _gather_peers(x, y, c)
        sibling = first[0]
        for n, slab in enumerate(slabs):
            mine = slab(buf[n], me)
            for k, peer in enumerate(first):
                pltpu.make_async_remote_copy(
                    src_ref=mine, dst_ref=mine, send_sem=send_a.at[n * N_FIRST + k],
                    recv_sem=recv_a.at[n * N_FIRST + k], device_id=peer, device_id_type=MESH).wait_send()
            theirs = slab(buf[n], _lin(*sibling))
            pltpu.make_async_remote_copy(
                src_ref=theirs, dst_ref=theirs, send_sem=send_a.at[n * N_FIRST], recv_sem=recv_a.at[n * N_FIRST],
                device_id=sibling, device_id_type=MESH).wait_recv()
            for j, (ox, oy, oc) in enumerate(origins):
                sent = slab(buf[n], _lin(ox, oy, oc))
                got = slab(buf[n], _lin(ox, oy, 1 - oc))
                pltpu.make_async_remote_copy(
                    src_ref=sent, dst_ref=got, send_sem=send_b.at[n * N_RELAY + j],
                    recv_sem=recv_b.at[n * N_RELAY + j], device_id=sibling, device_id_type=MESH).wait()

    out = pl.pallas_call(
        body, name=name, out_shape=tuple(pltpu.HBM(b.shape, b.dtype) for b in bufs),
        in_specs=[HBM] * nb + [SEM] * 4 + [ANY], out_specs=tuple([HBM] * nb),
        input_output_aliases={i: i for i in range(nb)},
        compiler_params=pltpu.CompilerParams(has_side_effects=EFFECT))(
            *bufs, first_sems[0], first_sems[1], relay_sems[0], relay_sems[1], after)
    return list(out)


def send_wait(name, bufs, copies, sems, after):
    nb = len(bufs)

    def body(*refs):
        buf = refs[:nb]
        send_sems, recv_sems = refs[nb], refs[nb + 1]
        x, y, c = _place()
        me = _lin(x, y, c)
        for slot, (s, src_slab, d, land_slab) in enumerate(copies):
            for k, peer in enumerate(_peers_of(x, y, c)):
                j = _lin(*peer)
                cp = pltpu.make_async_remote_copy(
                    src_ref=src_slab(buf[s], j, me), dst_ref=land_slab(buf[d], j),
                    send_sem=send_sems.at[slot * N_PEERS + k], recv_sem=recv_sems.at[slot * N_PEERS + k],
                    device_id=peer, device_id_type=MESH)
                cp.wait_send()
                cp.wait_recv()

    out = pl.pallas_call(
        body, name=name, out_shape=tuple(pltpu.HBM(b.shape, b.dtype) for b in bufs),
        in_specs=[HBM] * nb + [SEM, SEM, ANY], out_specs=tuple([HBM] * nb),
        input_output_aliases={i: i for i in range(nb)},
        compiler_params=pltpu.CompilerParams(has_side_effects=EFFECT))(*bufs, sems[0], sems[1], after)
    return list(out)


def local_step(x, target, weights, rep, emit):
    s = x.shape[0]
    bucket = jnp.asarray(_bucket_table())
    bias = bias_table(rep["rel_bias"], bucket, "bias_table")
    h = x
    saved = []
    kv = None
    h_kv = u_kv = None
    small = None
    u = rms_fwd(h, rep["norm_mix"][0:1], "rms_mix_fwd0")
    for l in range(4):
        g_ffn = rep["norm_ffn"][l:l + 1]
        rec = {"h_in": h, "u": u}
        if l < 2:
            w = weights(f"conv{l}", u)
            if l == 0:
                small = w
            cp = small["cp"][l]
            t = pw1_fwd(u, w["pw1"], small["b_pw1"], l, f"pw1_fwd{l}")
            z, y = conv_fwd(t, cp, f"conv_fwd{l}")
            relay_tok = weights(f"ffn{l}", z, relay=True)
            h, uf = mm_nn(f"pw2_fwd{l}", z, w["pw2"], (None, D_MODEL, D_MODEL), lambda i, j: (0, 0, j), D_MODEL,
                          D_MODEL, F32, res=h, norm=g_ffn, token=relay_tok,
                          bias=(small["b_pw2"], pl.BlockSpec((None, 1, D_MODEL), lambda i, j, l=l: (l, 0, j))))
            rec.update(t=t, z=z, y=y, cp=cp)
        else:
            a = l - 2
            w = weights(f"attn{a}", u)
            if a == 0:
                h_kv = h
                w_kv = w["wkv"]
                kv = mm_nn("kv_fwd", u_kv, w_kv, (D_MODEL, 2 * KV_DIM), lambda i, j: (0, 0), 2 * KV_DIM,
                           2 * KV_DIM, BF16)
            q = mm_nn(f"q_fwd{a}", u, w["wq"], (None, D_MODEL, D_MODEL), lambda i, j: (0, 0, j), D_MODEL, D_MODEL,
                      BF16)
            o = attn_fwd(q, kv, bias, rep["sinks"][a:a + 1], f"attn_fwd{a}")
            relay_tok = weights(f"ffn{l}", o, relay=True)
            h, uf = mm_nn(f"o_fwd{a}", o, w["wo"], (None, D_MODEL, D_MODEL), lambda i, j: (0, 0, j), D_MODEL,
                          D_MODEL, F32, res=h, norm=g_ffn, token=relay_tok)
            rec.update(q=q, o=o)
        rec["w"] = w
        rec["h_mid"] = h
        wf = weights(f"ffn{l}", uf)
        relay_tok = weights(("conv1", "attn0", "attn1")[l], uf, relay=True) if l < 3 else None
        nxt = [] if l == 3 else [rep["norm_mix"][l + 1:l + 2]] + ([rep["norm_kv"]] if l == 1 else [])
        h, gu, *normed = ffn_fwd(uf, h, wf["up"], wf["down"], 0, f"ffn_fwd{l}", norms=nxt, token=relay_tok)
        if normed:
            u = normed[0]
        if l == 1:
            u_kv = normed[1]
        rec.update(uf=uf, gu=gu, wf=wf)
        saved.append(rec)

    dh, d_nfin, loss = loss_head(h, rep["norm_final"], target, "loss_head")

    d_mix, d_ffn = [None] * 4, [None] * 4
    cp_grads = [None, None]
    dkv_parts, dbias_parts, dsinks = [], [], [None, None]
    d_nkv = None
    full_rows = lambda tk: (tk, D_MODEL)
    tok = None
    for l in reversed(range(4)):
        rec = saved[l]
        w, wf = rec["w"], rec["wf"]
        grads = {}
        g_mix = rep["norm_mix"][l:l + 1]
        g_ffn = rep["norm_ffn"][l:l + 1]
        dh_mid, d_ffn[l], act, dgu = ffn_bwd(dh, rec["gu"], wf["up"], wf["down"], 0, rec["h_mid"], g_ffn,
                                             f"ffn_bwd{l}", token=tok)
        g_down = mm_tn(
            f"down_grad{l}", act, dh, groups=4, a_block=lambda tk: (None, tk, FF_CHUNK), a_index=lambda j, k: (j, k, 0),
            b_block=full_rows, b_index=lambda j, k: (k, 0), o_block=(None, FF_CHUNK, D_MODEL),
            o_index=lambda j, k: (j, 0, 0), o_shape=(4, FF_CHUNK, D_MODEL), acc_shape=(FF_CHUNK, D_MODEL))
        g_up = mm_tn(
            f"up_grad{l}", dgu.reshape(8, s, FF_CHUNK), rec["uf"], groups=8, a_block=lambda tk: (None, tk, FF_CHUNK),
            a_index=lambda j, k: (j, k, 0), b_block=full_rows, b_index=lambda j, k: (k, 0),
            o_block=(None, FF_CHUNK, D_MODEL), o_index=lambda j, k: (j, 0, 0), o_shape=(8, FF_CHUNK, D_MODEL),
            acc_shape=(FF_CHUNK, D_MODEL))
        tok = emit(f"ffn{l}", {"up": g_up, "down": g_down.reshape(D_FF, D_MODEL)})
        dh = dh_mid
        if l < 2:
            dz = mm_nt(f"pw2_bwd{l}", dh, w["pw2"], (None, D_MODEL, D_MODEL), lambda i, k: (0, 0, 0), D_MODEL, F32,
                       token=tok)
            grads["pw2"], db2 = mm_tn(
                f"pw2_grad{l}", rec["z"], dh, groups=1, a_block=full_rows, a_index=lambda j, k: (k, 0),
                b_block=full_rows, b_index=lambda j, k: (k, 0), o_block=(D_MODEL, D_MODEL), o_index=lambda j, k: (0, 0),
                o_shape=(D_MODEL, D_MODEL), acc_shape=(D_MODEL, D_MODEL),
                colsum=((1, D_MODEL), pl.BlockSpec((1, D_MODEL), lambda j, k: (0, 0))))
            dt, stats = conv_bwd(rec["t"], rec["y"], dz, rec["cp"], f"conv_bwd{l}")
            grads["pw1"], db1 = pw1_grad(rec["u"], dt, f"pw1_grad{l}")
            cp_grads[l] = (stats, db2, db1)
            tok = emit(f"conv{l}", grads)
            dh, d_mix[l] = pw1_bwd(dt, w["pw1"], rec["h_in"], g_mix, dh, f"pw1_bwd{l}", token=tok)
        else:
            a = l - 2
            do = mm_nt(f"o_bwd{a}", dh, w["wo"], (None, D_MODEL, D_MODEL), lambda i, k: (0, 0, 0), D_MODEL, BF16,
                       token=tok)
            tok = None
            grads["wo"] = mm_tn(
                f"wo_grad{a}", rec["o"], dh, groups=1, a_block=full_rows, a_index=lambda j, k: (k, 0),
                b_block=full_rows, b_index=lambda j, k: (k, 0), o_block=(D_MODEL, D_MODEL), o_index=lambda j, k: (0, 0),
                o_shape=(D_MODEL, D_MODEL), acc_shape=(D_MODEL, D_MODEL))
            dq, dkv_p, dbias_p, dsinks[a] = attn_bwd(rec["q"], kv, do, bias, rep["sinks"][a:a + 1], f"attn_bwd{a}")
            dkv_parts.append(dkv_p)
            dbias_parts.append(dbias_p)
            grads["wq"] = mm_tn(
                f"wq_grad{a}", rec["u"], dq, groups=1, a_block=full_rows, a_index=lambda j, k: (k, 0),
                b_block=full_rows, b_index=lambda j, k: (k, 0), o_block=(D_MODEL, D_MODEL), o_index=lambda j, k: (0, 0),
                o_shape=(D_MODEL, D_MODEL), acc_shape=(D_MODEL, D_MODEL))
            if a == 1:
                tok = emit("attn1", grads)
            dh, d_mix[l] = nt_rms_bwd(f"q_bwd{a}", dq, w["wq"], (None, D_MODEL, D_MODEL), lambda i: (0, 0, 0),
                                      rec["h_in"], g_mix, dh)
        if l == 2:
            dkv = dkv_combine(dkv_parts[0], dkv_parts[1], "dkv_combine")
            grads["wkv"] = mm_tn(
                "wkv_grad", u_kv, dkv, groups=1, a_block=full_rows, a_index=lambda j, k: (k, 0),
                b_block=lambda tk: (tk, 2 * KV_DIM), b_index=lambda j, k: (k, 0), o_block=(D_MODEL, 2 * KV_DIM),
                o_index=lambda j, k: (0, 0), o_shape=(D_MODEL, 2 * KV_DIM), acc_shape=(D_MODEL, 2 * KV_DIM))
            tok = emit("attn0", grads)
            dh, d_nkv = nt_rms_bwd("kv_bwd", dkv, w_kv, (D_MODEL, 2 * KV_DIM), lambda i: (0, 0), h_kv,
                                   rep["norm_kv"], dh)

    d_relb = bias_grad(dbias_parts[0], dbias_parts[1], bucket, "bias_grad")
    d_sinks = jnp.concatenate([dsinks[0][0, :N_HEADS], dsinks[1][0, :N_HEADS]])
    tail = jnp.zeros((D_MODEL,), F32)
    rep_grad = jnp.concatenate([
        jnp.concatenate(d_mix, axis=0), jnp.concatenate(d_ffn, axis=0), d_nkv, d_nfin,
        tail.at[:2 * N_HEADS].set(d_sinks)[None], tail.at[:N_BUCKETS * N_HEADS].set(d_relb.reshape(-1))[None],
        tail.at[0].set(loss[0, 0])[None], jnp.zeros((REP_ROWS - ROW_LOSS - 1, D_MODEL), F32)], axis=0)
    return dh, cp_grads, rep_grad


def _pack_conv(w_dw, b_dw, ln_g, ln_b, b_pw2, b_pw1):
    rows = [w_dw, b_dw[:, None], ln_g[:, None], ln_b[:, None], b_pw2[:, None], b_pw1.reshape(2, 2, 128),
            jnp.zeros((2, PACK_ROWS - ROW_BPW1 - 2, 128), F32)]
    return jnp.concatenate(rows, axis=1)


def _unpack_conv(p):
    return (p[:, :CONV_WIDTH], p[:, ROW_BDW], p[:, ROW_LNG], p[:, ROW_LNB], p[:, ROW_BPW2],
            p[:, ROW_BPW1:ROW_BPW1 + 2].reshape(2, 256))


def _pack_rep(norm_mix, norm_ffn, norm_kv, norm_final, sinks, rel_bias):
    tail = jnp.zeros((D_MODEL,), F32)
    return jnp.concatenate([
        norm_mix, norm_ffn, norm_kv[None], norm_final[None], tail.at[:2 * N_HEADS].set(sinks.reshape(-1))[None],
        tail.at[:N_BUCKETS * N_HEADS].set(rel_bias.reshape(-1))[None],
        jnp.zeros((REP_ROWS - ROW_RELB - 1, D_MODEL), F32)], axis=0)


def _unpack_rep(p):
    return (p[0:4], p[4:8], p[ROW_NKV], p[ROW_NFIN], p[ROW_SINK, :2 * N_HEADS].reshape(2, N_HEADS),
            p[ROW_RELB, :N_BUCKETS * N_HEADS].reshape(N_BUCKETS, N_HEADS))


def kernel(x, norm_mix, norm_ffn, conv_w_pw1, conv_b_pw1, conv_w_dw, conv_b_dw, conv_ln_g, conv_ln_b, conv_w_pw2, conv_b_pw2, norm_kv, w_kv, w_q, w_o, sinks, rel_bias, ffn_w_up, ffn_w_down, norm_final, loss_target, m_norm_mix, m_norm_ffn, m_conv_w_pw1, m_conv_b_pw1, m_conv_w_dw, m_conv_b_dw, m_conv_ln_g, m_conv_ln_b, m_conv_w_pw2, m_conv_b_pw2, m_norm_kv, m_w_kv, m_w_q, m_w_o, m_sinks, m_rel_bias, m_ffn_w_up, m_ffn_w_down, m_norm_final, v_norm_mix, v_norm_ffn, v_conv_w_pw1, v_conv_b_pw1, v_conv_w_dw, v_conv_b_dw, v_conv_ln_g, v_conv_ln_b, v_conv_w_pw2, v_conv_b_pw2, v_norm_kv, v_w_kv, v_w_q, v_w_o, v_sinks, v_rel_bias, v_ffn_w_up, v_ffn_w_down, v_norm_final):
    s = x.shape[1]
    d = D_MODEL
    rsh = d // N_DEV
    dsh = D_FF // N_DEV

    me = _lin(*_place())
    lead_slab = lambda ref, j: ref.at[j]
    rows_of = lambda rows: (lambda ref, j: ref.at[pl.ds(j * rows, rows), :])

    conv_pack = _pack_conv(conv_w_dw, conv_b_dw, conv_ln_g, conv_ln_b, conv_b_pw2, conv_b_pw1)
    ag_order = ["conv0", "ffn0", "conv1", "ffn1", "attn0", "ffn2", "attn1", "ffn3"]
    ag_land, ag_copies, ag_members = [], [], {g: [] for g in ag_order}

    def gather(group, key, shard, land_shape, at, land_slab):
        i = len(ag_land)
        ag_land.append(lax.dynamic_update_slice(lax.empty(land_shape, shard.dtype), shard, at(me)))
        ag_copies.append((ag_order.index(group), i, land_slab))
        ag_members[group].append((key, i, land_slab))

    cols_at0 = lambda ref, j: ref.at[0, j]
    rows_at0 = lambda rows: (lambda ref, j: ref.at[0, pl.ds(j * rows, rows), :])
    col_at = lambda m: (0, m, 0, 0)
    row_at = lambda rows: (lambda m: (0, m * rows, 0))
    for l in range(2):
        gather(f"conv{l}", "pw1", conv_w_pw1[l].astype(BF16)[None, None], (1, N_DEV, d, 256), col_at, cols_at0)
        gather(f"conv{l}", "pw2", conv_w_pw2[l].astype(BF16)[None], (1, d, d), row_at(rsh), rows_at0(rsh))
    gather("conv0", "pack", conv_pack[None], (N_DEV, 2, PACK_ROWS, 128), lambda m: (m, 0, 0, 0), lead_slab)
    gather("attn0", "wkv", w_kv.astype(BF16), (d, 2 * KV_DIM), lambda m: (m * rsh, 0), rows_of(rsh))
    for a in range(2):
        gather(f"attn{a}", "wq", w_q[a].astype(BF16)[None], (1, d, d), row_at(rsh), rows_at0(rsh))
        gather(f"attn{a}", "wo", w_o[a].astype(BF16)[None], (1, d, d), row_at(rsh), rows_at0(rsh))
    up_t, m_up_t, v_up_t = (jnp.swapaxes(a, 1, 2) for a in (ffn_w_up, m_ffn_w_up, v_ffn_w_up))
    for l in range(4):
        gather(f"ffn{l}", "up", up_t[l].astype(BF16)[None, None], (1, N_DEV, FF_CHUNK, d), col_at, cols_at0)
        gather(f"ffn{l}", "down", ffn_w_down[l].astype(BF16)[None], (1, D_FF, d), row_at(dsh), rows_at0(dsh))
    ag_sems, ag_land_thru = gather_start("ag_start", ag_land, ag_copies, len(ag_order))
    relayed = {}

    def weights(group, after, relay=False):
        members = ag_members[group]
        slabs = [slab for _, _, slab in members]
        first_sems = ag_sems[ag_order.index(group)]
        if group not in relayed:
            relayed[group] = gather_relay(f"ag_relay_{group}", [ag_land_thru[i] for _, i, _ in members], slabs,
                                          first_sems, after)
        relay_sems, bufs, token = relayed[group]
        if relay:
            return token
        lands = gather_wait(f"ag_wait_{group}", bufs, slabs, first_sems, relay_sems, after)
        w = {key: land for (key, _, _), land in zip(members, lands)}
        if "up" in w:
            w["up"] = w["up"].reshape(1, 2, 4, FF_CHUNK, d)
            w["down"] = w["down"].reshape(1, 4, FF_CHUNK, d)
        if "pack" in w:
            pack_g = w.pop("pack")
            w["cp"] = jnp.transpose(pack_g, (1, 2, 0, 3)).reshape(2, PACK_ROWS, d)
            w["b_pw1"] = pack_g[:, :, ROW_BPW1:ROW_BPW1 + 2, :].transpose(1, 0, 2, 3).reshape(2, 1, 2 * d)
            w["b_pw2"] = w["cp"][:, ROW_BPW2:ROW_BPW2 + 1, :]
        return w

    shard_shapes = {"pw1": (d, 256), "pw2": (rsh, d), "wkv": (rsh, 2 * KV_DIM), "wq": (rsh, d), "wo": (rsh, d),
                    "up": (FF_CHUNK, d), "down": (dsh, d), "cp": (2, PACK_ROWS, 128), "rep": (REP_ROWS, d)}
    n_layers = {"pw1": 2, "pw2": 2, "wkv": 1, "wq": 2, "wo": 2, "up": 4, "down": 4, "cp": 1, "rep": 1}
    by_lead = (lambda ref, j, me_: ref.at[j], lambda g: lax.dynamic_index_in_dim(g, me, 0, keepdims=False))
    by_rows = lambda rows: (lambda ref, j, me_: ref.at[pl.ds(j * rows, rows), :],
                            lambda g: lax.dynamic_slice_in_dim(g, me * rows, rows, 0))
    all_of = (lambda ref, j, me_: ref, lambda g: g)
    owned = {"pw1": by_lead, "pw2": by_rows(rsh), "wkv": by_rows(rsh), "wq": by_rows(rsh), "wo": by_rows(rsh),
             "up": by_lead, "down": by_rows(dsh), "cp": by_lead, "rep": all_of}
    parts = {}
    pending = {}

    def finish(chain, after):
        keys, bufs, copies, sems, name = pending.pop(chain)
        done = send_wait(f"rs_wait_{name}", bufs, copies, sems, after)
        parts.update(zip(keys, done[len(keys):]))

    def exchange(chain, name, layer, grads):
        keys = list(grads)
        if chain in pending:
            finish(chain, grads[keys[0]])
        lands = []
        for k in keys:
            land = parts.pop(k) if k in parts else lax.empty((N_DEV, n_layers[k]) + shard_shapes[k], grads[k].dtype)
            mine = owned[k][1](grads[k])[None, None]
            lands.append(lax.dynamic_update_slice(land, mine, (me, layer) + (0,) * len(shard_shapes[k])))
        land_at = lambda ref, i: ref.at[i, layer]
        copies = [(0, n, owned[k][0], len(keys) + n, land_at) for n, k in enumerate(keys)]
        sems, thru, token = send_start(f"rs_start_{name}", [grads[k] for k in keys] + lands, copies, 1)
        pending[chain] = (keys, thru, [c[1:] for c in copies], sems[0], name)
        return token

    def emit(group, grads):
        return exchange(group[:-1], group, int(group[-1]), grads)

    rep = {"norm_mix": norm_mix, "norm_ffn": norm_ffn, "norm_kv": norm_kv[None], "norm_final": norm_final[None],
           "sinks": sinks, "rel_bias": rel_bias}

    grad_x, cp_grads, rep_grad = local_step(x[0], loss_target[0], weights, rep, emit)

    cp_full = []
    for l in range(2):
        stats, db2, db1 = cp_grads[l]
        cp_full.append(jnp.concatenate([
            stats[:ROW_BPW2], db2, db1.reshape(N_DEV, 2, 128).transpose(1, 0, 2).reshape(2, d),
            jnp.zeros((PACK_ROWS - ROW_BPW1 - 2, d), F32)], axis=0))
    cp_send = jnp.stack(cp_full).reshape(2, PACK_ROWS, N_DEV, 128).transpose(2, 0, 1, 3)
    tail_token = exchange("tail", "tail", 0, {"cp": cp_send, "rep": rep_grad})

    def update(key, w, m, v, name, token=None):
        p = parts[key]
        w3 = w.reshape(p.shape[1:])
        outs = adamw(w3, m.reshape(w3.shape), v.reshape(w3.shape), p, name, token=token)
        return [o.reshape(w.shape) for o in outs]

    res = {}
    finish("ffn", grad_x)
    up_res = update("up", up_t, m_up_t, v_up_t, "adam_up", tail_token)
    res["ffn_w_up"] = [jnp.swapaxes(o, 1, 2) for o in up_res]
    res["ffn_w_down"] = update("down", ffn_w_down, m_ffn_w_down, v_ffn_w_down, "adam_down", up_res[0])
    finish("attn", res["ffn_w_down"][0])
    res["w_kv"] = update("wkv", w_kv, m_w_kv, v_w_kv, "adam_wkv")
    res["w_q"] = update("wq", w_q, m_w_q, v_w_q, "adam_wq")
    res["w_o"] = update("wo", w_o, m_w_o, v_w_o, "adam_wo")
    finish("conv", res["w_o"][0])
    res["conv_w_pw1"] = update("pw1", conv_w_pw1, m_conv_w_pw1, v_conv_w_pw1, "adam_pw1")
    res["conv_w_pw2"] = update("pw2", conv_w_pw2, m_conv_w_pw2, v_conv_w_pw2, "adam_pw2")
    finish("tail", res["conv_w_pw2"][0])
    m_pack = _pack_conv(m_conv_w_dw, m_conv_b_dw, m_conv_ln_g, m_conv_ln_b, m_conv_b_pw2, m_conv_b_pw1)
    v_pack = _pack_conv(v_conv_w_dw, v_conv_b_dw, v_conv_ln_g, v_conv_ln_b, v_conv_b_pw2, v_conv_b_pw1)
    cp_res = adamw(conv_pack, m_pack, v_pack, parts["cp"].reshape(N_DEV, 2, PACK_ROWS, 128), "adam_conv_pack")
    rep_w = _pack_rep(norm_mix, norm_ffn, norm_kv, norm_final, sinks, rel_bias)
    rep_m = _pack_rep(m_norm_mix, m_norm_ffn, m_norm_kv, m_norm_final, m_sinks, m_rel_bias)
    rep_v = _pack_rep(v_norm_mix, v_norm_ffn, v_norm_kv, v_norm_final, v_sinks, v_rel_bias)
    rep_res = adamw(rep_w[None], rep_m[None], rep_v[None], parts["rep"], "adam_rep")
    loss = rep_res[0][0, ROW_LOSS, 0]

    outs = []
    for kind in range(4):
        cw_dw, cb_dw, cln_g, cln_b, cb_pw2, cb_pw1 = _unpack_conv(cp_res[kind])
        r_mix, r_ffn, r_nkv, r_nfin, r_sinks, r_relb = _unpack_rep(rep_res[kind][0])
        outs += [r_mix, r_ffn, res["conv_w_pw1"][kind], cb_pw1, cw_dw, cb_dw, cln_g, cln_b, res["conv_w_pw2"][kind],
                 cb_pw2, r_nkv, res["w_kv"][kind], res["w_q"][kind], res["w_o"][kind], r_sinks, r_relb,
                 res["ffn_w_up"][kind], res["ffn_w_down"][kind], r_nfin]
    return (loss, grad_x[None], *outs)
```

```python
import functools
import math

import numpy as np
import jax
import jax.numpy as jnp
from jax import lax
from jax.experimental import pallas as pl
from jax.experimental.pallas import tpu as pltpu

F32 = jnp.float32
BF16 = jnp.bfloat16

D_MODEL = 1024
D_FF = 2816
N_HEADS = 16
N_KV_HEADS = 4
GROUP = N_HEADS // N_KV_HEADS
HEAD_DIM = 64
KV_DIM = N_KV_HEADS * HEAD_DIM
BLOCK = 128
CONV_WIDTH = 31
HALO = 32
N_BUCKETS = 32
MAX_DISTANCE = 128
EPS = 1e-6
NEG_INF = -1e30
N_DEV = 8
FF_CHUNK = D_FF // 4
PACK_ROWS = 40
ROW_BDW, ROW_LNG, ROW_LNB, ROW_BPW2, ROW_BPW1 = 31, 32, 33, 34, 35
REP_ROWS = 16
ROW_NKV, ROW_NFIN, ROW_SINK, ROW_RELB, ROW_LOSS = 8, 9, 10, 11, 12

ADAM_LR, ADAM_B1, ADAM_B2, ADAM_EPS, ADAM_WD, ADAM_STEP = 0.001, 0.9, 0.999, 1e-08, 0.01, 10

VMEM_LIMIT_BYTES = 56 * 1024 * 1024
FFN_ROWS = 1024
FFN_BWD_ROWS = 512
GRAD_ROWS = 2048
ANY = pl.BlockSpec(memory_space=pl.ANY)
MESH = pl.DeviceIdType.MESH

NN = (((1,), (0,)), ((), ()))
NT = (((1,), (1,)), ((), ()))
TN = (((0,), (0,)), ((), ()))


def _dot(a, b, dims):
    return lax.dot_general(a, b, dims, preferred_element_type=F32)


def _pcall(body, name, out_shape, *, grid=None, in_specs=None, out_specs=None, scratch=(), sem=None, **kw):
    params = pltpu.CompilerParams(dimension_semantics=sem, vmem_limit_bytes=VMEM_LIMIT_BYTES)
    extra = {} if grid is None else {"grid": grid}
    return pl.pallas_call(body, name=name, out_shape=out_shape, in_specs=in_specs, out_specs=out_specs,
                          scratch_shapes=list(scratch), compiler_params=params, **extra, **kw)


def _sds(shape, dtype):
    return jax.ShapeDtypeStruct(tuple(shape), dtype)


def _row_tile(s, want):
    return want if s % want == 0 else s


def rms_fwd(h, g, name):
    s, d = h.shape
    tm = _row_tile(s, 512)

    def body(h_ref, g_ref, u_ref):
        x = h_ref[...]
        r = lax.rsqrt(jnp.mean(x * x, axis=-1, keepdims=True) + EPS)
        u_ref[...] = (x * r * g_ref[...]).astype(BF16)

    return _pcall(body, name, _sds((s, d), BF16), grid=(s // tm,),
                  in_specs=[pl.BlockSpec((tm, d), lambda i: (i, 0)), pl.BlockSpec((1, d), lambda i: (0, 0))],
                  out_specs=pl.BlockSpec((tm, d), lambda i: (i, 0)), sem=("parallel",))(h, g)


def _rms_rows(x, gain):
    return (x * lax.rsqrt(jnp.mean(x * x, axis=-1, keepdims=True) + EPS) * gain).astype(BF16)


def _mm(name, a, b, *, dims, grid, a_spec, b_spec, o_spec, o_shape, nk=1, acc_shape=None,
        bias=None, res=None, colsum=None, sem=None, token=None, norm=None):
    n_axes = len(grid)

    def body(*refs):
        it = iter(refs)
        a_ref, b_ref = next(it), next(it)
        bias_ref = next(it) if bias is not None else None
        res_ref = next(it) if res is not None else None
        gain_ref = next(it) if norm is not None else None
        if token is not None:
            next(it)
        o_ref = next(it)
        un_ref = next(it) if norm is not None else None
        cs_ref = next(it) if colsum is not None else None
        acc_ref = next(it) if nk > 1 else None
        k = pl.program_id(n_axes - 1)
        p = _dot(a_ref[...].astype(BF16), b_ref[...].astype(BF16), dims)

        def finish(acc):
            if bias_ref is not None:
                acc = acc + bias_ref[...]
            if res_ref is not None:
                acc = acc + res_ref[...]
            o_ref[...] = acc.astype(o_ref.dtype)
            if un_ref is not None:
                un_ref[...] = _rms_rows(acc, gain_ref[...])

        if nk == 1:
            finish(p)
        else:
            @pl.when(k == 0)
            def _():
                acc_ref[...] = p

            @pl.when(k > 0)
            def _():
                acc_ref[...] += p

            @pl.when(k == nk - 1)
            def _():
                finish(acc_ref[...])

        if cs_ref is not None:
            cs = jnp.sum(b_ref[...].astype(F32), axis=0, keepdims=True)

            @pl.when(k == 0)
            def _():
                cs_ref[...] = cs

            @pl.when(k > 0)
            def _():
                cs_ref[...] += cs

    ins, in_specs = [a, b], [a_spec, b_spec]
    gain = None if norm is None else (norm, pl.BlockSpec(norm.shape, lambda *_: (0,) * norm.ndim))
    for extra in (bias, res, gain, None if token is None else (token, ANY)):
        if extra is not None:
            ins.append(extra[0])
            in_specs.append(extra[1])
    out_shape, out_specs = o_shape, o_spec
    if norm is not None:
        out_shape, out_specs = (o_shape, _sds(o_shape.shape, BF16)), (o_spec, o_spec)
    if colsum is not None:
        out_shape, out_specs = (o_shape, _sds(colsum[0], F32)), (o_spec, colsum[1])
    scratch = [pltpu.VMEM(acc_shape, F32)] if nk > 1 else []
    if sem is None:
        sem = ("parallel",) * (n_axes - 1) + ("arbitrary",)
    return _pcall(body, name, out_shape, grid=grid, in_specs=in_specs, out_specs=out_specs, scratch=scratch,
                  sem=sem)(*ins)


def mm_nn(name, a, w, w_block, w_index, n, tn, out_dtype, bias=None, res=None, tm=1024, norm=None, token=None):
    s, k = a.shape
    tm = _row_tile(s, tm)
    col = lambda i, j: (i, j)
    extras = {"token": token}
    if bias is not None:
        extras["bias"] = bias
    if res is not None:
        extras["res"] = (res, pl.BlockSpec((tm, tn), col))
    if norm is not None:
        assert tn == n, "a fused norm needs whole rows"
        extras["norm"] = norm
    return _mm(name, a, w, dims=NN, grid=(s // tm, n // tn), a_spec=pl.BlockSpec((tm, k), lambda i, j: (i, 0)),
               b_spec=pl.BlockSpec(w_block, w_index), o_spec=pl.BlockSpec((tm, tn), col), o_shape=_sds((s, n), out_dtype),
               sem=("parallel", "arbitrary"), **extras)


def mm_nt(name, a, w, w_block, w_index, kout, out_dtype, nk=1, tk=None, tm=1024, token=None):
    s, n = a.shape
    tm = _row_tile(s, tm)
    tk = n if tk is None else tk
    return _mm(name, a, w, dims=NT, grid=(s // tm, nk), a_spec=pl.BlockSpec((tm, tk), lambda i, k: (i, k)),
               b_spec=pl.BlockSpec(w_block, w_index), o_spec=pl.BlockSpec((tm, kout), lambda i, k: (i, 0)),
               o_shape=_sds((s, kout), out_dtype), nk=nk, acc_shape=(tm, kout), token=token)


def mm_tn(name, a, b, *, groups, a_block, a_index, b_block, b_index, o_block, o_index, o_shape, acc_shape,
          colsum=None, tk=GRAD_ROWS):
    s = a.shape[-2]
    tk = _row_tile(s, tk)
    return _mm(name, a, b, dims=TN, grid=(groups, s // tk), a_spec=pl.BlockSpec(a_block(tk), a_index),
               b_spec=pl.BlockSpec(b_block(tk), b_index), o_spec=pl.BlockSpec(o_block, o_index),
               o_shape=_sds(o_shape, BF16), nk=s // tk, acc_shape=acc_shape, colsum=colsum)


FFN_SUB = 256


def _sub_rows(tm):
    sub = FFN_SUB if tm % FFN_SUB == 0 else tm
    return [slice(r * sub, (r + 1) * sub) for r in range(tm // sub)]


def ffn_fwd(u, h, w_up_t, w_down, layer, name, norms=(), token=None):
    s, d = u.shape
    tm = _row_tile(s, FFN_ROWS)
    nj = 4
    nn = len(norms)

    extra = [] if token is None else [token]
    nx = len(extra)

    def body(u_ref, h_ref, wup_ref, wd_ref, *rest):
        gain_refs, (hn_ref, gu_ref), un_refs = rest[:nn], rest[nn + nx:nn + nx + 2], rest[nn + nx + 2:]
        j = pl.program_id(1)

        @pl.when(j == 0)
        def _():
            hn_ref[...] = h_ref[...]

        for rows in _sub_rows(tm):
            uv = u_ref[rows, :]
            g = _dot(uv, wup_ref[0], NT)
            p = _dot(uv, wup_ref[1], NT)
            gu_ref[0, rows, :] = g.astype(BF16)
            gu_ref[1, rows, :] = p.astype(BF16)
            act = (g * jax.nn.sigmoid(g) * p).astype(BF16)
            hn_ref[rows, :] += _dot(act, wd_ref[...], NN)

        if nn:
            @pl.when(j == nj - 1)
            def _():
                for rows in _sub_rows(tm):
                    for gain_ref, un_ref in zip(gain_refs, un_refs):
                        un_ref[rows, :] = _rms_rows(hn_ref[rows, :], gain_ref[...])

    row = pl.BlockSpec((tm, d), lambda i, j: (i, 0))
    vec = pl.BlockSpec((1, d), lambda i, j: (0, 0))
    return _pcall(
        body, name, (_sds((s, d), F32), _sds((2, nj, s, FF_CHUNK), BF16)) + (_sds((s, d), BF16),) * nn,
        grid=(s // tm, nj),
        in_specs=[row, row,
                  pl.BlockSpec((None, 2, None, FF_CHUNK, d), lambda i, j: (layer, 0, j, 0, 0)),
                  pl.BlockSpec((None, None, FF_CHUNK, d), lambda i, j: (layer, j, 0, 0))] + [vec] * nn + [ANY] * nx,
        out_specs=(row, pl.BlockSpec((2, None, tm, FF_CHUNK), lambda i, j: (0, j, i, 0))) + (row,) * nn,
        sem=("parallel", "arbitrary"))(u, h, w_up_t, w_down, *norms, *extra)


def _rms_bwd_rows(x, gain, du, dh_in):
    r = lax.rsqrt(jnp.mean(x * x, axis=-1, keepdims=True) + EPS)
    xh = x * r
    dxh = du * gain
    dx = r * (dxh - xh * jnp.mean(dxh * xh, axis=-1, keepdims=True))
    return dh_in + dx, jnp.sum(du * xh, axis=0, keepdims=True)


def ffn_bwd(dh, gu, w_up_t, w_down, layer, h_mid, gain, name, token=None):
    s, d = dh.shape
    tm = _row_tile(s, FFN_BWD_ROWS)
    nj = 4

    def body(dh_ref, gu_ref, wup_ref, wd_ref, h_ref, gain_ref, *rest):
        dho_ref, dgain_ref, act_ref, dgu_ref, du_ref = rest[-5:]
        i, j = pl.program_id(0), pl.program_id(1)

        @pl.when(j == 0)
        def _():
            du_ref[...] = jnp.zeros_like(du_ref)

        @pl.when((i == 0) & (j == 0))
        def _():
            dgain_ref[...] = jnp.zeros_like(dgain_ref)

        for rows in _sub_rows(tm):
            dact = _dot(dh_ref[rows, :].astype(BF16), wd_ref[...], NT)
            g = gu_ref[0, rows, :].astype(F32)
            p = gu_ref[1, rows, :].astype(F32)
            sig = jax.nn.sigmoid(g)
            sl = g * sig
            act_ref[rows, :] = (sl * p).astype(BF16)
            dp = (dact * sl).astype(BF16)
            dg = (dact * p * (sig * (1.0 + g * (1.0 - sig)))).astype(BF16)
            dgu_ref[0, rows, :] = dg
            dgu_ref[1, rows, :] = dp
            du_ref[rows, :] += _dot(dg, wup_ref[0], NN) + _dot(dp, wup_ref[1], NN)

        @pl.when(j == nj - 1)
        def _():
            for rows in _sub_rows(tm):
                dho, part = _rms_bwd_rows(h_ref[rows, :], gain_ref[...], du_ref[rows, :], dh_ref[rows, :])
                dho_ref[rows, :] = dho
                dgain_ref[...] += part

    row = pl.BlockSpec((tm, d), lambda i, j: (i, 0))
    vec = pl.BlockSpec((1, d), lambda i, j: (0, 0))
    gu_spec = pl.BlockSpec((2, None, tm, FF_CHUNK), lambda i, j: (0, j, i, 0))
    extra = [] if token is None else [token]
    return _pcall(
        body, name,
        (_sds((s, d), F32), _sds((1, d), F32), _sds((nj, s, FF_CHUNK), BF16), _sds((2, nj, s, FF_CHUNK), BF16)),
        grid=(s // tm, nj),
        in_specs=[row, gu_spec,
                  pl.BlockSpec((None, 2, None, FF_CHUNK, d), lambda i, j: (layer, 0, j, 0, 0)),
                  pl.BlockSpec((None, None, FF_CHUNK, d), lambda i, j: (layer, j, 0, 0)), row, vec]
        + [ANY] * len(extra),
        out_specs=(row, vec, pl.BlockSpec((None, tm, FF_CHUNK), lambda i, j: (j, i, 0)), gu_spec),
        scratch=[pltpu.VMEM((tm, d), F32)],
        sem=("arbitrary", "arbitrary"))(dh, gu, w_up_t, w_down, h_mid, gain, *extra)


def nt_rms_bwd(name, a, w, w_block, w_index, h, gain, dh_in):
    s, n = a.shape
    d = h.shape[1]
    tm = _row_tile(s, 512)

    def body(a_ref, w_ref, h_ref, gain_ref, dhi_ref, dho_ref, dgain_ref):
        i = pl.program_id(0)

        @pl.when(i == 0)
        def _():
            dgain_ref[...] = jnp.zeros_like(dgain_ref)

        du = _dot(a_ref[...].astype(BF16), w_ref[...], NT)
        dho, part = _rms_bwd_rows(h_ref[...], gain_ref[...], du, dhi_ref[...])
        dho_ref[...] = dho
        dgain_ref[...] += part

    row = pl.BlockSpec((tm, d), lambda i: (i, 0))
    vec = pl.BlockSpec((1, d), lambda i: (0, 0))
    return _pcall(body, name, (_sds((s, d), F32), _sds((1, d), F32)), grid=(s // tm,),
                  in_specs=[pl.BlockSpec((tm, n), lambda i: (i, 0)), pl.BlockSpec(w_block, w_index), row, vec, row],
                  out_specs=(row, vec), sem=("arbitrary",))(a, w, h, gain, dh_in)


def pw1_fwd(u, w, b, layer, name):
    s, d = u.shape
    tm = _row_tile(s, FFN_ROWS)
    nb, wb = w.shape[1], w.shape[3]

    def body(u_ref, w_ref, b_ref, t_ref):
        for rows in _sub_rows(tm):
            uv = u_ref[rows, :]
            for j in range(nb):
                cols = slice(j * wb, (j + 1) * wb)
                t_ref[rows, cols] = (_dot(uv, w_ref[j], NN) + b_ref[:, cols]).astype(BF16)

    return _pcall(
        body, name, _sds((s, nb * wb), BF16), grid=(s // tm,),
        in_specs=[pl.BlockSpec((tm, d), lambda i: (i, 0)), pl.BlockSpec((None, nb, d, wb), lambda i: (0, 0, 0, 0)),
                  pl.BlockSpec((None, 1, nb * wb), lambda i: (layer, 0, 0))],
        out_specs=pl.BlockSpec((tm, nb * wb), lambda i: (i, 0)), sem=("parallel",))(u, w, b)


def pw1_bwd(dt, w, h, gain, dh_in, name, token=None):
    s = dt.shape[0]
    nb, d, wb = w.shape[1], w.shape[2], w.shape[3]
    tm = _row_tile(s, 512)

    def body(dt_ref, w_ref, h_ref, gain_ref, dhi_ref, *rest):
        dho_ref, dgain_ref = rest[-2:]
        i = pl.program_id(0)

        @pl.when(i == 0)
        def _():
            dgain_ref[...] = jnp.zeros_like(dgain_ref)

        for rows in _sub_rows(tm):
            du = _dot(dt_ref[rows, 0:wb], w_ref[0], NT)
            for j in range(1, nb):
                du = du + _dot(dt_ref[rows, j * wb:(j + 1) * wb], w_ref[j], NT)
            dho, part = _rms_bwd_rows(h_ref[rows, :], gain_ref[...], du, dhi_ref[rows, :])
            dho_ref[rows, :] = dho
            dgain_ref[...] += part

    row = pl.BlockSpec((tm, d), lambda i: (i, 0))
    vec = pl.BlockSpec((1, d), lambda i: (0, 0))
    extra = [] if token is None else [token]
    return _pcall(
        body, name, (_sds((s, d), F32), _sds((1, d), F32)), grid=(s // tm,),
        in_specs=[pl.BlockSpec((tm, nb * wb), lambda i: (i, 0)),
                  pl.BlockSpec((None, nb, d, wb), lambda i: (0, 0, 0, 0)), row, vec, row] + [ANY] * len(extra),
        out_specs=(row, vec), sem=("arbitrary",))(dt, w, h, gain, dh_in, *extra)


def pw1_grad(u, dt, name):
    s, d = u.shape
    n = dt.shape[1]
    nb = N_DEV
    wb = n // nb
    tk = _row_tile(s, 1024)
    nk = s // tk

    def body(u_ref, dt_ref, g_ref, db_ref, acc_ref):
        k = pl.program_id(0)
        p = _dot(u_ref[...], dt_ref[...], TN)
        cs = jnp.sum(dt_ref[...].astype(F32), axis=0, keepdims=True)

        @pl.when(k == 0)
        def _():
            acc_ref[...] = p
            db_ref[...] = cs

        @pl.when(k > 0)
        def _():
            acc_ref[...] += p
            db_ref[...] += cs

        @pl.when(k == nk - 1)
        def _():
            for j in range(nb):
                g_ref[j] = acc_ref[:, j * wb:(j + 1) * wb].astype(BF16)

    return _pcall(
        body, name, (_sds((nb, d, wb), BF16), _sds((1, n), F32)), grid=(nk,),
        in_specs=[pl.BlockSpec((tk, d), lambda k: (k, 0)), pl.BlockSpec((tk, n), lambda k: (k, 0))],
        out_specs=(pl.BlockSpec((nb, d, wb), lambda k: (0, 0, 0)), pl.BlockSpec((1, n), lambda k: (0, 0))),
        scratch=[pltpu.VMEM((d, n), F32)], sem=("arbitrary",))(u, dt)


def _glu(t):
    t = t.astype(F32)
    return t[:, :D_MODEL] * jax.nn.sigmoid(t[:, D_MODEL:])


def _conv_tile(s):
    return 256 if s % 256 == 0 else s


CONV_ROWS = 32
CONV_LANES = 512
SUBLANES = 8


def _shifted_copies(sh_ref, rows):
    for b in range(1, SUBLANES):
        sh_ref[b, 0:rows - SUBLANES, :] = sh_ref[0, b:b + rows - SUBLANES, :]


def conv_fwd(t, cp, name):
    s = t.shape[0]
    d = D_MODEL
    ts = _conv_tile(s)
    per = ts // HALO
    rows = HALO + ts
    lead = HALO - (CONV_WIDTH - 1)
    rc = CONV_ROWS

    def body(t_ref, tp_ref, cp_ref, z_ref, y_ref, sh_ref):
        i = pl.program_id(0)
        sh_ref[0, 0:HALO, :] = jnp.where(i > 0, _glu(tp_ref[...]), 0.0)
        sh_ref[0, HALO:rows, :] = _glu(t_ref[...])
        _shifted_copies(sh_ref, rows)

        def chunk(c, carry):
            r0 = pl.multiple_of(c * rc, rc)
            for lc in range(d // CONV_LANES):
                ln = slice(lc * CONV_LANES, (lc + 1) * CONV_LANES)
                acc = jnp.zeros((rc, CONV_LANES), F32) + cp_ref[ROW_BDW:ROW_BDW + 1, ln]
                for k in range(CONV_WIDTH):
                    a8, b = divmod(lead + k, SUBLANES)
                    acc = acc + cp_ref[k:k + 1, ln] * sh_ref[b, pl.ds(r0 + SUBLANES * a8, rc), ln]
                y_ref[pl.ds(r0, rc), ln] = acc
            y = y_ref[pl.ds(r0, rc), :]
            mu = jnp.mean(y, axis=-1, keepdims=True)
            yc = y - mu
            rstd = lax.rsqrt(jnp.mean(yc * yc, axis=-1, keepdims=True) + EPS)
            yn = yc * rstd * cp_ref[ROW_LNG:ROW_LNG + 1, :] + cp_ref[ROW_LNB:ROW_LNB + 1, :]
            z_ref[pl.ds(r0, rc), :] = (yn * jax.nn.sigmoid(yn)).astype(BF16)
            return carry

        lax.fori_loop(0, ts // rc, chunk, 0)

    row = pl.BlockSpec((ts, d), lambda i: (i, 0))
    return _pcall(
        body, name, (_sds((s, d), BF16), _sds((s, d), F32)), grid=(s // ts,),
        in_specs=[pl.BlockSpec((ts, 2 * d), lambda i: (i, 0)),
                  pl.BlockSpec((HALO, 2 * d), lambda i: (jnp.maximum(i * per - 1, 0), 0)),
                  pl.BlockSpec((PACK_ROWS, d), lambda i: (0, 0))],
        out_specs=(row, row),
        scratch=[pltpu.VMEM((SUBLANES, rows, d), F32)], sem=("parallel",))(t, t, cp)


def conv_bwd(t, y, dz, cp, name):
    s = t.shape[0]
    d = D_MODEL
    ts = _conv_tile(s)
    per = ts // HALO
    nt = s // ts
    te = ts + HALO
    rc = CONV_ROWS

    def body(t_ref, y_ref, yn_ref, dz_ref, dzn_ref, cp_ref, dt_ref, st_ref, shd_ref, dw_ref):
        i = pl.program_id(0)
        last = i == nt - 1

        @pl.when(i == 0)
        def _():
            st_ref[...] = jnp.zeros_like(st_ref)
            dw_ref[...] = jnp.zeros_like(dw_ref)

        gain = cp_ref[ROW_LNG:ROW_LNG + 1, :]

        def ln_bwd(yv, dzv):
            mu = jnp.mean(yv, axis=-1, keepdims=True)
            yc = yv - mu
            rstd = lax.rsqrt(jnp.mean(yc * yc, axis=-1, keepdims=True) + EPS)
            yh = yc * rstd
            yn = yh * gain + cp_ref[ROW_LNB:ROW_LNB + 1, :]
            sig = jax.nn.sigmoid(yn)
            dyn = dzv * (sig * (1.0 + yn * (1.0 - sig)))
            dyh = dyn * gain
            dy = rstd * (dyh - jnp.mean(dyh, axis=-1, keepdims=True)
                         - yh * jnp.mean(dyh * yh, axis=-1, keepdims=True))
            return dy, dyn, yh

        def norm_chunk(c, carry):
            r0 = pl.multiple_of(c * rc, rc)
            dy, dyn, yh = ln_bwd(y_ref[pl.ds(r0, rc), :], dz_ref[pl.ds(r0, rc), :])
            shd_ref[0, pl.ds(r0, rc), :] = dy
            st_ref[ROW_BDW:ROW_BDW + 1, :] += jnp.sum(dy, axis=0, keepdims=True)
            st_ref[ROW_LNG:ROW_LNG + 1, :] += jnp.sum(dyn * yh, axis=0, keepdims=True)
            st_ref[ROW_LNB:ROW_LNB + 1, :] += jnp.sum(dyn, axis=0, keepdims=True)
            return carry

        lax.fori_loop(0, ts // rc, norm_chunk, 0)
        dy_halo, _, _ = ln_bwd(yn_ref[...], jnp.where(last, 0.0, dzn_ref[...]))
        shd_ref[0, ts:te, :] = dy_halo
        _shifted_copies(shd_ref, te)

        def tap_chunk(c, carry):
            r0 = pl.multiple_of(c * rc, rc)
            for lc in range(d // CONV_LANES):
                ln = slice(lc * CONV_LANES, (lc + 1) * CONV_LANES)
                ln2 = slice(d + lc * CONV_LANES, d + (lc + 1) * CONV_LANES)
                t1 = t_ref[pl.ds(r0, rc), ln].astype(F32)
                sg = jax.nn.sigmoid(t_ref[pl.ds(r0, rc), ln2].astype(F32))
                a = t1 * sg
                da = jnp.zeros((rc, CONV_LANES), F32)
                for k in range(CONV_WIDTH):
                    a8, b = divmod(CONV_WIDTH - 1 - k, SUBLANES)
                    e = shd_ref[b, pl.ds(r0 + SUBLANES * a8, rc), ln]
                    da = da + cp_ref[k:k + 1, ln] * e
                    dw_ref[k, :, ln] += jnp.sum((a * e).reshape(rc // SUBLANES, SUBLANES, CONV_LANES), axis=0)
                dt_ref[pl.ds(r0, rc), ln] = (da * sg).astype(BF16)
                dt_ref[pl.ds(r0, rc), ln2] = (da * t1 * sg * (1.0 - sg)).astype(BF16)
            return carry

        lax.fori_loop(0, ts // rc, tap_chunk, 0)

        @pl.when(last)
        def _():
            for k in range(CONV_WIDTH):
                st_ref[k:k + 1, :] = jnp.sum(dw_ref[k], axis=0, keepdims=True)

    last_halo = s // HALO - 1
    row = pl.BlockSpec((ts, d), lambda i: (i, 0))
    halo = pl.BlockSpec((HALO, d), lambda i: (jnp.minimum((i + 1) * per, last_halo), 0))
    return _pcall(
        body, name, (_sds((s, 2 * d), BF16), _sds((PACK_ROWS, d), F32)), grid=(nt,),
        in_specs=[pl.BlockSpec((ts, 2 * d), lambda i: (i, 0)), row, halo, row, halo,
                  pl.BlockSpec((PACK_ROWS, d), lambda i: (0, 0))],
        out_specs=(pl.BlockSpec((ts, 2 * d), lambda i: (i, 0)), pl.BlockSpec((PACK_ROWS, d), lambda i: (0, 0))),
        scratch=[pltpu.VMEM((SUBLANES, te, d), F32), pltpu.VMEM((CONV_WIDTH, SUBLANES, d), F32)],
        sem=("arbitrary",))(t, y, y, dz, dz, cp)


def _bucket_table():
    qi = np.arange(BLOCK, dtype=np.int64)[:, None]
    kj = np.arange(2 * BLOCK, dtype=np.int64)[None, :]
    dist = qi + BLOCK - kj
    max_exact = N_BUCKETS // 2
    dd = np.maximum(dist, 0)
    ratio = (np.maximum(dd, 1).astype(np.float32) / np.float32(max_exact)).astype(np.float32)
    log_ratio = (np.log(ratio).astype(np.float32) / np.float32(math.log(MAX_DISTANCE / max_exact))).astype(np.float32)
    large = max_exact + (log_ratio * np.float32(N_BUCKETS - max_exact)).astype(np.int32)
    large = np.minimum(large, N_BUCKETS - 1)
    bucket = np.where(dd < max_exact, dd, large)
    return np.where((dist >= 0) & (dist < BLOCK), bucket, -1).astype(np.int32)


def bias_table(rel_bias, bucket, name):
    def body(rb_ref, bk_ref, o_ref):
        bk = bk_ref[...]
        for h in range(N_HEADS):
            acc = jnp.full((BLOCK, 2 * BLOCK), NEG_INF, F32)
            for b in range(N_BUCKETS):
                acc = jnp.where(bk == b, rb_ref[b, h], acc)
            o_ref[h] = acc

    return _pcall(body, name, _sds((N_HEADS, BLOCK, 2 * BLOCK), F32),
                  in_specs=[pl.BlockSpec(memory_space=pltpu.SMEM), pl.BlockSpec(memory_space=pltpu.VMEM)],
                  out_specs=pl.BlockSpec(memory_space=pltpu.VMEM))(rel_bias, bucket)


def bias_grad(dba, dbb, bucket, name):
    def body(a_ref, b_ref, bk_ref, o_ref):
        bk = bk_ref[...]
        for h in range(N_HEADS):
            db = a_ref[h] + b_ref[h]
            for b in range(N_BUCKETS):
                o_ref[b, h] = jnp.sum(jnp.where(bk == b, db, 0.0))

    vm = pl.BlockSpec(memory_space=pltpu.VMEM)
    return _pcall(body, name, _sds((N_BUCKETS, N_HEADS), F32), in_specs=[vm, vm, vm],
                  out_specs=pl.BlockSpec(memory_space=pltpu.SMEM))(dba, dbb, bucket)


def _band_specs():
    cur = pl.BlockSpec((BLOCK, 2 * KV_DIM), lambda n: (n, 0))
    prev = pl.BlockSpec((BLOCK, 2 * KV_DIM), lambda n: (jnp.maximum(n - 1, 0), 0))
    return cur, prev


def _scores(q_h, k_h, bias_h, first_row, sink):
    sc = _dot(q_h, k_h, NT) * (HEAD_DIM ** -0.5) + bias_h + first_row
    m = jnp.maximum(jnp.max(sc, axis=-1, keepdims=True), sink)
    p = jnp.exp(sc - m)
    e_sink = jnp.exp(sink - m)
    den = jnp.sum(p, axis=-1, keepdims=True) + e_sink
    return p, e_sink, den


def _first_block_row(n):
    col = lax.broadcasted_iota(jnp.int32, (1, 2 * BLOCK), 1)
    return jnp.where((col < BLOCK) & (n == 0), NEG_INF, 0.0)


def _head_lanes(hk, g):
    h = hk * GROUP + g
    return slice(h * HEAD_DIM, (h + 1) * HEAD_DIM)


def _group_rows(x_ref, hk):
    return jnp.concatenate([x_ref[:, _head_lanes(hk, g)] for g in range(GROUP)], axis=0)


def _group_bias(bias_ref, hk):
    return bias_ref[hk * GROUP:(hk + 1) * GROUP].reshape(GROUP * BLOCK, 2 * BLOCK)


def _group_sinks(sink_ref, hk):
    head = lax.broadcasted_iota(jnp.int32, (GROUP * BLOCK, 1), 0) // BLOCK
    col = jnp.zeros((GROUP * BLOCK, 1), F32) + sink_ref[0, hk * GROUP]
    for g in range(1, GROUP):
        col = jnp.where(head == g, sink_ref[0, hk * GROUP + g], col)
    return col


def attn_fwd(q, kv, bias, sinks, name):
    s = q.shape[0]
    nb = s // BLOCK

    def body(sink_ref, q_ref, kvc_ref, kvp_ref, bias_ref, o_ref, band_ref):
        n = pl.program_id(0)
        band_ref[0:BLOCK, :] = kvp_ref[...]
        band_ref[BLOCK:2 * BLOCK, :] = kvc_ref[...]
        first_row = _first_block_row(n)
        for hk in range(N_KV_HEADS):
            k_h = band_ref[:, hk * HEAD_DIM:(hk + 1) * HEAD_DIM]
            v_h = band_ref[:, KV_DIM + hk * HEAD_DIM:KV_DIM + (hk + 1) * HEAD_DIM]
            p, _, den = _scores(_group_rows(q_ref, hk), k_h, _group_bias(bias_ref, hk), first_row,
                                _group_sinks(sink_ref, hk))
            o = _dot((p * (1.0 / den)).astype(BF16), v_h, NN).astype(BF16)
            for g in range(GROUP):
                o_ref[:, _head_lanes(hk, g)] = o[g * BLOCK:(g + 1) * BLOCK]

    cur, prev = _band_specs()
    qs = pl.BlockSpec((BLOCK, D_MODEL), lambda n: (n, 0))
    return _pcall(
        body, name, _sds((s, D_MODEL), BF16), grid=(nb,),
        in_specs=[pl.BlockSpec(memory_space=pltpu.SMEM), qs, cur, prev,
                  pl.BlockSpec((N_HEADS, BLOCK, 2 * BLOCK), lambda n: (0, 0, 0))],
        out_specs=qs, scratch=[pltpu.VMEM((2 * BLOCK, 2 * KV_DIM), BF16)],
        sem=("parallel",))(sinks, q, kv, kv, bias)


def attn_bwd(q, kv, do, bias, sinks, name):
    s = q.shape[0]
    nb = s // BLOCK
    scale = HEAD_DIM ** -0.5

    def body(sink_ref, q_ref, do_ref, kvc_ref, kvp_ref, bias_ref, dq_ref, dkv_ref, db_ref, dsink_ref,
             band_ref, dsacc_ref):
        n = pl.program_id(0)
        band_ref[0:BLOCK, :] = kvp_ref[...]
        band_ref[BLOCK:2 * BLOCK, :] = kvc_ref[...]
        first_row = _first_block_row(n)
        lane = lax.broadcasted_iota(jnp.int32, (BLOCK, BLOCK), 1)

        @pl.when(n == 0)
        def _():
            db_ref[...] = jnp.zeros_like(db_ref)
            dsacc_ref[...] = jnp.zeros_like(dsacc_ref)

        for hk in range(N_KV_HEADS):
            k_h = band_ref[:, hk * HEAD_DIM:(hk + 1) * HEAD_DIM]
            v_h = band_ref[:, KV_DIM + hk * HEAD_DIM:KV_DIM + (hk + 1) * HEAD_DIM]
            q_g = _group_rows(q_ref, hk)
            do_g = _group_rows(do_ref, hk)
            p, e_sink, den = _scores(q_g, k_h, _group_bias(bias_ref, hk), first_row, _group_sinks(sink_ref, hk))
            inv = 1.0 / den
            p = p * inv
            dp = _dot(do_g, v_h, NT)
            delta = jnp.sum(p * dp, axis=-1, keepdims=True)
            ds = p * (dp - delta)
            db_ref[hk * GROUP:(hk + 1) * GROUP] += ds.reshape(GROUP, BLOCK, 2 * BLOCK)
            d_sink = -(e_sink * inv) * delta
            for g in range(GROUP):
                dsacc_ref[...] += jnp.where(lane == hk * GROUP + g, d_sink[g * BLOCK:(g + 1) * BLOCK], 0.0)
            dsb = ds.astype(BF16)
            dq = (_dot(dsb, k_h, NN) * scale).astype(BF16)
            for g in range(GROUP):
                dq_ref[:, _head_lanes(hk, g)] = dq[g * BLOCK:(g + 1) * BLOCK]
            dkv_ref[:, hk * HEAD_DIM:(hk + 1) * HEAD_DIM] = _dot(dsb, q_g, TN) * scale
            dkv_ref[:, KV_DIM + hk * HEAD_DIM:KV_DIM + (hk + 1) * HEAD_DIM] = _dot(p.astype(BF16), do_g, TN)

        @pl.when(n == nb - 1)
        def _():
            dsink_ref[...] = jnp.sum(dsacc_ref[...], axis=0, keepdims=True)

    cur, prev = _band_specs()
    qs = pl.BlockSpec((BLOCK, D_MODEL), lambda n: (n, 0))
    full_b = pl.BlockSpec((N_HEADS, BLOCK, 2 * BLOCK), lambda n: (0, 0, 0))
    return _pcall(
        body, name,
        (_sds((s, D_MODEL), BF16), _sds((nb, 2 * BLOCK, 2 * KV_DIM), F32),
         _sds((N_HEADS, BLOCK, 2 * BLOCK), F32), _sds((1, BLOCK), F32)),
        grid=(nb,),
        in_specs=[pl.BlockSpec(memory_space=pltpu.SMEM), qs, qs, cur, prev, full_b],
        out_specs=(qs, pl.BlockSpec((None, 2 * BLOCK, 2 * KV_DIM), lambda n: (n, 0, 0)), full_b,
                   pl.BlockSpec((1, BLOCK), lambda n: (0, 0))),
        scratch=[pltpu.VMEM((2 * BLOCK, 2 * KV_DIM), BF16), pltpu.VMEM((BLOCK, BLOCK), F32)],
        sem=("arbitrary",))(sinks, q, do, kv, kv, bias)


def dkv_combine(pa, pb, name):
    nb = pa.shape[0]
    pa2 = pa.reshape(2 * nb, BLOCK, 2 * KV_DIM)
    pb2 = pb.reshape(2 * nb, BLOCK, 2 * KV_DIM)

    def body(ac_ref, an_ref, bc_ref, bn_ref, o_ref):
        n = pl.program_id(0)
        nxt = jnp.where(n == nb - 1, 0.0, an_ref[...] + bn_ref[...])
        o_ref[...] = (ac_ref[...] + bc_ref[...] + nxt).astype(BF16)

    cur = pl.BlockSpec((None, BLOCK, 2 * KV_DIM), lambda n: (2 * n + 1, 0, 0))
    nxt = pl.BlockSpec((None, BLOCK, 2 * KV_DIM), lambda n: (jnp.minimum(2 * n + 2, 2 * nb - 2), 0, 0))
    return _pcall(body, name, _sds((nb * BLOCK, 2 * KV_DIM), BF16), grid=(nb,),
                  in_specs=[cur, nxt, cur, nxt], out_specs=pl.BlockSpec((BLOCK, 2 * KV_DIM), lambda n: (n, 0)),
                  sem=("parallel",))(pa2, pa2, pb2, pb2)


def loss_head(h, g, target, name):
    s, d = h.shape
    tm = _row_tile(s, 512)

    def body(h_ref, g_ref, t_ref, dh_ref, dg_ref, loss_ref):
        i = pl.program_id(0)
        x = h_ref[...]
        r = lax.rsqrt(jnp.mean(x * x, axis=-1, keepdims=True) + EPS)
        xh = x * r
        gv = g_ref[...]
        err = xh * gv - t_ref[...]
        part_loss = jnp.zeros((1, BLOCK), F32) + 0.5 * jnp.sum(jnp.mean(err * err, axis=-1, keepdims=True))
        dy = err * (1.0 / d)
        dxh = dy * gv
        dh_ref[...] = r * (dxh - xh * jnp.mean(dxh * xh, axis=-1, keepdims=True))
        part_g = jnp.sum(dy * xh, axis=0, keepdims=True)

        @pl.when(i == 0)
        def _():
            dg_ref[...] = part_g
            loss_ref[...] = part_loss

        @pl.when(i > 0)
        def _():
            dg_ref[...] += part_g
            loss_ref[...] += part_loss

    row = pl.BlockSpec((tm, d), lambda i: (i, 0))
    vec = pl.BlockSpec((1, d), lambda i: (0, 0))
    return _pcall(body, name, (_sds((s, d), F32), _sds((1, d), F32), _sds((1, BLOCK), F32)), grid=(s // tm,),
                  in_specs=[row, vec, row], out_specs=(row, vec, pl.BlockSpec((1, BLOCK), lambda i: (0, 0))),
                  sem=("arbitrary",))(h, g, target)


def adamw(w, m, v, parts, name, token=None):
    nl, r, c = w.shape
    tr = max(t for t in range(1, min(r, 512) + 1) if r % t == 0 and (t % 16 == 0 or t == r))
    c1 = 1.0 / (1.0 - ADAM_B1 ** ADAM_STEP)
    c2 = 1.0 / (1.0 - ADAM_B2 ** ADAM_STEP)

    def body(w_ref, m_ref, v_ref, p_ref, *rest):
        g_ref, d_ref, nm_ref, nv_ref = rest[-4:]
        g = p_ref[0].astype(F32)
        for dev in range(1, N_DEV):
            g = g + p_ref[dev].astype(F32)
        nm = ADAM_B1 * m_ref[...] + (1.0 - ADAM_B1) * g
        nv = ADAM_B2 * v_ref[...] + (1.0 - ADAM_B2) * (g * g)
        g_ref[...] = g
        nm_ref[...] = nm
        nv_ref[...] = nv
        d_ref[...] = -ADAM_LR * ((nm * c1) / (jnp.sqrt(nv * c2) + ADAM_EPS) + ADAM_WD * w_ref[...])

    blk = pl.BlockSpec((None, tr, c), lambda l, i: (l, i, 0))
    out = _sds((nl, r, c), F32)
    extra = [] if token is None else [token]
    return _pcall(body, name, (out, out, out, out), grid=(nl, r // tr),
                  in_specs=[blk, blk, blk, pl.BlockSpec((N_DEV, None, tr, c), lambda l, i: (0, l, i, 0))]
                  + [ANY] * len(extra),
                  out_specs=(blk, blk, blk, blk), sem=("parallel", "parallel"))(w, m, v, parts, *extra)


def _place():
    x, y, c = lax.axis_index("x"), lax.axis_index("y"), lax.axis_index("c")
    return x, y, c


def _lin(px, py, pc):
    return 4 * px + 2 * py + pc


HBM = pl.BlockSpec(memory_space=pltpu.HBM)
SEM = pl.BlockSpec(memory_space=pltpu.SEMAPHORE)
EFFECT = pltpu.SideEffectType.DATAFLOW_SIDE_EFFECTING
N_PEERS = N_DEV - 1


def _peers_of(x, y, c):
    return [(x, y, 1 - c), (1 - x, y, c), (x, 1 - y, c), (1 - x, 1 - y, c),
            (1 - x, y, 1 - c), (x, 1 - y, 1 - c), (1 - x, 1 - y, 1 - c)]


def _in_hbm(a):
    return pltpu.with_memory_space_constraint(a, pltpu.HBM)


def send_start(name, bufs, copies, n_groups):
    nb = len(bufs)
    per_group = [[i for i, cp in enumerate(copies) if cp[0] == g] for g in range(n_groups)]

    def body(*refs):
        buf = refs[:nb]
        sems = refs[nb:nb + 2 * n_groups]
        token = refs[2 * nb + 2 * n_groups]
        x, y, c = _place()
        me = _lin(x, y, c)
        for g in range(n_groups):
            for slot, i in enumerate(per_group[g]):
                _, s, src_slab, d, land_slab = copies[i]
                for k, peer in enumerate(_peers_of(x, y, c)):
                    pltpu.make_async_remote_copy(
                        src_ref=src_slab(buf[s], _lin(*peer), me), dst_ref=land_slab(buf[d], me),
                        send_sem=sems[2 * g].at[slot * N_PEERS + k], recv_sem=sems[2 * g + 1].at[slot * N_PEERS + k],
                        device_id=peer, device_id_type=MESH).start()
        token[...] = jnp.zeros_like(token)

    sem_shapes = []
    for g in range(n_groups):
        sem_shapes += [pltpu.SemaphoreType.DMA((len(per_group[g]) * N_PEERS,))] * 2
    out = pl.pallas_call(
        body, name=name,
        out_shape=tuple(sem_shapes) + tuple(pltpu.HBM(b.shape, b.dtype) for b in bufs) + (_sds((8, 128), F32),),
        in_specs=[HBM] * nb,
        out_specs=tuple([SEM] * len(sem_shapes)) + tuple([HBM] * nb) + (pl.BlockSpec(memory_space=pltpu.VMEM),),
        input_output_aliases={i: len(sem_shapes) + i for i in range(nb)},
        compiler_params=pltpu.CompilerParams(has_side_effects=EFFECT))(*[_in_hbm(b) for b in bufs])
    sems = [(out[2 * g], out[2 * g + 1]) for g in range(n_groups)]
    return sems, list(out[2 * n_groups:2 * n_groups + nb]), out[2 * n_groups + nb]


N_FIRST = 4
N_RELAY = 3


def _gather_peers(x, y, c):
    first = [(x, y, 1 - c), (1 - x, y, c), (x, 1 - y, c), (1 - x, 1 - y, c)]
    return first, first[1:]


def gather_start(name, bufs, copies, n_groups):
    nb = len(bufs)
    per_group = [[i for i, cp in enumerate(copies) if cp[0] == g] for g in range(n_groups)]

    def body(*refs):
        buf = refs[:nb]
        sems = refs[nb:nb + 2 * n_groups]
        token = refs[2 * nb + 2 * n_groups]
        x, y, c = _place()
        me = _lin(x, y, c)
        first, _ = _gather_peers(x, y, c)
        for g in range(n_groups):
            for slot, i in enumerate(per_group[g]):
                _, d, slab = copies[i]
                for k, peer in enumerate(first):
                    pltpu.make_async_remote_copy(
                        src_ref=slab(buf[d], me), dst_ref=slab(buf[d], me),
                        send_sem=sems[2 * g].at[slot * N_FIRST + k], recv_sem=sems[2 * g + 1].at[slot * N_FIRST + k],
                        device_id=peer, device_id_type=MESH).start()
        token[...] = jnp.zeros_like(token)

    sem_shapes = []
    for g in range(n_groups):
        sem_shapes += [pltpu.SemaphoreType.DMA((len(per_group[g]) * N_FIRST,))] * 2
    out = pl.pallas_call(
        body, name=name,
        out_shape=tuple(sem_shapes) + tuple(pltpu.HBM(b.shape, b.dtype) for b in bufs) + (_sds((8, 128), F32),),
        in_specs=[HBM] * nb,
        out_specs=tuple([SEM] * len(sem_shapes)) + tuple([HBM] * nb) + (pl.BlockSpec(memory_space=pltpu.VMEM),),
        input_output_aliases={i: len(sem_shapes) + i for i in range(nb)},
        compiler_params=pltpu.CompilerParams(has_side_effects=EFFECT))(*[_in_hbm(b) for b in bufs])
    return [(out[2 * g], out[2 * g + 1]) for g in range(n_groups)], list(out[2 * n_groups:2 * n_groups + nb])


def gather_relay(name, bufs, slabs, first_sems, after):
    nb = len(bufs)

    def body(*refs):
        buf = refs[:nb]
        send_a, recv_a = refs[nb], refs[nb + 1]
        send_b, recv_b = refs[nb + 3], refs[nb + 4]
        token = refs[2 * nb + 5]
        x, y, c = _place()
        first, origins = _gather_peers(x, y, c)
        for n, slab in enumerate(slabs):
            for j, origin in enumerate(origins):
                block = slab(buf[n], _lin(*origin))
                pltpu.make_async_remote_copy(
                    src_ref=block, dst_ref=block, send_sem=send_a.at[n * N_FIRST + 1 + j],
                    recv_sem=recv_a.at[n * N_FIRST + 1 + j], device_id=origin, device_id_type=MESH).wait_recv()
                pltpu.make_async_remote_copy(
                    src_ref=block, dst_ref=block, send_sem=send_b.at[n * N_RELAY + j],
                    recv_sem=recv_b.at[n * N_RELAY + j], device_id=first[0], device_id_type=MESH).start()
        token[...] = jnp.zeros_like(token)

    sem_shape = pltpu.SemaphoreType.DMA((nb * N_RELAY,))
    out = pl.pallas_call(
        body, name=name,
        out_shape=(sem_shape, sem_shape) + tuple(pltpu.HBM(b.shape, b.dtype) for b in bufs) + (_sds((8, 128), F32),),
        in_specs=[HBM] * nb + [SEM, SEM, ANY],
        out_specs=(SEM, SEM) + tuple([HBM] * nb) + (pl.BlockSpec(memory_space=pltpu.VMEM),),
        input_output_aliases={i: 2 + i for i in range(nb)},
        compiler_params=pltpu.CompilerParams(has_side_effects=EFFECT))(*bufs, first_sems[0], first_sems[1], after)
    return (out[0], out[1]), list(out[2:2 + nb]), out[2 + nb]


def gather_wait(name, bufs, slabs, first_sems, relay_sems, after):
    nb = len(bufs)

    def body(*refs):
        buf = refs[:nb]
        send_a, recv_a, send_b, recv_b = refs[nb:nb + 4]
        x, y, c = _place()
        me = _lin(x, y, c)
        first, origins = _gather_peers(x, y, c)
        sibling = first[0]
        for n, slab in enumerate(slabs):
            mine = slab(buf[n], me)
            for k, peer in enumerate(first):
                pltpu.make_async_remote_copy(
                    src_ref=mine, dst_ref=mine, send_sem=send_a.at[n * N_FIRST + k],
                    recv_sem=recv_a.at[n * N_FIRST + k], device_id=peer, device_id_type=MESH).wait_send()
            theirs = slab(buf[n], _lin(*sibling))
            pltpu.make_async_remote_copy(
                src_ref=theirs, dst_ref=theirs, send_sem=send_a.at[n * N_FIRST], recv_sem=recv_a.at[n * N_FIRST],
                device_id=sibling, device_id_type=MESH).wait_recv()
            for j, (ox, oy, oc) in enumerate(origins):
                sent = slab(buf[n], _lin(ox, oy, oc))
                got = slab(buf[n], _lin(ox, oy, 1 - oc))
                pltpu.make_async_remote_copy(
                    src_ref=sent, dst_ref=got, send_sem=send_b.at[n * N_RELAY + j],
                    recv_sem=recv_b.at[n * N_RELAY + j], device_id=sibling, device_id_type=MESH).wait()

    out = pl.pallas_call(
        body, name=name, out_shape=tuple(pltpu.HBM(b.shape, b.dtype) for b in bufs),
        in_specs=[HBM] * nb + [SEM] * 4 + [ANY], out_specs=tuple([HBM] * nb),
        input_output_aliases={i: i for i in range(nb)},
        compiler_params=pltpu.CompilerParams(has_side_effects=EFFECT))(
            *bufs, first_sems[0], first_sems[1], relay_sems[0], relay_sems[1], after)
    return list(out)


def send_wait(name, bufs, copies, sems, after):
    nb = len(bufs)

    def body(*refs):
        buf = refs[:nb]
        send_sems, recv_sems = refs[nb], refs[nb + 1]
        x, y, c = _place()
        me = _lin(x, y, c)
        for slot, (s, src_slab, d, land_slab) in enumerate(copies):
            for k, peer in enumerate(_peers_of(x, y, c)):
                j = _lin(*peer)
                cp = pltpu.make_async_remote_copy(
                    src_ref=src_slab(buf[s], j, me), dst_ref=land_slab(buf[d], j),
                    send_sem=send_sems.at[slot * N_PEERS + k], recv_sem=recv_sems.at[slot * N_PEERS + k],
                    device_id=peer, device_id_type=MESH)
                cp.wait_send()
                cp.wait_recv()

    out = pl.pallas_call(
        body, name=name, out_shape=tuple(pltpu.HBM(b.shape, b.dtype) for b in bufs),
        in_specs=[HBM] * nb + [SEM, SEM, ANY], out_specs=tuple([HBM] * nb),
        input_output_aliases={i: i for i in range(nb)},
        compiler_params=pltpu.CompilerParams(has_side_effects=EFFECT))(*bufs, sems[0], sems[1], after)
    return list(out)


def local_step(x, target, weights, rep, emit):
    s = x.shape[0]
    bucket = jnp.asarray(_bucket_table())
    bias = bias_table(rep["rel_bias"], bucket, "bias_table")
    h = x
    saved = []
    kv = None
    h_kv = u_kv = None
    small = None
    u = rms_fwd(h, rep["norm_mix"][0:1], "rms_mix_fwd0")
    for l in range(4):
        g_ffn = rep["norm_ffn"][l:l + 1]
        rec = {"h_in": h, "u": u}
        if l < 2:
            w = weights(f"conv{l}", u)
            if l == 0:
                small = w
            cp = small["cp"][l]
            t = pw1_fwd(u, w["pw1"], small["b_pw1"], l, f"pw1_fwd{l}")
            z, y = conv_fwd(t, cp, f"conv_fwd{l}")
            relay_tok = weights(f"ffn{l}", z, relay=True)
            h, uf = mm_nn(f"pw2_fwd{l}", z, w["pw2"], (None, D_MODEL, D_MODEL), lambda i, j: (0, 0, j), D_MODEL,
                          D_MODEL, F32, res=h, norm=g_ffn, token=relay_tok,
                          bias=(small["b_pw2"], pl.BlockSpec((None, 1, D_MODEL), lambda i, j, l=l: (l, 0, j))))
            rec.update(t=t, z=z, y=y, cp=cp)
        else:
            a = l - 2
            w = weights(f"attn{a}", u)
            if a == 0:
                h_kv = h
                w_kv = w["wkv"]
                kv = mm_nn("kv_fwd", u_kv, w_kv, (D_MODEL, 2 * KV_DIM), lambda i, j: (0, 0), 2 * KV_DIM,
                           2 * KV_DIM, BF16)
            q = mm_nn(f"q_fwd{a}", u, w["wq"], (None, D_MODEL, D_MODEL), lambda i, j: (0, 0, j), D_MODEL, D_MODEL,
                      BF16)
            o = attn_fwd(q, kv, bias, rep["sinks"][a:a + 1], f"attn_fwd{a}")
            relay_tok = weights(f"ffn{l}", o, relay=True)
            h, uf = mm_nn(f"o_fwd{a}", o, w["wo"], (None, D_MODEL, D_MODEL), lambda i, j: (0, 0, j), D_MODEL,
                          D_MODEL, F32, res=h, norm=g_ffn, token=relay_tok)
            rec.update(q=q, o=o)
        rec["w"] = w
        rec["h_mid"] = h
        wf = weights(f"ffn{l}", uf)
        relay_tok = weights(("conv1", "attn0", "attn1")[l], uf, relay=True) if l < 3 else None
        nxt = [] if l == 3 else [rep["norm_mix"][l + 1:l + 2]] + ([rep["norm_kv"]] if l == 1 else [])
        h, gu, *normed = ffn_fwd(uf, h, wf["up"], wf["down"], 0, f"ffn_fwd{l}", norms=nxt, token=relay_tok)
        if normed:
            u = normed[0]
        if l == 1:
            u_kv = normed[1]
        rec.update(uf=uf, gu=gu, wf=wf)
        saved.append(rec)

    dh, d_nfin, loss = loss_head(h, rep["norm_final"], target, "loss_head")

    d_mix, d_ffn = [None] * 4, [None] * 4
    cp_grads = [None, None]
    dkv_parts, dbias_parts, dsinks = [], [], [None, None]
    d_nkv = None
    full_rows = lambda tk: (tk, D_MODEL)
    tok = None
    for l in reversed(range(4)):
        rec = saved[l]
        w, wf = rec["w"], rec["wf"]
        grads = {}
        g_mix = rep["norm_mix"][l:l + 1]
        g_ffn = rep["norm_ffn"][l:l + 1]
        dh_mid, d_ffn[l], act, dgu = ffn_bwd(dh, rec["gu"], wf["up"], wf["down"], 0, rec["h_mid"], g_ffn,
                                             f"ffn_bwd{l}", token=tok)
        g_down = mm_tn(
            f"down_grad{l}", act, dh, groups=4, a_block=lambda tk: (None, tk, FF_CHUNK), a_index=lambda j, k: (j, k, 0),
            b_block=full_rows, b_index=lambda j, k: (k, 0), o_block=(None, FF_CHUNK, D_MODEL),
            o_index=lambda j, k: (j, 0, 0), o_shape=(4, FF_CHUNK, D_MODEL), acc_shape=(FF_CHUNK, D_MODEL))
        g_up = mm_tn(
            f"up_grad{l}", dgu.reshape(8, s, FF_CHUNK), rec["uf"], groups=8, a_block=lambda tk: (None, tk, FF_CHUNK),
            a_index=lambda j, k: (j, k, 0), b_block=full_rows, b_index=lambda j, k: (k, 0),
            o_block=(None, FF_CHUNK, D_MODEL), o_index=lambda j, k: (j, 0, 0), o_shape=(8, FF_CHUNK, D_MODEL),
            acc_shape=(FF_CHUNK, D_MODEL), tk=s)
        tok = emit(f"ffn{l}", {"up": g_up, "down": g_down.reshape(D_FF, D_MODEL)})
        dh = dh_mid
        if l < 2:
            dz = mm_nt(f"pw2_bwd{l}", dh, w["pw2"], (None, D_MODEL, D_MODEL), lambda i, k: (0, 0, 0), D_MODEL, F32,
                       token=tok)
            grads["pw2"], db2 = mm_tn(
                f"pw2_grad{l}", rec["z"], dh, groups=1, a_block=full_rows, a_index=lambda j, k: (k, 0),
                b_block=full_rows, b_index=lambda j, k: (k, 0), o_block=(D_MODEL, D_MODEL), o_index=lambda j, k: (0, 0),
                o_shape=(D_MODEL, D_MODEL), acc_shape=(D_MODEL, D_MODEL),
                colsum=((1, D_MODEL), pl.BlockSpec((1, D_MODEL), lambda j, k: (0, 0))))
            dt, stats = conv_bwd(rec["t"], rec["y"], dz, rec["cp"], f"conv_bwd{l}")
            grads["pw1"], db1 = pw1_grad(rec["u"], dt, f"pw1_grad{l}")
            cp_grads[l] = (stats, db2, db1)
            tok = emit(f"conv{l}", grads)
            dh, d_mix[l] = pw1_bwd(dt, w["pw1"], rec["h_in"], g_mix, dh, f"pw1_bwd{l}", token=tok)
        else:
            a = l - 2
            do = mm_nt(f"o_bwd{a}", dh, w["wo"], (None, D_MODEL, D_MODEL), lambda i, k: (0, 0, 0), D_MODEL, BF16,
                       token=tok)
            tok = None
            grads["wo"] = mm_tn(
                f"wo_grad{a}", rec["o"], dh, groups=1, a_block=full_rows, a_index=lambda j, k: (k, 0),
                b_block=full_rows, b_index=lambda j, k: (k, 0), o_block=(D_MODEL, D_MODEL), o_index=lambda j, k: (0, 0),
                o_shape=(D_MODEL, D_MODEL), acc_shape=(D_MODEL, D_MODEL))
            dq, dkv_p, dbias_p, dsinks[a] = attn_bwd(rec["q"], kv, do, bias, rep["sinks"][a:a + 1], f"attn_bwd{a}")
            dkv_parts.append(dkv_p)
            dbias_parts.append(dbias_p)
            grads["wq"] = mm_tn(
                f"wq_grad{a}", rec["u"], dq, groups=1, a_block=full_rows, a_index=lambda j, k: (k, 0),
                b_block=full_rows, b_index=lambda j, k: (k, 0), o_block=(D_MODEL, D_MODEL), o_index=lambda j, k: (0, 0),
                o_shape=(D_MODEL, D_MODEL), acc_shape=(D_MODEL, D_MODEL))
            if a == 1:
                tok = emit("attn1", grads)
            dh, d_mix[l] = nt_rms_bwd(f"q_bwd{a}", dq, w["wq"], (None, D_MODEL, D_MODEL), lambda i: (0, 0, 0),
                                      rec["h_in"], g_mix, dh)
        if l == 2:
            dkv = dkv_combine(dkv_parts[0], dkv_parts[1], "dkv_combine")
            grads["wkv"] = mm_tn(
                "wkv_grad", u_kv, dkv, groups=1, a_block=full_rows, a_index=lambda j, k: (k, 0),
                b_block=lambda tk: (tk, 2 * KV_DIM), b_index=lambda j, k: (k, 0), o_block=(D_MODEL, 2 * KV_DIM),
                o_index=lambda j, k: (0, 0), o_shape=(D_MODEL, 2 * KV_DIM), acc_shape=(D_MODEL, 2 * KV_DIM))
            tok = emit("attn0", grads)
            dh, d_nkv = nt_rms_bwd("kv_bwd", dkv, w_kv, (D_MODEL, 2 * KV_DIM), lambda i: (0, 0), h_kv,
                                   rep["norm_kv"], dh)

    d_relb = bias_grad(dbias_parts[0], dbias_parts[1], bucket, "bias_grad")
    d_sinks = jnp.concatenate([dsinks[0][0, :N_HEADS], dsinks[1][0, :N_HEADS]])
    tail = jnp.zeros((D_MODEL,), F32)
    rep_grad = jnp.concatenate([
        jnp.concatenate(d_mix, axis=0), jnp.concatenate(d_ffn, axis=0), d_nkv, d_nfin,
        tail.at[:2 * N_HEADS].set(d_sinks)[None], tail.at[:N_BUCKETS * N_HEADS].set(d_relb.reshape(-1))[None],
        tail.at[0].set(loss[0, 0])[None], jnp.zeros((REP_ROWS - ROW_LOSS - 1, D_MODEL), F32)], axis=0)
    return dh, cp_grads, rep_grad


def _pack_conv(w_dw, b_dw, ln_g, ln_b, b_pw2, b_pw1):
    rows = [w_dw, b_dw[:, None], ln_g[:, None], ln_b[:, None], b_pw2[:, None], b_pw1.reshape(2, 2, 128),
            jnp.zeros((2, PACK_ROWS - ROW_BPW1 - 2, 128), F32)]
    return jnp.concatenate(rows, axis=1)


def _unpack_conv(p):
    return (p[:, :CONV_WIDTH], p[:, ROW_BDW], p[:, ROW_LNG], p[:, ROW_LNB], p[:, ROW_BPW2],
            p[:, ROW_BPW1:ROW_BPW1 + 2].reshape(2, 256))


def _pack_rep(norm_mix, norm_ffn, norm_kv, norm_final, sinks, rel_bias):
    tail = jnp.zeros((D_MODEL,), F32)
    return jnp.concatenate([
        norm_mix, norm_ffn, norm_kv[None], norm_final[None], tail.at[:2 * N_HEADS].set(sinks.reshape(-1))[None],
        tail.at[:N_BUCKETS * N_HEADS].set(rel_bias.reshape(-1))[None],
        jnp.zeros((REP_ROWS - ROW_RELB - 1, D_MODEL), F32)], axis=0)


def _unpack_rep(p):
    return (p[0:4], p[4:8], p[ROW_NKV], p[ROW_NFIN], p[ROW_SINK, :2 * N_HEADS].reshape(2, N_HEADS),
            p[ROW_RELB, :N_BUCKETS * N_HEADS].reshape(N_BUCKETS, N_HEADS))


def kernel(x, norm_mix, norm_ffn, conv_w_pw1, conv_b_pw1, conv_w_dw, conv_b_dw, conv_ln_g, conv_ln_b, conv_w_pw2, conv_b_pw2, norm_kv, w_kv, w_q, w_o, sinks, rel_bias, ffn_w_up, ffn_w_down, norm_final, loss_target, m_norm_mix, m_norm_ffn, m_conv_w_pw1, m_conv_b_pw1, m_conv_w_dw, m_conv_b_dw, m_conv_ln_g, m_conv_ln_b, m_conv_w_pw2, m_conv_b_pw2, m_norm_kv, m_w_kv, m_w_q, m_w_o, m_sinks, m_rel_bias, m_ffn_w_up, m_ffn_w_down, m_norm_final, v_norm_mix, v_norm_ffn, v_conv_w_pw1, v_conv_b_pw1, v_conv_w_dw, v_conv_b_dw, v_conv_ln_g, v_conv_ln_b, v_conv_w_pw2, v_conv_b_pw2, v_norm_kv, v_w_kv, v_w_q, v_w_o, v_sinks, v_rel_bias, v_ffn_w_up, v_ffn_w_down, v_norm_final):
    s = x.shape[1]
    d = D_MODEL
    rsh = d // N_DEV
    dsh = D_FF // N_DEV

    me = _lin(*_place())
    lead_slab = lambda ref, j: ref.at[j]
    rows_of = lambda rows: (lambda ref, j: ref.at[pl.ds(j * rows, rows), :])

    conv_pack = _pack_conv(conv_w_dw, conv_b_dw, conv_ln_g, conv_ln_b, conv_b_pw2, conv_b_pw1)
    ag_order = ["conv0", "ffn0", "conv1", "ffn1", "attn0", "ffn2", "attn1", "ffn3"]
    ag_land, ag_copies, ag_members = [], [], {g: [] for g in ag_order}

    def gather(group, key, shard, land_shape, at, land_slab):
        i = len(ag_land)
        ag_land.append(lax.dynamic_update_slice(lax.empty(land_shape, shard.dtype), shard, at(me)))
        ag_copies.append((ag_order.index(group), i, land_slab))
        ag_members[group].append((key, i, land_slab))

    cols_at0 = lambda ref, j: ref.at[0, j]
    rows_at0 = lambda rows: (lambda ref, j: ref.at[0, pl.ds(j * rows, rows), :])
    col_at = lambda m: (0, m, 0, 0)
    row_at = lambda rows: (lambda m: (0, m * rows, 0))
    for l in range(2):
        gather(f"conv{l}", "pw1", conv_w_pw1[l].astype(BF16)[None, None], (1, N_DEV, d, 256), col_at, cols_at0)
        gather(f"conv{l}", "pw2", conv_w_pw2[l].astype(BF16)[None], (1, d, d), row_at(rsh), rows_at0(rsh))
    gather("conv0", "pack", conv_pack[None], (N_DEV, 2, PACK_ROWS, 128), lambda m: (m, 0, 0, 0), lead_slab)
    gather("attn0", "wkv", w_kv.astype(BF16), (d, 2 * KV_DIM), lambda m: (m * rsh, 0), rows_of(rsh))
    for a in range(2):
        gather(f"attn{a}", "wq", w_q[a].astype(BF16)[None], (1, d, d), row_at(rsh), rows_at0(rsh))
        gather(f"attn{a}", "wo", w_o[a].astype(BF16)[None], (1, d, d), row_at(rsh), rows_at0(rsh))
    up_t, m_up_t, v_up_t = (jnp.swapaxes(a, 1, 2) for a in (ffn_w_up, m_ffn_w_up, v_ffn_w_up))
    for l in range(4):
        gather(f"ffn{l}", "up", up_t[l].astype(BF16)[None, None], (1, N_DEV, FF_CHUNK, d), col_at, cols_at0)
        gather(f"ffn{l}", "down", ffn_w_down[l].astype(BF16)[None], (1, D_FF, d), row_at(dsh), rows_at0(dsh))
    ag_sems, ag_land_thru = gather_start("ag_start", ag_land, ag_copies, len(ag_order))
    relayed = {}

    def weights(group, after, relay=False):
        members = ag_members[group]
        slabs = [slab for _, _, slab in members]
        first_sems = ag_sems[ag_order.index(group)]
        if group not in relayed:
            relayed[group] = gather_relay(f"ag_relay_{group}", [ag_land_thru[i] for _, i, _ in members], slabs,
                                          first_sems, after)
        relay_sems, bufs, token = relayed[group]
        if relay:
            return token
        lands = gather_wait(f"ag_wait_{group}", bufs, slabs, first_sems, relay_sems, after)
        w = {key: land for (key, _, _), land in zip(members, lands)}
        if "up" in w:
            w["up"] = w["up"].reshape(1, 2, 4, FF_CHUNK, d)
            w["down"] = w["down"].reshape(1, 4, FF_CHUNK, d)
        if "pack" in w:
            pack_g = w.pop("pack")
            w["cp"] = jnp.transpose(pack_g, (1, 2, 0, 3)).reshape(2, PACK_ROWS, d)
            w["b_pw1"] = pack_g[:, :, ROW_BPW1:ROW_BPW1 + 2, :].transpose(1, 0, 2, 3).reshape(2, 1, 2 * d)
            w["b_pw2"] = w["cp"][:, ROW_BPW2:ROW_BPW2 + 1, :]
        return w

    shard_shapes = {"pw1": (d, 256), "pw2": (rsh, d), "wkv": (rsh, 2 * KV_DIM), "wq": (rsh, d), "wo": (rsh, d),
                    "up": (FF_CHUNK, d), "down": (dsh, d), "cp": (2, PACK_ROWS, 128), "rep": (REP_ROWS, d)}
    n_layers = {"pw1": 2, "pw2": 2, "wkv": 1, "wq": 2, "wo": 2, "up": 4, "down": 4, "cp": 1, "rep": 1}
    by_lead = (lambda ref, j, me_: ref.at[j], lambda g: lax.dynamic_index_in_dim(g, me, 0, keepdims=False))
    by_rows = lambda rows: (lambda ref, j, me_: ref.at[pl.ds(j * rows, rows), :],
                            lambda g: lax.dynamic_slice_in_dim(g, me * rows, rows, 0))
    all_of = (lambda ref, j, me_: ref, lambda g: g)
    owned = {"pw1": by_lead, "pw2": by_rows(rsh), "wkv": by_rows(rsh), "wq": by_rows(rsh), "wo": by_rows(rsh),
             "up": by_lead, "down": by_rows(dsh), "cp": by_lead, "rep": all_of}
    parts = {}
    pending = {}

    def finish(chain, after):
        keys, bufs, copies, sems, name = pending.pop(chain)
        done = send_wait(f"rs_wait_{name}", bufs, copies, sems, after)
        parts.update(zip(keys, done[len(keys):]))

    def exchange(chain, name, layer, grads):
        keys = list(grads)
        if chain in pending:
            finish(chain, grads[keys[0]])
        lands = []
        for k in keys:
            land = parts.pop(k) if k in parts else lax.empty((N_DEV, n_layers[k]) + shard_shapes[k], grads[k].dtype)
            mine = owned[k][1](grads[k])[None, None]
            lands.append(lax.dynamic_update_slice(land, mine, (me, layer) + (0,) * len(shard_shapes[k])))
        land_at = lambda ref, i: ref.at[i, layer]
        copies = [(0, n, owned[k][0], len(keys) + n, land_at) for n, k in enumerate(keys)]
        sems, thru, token = send_start(f"rs_start_{name}", [grads[k] for k in keys] + lands, copies, 1)
        pending[chain] = (keys, thru, [c[1:] for c in copies], sems[0], name)
        return token

    def emit(group, grads):
        return exchange(group[:-1], group, int(group[-1]), grads)

    rep = {"norm_mix": norm_mix, "norm_ffn": norm_ffn, "norm_kv": norm_kv[None], "norm_final": norm_final[None],
           "sinks": sinks, "rel_bias": rel_bias}

    grad_x, cp_grads, rep_grad = local_step(x[0], loss_target[0], weights, rep, emit)

    cp_full = []
    for l in range(2):
        stats, db2, db1 = cp_grads[l]
        cp_full.append(jnp.concatenate([
            stats[:ROW_BPW2], db2, db1.reshape(N_DEV, 2, 128).transpose(1, 0, 2).reshape(2, d),
            jnp.zeros((PACK_ROWS - ROW_BPW1 - 2, d), F32)], axis=0))
    cp_send = jnp.stack(cp_full).reshape(2, PACK_ROWS, N_DEV, 128).transpose(2, 0, 1, 3)
    tail_token = exchange("tail", "tail", 0, {"cp": cp_send, "rep": rep_grad})

    def update(key, w, m, v, name, token=None):
        p = parts[key]
        w3 = w.reshape(p.shape[1:])
        outs = adamw(w3, m.reshape(w3.shape), v.reshape(w3.shape), p, name, token=token)
        return [o.reshape(w.shape) for o in outs]

    res = {}
    finish("ffn", grad_x)
    up_res = update("up", up_t, m_up_t, v_up_t, "adam_up", tail_token)
    res["ffn_w_up"] = [jnp.swapaxes(o, 1, 2) for o in up_res]
    res["ffn_w_down"] = update("down", ffn_w_down, m_ffn_w_down, v_ffn_w_down, "adam_down", up_res[0])
    finish("attn", res["ffn_w_down"][0])
    res["w_kv"] = update("wkv", w_kv, m_w_kv, v_w_kv, "adam_wkv")
    res["w_q"] = update("wq", w_q, m_w_q, v_w_q, "adam_wq")
    res["w_o"] = update("wo", w_o, m_w_o, v_w_o, "adam_wo")
    finish("conv", res["w_o"][0])
    res["conv_w_pw1"] = update("pw1", conv_w_pw1, m_conv_w_pw1, v_conv_w_pw1, "adam_pw1")
    res["conv_w_pw2"] = update("pw2", conv_w_pw2, m_conv_w_pw2, v_conv_w_pw2, "adam_pw2")
    finish("tail", res["conv_w_pw2"][0])
    m_pack = _pack_conv(m_conv_w_dw, m_conv_b_dw, m_conv_ln_g, m_conv_ln_b, m_conv_b_pw2, m_conv_b_pw1)
    v_pack = _pack_conv(v_conv_w_dw, v_conv_b_dw, v_conv_ln_g, v_conv_ln_b, v_conv_b_pw2, v_conv_b_pw1)
    cp_res = adamw(conv_pack, m_pack, v_pack, parts["cp"].reshape(N_DEV, 2, PACK_ROWS, 128), "adam_conv_pack")
    rep_w = _pack_rep(norm_mix, norm_ffn, norm_kv, norm_final, sinks, rel_bias)
    rep_m = _pack_rep(m_norm_mix, m_norm_ffn, m_norm_kv, m_norm_final, m_sinks, m_rel_bias)
    rep_v = _pack_rep(v_norm_mix, v_norm_ffn, v_norm_kv, v_norm_final, v_sinks, v_rel_bias)
    rep_res = adamw(rep_w[None], rep_m[None], rep_v[None], parts["rep"], "adam_rep")
    loss = rep_res[0][0, ROW_LOSS, 0]

    outs = []
    for kind in range(4):
        cw_dw, cb_dw, cln_g, cln_b, cb_pw2, cb_pw1 = _unpack_conv(cp_res[kind])
        r_mix, r_ffn, r_nkv, r_nfin, r_sinks, r_relb = _unpack_rep(rep_res[kind][0])
        outs += [r_mix, r_ffn, res["conv_w_pw1"][kind], cb_pw1, cw_dw, cb_dw, cln_g, cln_b, res["conv_w_pw2"][kind],
                 cb_pw2, r_nkv, res["w_kv"][kind], res["w_q"][kind], res["w_o"][kind], r_sinks, r_relb,
                 res["ffn_w_up"][kind], res["ffn_w_down"][kind], r_nfin]
    return (loss, grad_x[None], *outs)
```

```python
import functools
import math

import numpy as np
import jax
import jax.numpy as jnp
from jax import lax
from jax.experimental import pallas as pl
from jax.experimental.pallas import tpu as pltpu

F32 = jnp.float32
BF16 = jnp.bfloat16

D_MODEL = 1024
D_FF = 2816
N_HEADS = 16
N_KV_HEADS = 4
GROUP = N_HEADS // N_KV_HEADS
HEAD_DIM = 64
KV_DIM = N_KV_HEADS * HEAD_DIM
BLOCK = 128
CONV_WIDTH = 31
HALO = 32
N_BUCKETS = 32
MAX_DISTANCE = 128
EPS = 1e-6
NEG_INF = -1e30
N_DEV = 8
FF_CHUNK = D_FF // 4
PACK_ROWS = 40
ROW_BDW, ROW_LNG, ROW_LNB, ROW_BPW2, ROW_BPW1 = 31, 32, 33, 34, 35
REP_ROWS = 16
ROW_NKV, ROW_NFIN, ROW_SINK, ROW_RELB, ROW_LOSS = 8, 9, 10, 11, 12

ADAM_LR, ADAM_B1, ADAM_B2, ADAM_EPS, ADAM_WD, ADAM_STEP = 0.001, 0.9, 0.999, 1e-08, 0.01, 10

VMEM_LIMIT_BYTES = 56 * 1024 * 1024
FFN_ROWS = 1024
FFN_BWD_ROWS = 512
GRAD_ROWS = 2048
ANY = pl.BlockSpec(memory_space=pl.ANY)
MESH = pl.DeviceIdType.MESH

NN = (((1,), (0,)), ((), ()))
NT = (((1,), (1,)), ((), ()))
TN = (((0,), (0,)), ((), ()))


def _dot(a, b, dims):
    return lax.dot_general(a, b, dims, preferred_element_type=F32)


def _pcall(body, name, out_shape, *, grid=None, in_specs=None, out_specs=None, scratch=(), sem=None, **kw):
    params = pltpu.CompilerParams(dimension_semantics=sem, vmem_limit_bytes=VMEM_LIMIT_BYTES)
    extra = {} if grid is None else {"grid": grid}
    return pl.pallas_call(body, name=name, out_shape=out_shape, in_specs=in_specs, out_specs=out_specs,
                          scratch_shapes=list(scratch), compiler_params=params, **extra, **kw)


def _sds(shape, dtype):
    return jax.ShapeDtypeStruct(tuple(shape), dtype)


def _row_tile(s, want):
    return want if s % want == 0 else s


def rms_fwd(h, g, name):
    s, d = h.shape
    tm = _row_tile(s, 512)

    def body(h_ref, g_ref, u_ref):
        x = h_ref[...]
        r = lax.rsqrt(jnp.mean(x * x, axis=-1, keepdims=True) + EPS)
        u_ref[...] = (x * r * g_ref[...]).astype(BF16)

    return _pcall(body, name, _sds((s, d), BF16), grid=(s // tm,),
                  in_specs=[pl.BlockSpec((tm, d), lambda i: (i, 0)), pl.BlockSpec((1, d), lambda i: (0, 0))],
                  out_specs=pl.BlockSpec((tm, d), lambda i: (i, 0)), sem=("parallel",))(h, g)


def _rms_rows(x, gain):
    return (x * lax.rsqrt(jnp.mean(x * x, axis=-1, keepdims=True) + EPS) * gain).astype(BF16)


def _mm(name, a, b, *, dims, grid, a_spec, b_spec, o_spec, o_shape, nk=1, acc_shape=None,
        bias=None, res=None, colsum=None, sem=None, token=None, norm=None):
    n_axes = len(grid)

    def body(*refs):
        it = iter(refs)
        a_ref, b_ref = next(it), next(it)
        bias_ref = next(it) if bias is not None else None
        res_ref = next(it) if res is not None else None
        gain_ref = next(it) if norm is not None else None
        if token is not None:
            next(it)
        o_ref = next(it)
        un_ref = next(it) if norm is not None else None
        cs_ref = next(it) if colsum is not None else None
        acc_ref = next(it) if nk > 1 else None
        k = pl.program_id(n_axes - 1)
        p = _dot(a_ref[...].astype(BF16), b_ref[...].astype(BF16), dims)

        def finish(acc):
            if bias_ref is not None:
                acc = acc + bias_ref[...]
            if res_ref is not None:
                acc = acc + res_ref[...]
            o_ref[...] = acc.astype(o_ref.dtype)
            if un_ref is not None:
                un_ref[...] = _rms_rows(acc, gain_ref[...])

        if nk == 1:
            finish(p)
        else:
            @pl.when(k == 0)
            def _():
                acc_ref[...] = p

            @pl.when(k > 0)
            def _():
                acc_ref[...] += p

            @pl.when(k == nk - 1)
            def _():
                finish(acc_ref[...])

        if cs_ref is not None:
            cs = jnp.sum(b_ref[...].astype(F32), axis=0, keepdims=True)

            @pl.when(k == 0)
            def _():
                cs_ref[...] = cs

            @pl.when(k > 0)
            def _():
                cs_ref[...] += cs

    ins, in_specs = [a, b], [a_spec, b_spec]
    gain = None if norm is None else (norm, pl.BlockSpec(norm.shape, lambda *_: (0,) * norm.ndim))
    for extra in (bias, res, gain, None if token is None else (token, ANY)):
        if extra is not None:
            ins.append(extra[0])
            in_specs.append(extra[1])
    out_shape, out_specs = o_shape, o_spec
    if norm is not None:
        out_shape, out_specs = (o_shape, _sds(o_shape.shape, BF16)), (o_spec, o_spec)
    if colsum is not None:
        out_shape, out_specs = (o_shape, _sds(colsum[0], F32)), (o_spec, colsum[1])
    scratch = [pltpu.VMEM(acc_shape, F32)] if nk > 1 else []
    if sem is None:
        sem = ("parallel",) * (n_axes - 1) + ("arbitrary",)
    return _pcall(body, name, out_shape, grid=grid, in_specs=in_specs, out_specs=out_specs, scratch=scratch,
                  sem=sem)(*ins)


def mm_nn(name, a, w, w_block, w_index, n, tn, out_dtype, bias=None, res=None, tm=1024, norm=None, token=None):
    s, k = a.shape
    tm = _row_tile(s, tm)
    col = lambda i, j: (i, j)
    extras = {"token": token}
    if bias is not None:
        extras["bias"] = bias
    if res is not None:
        extras["res"] = (res, pl.BlockSpec((tm, tn), col))
    if norm is not None:
        assert tn == n, "a fused norm needs whole rows"
        extras["norm"] = norm
    return _mm(name, a, w, dims=NN, grid=(s // tm, n // tn), a_spec=pl.BlockSpec((tm, k), lambda i, j: (i, 0)),
               b_spec=pl.BlockSpec(w_block, w_index), o_spec=pl.BlockSpec((tm, tn), col), o_shape=_sds((s, n), out_dtype),
               sem=("parallel", "arbitrary"), **extras)


def mm_nt(name, a, w, w_block, w_index, kout, out_dtype, nk=1, tk=None, tm=1024, token=None):
    s, n = a.shape
    tm = _row_tile(s, tm)
    tk = n if tk is None else tk
    return _mm(name, a, w, dims=NT, grid=(s // tm, nk), a_spec=pl.BlockSpec((tm, tk), lambda i, k: (i, k)),
               b_spec=pl.BlockSpec(w_block, w_index), o_spec=pl.BlockSpec((tm, kout), lambda i, k: (i, 0)),
               o_shape=_sds((s, kout), out_dtype), nk=nk, acc_shape=(tm, kout), token=token)


def mm_tn(name, a, b, *, groups, a_block, a_index, b_block, b_index, o_block, o_index, o_shape, acc_shape,
          colsum=None, tk=GRAD_ROWS):
    s = a.shape[-2]
    tk = _row_tile(s, tk)
    return _mm(name, a, b, dims=TN, grid=(groups, s // tk), a_spec=pl.BlockSpec(a_block(tk), a_index),
               b_spec=pl.BlockSpec(b_block(tk), b_index), o_spec=pl.BlockSpec(o_block, o_index),
               o_shape=_sds(o_shape, BF16), nk=s // tk, acc_shape=acc_shape, colsum=colsum)


FFN_SUB = 256


def _sub_rows(tm):
    sub = FFN_SUB if tm % FFN_SUB == 0 else tm
    return [slice(r * sub, (r + 1) * sub) for r in range(tm // sub)]


def ffn_fwd(u, h, w_up_t, w_down, layer, name, norms=(), token=None):
    s, d = u.shape
    tm = _row_tile(s, FFN_ROWS)
    nj = 4
    nn = len(norms)

    extra = [] if token is None else [token]
    nx = len(extra)

    def body(u_ref, h_ref, wup_ref, wd_ref, *rest):
        gain_refs, (hn_ref, gu_ref), un_refs = rest[:nn], rest[nn + nx:nn + nx + 2], rest[nn + nx + 2:]
        j = pl.program_id(1)

        @pl.when(j == 0)
        def _():
            hn_ref[...] = h_ref[...]

        for rows in _sub_rows(tm):
            uv = u_ref[rows, :]
            g = _dot(uv, wup_ref[0], NT)
            p = _dot(uv, wup_ref[1], NT)
            gu_ref[0, rows, :] = g.astype(BF16)
            gu_ref[1, rows, :] = p.astype(BF16)
            act = (g * jax.nn.sigmoid(g) * p).astype(BF16)
            hn_ref[rows, :] += _dot(act, wd_ref[...], NN)

        if nn:
            @pl.when(j == nj - 1)
            def _():
                for rows in _sub_rows(tm):
                    for gain_ref, un_ref in zip(gain_refs, un_refs):
                        un_ref[rows, :] = _rms_rows(hn_ref[rows, :], gain_ref[...])

    row = pl.BlockSpec((tm, d), lambda i, j: (i, 0))
    vec = pl.BlockSpec((1, d), lambda i, j: (0, 0))
    return _pcall(
        body, name, (_sds((s, d), F32), _sds((2, nj, s, FF_CHUNK), BF16)) + (_sds((s, d), BF16),) * nn,
        grid=(s // tm, nj),
        in_specs=[row, row,
                  pl.BlockSpec((None, 2, None, FF_CHUNK, d), lambda i, j: (layer, 0, j, 0, 0)),
                  pl.BlockSpec((None, None, FF_CHUNK, d), lambda i, j: (layer, j, 0, 0))] + [vec] * nn + [ANY] * nx,
        out_specs=(row, pl.BlockSpec((2, None, tm, FF_CHUNK), lambda i, j: (0, j, i, 0))) + (row,) * nn,
        sem=("parallel", "arbitrary"))(u, h, w_up_t, w_down, *norms, *extra)


def _rms_bwd_rows(x, gain, du, dh_in):
    r = lax.rsqrt(jnp.mean(x * x, axis=-1, keepdims=True) + EPS)
    xh = x * r
    dxh = du * gain
    dx = r * (dxh - xh * jnp.mean(dxh * xh, axis=-1, keepdims=True))
    return dh_in + dx, jnp.sum(du * xh, axis=0, keepdims=True)


def ffn_bwd(dh, gu, w_up_t, w_down, layer, h_mid, gain, name, token=None):
    s, d = dh.shape
    tm = _row_tile(s, FFN_BWD_ROWS)
    nj = 4

    def body(dh_ref, gu_ref, wup_ref, wd_ref, h_ref, gain_ref, *rest):
        dho_ref, dgain_ref, act_ref, dgu_ref, du_ref = rest[-5:]
        i, j = pl.program_id(0), pl.program_id(1)

        @pl.when(j == 0)
        def _():
            du_ref[...] = jnp.zeros_like(du_ref)

        @pl.when((i == 0) & (j == 0))
        def _():
            dgain_ref[...] = jnp.zeros_like(dgain_ref)

        for rows in _sub_rows(tm):
            dact = _dot(dh_ref[rows, :].astype(BF16), wd_ref[...], NT)
            g = gu_ref[0, rows, :].astype(F32)
            p = gu_ref[1, rows, :].astype(F32)
            sig = jax.nn.sigmoid(g)
            sl = g * sig
            act_ref[rows, :] = (sl * p).astype(BF16)
            dp = (dact * sl).astype(BF16)
            dg = (dact * p * (sig * (1.0 + g * (1.0 - sig)))).astype(BF16)
            dgu_ref[0, rows, :] = dg
            dgu_ref[1, rows, :] = dp
            du_ref[rows, :] += _dot(dg, wup_ref[0], NN) + _dot(dp, wup_ref[1], NN)

        @pl.when(j == nj - 1)
        def _():
            for rows in _sub_rows(tm):
                dho, part = _rms_bwd_rows(h_ref[rows, :], gain_ref[...], du_ref[rows, :], dh_ref[rows, :])
                dho_ref[rows, :] = dho
                dgain_ref[...] += part

    row = pl.BlockSpec((tm, d), lambda i, j: (i, 0))
    vec = pl.BlockSpec((1, d), lambda i, j: (0, 0))
    gu_spec = pl.BlockSpec((2, None, tm, FF_CHUNK), lambda i, j: (0, j, i, 0))
    extra = [] if token is None else [token]
    return _pcall(
        body, name,
        (_sds((s, d), F32), _sds((1, d), F32), _sds((nj, s, FF_CHUNK), BF16), _sds((2, nj, s, FF_CHUNK), BF16)),
        grid=(s // tm, nj),
        in_specs=[row, gu_spec,
                  pl.BlockSpec((None, 2, None, FF_CHUNK, d), lambda i, j: (layer, 0, j, 0, 0)),
                  pl.BlockSpec((None, None, FF_CHUNK, d), lambda i, j: (layer, j, 0, 0)), row, vec]
        + [ANY] * len(extra),
        out_specs=(row, vec, pl.BlockSpec((None, tm, FF_CHUNK), lambda i, j: (j, i, 0)), gu_spec),
        scratch=[pltpu.VMEM((tm, d), F32)],
        sem=("arbitrary", "arbitrary"))(dh, gu, w_up_t, w_down, h_mid, gain, *extra)


def nt_rms_bwd(name, a, w, w_block, w_index, h, gain, dh_in):
    s, n = a.shape
    d = h.shape[1]
    tm = _row_tile(s, 1024)

    def body(a_ref, w_ref, h_ref, gain_ref, dhi_ref, dho_ref, dgain_ref):
        i = pl.program_id(0)

        @pl.when(i == 0)
        def _():
            dgain_ref[...] = jnp.zeros_like(dgain_ref)

        du = _dot(a_ref[...].astype(BF16), w_ref[...], NT)
        dho, part = _rms_bwd_rows(h_ref[...], gain_ref[...], du, dhi_ref[...])
        dho_ref[...] = dho
        dgain_ref[...] += part

    row = pl.BlockSpec((tm, d), lambda i: (i, 0))
    vec = pl.BlockSpec((1, d), lambda i: (0, 0))
    return _pcall(body, name, (_sds((s, d), F32), _sds((1, d), F32)), grid=(s // tm,),
                  in_specs=[pl.BlockSpec((tm, n), lambda i: (i, 0)), pl.BlockSpec(w_block, w_index), row, vec, row],
                  out_specs=(row, vec), sem=("arbitrary",))(a, w, h, gain, dh_in)


def pw1_fwd(u, w, b, layer, name):
    s, d = u.shape
    tm = _row_tile(s, FFN_ROWS)
    nb, wb = w.shape[1], w.shape[3]

    def body(u_ref, w_ref, b_ref, t_ref):
        for rows in _sub_rows(tm):
            uv = u_ref[rows, :]
            for j in range(nb):
                cols = slice(j * wb, (j + 1) * wb)
                t_ref[rows, cols] = (_dot(uv, w_ref[j], NN) + b_ref[:, cols]).astype(BF16)

    return _pcall(
        body, name, _sds((s, nb * wb), BF16), grid=(s // tm,),
        in_specs=[pl.BlockSpec((tm, d), lambda i: (i, 0)), pl.BlockSpec((None, nb, d, wb), lambda i: (0, 0, 0, 0)),
                  pl.BlockSpec((None, 1, nb * wb), lambda i: (layer, 0, 0))],
        out_specs=pl.BlockSpec((tm, nb * wb), lambda i: (i, 0)), sem=("parallel",))(u, w, b)


def pw1_bwd(dt, w, h, gain, dh_in, name, token=None):
    s = dt.shape[0]
    nb, d, wb = w.shape[1], w.shape[2], w.shape[3]
    tm = _row_tile(s, 512)

    def body(dt_ref, w_ref, h_ref, gain_ref, dhi_ref, *rest):
        dho_ref, dgain_ref = rest[-2:]
        i = pl.program_id(0)

        @pl.when(i == 0)
        def _():
            dgain_ref[...] = jnp.zeros_like(dgain_ref)

        for rows in _sub_rows(tm):
            du = _dot(dt_ref[rows, 0:wb], w_ref[0], NT)
            for j in range(1, nb):
                du = du + _dot(dt_ref[rows, j * wb:(j + 1) * wb], w_ref[j], NT)
            dho, part = _rms_bwd_rows(h_ref[rows, :], gain_ref[...], du, dhi_ref[rows, :])
            dho_ref[rows, :] = dho
            dgain_ref[...] += part

    row = pl.BlockSpec((tm, d), lambda i: (i, 0))
    vec = pl.BlockSpec((1, d), lambda i: (0, 0))
    extra = [] if token is None else [token]
    return _pcall(
        body, name, (_sds((s, d), F32), _sds((1, d), F32)), grid=(s // tm,),
        in_specs=[pl.BlockSpec((tm, nb * wb), lambda i: (i, 0)),
                  pl.BlockSpec((None, nb, d, wb), lambda i: (0, 0, 0, 0)), row, vec, row] + [ANY] * len(extra),
        out_specs=(row, vec), sem=("arbitrary",))(dt, w, h, gain, dh_in, *extra)


def pw1_grad(u, dt, name):
    s, d = u.shape
    n = dt.shape[1]
    nb = N_DEV
    wb = n // nb
    tk = _row_tile(s, 1024)
    nk = s // tk

    def body(u_ref, dt_ref, g_ref, db_ref, acc_ref):
        k = pl.program_id(0)
        p = _dot(u_ref[...], dt_ref[...], TN)
        cs = jnp.sum(dt_ref[...].astype(F32), axis=0, keepdims=True)

        @pl.when(k == 0)
        def _():
            acc_ref[...] = p
            db_ref[...] = cs

        @pl.when(k > 0)
        def _():
            acc_ref[...] += p
            db_ref[...] += cs

        @pl.when(k == nk - 1)
        def _():
            for j in range(nb):
                g_ref[j] = acc_ref[:, j * wb:(j + 1) * wb].astype(BF16)

    return _pcall(
        body, name, (_sds((nb, d, wb), BF16), _sds((1, n), F32)), grid=(nk,),
        in_specs=[pl.BlockSpec((tk, d), lambda k: (k, 0)), pl.BlockSpec((tk, n), lambda k: (k, 0))],
        out_specs=(pl.BlockSpec((nb, d, wb), lambda k: (0, 0, 0)), pl.BlockSpec((1, n), lambda k: (0, 0))),
        scratch=[pltpu.VMEM((d, n), F32)], sem=("arbitrary",))(u, dt)


def _glu(t):
    t = t.astype(F32)
    return t[:, :D_MODEL] * jax.nn.sigmoid(t[:, D_MODEL:])


CONV_TILE = 256


def _conv_tile(s):
    return CONV_TILE if s % CONV_TILE == 0 else s


CONV_ROWS = 32
CONV_LANES = 512
SUBLANES = 8


def _shifted_copies(sh_ref, rows):
    for b in range(1, SUBLANES):
        sh_ref[b, 0:rows - SUBLANES, :] = sh_ref[0, b:b + rows - SUBLANES, :]


def conv_fwd(t, cp, name):
    s = t.shape[0]
    d = D_MODEL
    ts = _conv_tile(s)
    per = ts // HALO
    rows = HALO + ts
    lead = HALO - (CONV_WIDTH - 1)
    rc = CONV_ROWS

    def body(t_ref, tp_ref, cp_ref, z_ref, y_ref, sh_ref):
        i = pl.program_id(0)
        sh_ref[0, 0:HALO, :] = jnp.where(i > 0, _glu(tp_ref[...]), 0.0)
        sh_ref[0, HALO:rows, :] = _glu(t_ref[...])
        _shifted_copies(sh_ref, rows)

        def chunk(c, carry):
            r0 = pl.multiple_of(c * rc, rc)
            for lc in range(d // CONV_LANES):
                ln = slice(lc * CONV_LANES, (lc + 1) * CONV_LANES)
                acc = jnp.zeros((rc, CONV_LANES), F32) + cp_ref[ROW_BDW:ROW_BDW + 1, ln]
                for k in range(CONV_WIDTH):
                    a8, b = divmod(lead + k, SUBLANES)
                    acc = acc + cp_ref[k:k + 1, ln] * sh_ref[b, pl.ds(r0 + SUBLANES * a8, rc), ln]
                y_ref[pl.ds(r0, rc), ln] = acc
            y = y_ref[pl.ds(r0, rc), :]
            mu = jnp.mean(y, axis=-1, keepdims=True)
            yc = y - mu
            rstd = lax.rsqrt(jnp.mean(yc * yc, axis=-1, keepdims=True) + EPS)
            yn = yc * rstd * cp_ref[ROW_LNG:ROW_LNG + 1, :] + cp_ref[ROW_LNB:ROW_LNB + 1, :]
            z_ref[pl.ds(r0, rc), :] = (yn * jax.nn.sigmoid(yn)).astype(BF16)
            return carry

        lax.fori_loop(0, ts // rc, chunk, 0)

    row = pl.BlockSpec((ts, d), lambda i: (i, 0))
    return _pcall(
        body, name, (_sds((s, d), BF16), _sds((s, d), F32)), grid=(s // ts,),
        in_specs=[pl.BlockSpec((ts, 2 * d), lambda i: (i, 0)),
                  pl.BlockSpec((HALO, 2 * d), lambda i: (jnp.maximum(i * per - 1, 0), 0)),
                  pl.BlockSpec((PACK_ROWS, d), lambda i: (0, 0))],
        out_specs=(row, row),
        scratch=[pltpu.VMEM((SUBLANES, rows, d), F32)], sem=("parallel",))(t, t, cp)


def conv_bwd(t, y, dz, cp, name):
    s = t.shape[0]
    d = D_MODEL
    ts = _conv_tile(s)
    per = ts // HALO
    nt = s // ts
    te = ts + HALO
    rc = CONV_ROWS

    def body(t_ref, y_ref, yn_ref, dz_ref, dzn_ref, cp_ref, dt_ref, st_ref, shd_ref, dw_ref):
        i = pl.program_id(0)
        last = i == nt - 1

        @pl.when(i == 0)
        def _():
            st_ref[...] = jnp.zeros_like(st_ref)
            dw_ref[...] = jnp.zeros_like(dw_ref)

        gain = cp_ref[ROW_LNG:ROW_LNG + 1, :]

        def ln_bwd(yv, dzv):
            mu = jnp.mean(yv, axis=-1, keepdims=True)
            yc = yv - mu
            rstd = lax.rsqrt(jnp.mean(yc * yc, axis=-1, keepdims=True) + EPS)
            yh = yc * rstd
            yn = yh * gain + cp_ref[ROW_LNB:ROW_LNB + 1, :]
            sig = jax.nn.sigmoid(yn)
            dyn = dzv * (sig * (1.0 + yn * (1.0 - sig)))
            dyh = dyn * gain
            dy = rstd * (dyh - jnp.mean(dyh, axis=-1, keepdims=True)
                         - yh * jnp.mean(dyh * yh, axis=-1, keepdims=True))
            return dy, dyn, yh

        def norm_chunk(c, carry):
            r0 = pl.multiple_of(c * rc, rc)
            dy, dyn, yh = ln_bwd(y_ref[pl.ds(r0, rc), :], dz_ref[pl.ds(r0, rc), :])
            shd_ref[0, pl.ds(r0, rc), :] = dy
            st_ref[ROW_BDW:ROW_BDW + 1, :] += jnp.sum(dy, axis=0, keepdims=True)
            st_ref[ROW_LNG:ROW_LNG + 1, :] += jnp.sum(dyn * yh, axis=0, keepdims=True)
            st_ref[ROW_LNB:ROW_LNB + 1, :] += jnp.sum(dyn, axis=0, keepdims=True)
            return carry

        lax.fori_loop(0, ts // rc, norm_chunk, 0)
        dy_halo, _, _ = ln_bwd(yn_ref[...], jnp.where(last, 0.0, dzn_ref[...]))
        shd_ref[0, ts:te, :] = dy_halo
        _shifted_copies(shd_ref, te)

        def tap_chunk(c, carry):
            r0 = pl.multiple_of(c * rc, rc)
            for lc in range(d // CONV_LANES):
                ln = slice(lc * CONV_LANES, (lc + 1) * CONV_LANES)
                ln2 = slice(d + lc * CONV_LANES, d + (lc + 1) * CONV_LANES)
                t1 = t_ref[pl.ds(r0, rc), ln].astype(F32)
                sg = jax.nn.sigmoid(t_ref[pl.ds(r0, rc), ln2].astype(F32))
                a = t1 * sg
                da = jnp.zeros((rc, CONV_LANES), F32)
                for k in range(CONV_WIDTH):
                    a8, b = divmod(CONV_WIDTH - 1 - k, SUBLANES)
                    e = shd_ref[b, pl.ds(r0 + SUBLANES * a8, rc), ln]
                    da = da + cp_ref[k:k + 1, ln] * e
                    dw_ref[k, :, ln] += jnp.sum((a * e).reshape(rc // SUBLANES, SUBLANES, CONV_LANES), axis=0)
                dt_ref[pl.ds(r0, rc), ln] = (da * sg).astype(BF16)
                dt_ref[pl.ds(r0, rc), ln2] = (da * t1 * sg * (1.0 - sg)).astype(BF16)
            return carry

        lax.fori_loop(0, ts // rc, tap_chunk, 0)

        @pl.when(last)
        def _():
            for k in range(CONV_WIDTH):
                st_ref[k:k + 1, :] = jnp.sum(dw_ref[k], axis=0, keepdims=True)

    last_halo = s // HALO - 1
    row = pl.BlockSpec((ts, d), lambda i: (i, 0))
    halo = pl.BlockSpec((HALO, d), lambda i: (jnp.minimum((i + 1) * per, last_halo), 0))
    return _pcall(
        body, name, (_sds((s, 2 * d), BF16), _sds((PACK_ROWS, d), F32)), grid=(nt,),
        in_specs=[pl.BlockSpec((ts, 2 * d), lambda i: (i, 0)), row, halo, row, halo,
                  pl.BlockSpec((PACK_ROWS, d), lambda i: (0, 0))],
        out_specs=(pl.BlockSpec((ts, 2 * d), lambda i: (i, 0)), pl.BlockSpec((PACK_ROWS, d), lambda i: (0, 0))),
        scratch=[pltpu.VMEM((SUBLANES, te, d), F32), pltpu.VMEM((CONV_WIDTH, SUBLANES, d), F32)],
        sem=("arbitrary",))(t, y, y, dz, dz, cp)


def _bucket_table():
    qi = np.arange(BLOCK, dtype=np.int64)[:, None]
    kj = np.arange(2 * BLOCK, dtype=np.int64)[None, :]
    dist = qi + BLOCK - kj
    max_exact = N_BUCKETS // 2
    dd = np.maximum(dist, 0)
    ratio = (np.maximum(dd, 1).astype(np.float32) / np.float32(max_exact)).astype(np.float32)
    log_ratio = (np.log(ratio).astype(np.float32) / np.float32(math.log(MAX_DISTANCE / max_exact))).astype(np.float32)
    large = max_exact + (log_ratio * np.float32(N_BUCKETS - max_exact)).astype(np.int32)
    large = np.minimum(large, N_BUCKETS - 1)
    bucket = np.where(dd < max_exact, dd, large)
    return np.where((dist >= 0) & (dist < BLOCK), bucket, -1).astype(np.int32)


def bias_table(rel_bias, bucket, name):
    def body(rb_ref, bk_ref, o_ref):
        bk = bk_ref[...]
        for h in range(N_HEADS):
            acc = jnp.full((BLOCK, 2 * BLOCK), NEG_INF, F32)
            for b in range(N_BUCKETS):
                acc = jnp.where(bk == b, rb_ref[b, h], acc)
            o_ref[h] = acc

    return _pcall(body, name, _sds((N_HEADS, BLOCK, 2 * BLOCK), F32),
                  in_specs=[pl.BlockSpec(memory_space=pltpu.SMEM), pl.BlockSpec(memory_space=pltpu.VMEM)],
                  out_specs=pl.BlockSpec(memory_space=pltpu.VMEM))(rel_bias, bucket)


def bias_grad(dba, dbb, bucket, name):
    def body(a_ref, b_ref, bk_ref, o_ref):
        bk = bk_ref[...]
        for h in range(N_HEADS):
            db = a_ref[h] + b_ref[h]
            for b in range(N_BUCKETS):
                o_ref[b, h] = jnp.sum(jnp.where(bk == b, db, 0.0))

    vm = pl.BlockSpec(memory_space=pltpu.VMEM)
    return _pcall(body, name, _sds((N_BUCKETS, N_HEADS), F32), in_specs=[vm, vm, vm],
                  out_specs=pl.BlockSpec(memory_space=pltpu.SMEM))(dba, dbb, bucket)


def _band_specs():
    cur = pl.BlockSpec((BLOCK, 2 * KV_DIM), lambda n: (n, 0))
    prev = pl.BlockSpec((BLOCK, 2 * KV_DIM), lambda n: (jnp.maximum(n - 1, 0), 0))
    return cur, prev


def _scores(q_h, k_h, bias_h, first_row, sink):
    sc = _dot(q_h, k_h, NT) * (HEAD_DIM ** -0.5) + bias_h + first_row
    m = jnp.maximum(jnp.max(sc, axis=-1, keepdims=True), sink)
    p = jnp.exp(sc - m)
    e_sink = jnp.exp(sink - m)
    den = jnp.sum(p, axis=-1, keepdims=True) + e_sink
    return p, e_sink, den


def _first_block_row(n):
    col = lax.broadcasted_iota(jnp.int32, (1, 2 * BLOCK), 1)
    return jnp.where((col < BLOCK) & (n == 0), NEG_INF, 0.0)


def _head_lanes(hk, g):
    h = hk * GROUP + g
    return slice(h * HEAD_DIM, (h + 1) * HEAD_DIM)


def _group_rows(x_ref, hk):
    return jnp.concatenate([x_ref[:, _head_lanes(hk, g)] for g in range(GROUP)], axis=0)


def _group_bias(bias_ref, hk):
    return bias_ref[hk * GROUP:(hk + 1) * GROUP].reshape(GROUP * BLOCK, 2 * BLOCK)


def _group_sinks(sink_ref, hk):
    head = lax.broadcasted_iota(jnp.int32, (GROUP * BLOCK, 1), 0) // BLOCK
    col = jnp.zeros((GROUP * BLOCK, 1), F32) + sink_ref[0, hk * GROUP]
    for g in range(1, GROUP):
        col = jnp.where(head == g, sink_ref[0, hk * GROUP + g], col)
    return col


def attn_fwd(q, kv, bias, sinks, name):
    s = q.shape[0]
    nb = s // BLOCK

    def body(sink_ref, q_ref, kvc_ref, kvp_ref, bias_ref, o_ref, band_ref):
        n = pl.program_id(0)
        band_ref[0:BLOCK, :] = kvp_ref[...]
        band_ref[BLOCK:2 * BLOCK, :] = kvc_ref[...]
        first_row = _first_block_row(n)
        for hk in range(N_KV_HEADS):
            k_h = band_ref[:, hk * HEAD_DIM:(hk + 1) * HEAD_DIM]
            v_h = band_ref[:, KV_DIM + hk * HEAD_DIM:KV_DIM + (hk + 1) * HEAD_DIM]
            p, _, den = _scores(_group_rows(q_ref, hk), k_h, _group_bias(bias_ref, hk), first_row,
                                _group_sinks(sink_ref, hk))
            o = _dot((p * (1.0 / den)).astype(BF16), v_h, NN).astype(BF16)
            for g in range(GROUP):
                o_ref[:, _head_lanes(hk, g)] = o[g * BLOCK:(g + 1) * BLOCK]

    cur, prev = _band_specs()
    qs = pl.BlockSpec((BLOCK, D_MODEL), lambda n: (n, 0))
    return _pcall(
        body, name, _sds((s, D_MODEL), BF16), grid=(nb,),
        in_specs=[pl.BlockSpec(memory_space=pltpu.SMEM), qs, cur, prev,
                  pl.BlockSpec((N_HEADS, BLOCK, 2 * BLOCK), lambda n: (0, 0, 0))],
        out_specs=qs, scratch=[pltpu.VMEM((2 * BLOCK, 2 * KV_DIM), BF16)],
        sem=("parallel",))(sinks, q, kv, kv, bias)


def attn_bwd(q, kv, do, bias, sinks, name):
    s = q.shape[0]
    nb = s // BLOCK
    scale = HEAD_DIM ** -0.5

    def body(sink_ref, q_ref, do_ref, kvc_ref, kvp_ref, bias_ref, dq_ref, dkv_ref, db_ref, dsink_ref,
             band_ref, dsacc_ref):
        n = pl.program_id(0)
        band_ref[0:BLOCK, :] = kvp_ref[...]
        band_ref[BLOCK:2 * BLOCK, :] = kvc_ref[...]
        first_row = _first_block_row(n)
        lane = lax.broadcasted_iota(jnp.int32, (BLOCK, BLOCK), 1)

        @pl.when(n == 0)
        def _():
            db_ref[...] = jnp.zeros_like(db_ref)
            dsacc_ref[...] = jnp.zeros_like(dsacc_ref)

        for hk in range(N_KV_HEADS):
            k_h = band_ref[:, hk * HEAD_DIM:(hk + 1) * HEAD_DIM]
            v_h = band_ref[:, KV_DIM + hk * HEAD_DIM:KV_DIM + (hk + 1) * HEAD_DIM]
            q_g = _group_rows(q_ref, hk)
            do_g = _group_rows(do_ref, hk)
            p, e_sink, den = _scores(q_g, k_h, _group_bias(bias_ref, hk), first_row, _group_sinks(sink_ref, hk))
            inv = 1.0 / den
            p = p * inv
            dp = _dot(do_g, v_h, NT)
            delta = jnp.sum(p * dp, axis=-1, keepdims=True)
            ds = p * (dp - delta)
            db_ref[hk * GROUP:(hk + 1) * GROUP] += ds.reshape(GROUP, BLOCK, 2 * BLOCK)
            d_sink = -(e_sink * inv) * delta
            for g in range(GROUP):
                dsacc_ref[...] += jnp.where(lane == hk * GROUP + g, d_sink[g * BLOCK:(g + 1) * BLOCK], 0.0)
            dsb = ds.astype(BF16)
            dq = (_dot(dsb, k_h, NN) * scale).astype(BF16)
            for g in range(GROUP):
                dq_ref[:, _head_lanes(hk, g)] = dq[g * BLOCK:(g + 1) * BLOCK]
            dkv_ref[:, hk * HEAD_DIM:(hk + 1) * HEAD_DIM] = _dot(dsb, q_g, TN) * scale
            dkv_ref[:, KV_DIM + hk * HEAD_DIM:KV_DIM + (hk + 1) * HEAD_DIM] = _dot(p.astype(BF16), do_g, TN)

        @pl.when(n == nb - 1)
        def _():
            dsink_ref[...] = jnp.sum(dsacc_ref[...], axis=0, keepdims=True)

    cur, prev = _band_specs()
    qs = pl.BlockSpec((BLOCK, D_MODEL), lambda n: (n, 0))
    full_b = pl.BlockSpec((N_HEADS, BLOCK, 2 * BLOCK), lambda n: (0, 0, 0))
    return _pcall(
        body, name,
        (_sds((s, D_MODEL), BF16), _sds((nb, 2 * BLOCK, 2 * KV_DIM), F32),
         _sds((N_HEADS, BLOCK, 2 * BLOCK), F32), _sds((1, BLOCK), F32)),
        grid=(nb,),
        in_specs=[pl.BlockSpec(memory_space=pltpu.SMEM), qs, qs, cur, prev, full_b],
        out_specs=(qs, pl.BlockSpec((None, 2 * BLOCK, 2 * KV_DIM), lambda n: (n, 0, 0)), full_b,
                   pl.BlockSpec((1, BLOCK), lambda n: (0, 0))),
        scratch=[pltpu.VMEM((2 * BLOCK, 2 * KV_DIM), BF16), pltpu.VMEM((BLOCK, BLOCK), F32)],
        sem=("arbitrary",))(sinks, q, do, kv, kv, bias)


def dkv_combine(pa, pb, name):
    nb = pa.shape[0]
    pa2 = pa.reshape(2 * nb, BLOCK, 2 * KV_DIM)
    pb2 = pb.reshape(2 * nb, BLOCK, 2 * KV_DIM)

    def body(ac_ref, an_ref, bc_ref, bn_ref, o_ref):
        n = pl.program_id(0)
        nxt = jnp.where(n == nb - 1, 0.0, an_ref[...] + bn_ref[...])
        o_ref[...] = (ac_ref[...] + bc_ref[...] + nxt).astype(BF16)

    cur = pl.BlockSpec((None, BLOCK, 2 * KV_DIM), lambda n: (2 * n + 1, 0, 0))
    nxt = pl.BlockSpec((None, BLOCK, 2 * KV_DIM), lambda n: (jnp.minimum(2 * n + 2, 2 * nb - 2), 0, 0))
    return _pcall(body, name, _sds((nb * BLOCK, 2 * KV_DIM), BF16), grid=(nb,),
                  in_specs=[cur, nxt, cur, nxt], out_specs=pl.BlockSpec((BLOCK, 2 * KV_DIM), lambda n: (n, 0)),
                  sem=("parallel",))(pa2, pa2, pb2, pb2)


def loss_head(h, g, target, name):
    s, d = h.shape
    tm = _row_tile(s, 512)

    def body(h_ref, g_ref, t_ref, dh_ref, dg_ref, loss_ref):
        i = pl.program_id(0)
        x = h_ref[...]
        r = lax.rsqrt(jnp.mean(x * x, axis=-1, keepdims=True) + EPS)
        xh = x * r
        gv = g_ref[...]
        err = xh * gv - t_ref[...]
        part_loss = jnp.zeros((1, BLOCK), F32) + 0.5 * jnp.sum(jnp.mean(err * err, axis=-1, keepdims=True))
        dy = err * (1.0 / d)
        dxh = dy * gv
        dh_ref[...] = r * (dxh - xh * jnp.mean(dxh * xh, axis=-1, keepdims=True))
        part_g = jnp.sum(dy * xh, axis=0, keepdims=True)

        @pl.when(i == 0)
        def _():
            dg_ref[...] = part_g
            loss_ref[...] = part_loss

        @pl.when(i > 0)
        def _():
            dg_ref[...] += part_g
            loss_ref[...] += part_loss

    row = pl.BlockSpec((tm, d), lambda i: (i, 0))
    vec = pl.BlockSpec((1, d), lambda i: (0, 0))
    return _pcall(body, name, (_sds((s, d), F32), _sds((1, d), F32), _sds((1, BLOCK), F32)), grid=(s // tm,),
                  in_specs=[row, vec, row], out_specs=(row, vec, pl.BlockSpec((1, BLOCK), lambda i: (0, 0))),
                  sem=("arbitrary",))(h, g, target)


def adamw(w, m, v, parts, name, token=None):
    nl, r, c = w.shape
    tr = max(t for t in range(1, min(r, 512) + 1) if r % t == 0 and (t % 16 == 0 or t == r))
    c1 = 1.0 / (1.0 - ADAM_B1 ** ADAM_STEP)
    c2 = 1.0 / (1.0 - ADAM_B2 ** ADAM_STEP)

    def body(w_ref, m_ref, v_ref, p_ref, *rest):
        g_ref, d_ref, nm_ref, nv_ref = rest[-4:]
        g = p_ref[0].astype(F32)
        for dev in range(1, N_DEV):
            g = g + p_ref[dev].astype(F32)
        nm = ADAM_B1 * m_ref[...] + (1.0 - ADAM_B1) * g
        nv = ADAM_B2 * v_ref[...] + (1.0 - ADAM_B2) * (g * g)
        g_ref[...] = g
        nm_ref[...] = nm
        nv_ref[...] = nv
        d_ref[...] = -ADAM_LR * ((nm * c1) / (jnp.sqrt(nv * c2) + ADAM_EPS) + ADAM_WD * w_ref[...])

    blk = pl.BlockSpec((None, tr, c), lambda l, i: (l, i, 0))
    out = _sds((nl, r, c), F32)
    extra = [] if token is None else [token]
    return _pcall(body, name, (out, out, out, out), grid=(nl, r // tr),
                  in_specs=[blk, blk, blk, pl.BlockSpec((N_DEV, None, tr, c), lambda l, i: (0, l, i, 0))]
                  + [ANY] * len(extra),
                  out_specs=(blk, blk, blk, blk), sem=("parallel", "parallel"))(w, m, v, parts, *extra)


def _place():
    x, y, c = lax.axis_index("x"), lax.axis_index("y"), lax.axis_index("c")
    return x, y, c


def _lin(px, py, pc):
    return 4 * px + 2 * py + pc


HBM = pl.BlockSpec(memory_space=pltpu.HBM)
SEM = pl.BlockSpec(memory_space=pltpu.SEMAPHORE)
EFFECT = pltpu.SideEffectType.DATAFLOW_SIDE_EFFECTING
N_PEERS = N_DEV - 1


def _peers_of(x, y, c):
    return [(x, y, 1 - c), (1 - x, y, c), (x, 1 - y, c), (1 - x, 1 - y, c),
            (1 - x, y, 1 - c), (x, 1 - y, 1 - c), (1 - x, 1 - y, 1 - c)]


def _in_hbm(a):
    return pltpu.with_memory_space_constraint(a, pltpu.HBM)


def send_start(name, bufs, copies, n_groups):
    nb = len(bufs)
    per_group = [[i for i, cp in enumerate(copies) if cp[0] == g] for g in range(n_groups)]

    def body(*refs):
        buf = refs[:nb]
        sems = refs[nb:nb + 2 * n_groups]
        token = refs[2 * nb + 2 * n_groups]
        x, y, c = _place()
        me = _lin(x, y, c)
        for g in range(n_groups):
            for slot, i in enumerate(per_group[g]):
                _, s, src_slab, d, land_slab = copies[i]
                for k, peer in enumerate(_peers_of(x, y, c)):
                    pltpu.make_async_remote_copy(
                        src_ref=src_slab(buf[s], _lin(*peer), me), dst_ref=land_slab(buf[d], me),
                        send_sem=sems[2 * g].at[slot * N_PEERS + k], recv_sem=sems[2 * g + 1].at[slot * N_PEERS + k],
                        device_id=peer, device_id_type=MESH).start()
        token[...] = jnp.zeros_like(token)

    sem_shapes = []
    for g in range(n_groups):
        sem_shapes += [pltpu.SemaphoreType.DMA((len(per_group[g]) * N_PEERS,))] * 2
    out = pl.pallas_call(
        body, name=name,
        out_shape=tuple(sem_shapes) + tuple(pltpu.HBM(b.shape, b.dtype) for b in bufs) + (_sds((8, 128), F32),),
        in_specs=[HBM] * nb,
        out_specs=tuple([SEM] * len(sem_shapes)) + tuple([HBM] * nb) + (pl.BlockSpec(memory_space=pltpu.VMEM),),
        input_output_aliases={i: len(sem_shapes) + i for i in range(nb)},
        compiler_params=pltpu.CompilerParams(has_side_effects=EFFECT))(*[_in_hbm(b) for b in bufs])
    sems = [(out[2 * g], out[2 * g + 1]) for g in range(n_groups)]
    return sems, list(out[2 * n_groups:2 * n_groups + nb]), out[2 * n_groups + nb]


N_FIRST = 4
N_RELAY = 3


def _gather_peers(x, y, c):
    first = [(x, y, 1 - c), (1 - x, y, c), (x, 1 - y, c), (1 - x, 1 - y, c)]
    return first, first[1:]


def gather_start(name, bufs, copies, n_groups):
    nb = len(bufs)
    per_group = [[i for i, cp in enumerate(copies) if cp[0] == g] for g in range(n_groups)]

    def body(*refs):
        buf = refs[:nb]
        sems = refs[nb:nb + 2 * n_groups]
        token = refs[2 * nb + 2 * n_groups]
        x, y, c = _place()
        me = _lin(x, y, c)
        first, _ = _gather_peers(x, y, c)
        for g in range(n_groups):
            for slot, i in enumerate(per_group[g]):
                _, d, slab = copies[i]
                for k, peer in enumerate(first):
                    pltpu.make_async_remote_copy(
                        src_ref=slab(buf[d], me), dst_ref=slab(buf[d], me),
                        send_sem=sems[2 * g].at[slot * N_FIRST + k], recv_sem=sems[2 * g + 1].at[slot * N_FIRST + k],
                        device_id=peer, device_id_type=MESH).start()
        token[...] = jnp.zeros_like(token)

    sem_shapes = []
    for g in range(n_groups):
        sem_shapes += [pltpu.SemaphoreType.DMA((len(per_group[g]) * N_FIRST,))] * 2
    out = pl.pallas_call(
        body, name=name,
        out_shape=tuple(sem_shapes) + tuple(pltpu.HBM(b.shape, b.dtype) for b in bufs) + (_sds((8, 128), F32),),
        in_specs=[HBM] * nb,
        out_specs=tuple([SEM] * len(sem_shapes)) + tuple([HBM] * nb) + (pl.BlockSpec(memory_space=pltpu.VMEM),),
        input_output_aliases={i: len(sem_shapes) + i for i in range(nb)},
        compiler_params=pltpu.CompilerParams(has_side_effects=EFFECT))(*[_in_hbm(b) for b in bufs])
    return [(out[2 * g], out[2 * g + 1]) for g in range(n_groups)], list(out[2 * n_groups:2 * n_groups + nb])


def gather_relay(name, bufs, slabs, first_sems, after):
    nb = len(bufs)

    def body(*refs):
        buf = refs[:nb]
        send_a, recv_a = refs[nb], refs[nb + 1]
        send_b, recv_b = refs[nb + 3], refs[nb + 4]
        token = refs[2 * nb + 5]
        x, y, c = _place()
        first, origins = _gather_peers(x, y, c)
        for n, slab in enumerate(slabs):
            for j, origin in enumerate(origins):
                block = slab(buf[n], _lin(*origin))
                pltpu.make_async_remote_copy(
                    src_ref=block, dst_ref=block, send_sem=send_a.at[n * N_FIRST + 1 + j],
                    recv_sem=recv_a.at[n * N_FIRST + 1 + j], device_id=origin, device_id_type=MESH).wait_recv()
                pltpu.make_async_remote_copy(
                    src_ref=block, dst_ref=block, send_sem=send_b.at[n * N_RELAY + j],
                    recv_sem=recv_b.at[n * N_RELAY + j], device_id=first[0], device_id_type=MESH).start()
        token[...] = jnp.zeros_like(token)

    sem_shape = pltpu.SemaphoreType.DMA((nb * N_RELAY,))
    out = pl.pallas_call(
        body, name=name,
        out_shape=(sem_shape, sem_shape) + tuple(pltpu.HBM(b.shape, b.dtype) for b in bufs) + (_sds((8, 128), F32),),
        in_specs=[HBM] * nb + [SEM, SEM, ANY],
        out_specs=(SEM, SEM) + tuple([HBM] * nb) + (pl.BlockSpec(memory_space=pltpu.VMEM),),
        input_output_aliases={i: 2 + i for i in range(nb)},
        compiler_params=pltpu.CompilerParams(has_side_effects=EFFECT))(*bufs, first_sems[0], first_sems[1], after)
    return (out[0], out[1]), list(out[2:2 + nb]), out[2 + nb]


def gather_wait(name, bufs, slabs, first_sems, relay_sems, after):
    nb = len(bufs)

    def body(*refs):
        buf = refs[:nb]
        send_a, recv_a, send_b, recv_b = refs[nb:nb + 4]
        x, y, c = _place()
        me = _lin(x, y, c)
        first, origins = _gather_peers(x, y, c)
        sibling = first[0]
        for n, slab in enumerate(slabs):
            mine = slab(buf[n], me)
            for k, peer in enumerate(first):
                pltpu.make_async_remote_copy(
                    src_ref=mine, dst_ref=mine, send_sem=send_a.at[n * N_FIRST + k],
                    recv_sem=recv_a.at[n * N_FIRST + k], device_id=peer, device_id_type=MESH).wait_send()
            theirs = slab(buf[n], _lin(*sibling))
            pltpu.make_async_remote_copy(
                src_ref=theirs, dst_ref=theirs, send_sem=send_a.at[n * N_FIRST], recv_sem=recv_a.at[n * N_FIRST],
                device_id=sibling, device_id_type=MESH).wait_recv()
            for j, (ox, oy, oc) in enumerate(origins):
                sent = slab(buf[n], _lin(ox, oy, oc))
                got = slab(buf[n], _lin(ox, oy, 1 - oc))
                pltpu.make_async_remote_copy(
                    src_ref=sent, dst_ref=got, send_sem=send_b.at[n * N_RELAY + j],
                    recv_sem=recv_b.at[n * N_RELAY + j], device_id=sibling, device_id_type=MESH).wait()

    out = pl.pallas_call(
        body, name=name, out_shape=tuple(pltpu.HBM(b.shape, b.dtype) for b in bufs),
        in_specs=[HBM] * nb + [SEM] * 4 + [ANY], out_specs=tuple([HBM] * nb),
        input_output_aliases={i: i for i in range(nb)},
        compiler_params=pltpu.CompilerParams(has_side_effects=EFFECT))(
            *bufs, first_sems[0], first_sems[1], relay_sems[0], relay_sems[1], after)
    return list(out)


def send_wait(name, bufs, copies, sems, after):
    nb = len(bufs)

    def body(*refs):
        buf = refs[:nb]
        send_sems, recv_sems = refs[nb], refs[nb + 1]
        x, y, c = _place()
        me = _lin(x, y, c)
        for slot, (s, src_slab, d, land_slab) in enumerate(copies):
            for k, peer in enumerate(_peers_of(x, y, c)):
                j = _lin(*peer)
                cp = pltpu.make_async_remote_copy(
                    src_ref=src_slab(buf[s], j, me), dst_ref=land_slab(buf[d], j),
                    send_sem=send_sems.at[slot * N_PEERS + k], recv_sem=recv_sems.at[slot * N_PEERS + k],
                    device_id=peer, device_id_type=MESH)
                cp.wait_send()
                cp.wait_recv()

    out = pl.pallas_call(
        body, name=name, out_shape=tuple(pltpu.HBM(b.shape, b.dtype) for b in bufs),
        in_specs=[HBM] * nb + [SEM, SEM, ANY], out_specs=tuple([HBM] * nb),
        input_output_aliases={i: i for i in range(nb)},
        compiler_params=pltpu.CompilerParams(has_side_effects=EFFECT))(*bufs, sems[0], sems[1], after)
    return list(out)


def local_step(x, target, weights, rep, emit):
    s = x.shape[0]
    bucket = jnp.asarray(_bucket_table())
    bias = bias_table(rep["rel_bias"], bucket, "bias_table")
    h = x
    saved = []
    kv = None
    h_kv = u_kv = None
    small = None
    u = rms_fwd(h, rep["norm_mix"][0:1], "rms_mix_fwd0")
    for l in range(4):
        g_ffn = rep["norm_ffn"][l:l + 1]
        rec = {"h_in": h, "u": u}
        if l < 2:
            w = weights(f"conv{l}", u)
            if l == 0:
                small = w
            cp = small["cp"][l]
            t = pw1_fwd(u, w["pw1"], small["b_pw1"], l, f"pw1_fwd{l}")
            z, y = conv_fwd(t, cp, f"conv_fwd{l}")
            relay_tok = weights(f"ffn{l}", z, relay=True)
            h, uf = mm_nn(f"pw2_fwd{l}", z, w["pw2"], (None, D_MODEL, D_MODEL), lambda i, j: (0, 0, j), D_MODEL,
                          D_MODEL, F32, res=h, norm=g_ffn, token=relay_tok,
                          bias=(small["b_pw2"], pl.BlockSpec((None, 1, D_MODEL), lambda i, j, l=l: (l, 0, j))))
            rec.update(t=t, z=z, y=y, cp=cp)
        else:
            a = l - 2
            w = weights(f"attn{a}", u)
            if a == 0:
                h_kv = h
                w_kv = w["wkv"]
                kv = mm_nn("kv_fwd", u_kv, w_kv, (D_MODEL, 2 * KV_DIM), lambda i, j: (0, 0), 2 * KV_DIM,
                           2 * KV_DIM, BF16)
            q = mm_nn(f"q_fwd{a}", u, w["wq"], (None, D_MODEL, D_MODEL), lambda i, j: (0, 0, j), D_MODEL, D_MODEL,
                      BF16)
            o = attn_fwd(q, kv, bias, rep["sinks"][a:a + 1], f"attn_fwd{a}")
            relay_tok = weights(f"ffn{l}", o, relay=True)
            h, uf = mm_nn(f"o_fwd{a}", o, w["wo"], (None, D_MODEL, D_MODEL), lambda i, j: (0, 0, j), D_MODEL,
                          D_MODEL, F32, res=h, norm=g_ffn, token=relay_tok)
            rec.update(q=q, o=o)
        rec["w"] = w
        rec["h_mid"] = h
        wf = weights(f"ffn{l}", uf)
        relay_tok = weights(("conv1", "attn0", "attn1")[l], uf, relay=True) if l < 3 else None
        nxt = [] if l == 3 else [rep["norm_mix"][l + 1:l + 2]] + ([rep["norm_kv"]] if l == 1 else [])
        h, gu, *normed = ffn_fwd(uf, h, wf["up"], wf["down"], 0, f"ffn_fwd{l}", norms=nxt, token=relay_tok)
        if normed:
            u = normed[0]
        if l == 1:
            u_kv = normed[1]
        rec.update(uf=uf, gu=gu, wf=wf)
        saved.append(rec)

    dh, d_nfin, loss = loss_head(h, rep["norm_final"], target, "loss_head")

    d_mix, d_ffn = [None] * 4, [None] * 4
    cp_grads = [None, None]
    dkv_parts, dbias_parts, dsinks = [], [], [None, None]
    d_nkv = None
    full_rows = lambda tk: (tk, D_MODEL)
    tok = None
    for l in reversed(range(4)):
        rec = saved[l]
        w, wf = rec["w"], rec["wf"]
        grads = {}
        g_mix = rep["norm_mix"][l:l + 1]
        g_ffn = rep["norm_ffn"][l:l + 1]
        dh_mid, d_ffn[l], act, dgu = ffn_bwd(dh, rec["gu"], wf["up"], wf["down"], 0, rec["h_mid"], g_ffn,
                                             f"ffn_bwd{l}", token=tok)
        g_down = mm_tn(
            f"down_grad{l}", act, dh, groups=4, a_block=lambda tk: (None, tk, FF_CHUNK), a_index=lambda j, k: (j, k, 0),
            b_block=full_rows, b_index=lambda j, k: (k, 0), o_block=(None, FF_CHUNK, D_MODEL),
            o_index=lambda j, k: (j, 0, 0), o_shape=(4, FF_CHUNK, D_MODEL), acc_shape=(FF_CHUNK, D_MODEL), tk=s)
        g_up = mm_tn(
            f"up_grad{l}", dgu.reshape(8, s, FF_CHUNK), rec["uf"], groups=8, a_block=lambda tk: (None, tk, FF_CHUNK),
            a_index=lambda j, k: (j, k, 0), b_block=full_rows, b_index=lambda j, k: (k, 0),
            o_block=(None, FF_CHUNK, D_MODEL), o_index=lambda j, k: (j, 0, 0), o_shape=(8, FF_CHUNK, D_MODEL),
            acc_shape=(FF_CHUNK, D_MODEL), tk=s)
        tok = emit(f"ffn{l}", {"up": g_up, "down": g_down.reshape(D_FF, D_MODEL)})
        dh = dh_mid
        if l < 2:
            dz = mm_nt(f"pw2_bwd{l}", dh, w["pw2"], (None, D_MODEL, D_MODEL), lambda i, k: (0, 0, 0), D_MODEL, F32,
                       token=tok)
            grads["pw2"], db2 = mm_tn(
                f"pw2_grad{l}", rec["z"], dh, groups=1, a_block=full_rows, a_index=lambda j, k: (k, 0),
                b_block=full_rows, b_index=lambda j, k: (k, 0), o_block=(D_MODEL, D_MODEL), o_index=lambda j, k: (0, 0),
                o_shape=(D_MODEL, D_MODEL), acc_shape=(D_MODEL, D_MODEL),
                colsum=((1, D_MODEL), pl.BlockSpec((1, D_MODEL), lambda j, k: (0, 0))))
            dt, stats = conv_bwd(rec["t"], rec["y"], dz, rec["cp"], f"conv_bwd{l}")
            grads["pw1"], db1 = pw1_grad(rec["u"], dt, f"pw1_grad{l}")
            cp_grads[l] = (stats, db2, db1)
            tok = emit(f"conv{l}", grads)
            dh, d_mix[l] = pw1_bwd(dt, w["pw1"], rec["h_in"], g_mix, dh, f"pw1_bwd{l}", token=tok)
        else:
            a = l - 2
            do = mm_nt(f"o_bwd{a}", dh, w["wo"], (None, D_MODEL, D_MODEL), lambda i, k: (0, 0, 0), D_MODEL, BF16,
                       token=tok)
            tok = None
            grads["wo"] = mm_tn(
                f"wo_grad{a}", rec["o"], dh, groups=1, a_block=full_rows, a_index=lambda j, k: (k, 0),
                b_block=full_rows, b_index=lambda j, k: (k, 0), o_block=(D_MODEL, D_MODEL), o_index=lambda j, k: (0, 0),
                o_shape=(D_MODEL, D_MODEL), acc_shape=(D_MODEL, D_MODEL))
            dq, dkv_p, dbias_p, dsinks[a] = attn_bwd(rec["q"], kv, do, bias, rep["sinks"][a:a + 1], f"attn_bwd{a}")
            dkv_parts.append(dkv_p)
            dbias_parts.append(dbias_p)
            grads["wq"] = mm_tn(
                f"wq_grad{a}", rec["u"], dq, groups=1, a_block=full_rows, a_index=lambda j, k: (k, 0),
                b_block=full_rows, b_index=lambda j, k: (k, 0), o_block=(D_MODEL, D_MODEL), o_index=lambda j, k: (0, 0),
                o_shape=(D_MODEL, D_MODEL), acc_shape=(D_MODEL, D_MODEL))
            if a == 1:
                tok = emit("attn1", grads)
            dh, d_mix[l] = nt_rms_bwd(f"q_bwd{a}", dq, w["wq"], (None, D_MODEL, D_MODEL), lambda i: (0, 0, 0),
                                      rec["h_in"], g_mix, dh)
        if l == 2:
            dkv = dkv_combine(dkv_parts[0], dkv_parts[1], "dkv_combine")
            grads["wkv"] = mm_tn(
                "wkv_grad", u_kv, dkv, groups=1, a_block=full_rows, a_index=lambda j, k: (k, 0),
                b_block=lambda tk: (tk, 2 * KV_DIM), b_index=lambda j, k: (k, 0), o_block=(D_MODEL, 2 * KV_DIM),
                o_index=lambda j, k: (0, 0), o_shape=(D_MODEL, 2 * KV_DIM), acc_shape=(D_MODEL, 2 * KV_DIM))
            tok = emit("attn0", grads)
            dh, d_nkv = nt_rms_bwd("kv_bwd", dkv, w_kv, (D_MODEL, 2 * KV_DIM), lambda i: (0, 0), h_kv,
                                   rep["norm_kv"], dh)

    d_relb = bias_grad(dbias_parts[0], dbias_parts[1], bucket, "bias_grad")
    d_sinks = jnp.concatenate([dsinks[0][0, :N_HEADS], dsinks[1][0, :N_HEADS]])
    tail = jnp.zeros((D_MODEL,), F32)
    rep_grad = jnp.concatenate([
        jnp.concatenate(d_mix, axis=0), jnp.concatenate(d_ffn, axis=0), d_nkv, d_nfin,
        tail.at[:2 * N_HEADS].set(d_sinks)[None], tail.at[:N_BUCKETS * N_HEADS].set(d_relb.reshape(-1))[None],
        tail.at[0].set(loss[0, 0])[None], jnp.zeros((REP_ROWS - ROW_LOSS - 1, D_MODEL), F32)], axis=0)
    return dh, cp_grads, rep_grad


def _pack_conv(w_dw, b_dw, ln_g, ln_b, b_pw2, b_pw1):
    rows = [w_dw, b_dw[:, None], ln_g[:, None], ln_b[:, None], b_pw2[:, None], b_pw1.reshape(2, 2, 128),
            jnp.zeros((2, PACK_ROWS - ROW_BPW1 - 2, 128), F32)]
    return jnp.concatenate(rows, axis=1)


def _unpack_conv(p):
    return (p[:, :CONV_WIDTH], p[:, ROW_BDW], p[:, ROW_LNG], p[:, ROW_LNB], p[:, ROW_BPW2],
            p[:, ROW_BPW1:ROW_BPW1 + 2].reshape(2, 256))


def _pack_rep(norm_mix, norm_ffn, norm_kv, norm_final, sinks, rel_bias):
    tail = jnp.zeros((D_MODEL,), F32)
    return jnp.concatenate([
        norm_mix, norm_ffn, norm_kv[None], norm_final[None], tail.at[:2 * N_HEADS].set(sinks.reshape(-1))[None],
        tail.at[:N_BUCKETS * N_HEADS].set(rel_bias.reshape(-1))[None],
        jnp.zeros((REP_ROWS - ROW_RELB - 1, D_MODEL), F32)], axis=0)


def _unpack_rep(p):
    return (p[0:4], p[4:8], p[ROW_NKV], p[ROW_NFIN], p[ROW_SINK, :2 * N_HEADS].reshape(2, N_HEADS),
            p[ROW_RELB, :N_BUCKETS * N_HEADS].reshape(N_BUCKETS, N_HEADS))


def kernel(x, norm_mix, norm_ffn, conv_w_pw1, conv_b_pw1, conv_w_dw, conv_b_dw, conv_ln_g, conv_ln_b, conv_w_pw2, conv_b_pw2, norm_kv, w_kv, w_q, w_o, sinks, rel_bias, ffn_w_up, ffn_w_down, norm_final, loss_target, m_norm_mix, m_norm_ffn, m_conv_w_pw1, m_conv_b_pw1, m_conv_w_dw, m_conv_b_dw, m_conv_ln_g, m_conv_ln_b, m_conv_w_pw2, m_conv_b_pw2, m_norm_kv, m_w_kv, m_w_q, m_w_o, m_sinks, m_rel_bias, m_ffn_w_up, m_ffn_w_down, m_norm_final, v_norm_mix, v_norm_ffn, v_conv_w_pw1, v_conv_b_pw1, v_conv_w_dw, v_conv_b_dw, v_conv_ln_g, v_conv_ln_b, v_conv_w_pw2, v_conv_b_pw2, v_norm_kv, v_w_kv, v_w_q, v_w_o, v_sinks, v_rel_bias, v_ffn_w_up, v_ffn_w_down, v_norm_final):
    s = x.shape[1]
    d = D_MODEL
    rsh = d // N_DEV
    dsh = D_FF // N_DEV

    me = _lin(*_place())
    lead_slab = lambda ref, j: ref.at[j]
    rows_of = lambda rows: (lambda ref, j: ref.at[pl.ds(j * rows, rows), :])

    conv_pack = _pack_conv(conv_w_dw, conv_b_dw, conv_ln_g, conv_ln_b, conv_b_pw2, conv_b_pw1)
    ag_order = ["conv0", "ffn0", "conv1", "ffn1", "attn0", "ffn2", "attn1", "ffn3"]
    ag_land, ag_copies, ag_members = [], [], {g: [] for g in ag_order}

    def gather(group, key, shard, land_shape, at, land_slab):
        i = len(ag_land)
        ag_land.append(lax.dynamic_update_slice(lax.empty(land_shape, shard.dtype), shard, at(me)))
        ag_copies.append((ag_order.index(group), i, land_slab))
        ag_members[group].append((key, i, land_slab))

    cols_at0 = lambda ref, j: ref.at[0, j]
    rows_at0 = lambda rows: (lambda ref, j: ref.at[0, pl.ds(j * rows, rows), :])
    col_at = lambda m: (0, m, 0, 0)
    row_at = lambda rows: (lambda m: (0, m * rows, 0))
    for l in range(2):
        gather(f"conv{l}", "pw1", conv_w_pw1[l].astype(BF16)[None, None], (1, N_DEV, d, 256), col_at, cols_at0)
        gather(f"conv{l}", "pw2", conv_w_pw2[l].astype(BF16)[None], (1, d, d), row_at(rsh), rows_at0(rsh))
    gather("conv0", "pack", conv_pack[None], (N_DEV, 2, PACK_ROWS, 128), lambda m: (m, 0, 0, 0), lead_slab)
    gather("attn0", "wkv", w_kv.astype(BF16), (d, 2 * KV_DIM), lambda m: (m * rsh, 0), rows_of(rsh))
    for a in range(2):
        gather(f"attn{a}", "wq", w_q[a].astype(BF16)[None], (1, d, d), row_at(rsh), rows_at0(rsh))
        gather(f"attn{a}", "wo", w_o[a].astype(BF16)[None], (1, d, d), row_at(rsh), rows_at0(rsh))
    up_t, m_up_t, v_up_t = (jnp.swapaxes(a, 1, 2) for a in (ffn_w_up, m_ffn_w_up, v_ffn_w_up))
    for l in range(4):
        gather(f"ffn{l}", "up", up_t[l].astype(BF16)[None, None], (1, N_DEV, FF_CHUNK, d), col_at, cols_at0)
        gather(f"ffn{l}", "down", ffn_w_down[l].astype(BF16)[None], (1, D_FF, d), row_at(dsh), rows_at0(dsh))
    ag_sems, ag_land_thru = gather_start("ag_start", ag_land, ag_copies, len(ag_order))
    relayed = {}

    def weights(group, after, relay=False):
        members = ag_members[group]
        slabs = [slab for _, _, slab in members]
        first_sems = ag_sems[ag_order.index(group)]
        if group not in relayed:
            relayed[group] = gather_relay(f"ag_relay_{group}", [ag_land_thru[i] for _, i, _ in members], slabs,
                                          first_sems, after)
        relay_sems, bufs, token = relayed[group]
        if relay:
            return token
        lands = gather_wait(f"ag_wait_{group}", bufs, slabs, first_sems, relay_sems, after)
        w = {key: land for (key, _, _), land in zip(members, lands)}
        if "up" in w:
            w["up"] = w["up"].reshape(1, 2, 4, FF_CHUNK, d)
            w["down"] = w["down"].reshape(1, 4, FF_CHUNK, d)
        if "pack" in w:
            pack_g = w.pop("pack")
            w["cp"] = jnp.transpose(pack_g, (1, 2, 0, 3)).reshape(2, PACK_ROWS, d)
            w["b_pw1"] = pack_g[:, :, ROW_BPW1:ROW_BPW1 + 2, :].transpose(1, 0, 2, 3).reshape(2, 1, 2 * d)
            w["b_pw2"] = w["cp"][:, ROW_BPW2:ROW_BPW2 + 1, :]
        return w

    shard_shapes = {"pw1": (d, 256), "pw2": (rsh, d), "wkv": (rsh, 2 * KV_DIM), "wq": (rsh, d), "wo": (rsh, d),
                    "up": (FF_CHUNK, d), "down": (dsh, d), "cp": (2, PACK_ROWS, 128), "rep": (REP_ROWS, d)}
    n_layers = {"pw1": 2, "pw2": 2, "wkv": 1, "wq": 2, "wo": 2, "up": 4, "down": 4, "cp": 1, "rep": 1}
    by_lead = (lambda ref, j, me_: ref.at[j], lambda g: lax.dynamic_index_in_dim(g, me, 0, keepdims=False))
    by_rows = lambda rows: (lambda ref, j, me_: ref.at[pl.ds(j * rows, rows), :],
                            lambda g: lax.dynamic_slice_in_dim(g, me * rows, rows, 0))
    all_of = (lambda ref, j, me_: ref, lambda g: g)
    owned = {"pw1": by_lead, "pw2": by_rows(rsh), "wkv": by_rows(rsh), "wq": by_rows(rsh), "wo": by_rows(rsh),
             "up": by_lead, "down": by_rows(dsh), "cp": by_lead, "rep": all_of}
    parts = {}
    pending = {}

    def finish(chain, after):
        keys, bufs, copies, sems, name = pending.pop(chain)
        done = send_wait(f"rs_wait_{name}", bufs, copies, sems, after)
        parts.update(zip(keys, done[len(keys):]))

    def exchange(chain, name, layer, grads):
        keys = list(grads)
        if chain in pending:
            finish(chain, grads[keys[0]])
        lands = []
        for k in keys:
            land = parts.pop(k) if k in parts else lax.empty((N_DEV, n_layers[k]) + shard_shapes[k], grads[k].dtype)
            mine = owned[k][1](grads[k])[None, None]
            lands.append(lax.dynamic_update_slice(land, mine, (me, layer) + (0,) * len(shard_shapes[k])))
        land_at = lambda ref, i: ref.at[i, layer]
        copies = [(0, n, owned[k][0], len(keys) + n, land_at) for n, k in enumerate(keys)]
        sems, thru, token = send_start(f"rs_start_{name}", [grads[k] for k in keys] + lands, copies, 1)
        pending[chain] = (keys, thru, [c[1:] for c in copies], sems[0], name)
        return token

    def emit(group, grads):
        return exchange(group[:-1], group, int(group[-1]), grads)

    rep = {"norm_mix": norm_mix, "norm_ffn": norm_ffn, "norm_kv": norm_kv[None], "norm_final": norm_final[None],
           "sinks": sinks, "rel_bias": rel_bias}

    grad_x, cp_grads, rep_grad = local_step(x[0], loss_target[0], weights, rep, emit)

    cp_full = []
    for l in range(2):
        stats, db2, db1 = cp_grads[l]
        cp_full.append(jnp.concatenate([
            stats[:ROW_BPW2], db2, db1.reshape(N_DEV, 2, 128).transpose(1, 0, 2).reshape(2, d),
            jnp.zeros((PACK_ROWS - ROW_BPW1 - 2, d), F32)], axis=0))
    cp_send = jnp.stack(cp_full).reshape(2, PACK_ROWS, N_DEV, 128).transpose(2, 0, 1, 3)
    tail_token = exchange("tail", "tail", 0, {"cp": cp_send, "rep": rep_grad})

    def update(key, w, m, v, name, token=None):
        p = parts[key]
        w3 = w.reshape(p.shape[1:])
        outs = adamw(w3, m.reshape(w3.shape), v.reshape(w3.shape), p, name, token=token)
        return [o.reshape(w.shape) for o in outs]

    res = {}
    finish("ffn", grad_x)
    up_res = update("up", up_t, m_up_t, v_up_t, "adam_up", tail_token)
    res["ffn_w_up"] = [jnp.swapaxes(o, 1, 2) for o in up_res]
    res["ffn_w_down"] = update("down", ffn_w_down, m_ffn_w_down, v_ffn_w_down, "adam_down", up_res[0])
    finish("attn", res["ffn_w_down"][0])
    res["w_kv"] = update("wkv", w_kv, m_w_kv, v_w_kv, "adam_wkv")
    res["w_q"] = update("wq", w_q, m_w_q, v_w_q, "adam_wq")
    res["w_o"] = update("wo", w_o, m_w_o, v_w_o, "adam_wo")
    finish("conv", res["w_o"][0])
    res["conv_w_pw1"] = update("pw1", conv_w_pw1, m_conv_w_pw1, v_conv_w_pw1, "adam_pw1")
    res["conv_w_pw2"] = update("pw2", conv_w_pw2, m_conv_w_pw2, v_conv_w_pw2, "adam_pw2")
    finish("tail", res["conv_w_pw2"][0])
    m_pack = _pack_conv(m_conv_w_dw, m_conv_b_dw, m_conv_ln_g, m_conv_ln_b, m_conv_b_pw2, m_conv_b_pw1)
    v_pack = _pack_conv(v_conv_w_dw, v_conv_b_dw, v_conv_ln_g, v_conv_ln_b, v_conv_b_pw2, v_conv_b_pw1)
    cp_res = adamw(conv_pack, m_pack, v_pack, parts["cp"].reshape(N_DEV, 2, PACK_ROWS, 128), "adam_conv_pack")
    rep_w = _pack_rep(norm_mix, norm_ffn, norm_kv, norm_final, sinks, rel_bias)
    rep_m = _pack_rep(m_norm_mix, m_norm_ffn, m_norm_kv, m_norm_final, m_sinks, m_rel_bias)
    rep_v = _pack_rep(v_norm_mix, v_norm_ffn, v_norm_kv, v_norm_final, v_sinks, v_rel_bias)
    rep_res = adamw(rep_w[None], rep_m[None], rep_v[None], parts["rep"], "adam_rep")
    loss = rep_res[0][0, ROW_LOSS, 0]

    outs = []
    for kind in range(4):
        cw_dw, cb_dw, cln_g, cln_b, cb_pw2, cb_pw1 = _unpack_conv(cp_res[kind])
        r_mix, r_ffn, r_nkv, r_nfin, r_sinks, r_relb = _unpack_rep(rep_res[kind][0])
        outs += [r_mix, r_ffn, res["conv_w_pw1"][kind], cb_pw1, cw_dw, cb_dw, cln_g, cln_b, res["conv_w_pw2"][kind],
                 cb_pw2, r_nkv, res["w_kv"][kind], res["w_q"][kind], res["w_o"][kind], r_sinks, r_relb,
                 res["ffn_w_up"][kind], res["ffn_w_down"][kind], r_nfin]
    return (loss, grad_x[None], *outs)
```

```python
import functools
import math

import numpy as np
import jax
import jax.numpy as jnp
from jax import lax
from jax.experimental import pallas as pl
from jax.experimental.pallas import tpu as pltpu

F32 = jnp.float32
BF16 = jnp.bfloat16

D_MODEL = 1024
D_FF = 2816
N_HEADS = 16
N_KV_HEADS = 4
GROUP = N_HEADS // N_KV_HEADS
HEAD_DIM = 64
KV_DIM = N_KV_HEADS * HEAD_DIM
BLOCK = 128
CONV_WIDTH = 31
HALO = 32
N_BUCKETS = 32
MAX_DISTANCE = 128
EPS = 1e-6
NEG_INF = -1e30
N_DEV = 8
FF_CHUNK = D_FF // 4
PACK_ROWS = 40
ROW_BDW, ROW_LNG, ROW_LNB, ROW_BPW2, ROW_BPW1 = 31, 32, 33, 34, 35
REP_ROWS = 16
ROW_NKV, ROW_NFIN, ROW_SINK, ROW_RELB, ROW_LOSS = 8, 9, 10, 11, 12

ADAM_LR, ADAM_B1, ADAM_B2, ADAM_EPS, ADAM_WD, ADAM_STEP = 0.001, 0.9, 0.999, 1e-08, 0.01, 10

VMEM_LIMIT_BYTES = 56 * 1024 * 1024
FFN_ROWS = 1024
FFN_BWD_ROWS = 512
GRAD_ROWS = 1024
ANY = pl.BlockSpec(memory_space=pl.ANY)
MESH = pl.DeviceIdType.MESH

NN = (((1,), (0,)), ((), ()))
NT = (((1,), (1,)), ((), ()))
TN = (((0,), (0,)), ((), ()))


def _dot(a, b, dims):
    return lax.dot_general(a, b, dims, preferred_element_type=F32)


def _pcall(body, name, out_shape, *, grid=None, in_specs=None, out_specs=None, scratch=(), sem=None, **kw):
    params = pltpu.CompilerParams(dimension_semantics=sem, vmem_limit_bytes=VMEM_LIMIT_BYTES)
    extra = {} if grid is None else {"grid": grid}
    return pl.pallas_call(body, name=name, out_shape=out_shape, in_specs=in_specs, out_specs=out_specs,
                          scratch_shapes=list(scratch), compiler_params=params, **extra, **kw)


def _sds(shape, dtype):
    return jax.ShapeDtypeStruct(tuple(shape), dtype)


def _row_tile(s, want):
    return want if s % want == 0 else s


def rms_fwd(h, g, name):
    s, d = h.shape
    tm = _row_tile(s, 512)

    def body(h_ref, g_ref, u_ref):
        x = h_ref[...]
        r = lax.rsqrt(jnp.mean(x * x, axis=-1, keepdims=True) + EPS)
        u_ref[...] = (x * r * g_ref[...]).astype(BF16)

    return _pcall(body, name, _sds((s, d), BF16), grid=(s // tm,),
                  in_specs=[pl.BlockSpec((tm, d), lambda i: (i, 0)), pl.BlockSpec((1, d), lambda i: (0, 0))],
                  out_specs=pl.BlockSpec((tm, d), lambda i: (i, 0)), sem=("parallel",))(h, g)


def _rms_rows(x, gain):
    return (x * lax.rsqrt(jnp.mean(x * x, axis=-1, keepdims=True) + EPS) * gain).astype(BF16)


def _mm(name, a, b, *, dims, grid, a_spec, b_spec, o_spec, o_shape, nk=1, acc_shape=None,
        bias=None, res=None, colsum=None, sem=None, token=None, norm=None):
    n_axes = len(grid)

    def body(*refs):
        it = iter(refs)
        a_ref, b_ref = next(it), next(it)
        bias_ref = next(it) if bias is not None else None
        res_ref = next(it) if res is not None else None
        gain_ref = next(it) if norm is not None else None
        if token is not None:
            next(it)
        o_ref = next(it)
        un_ref = next(it) if norm is not None else None
        cs_ref = next(it) if colsum is not None else None
        acc_ref = next(it) if nk > 1 else None
        k = pl.program_id(n_axes - 1)
        p = _dot(a_ref[...].astype(BF16), b_ref[...].astype(BF16), dims)

        def finish(acc):
            if bias_ref is not None:
                acc = acc + bias_ref[...]
            if res_ref is not None:
                acc = acc + res_ref[...]
            o_ref[...] = acc.astype(o_ref.dtype)
            if un_ref is not None:
                un_ref[...] = _rms_rows(acc, gain_ref[...])

        if nk == 1:
            finish(p)
        else:
            @pl.when(k == 0)
            def _():
                acc_ref[...] = p

            @pl.when(k > 0)
            def _():
                acc_ref[...] += p

            @pl.when(k == nk - 1)
            def _():
                finish(acc_ref[...])

        if cs_ref is not None:
            cs = jnp.sum(b_ref[...].astype(F32), axis=0, keepdims=True)

            @pl.when(k == 0)
            def _():
                cs_ref[...] = cs

            @pl.when(k > 0)
            def _():
                cs_ref[...] += cs

    ins, in_specs = [a, b], [a_spec, b_spec]
    gain = None if norm is None else (norm, pl.BlockSpec(norm.shape, lambda *_: (0,) * norm.ndim))
    for extra in (bias, res, gain, None if token is None else (token, ANY)):
        if extra is not None:
            ins.append(extra[0])
            in_specs.append(extra[1])
    out_shape, out_specs = o_shape, o_spec
    if norm is not None:
        out_shape, out_specs = (o_shape, _sds(o_shape.shape, BF16)), (o_spec, o_spec)
    if colsum is not None:
        out_shape, out_specs = (o_shape, _sds(colsum[0], F32)), (o_spec, colsum[1])
    scratch = [pltpu.VMEM(acc_shape, F32)] if nk > 1 else []
    if sem is None:
        sem = ("parallel",) * (n_axes - 1) + ("arbitrary",)
    return _pcall(body, name, out_shape, grid=grid, in_specs=in_specs, out_specs=out_specs, scratch=scratch,
                  sem=sem)(*ins)


def mm_nn(name, a, w, w_block, w_index, n, tn, out_dtype, bias=None, res=None, tm=1024, norm=None, token=None):
    s, k = a.shape
    tm = _row_tile(s, tm)
    col = lambda i, j: (i, j)
    extras = {"token": token}
    if bias is not None:
        extras["bias"] = bias
    if res is not None:
        extras["res"] = (res, pl.BlockSpec((tm, tn), col))
    if norm is not None:
        assert tn == n, "a fused norm needs whole rows"
        extras["norm"] = norm
    return _mm(name, a, w, dims=NN, grid=(s // tm, n // tn), a_spec=pl.BlockSpec((tm, k), lambda i, j: (i, 0)),
               b_spec=pl.BlockSpec(w_block, w_index), o_spec=pl.BlockSpec((tm, tn), col), o_shape=_sds((s, n), out_dtype),
               sem=("parallel", "arbitrary"), **extras)


def mm_nt(name, a, w, w_block, w_index, kout, out_dtype, nk=1, tk=None, tm=1024, token=None):
    s, n = a.shape
    tm = _row_tile(s, tm)
    tk = n if tk is None else tk
    return _mm(name, a, w, dims=NT, grid=(s // tm, nk), a_spec=pl.BlockSpec((tm, tk), lambda i, k: (i, k)),
               b_spec=pl.BlockSpec(w_block, w_index), o_spec=pl.BlockSpec((tm, kout), lambda i, k: (i, 0)),
               o_shape=_sds((s, kout), out_dtype), nk=nk, acc_shape=(tm, kout), token=token)


def mm_tn(name, a, b, *, groups, a_block, a_index, b_block, b_index, o_block, o_index, o_shape, acc_shape,
          colsum=None, tk=GRAD_ROWS):
    s = a.shape[-2]
    tk = _row_tile(s, tk)
    return _mm(name, a, b, dims=TN, grid=(groups, s // tk), a_spec=pl.BlockSpec(a_block(tk), a_index),
               b_spec=pl.BlockSpec(b_block(tk), b_index), o_spec=pl.BlockSpec(o_block, o_index),
               o_shape=_sds(o_shape, BF16), nk=s // tk, acc_shape=acc_shape, colsum=colsum)


FFN_SUB = 256


def _sub_rows(tm):
    sub = FFN_SUB if tm % FFN_SUB == 0 else tm
    return [slice(r * sub, (r + 1) * sub) for r in range(tm // sub)]


def ffn_fwd(u, h, w_up_t, w_down, layer, name, norms=(), token=None):
    s, d = u.shape
    tm = _row_tile(s, FFN_ROWS)
    nj = 4
    nn = len(norms)

    extra = [] if token is None else [token]
    nx = len(extra)

    def body(u_ref, h_ref, wup_ref, wd_ref, *rest):
        gain_refs, (hn_ref, gu_ref), un_refs = rest[:nn], rest[nn + nx:nn + nx + 2], rest[nn + nx + 2:]
        j = pl.program_id(1)

        @pl.when(j == 0)
        def _():
            hn_ref[...] = h_ref[...]

        for rows in _sub_rows(tm):
            uv = u_ref[rows, :]
            g = _dot(uv, wup_ref[0], NT)
            p = _dot(uv, wup_ref[1], NT)
            gu_ref[0, rows, :] = g.astype(BF16)
            gu_ref[1, rows, :] = p.astype(BF16)
            act = (g * jax.nn.sigmoid(g) * p).astype(BF16)
            hn_ref[rows, :] += _dot(act, wd_ref[...], NN)

        if nn:
            @pl.when(j == nj - 1)
            def _():
                for rows in _sub_rows(tm):
                    for gain_ref, un_ref in zip(gain_refs, un_refs):
                        un_ref[rows, :] = _rms_rows(hn_ref[rows, :], gain_ref[...])

    row = pl.BlockSpec((tm, d), lambda i, j: (i, 0))
    vec = pl.BlockSpec((1, d), lambda i, j: (0, 0))
    return _pcall(
        body, name, (_sds((s, d), F32), _sds((2, nj, s, FF_CHUNK), BF16)) + (_sds((s, d), BF16),) * nn,
        grid=(s // tm, nj),
        in_specs=[row, row,
                  pl.BlockSpec((None, 2, None, FF_CHUNK, d), lambda i, j: (layer, 0, j, 0, 0)),
                  pl.BlockSpec((None, None, FF_CHUNK, d), lambda i, j: (layer, j, 0, 0))] + [vec] * nn + [ANY] * nx,
        out_specs=(row, pl.BlockSpec((2, None, tm, FF_CHUNK), lambda i, j: (0, j, i, 0))) + (row,) * nn,
        sem=("parallel", "arbitrary"))(u, h, w_up_t, w_down, *norms, *extra)


def _rms_bwd_rows(x, gain, du, dh_in):
    r = lax.rsqrt(jnp.mean(x * x, axis=-1, keepdims=True) + EPS)
    xh = x * r
    dxh = du * gain
    dx = r * (dxh - xh * jnp.mean(dxh * xh, axis=-1, keepdims=True))
    return dh_in + dx, jnp.sum(du * xh, axis=0, keepdims=True)


def ffn_bwd(dh, gu, w_up_t, w_down, layer, h_mid, gain, name, token=None):
    s, d = dh.shape
    tm = _row_tile(s, FFN_BWD_ROWS)
    nj = 4

    def body(dh_ref, gu_ref, wup_ref, wd_ref, h_ref, gain_ref, *rest):
        dho_ref, dgain_ref, act_ref, dgu_ref, du_ref = rest[-5:]
        i, j = pl.program_id(0), pl.program_id(1)

        @pl.when(j == 0)
        def _():
            du_ref[...] = jnp.zeros_like(du_ref)

        @pl.when((i == 0) & (j == 0))
        def _():
            dgain_ref[...] = jnp.zeros_like(dgain_ref)

        for rows in _sub_rows(tm):
            dact = _dot(dh_ref[rows, :].astype(BF16), wd_ref[...], NT)
            g = gu_ref[0, rows, :].astype(F32)
            p = gu_ref[1, rows, :].astype(F32)
            sig = jax.nn.sigmoid(g)
            sl = g * sig
            act_ref[rows, :] = (sl * p).astype(BF16)
            dp = (dact * sl).astype(BF16)
            dg = (dact * p * (sig * (1.0 + g * (1.0 - sig)))).astype(BF16)
            dgu_ref[0, rows, :] = dg
            dgu_ref[1, rows, :] = dp
            du_ref[rows, :] += _dot(dg, wup_ref[0], NN) + _dot(dp, wup_ref[1], NN)

        @pl.when(j == nj - 1)
        def _():
            for rows in _sub_rows(tm):
                dho, part = _rms_bwd_rows(h_ref[rows, :], gain_ref[...], du_ref[rows, :], dh_ref[rows, :])
                dho_ref[rows, :] = dho
                dgain_ref[...] += part

    row = pl.BlockSpec((tm, d), lambda i, j: (i, 0))
    vec = pl.BlockSpec((1, d), lambda i, j: (0, 0))
    gu_spec = pl.BlockSpec((2, None, tm, FF_CHUNK), lambda i, j: (0, j, i, 0))
    extra = [] if token is None else [token]
    return _pcall(
        body, name,
        (_sds((s, d), F32), _sds((1, d), F32), _sds((nj, s, FF_CHUNK), BF16), _sds((2, nj, s, FF_CHUNK), BF16)),
        grid=(s // tm, nj),
        in_specs=[row, gu_spec,
                  pl.BlockSpec((None, 2, None, FF_CHUNK, d), lambda i, j: (layer, 0, j, 0, 0)),
                  pl.BlockSpec((None, None, FF_CHUNK, d), lambda i, j: (layer, j, 0, 0)), row, vec]
        + [ANY] * len(extra),
        out_specs=(row, vec, pl.BlockSpec((None, tm, FF_CHUNK), lambda i, j: (j, i, 0)), gu_spec),
        scratch=[pltpu.VMEM((tm, d), F32)],
        sem=("arbitrary", "arbitrary"))(dh, gu, w_up_t, w_down, h_mid, gain, *extra)


def nt_rms_bwd(name, a, w, w_block, w_index, h, gain, dh_in):
    s, n = a.shape
    d = h.shape[1]
    tm = _row_tile(s, 1024)

    def body(a_ref, w_ref, h_ref, gain_ref, dhi_ref, dho_ref, dgain_ref):
        i = pl.program_id(0)

        @pl.when(i == 0)
        def _():
            dgain_ref[...] = jnp.zeros_like(dgain_ref)

        du = _dot(a_ref[...].astype(BF16), w_ref[...], NT)
        dho, part = _rms_bwd_rows(h_ref[...], gain_ref[...], du, dhi_ref[...])
        dho_ref[...] = dho
        dgain_ref[...] += part

    row = pl.BlockSpec((tm, d), lambda i: (i, 0))
    vec = pl.BlockSpec((1, d), lambda i: (0, 0))
    return _pcall(body, name, (_sds((s, d), F32), _sds((1, d), F32)), grid=(s // tm,),
                  in_specs=[pl.BlockSpec((tm, n), lambda i: (i, 0)), pl.BlockSpec(w_block, w_index), row, vec, row],
                  out_specs=(row, vec), sem=("arbitrary",))(a, w, h, gain, dh_in)


def pw1_fwd(u, w, b, layer, name):
    s, d = u.shape
    tm = _row_tile(s, FFN_ROWS)
    nb, wb = w.shape[1], w.shape[3]

    def body(u_ref, w_ref, b_ref, t_ref):
        for rows in _sub_rows(tm):
            uv = u_ref[rows, :]
            for j in range(nb):
                cols = slice(j * wb, (j + 1) * wb)
                t_ref[rows, cols] = (_dot(uv, w_ref[j], NN) + b_ref[:, cols]).astype(BF16)

    return _pcall(
        body, name, _sds((s, nb * wb), BF16), grid=(s // tm,),
        in_specs=[pl.BlockSpec((tm, d), lambda i: (i, 0)), pl.BlockSpec((None, nb, d, wb), lambda i: (0, 0, 0, 0)),
                  pl.BlockSpec((None, 1, nb * wb), lambda i: (layer, 0, 0))],
        out_specs=pl.BlockSpec((tm, nb * wb), lambda i: (i, 0)), sem=("parallel",))(u, w, b)


def pw1_bwd(dt, w, h, gain, dh_in, name, token=None):
    s = dt.shape[0]
    nb, d, wb = w.shape[1], w.shape[2], w.shape[3]
    tm = _row_tile(s, 512)

    def body(dt_ref, w_ref, h_ref, gain_ref, dhi_ref, *rest):
        dho_ref, dgain_ref = rest[-2:]
        i = pl.program_id(0)

        @pl.when(i == 0)
        def _():
            dgain_ref[...] = jnp.zeros_like(dgain_ref)

        for rows in _sub_rows(tm):
            du = _dot(dt_ref[rows, 0:wb], w_ref[0], NT)
            for j in range(1, nb):
                du = du + _dot(dt_ref[rows, j * wb:(j + 1) * wb], w_ref[j], NT)
            dho, part = _rms_bwd_rows(h_ref[rows, :], gain_ref[...], du, dhi_ref[rows, :])
            dho_ref[rows, :] = dho
            dgain_ref[...] += part

    row = pl.BlockSpec((tm, d), lambda i: (i, 0))
    vec = pl.BlockSpec((1, d), lambda i: (0, 0))
    extra = [] if token is None else [token]
    return _pcall(
        body, name, (_sds((s, d), F32), _sds((1, d), F32)), grid=(s // tm,),
        in_specs=[pl.BlockSpec((tm, nb * wb), lambda i: (i, 0)),
                  pl.BlockSpec((None, nb, d, wb), lambda i: (0, 0, 0, 0)), row, vec, row] + [ANY] * len(extra),
        out_specs=(row, vec), sem=("arbitrary",))(dt, w, h, gain, dh_in, *extra)


def pw1_grad(u, dt, name):
    s, d = u.shape
    n = dt.shape[1]
    nb = N_DEV
    wb = n // nb
    tk = _row_tile(s, 1024)
    nk = s // tk

    def body(u_ref, dt_ref, g_ref, db_ref, acc_ref):
        k = pl.program_id(0)
        p = _dot(u_ref[...], dt_ref[...], TN)
        cs = jnp.sum(dt_ref[...].astype(F32), axis=0, keepdims=True)

        @pl.when(k == 0)
        def _():
            acc_ref[...] = p
            db_ref[...] = cs

        @pl.when(k > 0)
        def _():
            acc_ref[...] += p
            db_ref[...] += cs

        @pl.when(k == nk - 1)
        def _():
            for j in range(nb):
                g_ref[j] = acc_ref[:, j * wb:(j + 1) * wb].astype(BF16)

    return _pcall(
        body, name, (_sds((nb, d, wb), BF16), _sds((1, n), F32)), grid=(nk,),
        in_specs=[pl.BlockSpec((tk, d), lambda k: (k, 0)), pl.BlockSpec((tk, n), lambda k: (k, 0))],
        out_specs=(pl.BlockSpec((nb, d, wb), lambda k: (0, 0, 0)), pl.BlockSpec((1, n), lambda k: (0, 0))),
        scratch=[pltpu.VMEM((d, n), F32)], sem=("arbitrary",))(u, dt)


def _glu(t):
    t = t.astype(F32)
    return t[:, :D_MODEL] * jax.nn.sigmoid(t[:, D_MODEL:])


CONV_TILE = 256


def _conv_tile(s):
    return CONV_TILE if s % CONV_TILE == 0 else s


CONV_ROWS = 32
CONV_LANES = 512
SUBLANES = 8


def _shifted_copies(sh_ref, rows):
    for b in range(1, SUBLANES):
        sh_ref[b, 0:rows - SUBLANES, :] = sh_ref[0, b:b + rows - SUBLANES, :]


def conv_fwd(t, cp, name):
    s = t.shape[0]
    d = D_MODEL
    ts = _conv_tile(s)
    per = ts // HALO
    rows = HALO + ts
    lead = HALO - (CONV_WIDTH - 1)
    rc = CONV_ROWS

    def body(t_ref, tp_ref, cp_ref, z_ref, y_ref, sh_ref):
        i = pl.program_id(0)
        sh_ref[0, 0:HALO, :] = jnp.where(i > 0, _glu(tp_ref[...]), 0.0)
        sh_ref[0, HALO:rows, :] = _glu(t_ref[...])
        _shifted_copies(sh_ref, rows)

        def chunk(c, carry):
            r0 = pl.multiple_of(c * rc, rc)
            for lc in range(d // CONV_LANES):
                ln = slice(lc * CONV_LANES, (lc + 1) * CONV_LANES)
                acc = jnp.zeros((rc, CONV_LANES), F32) + cp_ref[ROW_BDW:ROW_BDW + 1, ln]
                for k in range(CONV_WIDTH):
                    a8, b = divmod(lead + k, SUBLANES)
                    acc = acc + cp_ref[k:k + 1, ln] * sh_ref[b, pl.ds(r0 + SUBLANES * a8, rc), ln]
                y_ref[pl.ds(r0, rc), ln] = acc
            y = y_ref[pl.ds(r0, rc), :]
            mu = jnp.mean(y, axis=-1, keepdims=True)
            yc = y - mu
            rstd = lax.rsqrt(jnp.mean(yc * yc, axis=-1, keepdims=True) + EPS)
            yn = yc * rstd * cp_ref[ROW_LNG:ROW_LNG + 1, :] + cp_ref[ROW_LNB:ROW_LNB + 1, :]
            z_ref[pl.ds(r0, rc), :] = (yn * jax.nn.sigmoid(yn)).astype(BF16)
            return carry

        lax.fori_loop(0, ts // rc, chunk, 0)

    row = pl.BlockSpec((ts, d), lambda i: (i, 0))
    return _pcall(
        body, name, (_sds((s, d), BF16), _sds((s, d), F32)), grid=(s // ts,),
        in_specs=[pl.BlockSpec((ts, 2 * d), lambda i: (i, 0)),
                  pl.BlockSpec((HALO, 2 * d), lambda i: (jnp.maximum(i * per - 1, 0), 0)),
                  pl.BlockSpec((PACK_ROWS, d), lambda i: (0, 0))],
        out_specs=(row, row),
        scratch=[pltpu.VMEM((SUBLANES, rows, d), F32)], sem=("parallel",))(t, t, cp)


def conv_bwd(t, y, dz, cp, name):
    s = t.shape[0]
    d = D_MODEL
    ts = _conv_tile(s)
    per = ts // HALO
    nt = s // ts
    te = ts + HALO
    rc = CONV_ROWS

    def body(t_ref, y_ref, yn_ref, dz_ref, dzn_ref, cp_ref, dt_ref, st_ref, shd_ref, dw_ref):
        i = pl.program_id(0)
        last = i == nt - 1

        @pl.when(i == 0)
        def _():
            st_ref[...] = jnp.zeros_like(st_ref)
            dw_ref[...] = jnp.zeros_like(dw_ref)

        gain = cp_ref[ROW_LNG:ROW_LNG + 1, :]

        def ln_bwd(yv, dzv):
            mu = jnp.mean(yv, axis=-1, keepdims=True)
            yc = yv - mu
            rstd = lax.rsqrt(jnp.mean(yc * yc, axis=-1, keepdims=True) + EPS)
            yh = yc * rstd
            yn = yh * gain + cp_ref[ROW_LNB:ROW_LNB + 1, :]
            sig = jax.nn.sigmoid(yn)
            dyn = dzv * (sig * (1.0 + yn * (1.0 - sig)))
            dyh = dyn * gain
            dy = rstd * (dyh - jnp.mean(dyh, axis=-1, keepdims=True)
                         - yh * jnp.mean(dyh * yh, axis=-1, keepdims=True))
            return dy, dyn, yh

        def norm_chunk(c, carry):
            r0 = pl.multiple_of(c * rc, rc)
            dy, dyn, yh = ln_bwd(y_ref[pl.ds(r0, rc), :], dz_ref[pl.ds(r0, rc), :])
            shd_ref[0, pl.ds(r0, rc), :] = dy
            st_ref[ROW_BDW:ROW_BDW + 1, :] += jnp.sum(dy, axis=0, keepdims=True)
            st_ref[ROW_LNG:ROW_LNG + 1, :] += jnp.sum(dyn * yh, axis=0, keepdims=True)
            st_ref[ROW_LNB:ROW_LNB + 1, :] += jnp.sum(dyn, axis=0, keepdims=True)
            return carry

        lax.fori_loop(0, ts // rc, norm_chunk, 0)
        dy_halo, _, _ = ln_bwd(yn_ref[...], jnp.where(last, 0.0, dzn_ref[...]))
        shd_ref[0, ts:te, :] = dy_halo
        _shifted_copies(shd_ref, te)

        def tap_chunk(c, carry):
            r0 = pl.multiple_of(c * rc, rc)
            for lc in range(d // CONV_LANES):
                ln = slice(lc * CONV_LANES, (lc + 1) * CONV_LANES)
                ln2 = slice(d + lc * CONV_LANES, d + (lc + 1) * CONV_LANES)
                t1 = t_ref[pl.ds(r0, rc), ln].astype(F32)
                sg = jax.nn.sigmoid(t_ref[pl.ds(r0, rc), ln2].astype(F32))
                a = t1 * sg
                da = jnp.zeros((rc, CONV_LANES), F32)
                for k in range(CONV_WIDTH):
                    a8, b = divmod(CONV_WIDTH - 1 - k, SUBLANES)
                    e = shd_ref[b, pl.ds(r0 + SUBLANES * a8, rc), ln]
                    da = da + cp_ref[k:k + 1, ln] * e
                    dw_ref[k, :, ln] += jnp.sum((a * e).reshape(rc // SUBLANES, SUBLANES, CONV_LANES), axis=0)
                dt_ref[pl.ds(r0, rc), ln] = (da * sg).astype(BF16)
                dt_ref[pl.ds(r0, rc), ln2] = (da * t1 * sg * (1.0 - sg)).astype(BF16)
            return carry

        lax.fori_loop(0, ts // rc, tap_chunk, 0)

        @pl.when(last)
        def _():
            for k in range(CONV_WIDTH):
                st_ref[k:k + 1, :] = jnp.sum(dw_ref[k], axis=0, keepdims=True)

    last_halo = s // HALO - 1
    row = pl.BlockSpec((ts, d), lambda i: (i, 0))
    halo = pl.BlockSpec((HALO, d), lambda i: (jnp.minimum((i + 1) * per, last_halo), 0))
    return _pcall(
        body, name, (_sds((s, 2 * d), BF16), _sds((PACK_ROWS, d), F32)), grid=(nt,),
        in_specs=[pl.BlockSpec((ts, 2 * d), lambda i: (i, 0)), row, halo, row, halo,
                  pl.BlockSpec((PACK_ROWS, d), lambda i: (0, 0))],
        out_specs=(pl.BlockSpec((ts, 2 * d), lambda i: (i, 0)), pl.BlockSpec((PACK_ROWS, d), lambda i: (0, 0))),
        scratch=[pltpu.VMEM((SUBLANES, te, d), F32), pltpu.VMEM((CONV_WIDTH, SUBLANES, d), F32)],
        sem=("arbitrary",))(t, y, y, dz, dz, cp)


def _bucket_table():
    qi = np.arange(BLOCK, dtype=np.int64)[:, None]
    kj = np.arange(2 * BLOCK, dtype=np.int64)[None, :]
    dist = qi + BLOCK - kj
    max_exact = N_BUCKETS // 2
    dd = np.maximum(dist, 0)
    ratio = (np.maximum(dd, 1).astype(np.float32) / np.float32(max_exact)).astype(np.float32)
    log_ratio = (np.log(ratio).astype(np.float32) / np.float32(math.log(MAX_DISTANCE / max_exact))).astype(np.float32)
    large = max_exact + (log_ratio * np.float32(N_BUCKETS - max_exact)).astype(np.int32)
    large = np.minimum(large, N_BUCKETS - 1)
    bucket = np.where(dd < max_exact, dd, large)
    return np.where((dist >= 0) & (dist < BLOCK), bucket, -1).astype(np.int32)


def bias_table(rel_bias, bucket, name):
    def body(rb_ref, bk_ref, o_ref):
        bk = bk_ref[...]
        for h in range(N_HEADS):
            acc = jnp.full((BLOCK, 2 * BLOCK), NEG_INF, F32)
            for b in range(N_BUCKETS):
                acc = jnp.where(bk == b, rb_ref[b, h], acc)
            o_ref[h] = acc

    return _pcall(body, name, _sds((N_HEADS, BLOCK, 2 * BLOCK), F32),
                  in_specs=[pl.BlockSpec(memory_space=pltpu.SMEM), pl.BlockSpec(memory_space=pltpu.VMEM)],
                  out_specs=pl.BlockSpec(memory_space=pltpu.VMEM))(rel_bias, bucket)


def bias_grad(dba, dbb, bucket, name):
    def body(a_ref, b_ref, bk_ref, o_ref):
        bk = bk_ref[...]
        for h in range(N_HEADS):
            db = a_ref[h] + b_ref[h]
            for b in range(N_BUCKETS):
                o_ref[b, h] = jnp.sum(jnp.where(bk == b, db, 0.0))

    vm = pl.BlockSpec(memory_space=pltpu.VMEM)
    return _pcall(body, name, _sds((N_BUCKETS, N_HEADS), F32), in_specs=[vm, vm, vm],
                  out_specs=pl.BlockSpec(memory_space=pltpu.SMEM))(dba, dbb, bucket)


def _band_specs():
    cur = pl.BlockSpec((BLOCK, 2 * KV_DIM), lambda n: (n, 0))
    prev = pl.BlockSpec((BLOCK, 2 * KV_DIM), lambda n: (jnp.maximum(n - 1, 0), 0))
    return cur, prev


def _scores(q_h, k_h, bias_h, first_row, sink):
    sc = _dot(q_h, k_h, NT) * (HEAD_DIM ** -0.5) + bias_h + first_row
    m = jnp.maximum(jnp.max(sc, axis=-1, keepdims=True), sink)
    p = jnp.exp(sc - m)
    e_sink = jnp.exp(sink - m)
    den = jnp.sum(p, axis=-1, keepdims=True) + e_sink
    return p, e_sink, den


def _first_block_row(n):
    col = lax.broadcasted_iota(jnp.int32, (1, 2 * BLOCK), 1)
    return jnp.where((col < BLOCK) & (n == 0), NEG_INF, 0.0)


def _head_lanes(hk, g):
    h = hk * GROUP + g
    return slice(h * HEAD_DIM, (h + 1) * HEAD_DIM)


def _group_rows(x_ref, hk):
    return jnp.concatenate([x_ref[:, _head_lanes(hk, g)] for g in range(GROUP)], axis=0)


def _group_bias(bias_ref, hk):
    return bias_ref[hk * GROUP:(hk + 1) * GROUP].reshape(GROUP * BLOCK, 2 * BLOCK)


def _group_sinks(sink_ref, hk):
    head = lax.broadcasted_iota(jnp.int32, (GROUP * BLOCK, 1), 0) // BLOCK
    col = jnp.zeros((GROUP * BLOCK, 1), F32) + sink_ref[0, hk * GROUP]
    for g in range(1, GROUP):
        col = jnp.where(head == g, sink_ref[0, hk * GROUP + g], col)
    return col


def attn_fwd(q, kv, bias, sinks, name):
    s = q.shape[0]
    nb = s // BLOCK

    def body(sink_ref, q_ref, kvc_ref, kvp_ref, bias_ref, o_ref, band_ref):
        n = pl.program_id(0)
        band_ref[0:BLOCK, :] = kvp_ref[...]
        band_ref[BLOCK:2 * BLOCK, :] = kvc_ref[...]
        first_row = _first_block_row(n)
        for hk in range(N_KV_HEADS):
            k_h = band_ref[:, hk * HEAD_DIM:(hk + 1) * HEAD_DIM]
            v_h = band_ref[:, KV_DIM + hk * HEAD_DIM:KV_DIM + (hk + 1) * HEAD_DIM]
            p, _, den = _scores(_group_rows(q_ref, hk), k_h, _group_bias(bias_ref, hk), first_row,
                                _group_sinks(sink_ref, hk))
            o = _dot((p * (1.0 / den)).astype(BF16), v_h, NN).astype(BF16)
            for g in range(GROUP):
                o_ref[:, _head_lanes(hk, g)] = o[g * BLOCK:(g + 1) * BLOCK]

    cur, prev = _band_specs()
    qs = pl.BlockSpec((BLOCK, D_MODEL), lambda n: (n, 0))
    return _pcall(
        body, name, _sds((s, D_MODEL), BF16), grid=(nb,),
        in_specs=[pl.BlockSpec(memory_space=pltpu.SMEM), qs, cur, prev,
                  pl.BlockSpec((N_HEADS, BLOCK, 2 * BLOCK), lambda n: (0, 0, 0))],
        out_specs=qs, scratch=[pltpu.VMEM((2 * BLOCK, 2 * KV_DIM), BF16)],
        sem=("parallel",))(sinks, q, kv, kv, bias)


def attn_bwd(q, kv, do, bias, sinks, name):
    s = q.shape[0]
    nb = s // BLOCK
    scale = HEAD_DIM ** -0.5

    def body(sink_ref, q_ref, do_ref, kvc_ref, kvp_ref, bias_ref, dq_ref, dkv_ref, db_ref, dsink_ref,
             band_ref, dsacc_ref):
        n = pl.program_id(0)
        band_ref[0:BLOCK, :] = kvp_ref[...]
        band_ref[BLOCK:2 * BLOCK, :] = kvc_ref[...]
        first_row = _first_block_row(n)
        lane = lax.broadcasted_iota(jnp.int32, (BLOCK, BLOCK), 1)

        @pl.when(n == 0)
        def _():
            db_ref[...] = jnp.zeros_like(db_ref)
            dsacc_ref[...] = jnp.zeros_like(dsacc_ref)

        for hk in range(N_KV_HEADS):
            k_h = band_ref[:, hk * HEAD_DIM:(hk + 1) * HEAD_DIM]
            v_h = band_ref[:, KV_DIM + hk * HEAD_DIM:KV_DIM + (hk + 1) * HEAD_DIM]
            q_g = _group_rows(q_ref, hk)
            do_g = _group_rows(do_ref, hk)
            p, e_sink, den = _scores(q_g, k_h, _group_bias(bias_ref, hk), first_row, _group_sinks(sink_ref, hk))
            inv = 1.0 / den
            p = p * inv
            dp = _dot(do_g, v_h, NT)
            delta = jnp.sum(p * dp, axis=-1, keepdims=True)
            ds = p * (dp - delta)
            db_ref[hk * GROUP:(hk + 1) * GROUP] += ds.reshape(GROUP, BLOCK, 2 * BLOCK)
            d_sink = -(e_sink * inv) * delta
            for g in range(GROUP):
                dsacc_ref[...] += jnp.where(lane == hk * GROUP + g, d_sink[g * BLOCK:(g + 1) * BLOCK], 0.0)
            dsb = ds.astype(BF16)
            dq = (_dot(dsb, k_h, NN) * scale).astype(BF16)
            for g in range(GROUP):
                dq_ref[:, _head_lanes(hk, g)] = dq[g * BLOCK:(g + 1) * BLOCK]
            dkv_ref[:, hk * HEAD_DIM:(hk + 1) * HEAD_DIM] = _dot(dsb, q_g, TN) * scale
            dkv_ref[:, KV_DIM + hk * HEAD_DIM:KV_DIM + (hk + 1) * HEAD_DIM] = _dot(p.astype(BF16), do_g, TN)

        @pl.when(n == nb - 1)
        def _():
            dsink_ref[...] = jnp.sum(dsacc_ref[...], axis=0, keepdims=True)

    cur, prev = _band_specs()
    qs = pl.BlockSpec((BLOCK, D_MODEL), lambda n: (n, 0))
    full_b = pl.BlockSpec((N_HEADS, BLOCK, 2 * BLOCK), lambda n: (0, 0, 0))
    return _pcall(
        body, name,
        (_sds((s, D_MODEL), BF16), _sds((nb, 2 * BLOCK, 2 * KV_DIM), F32),
         _sds((N_HEADS, BLOCK, 2 * BLOCK), F32), _sds((1, BLOCK), F32)),
        grid=(nb,),
        in_specs=[pl.BlockSpec(memory_space=pltpu.SMEM), qs, qs, cur, prev, full_b],
        out_specs=(qs, pl.BlockSpec((None, 2 * BLOCK, 2 * KV_DIM), lambda n: (n, 0, 0)), full_b,
                   pl.BlockSpec((1, BLOCK), lambda n: (0, 0))),
        scratch=[pltpu.VMEM((2 * BLOCK, 2 * KV_DIM), BF16), pltpu.VMEM((BLOCK, BLOCK), F32)],
        sem=("arbitrary",))(sinks, q, do, kv, kv, bias)


def dkv_combine(pa, pb, name):
    nb = pa.shape[0]
    pa2 = pa.reshape(2 * nb, BLOCK, 2 * KV_DIM)
    pb2 = pb.reshape(2 * nb, BLOCK, 2 * KV_DIM)

    def body(ac_ref, an_ref, bc_ref, bn_ref, o_ref):
        n = pl.program_id(0)
        nxt = jnp.where(n == nb - 1, 0.0, an_ref[...] + bn_ref[...])
        o_ref[...] = (ac_ref[...] + bc_ref[...] + nxt).astype(BF16)

    cur = pl.BlockSpec((None, BLOCK, 2 * KV_DIM), lambda n: (2 * n + 1, 0, 0))
    nxt = pl.BlockSpec((None, BLOCK, 2 * KV_DIM), lambda n: (jnp.minimum(2 * n + 2, 2 * nb - 2), 0, 0))
    return _pcall(body, name, _sds((nb * BLOCK, 2 * KV_DIM), BF16), grid=(nb,),
                  in_specs=[cur, nxt, cur, nxt], out_specs=pl.BlockSpec((BLOCK, 2 * KV_DIM), lambda n: (n, 0)),
                  sem=("parallel",))(pa2, pa2, pb2, pb2)


def loss_head(h, g, target, name):
    s, d = h.shape
    tm = _row_tile(s, 512)

    def body(h_ref, g_ref, t_ref, dh_ref, dg_ref, loss_ref):
        i = pl.program_id(0)
        x = h_ref[...]
        r = lax.rsqrt(jnp.mean(x * x, axis=-1, keepdims=True) + EPS)
        xh = x * r
        gv = g_ref[...]
        err = xh * gv - t_ref[...]
        part_loss = jnp.zeros((1, BLOCK), F32) + 0.5 * jnp.sum(jnp.mean(err * err, axis=-1, keepdims=True))
        dy = err * (1.0 / d)
        dxh = dy * gv
        dh_ref[...] = r * (dxh - xh * jnp.mean(dxh * xh, axis=-1, keepdims=True))
        part_g = jnp.sum(dy * xh, axis=0, keepdims=True)

        @pl.when(i == 0)
        def _():
            dg_ref[...] = part_g
            loss_ref[...] = part_loss

        @pl.when(i > 0)
        def _():
            dg_ref[...] += part_g
            loss_ref[...] += part_loss

    row = pl.BlockSpec((tm, d), lambda i: (i, 0))
    vec = pl.BlockSpec((1, d), lambda i: (0, 0))
    return _pcall(body, name, (_sds((s, d), F32), _sds((1, d), F32), _sds((1, BLOCK), F32)), grid=(s // tm,),
                  in_specs=[row, vec, row], out_specs=(row, vec, pl.BlockSpec((1, BLOCK), lambda i: (0, 0))),
                  sem=("arbitrary",))(h, g, target)


def adamw(w, m, v, parts, name, token=None):
    nl, r, c = w.shape
    tr = max(t for t in range(1, min(r, 512) + 1) if r % t == 0 and (t % 16 == 0 or t == r))
    c1 = 1.0 / (1.0 - ADAM_B1 ** ADAM_STEP)
    c2 = 1.0 / (1.0 - ADAM_B2 ** ADAM_STEP)

    def body(w_ref, m_ref, v_ref, p_ref, *rest):
        g_ref, d_ref, nm_ref, nv_ref = rest[-4:]
        g = p_ref[0].astype(F32)
        for dev in range(1, N_DEV):
            g = g + p_ref[dev].astype(F32)
        nm = ADAM_B1 * m_ref[...] + (1.0 - ADAM_B1) * g
        nv = ADAM_B2 * v_ref[...] + (1.0 - ADAM_B2) * (g * g)
        g_ref[...] = g
        nm_ref[...] = nm
        nv_ref[...] = nv
        d_ref[...] = -ADAM_LR * ((nm * c1) / (jnp.sqrt(nv * c2) + ADAM_EPS) + ADAM_WD * w_ref[...])

    blk = pl.BlockSpec((None, tr, c), lambda l, i: (l, i, 0))
    out = _sds((nl, r, c), F32)
    extra = [] if token is None else [token]
    return _pcall(body, name, (out, out, out, out), grid=(nl, r // tr),
                  in_specs=[blk, blk, blk, pl.BlockSpec((N_DEV, None, tr, c), lambda l, i: (0, l, i, 0))]
                  + [ANY] * len(extra),
                  out_specs=(blk, blk, blk, blk), sem=("parallel", "parallel"))(w, m, v, parts, *extra)


def _place():
    x, y, c = lax.axis_index("x"), lax.axis_index("y"), lax.axis_index("c")
    return x, y, c


def _lin(px, py, pc):
    return 4 * px + 2 * py + pc


HBM = pl.BlockSpec(memory_space=pltpu.HBM)
SEM = pl.BlockSpec(memory_space=pltpu.SEMAPHORE)
EFFECT = pltpu.SideEffectType.DATAFLOW_SIDE_EFFECTING
N_PEERS = N_DEV - 1


def _peers_of(x, y, c):
    return [(x, y, 1 - c), (1 - x, y, c), (x, 1 - y, c), (1 - x, 1 - y, c),
            (1 - x, y, 1 - c), (x, 1 - y, 1 - c), (1 - x, 1 - y, 1 - c)]


def _in_hbm(a):
    return pltpu.with_memory_space_constraint(a, pltpu.HBM)


def send_start(name, bufs, copies, n_groups):
    nb = len(bufs)
    per_group = [[i for i, cp in enumerate(copies) if cp[0] == g] for g in range(n_groups)]

    def body(*refs):
        buf = refs[:nb]
        sems = refs[nb:nb + 2 * n_groups]
        token = refs[2 * nb + 2 * n_groups]
        x, y, c = _place()
        me = _lin(x, y, c)
        for g in range(n_groups):
            for slot, i in enumerate(per_group[g]):
                _, s, src_slab, d, land_slab = copies[i]
                for k, peer in enumerate(_peers_of(x, y, c)):
                    pltpu.make_async_remote_copy(
                        src_ref=src_slab(buf[s], _lin(*peer), me), dst_ref=land_slab(buf[d], me),
                        send_sem=sems[2 * g].at[slot * N_PEERS + k], recv_sem=sems[2 * g + 1].at[slot * N_PEERS + k],
                        device_id=peer, device_id_type=MESH).start()
        token[...] = jnp.zeros_like(token)

    sem_shapes = []
    for g in range(n_groups):
        sem_shapes += [pltpu.SemaphoreType.DMA((len(per_group[g]) * N_PEERS,))] * 2
    out = pl.pallas_call(
        body, name=name,
        out_shape=tuple(sem_shapes) + tuple(pltpu.HBM(b.shape, b.dtype) for b in bufs) + (_sds((8, 128), F32),),
        in_specs=[HBM] * nb,
        out_specs=tuple([SEM] * len(sem_shapes)) + tuple([HBM] * nb) + (pl.BlockSpec(memory_space=pltpu.VMEM),),
        input_output_aliases={i: len(sem_shapes) + i for i in range(nb)},
        compiler_params=pltpu.CompilerParams(has_side_effects=EFFECT))(*[_in_hbm(b) for b in bufs])
    sems = [(out[2 * g], out[2 * g + 1]) for g in range(n_groups)]
    return sems, list(out[2 * n_groups:2 * n_groups + nb]), out[2 * n_groups + nb]


N_FIRST = 4
N_RELAY = 3


def _gather_peers(x, y, c):
    first = [(x, y, 1 - c), (1 - x, y, c), (x, 1 - y, c), (1 - x, 1 - y, c)]
    return first, first[1:]


def gather_start(name, bufs, copies, n_groups):
    nb = len(bufs)
    per_group = [[i for i, cp in enumerate(copies) if cp[0] == g] for g in range(n_groups)]

    def body(*refs):
        buf = refs[:nb]
        sems = refs[nb:nb + 2 * n_groups]
        token = refs[2 * nb + 2 * n_groups]
        x, y, c = _place()
        me = _lin(x, y, c)
        first, _ = _gather_peers(x, y, c)
        for g in range(n_groups):
            for slot, i in enumerate(per_group[g]):
                _, d, slab = copies[i]
                for k, peer in enumerate(first):
                    pltpu.make_async_remote_copy(
                        src_ref=slab(buf[d], me), dst_ref=slab(buf[d], me),
                        send_sem=sems[2 * g].at[slot * N_FIRST + k], recv_sem=sems[2 * g + 1].at[slot * N_FIRST + k],
                        device_id=peer, device_id_type=MESH).start()
        token[...] = jnp.zeros_like(token)

    sem_shapes = []
    for g in range(n_groups):
        sem_shapes += [pltpu.SemaphoreType.DMA((len(per_group[g]) * N_FIRST,))] * 2
    out = pl.pallas_call(
        body, name=name,
        out_shape=tuple(sem_shapes) + tuple(pltpu.HBM(b.shape, b.dtype) for b in bufs) + (_sds((8, 128), F32),),
        in_specs=[HBM] * nb,
        out_specs=tuple([SEM] * len(sem_shapes)) + tuple([HBM] * nb) + (pl.BlockSpec(memory_space=pltpu.VMEM),),
        input_output_aliases={i: len(sem_shapes) + i for i in range(nb)},
        compiler_params=pltpu.CompilerParams(has_side_effects=EFFECT))(*[_in_hbm(b) for b in bufs])
    return [(out[2 * g], out[2 * g + 1]) for g in range(n_groups)], list(out[2 * n_groups:2 * n_groups + nb])


def gather_relay(name, bufs, slabs, first_sems, after):
    nb = len(bufs)

    def body(*refs):
        buf = refs[:nb]
        send_a, recv_a = refs[nb], refs[nb + 1]
        send_b, recv_b = refs[nb + 3], refs[nb + 4]
        token = refs[2 * nb + 5]
        x, y, c = _place()
        first, origins = _gather_peers(x, y, c)
        for n, slab in enumerate(slabs):
            for j, origin in enumerate(origins):
                block = slab(buf[n], _lin(*origin))
                pltpu.make_async_remote_copy(
                    src_ref=block, dst_ref=block, send_sem=send_a.at[n * N_FIRST + 1 + j],
                    recv_sem=recv_a.at[n * N_FIRST + 1 + j], device_id=origin, device_id_type=MESH).wait_recv()
                pltpu.make_async_remote_copy(
                    src_ref=block, dst_ref=block, send_sem=send_b.at[n * N_RELAY + j],
                    recv_sem=recv_b.at[n * N_RELAY + j], device_id=first[0], device_id_type=MESH).start()
        token[...] = jnp.zeros_like(token)

    sem_shape = pltpu.SemaphoreType.DMA((nb * N_RELAY,))
    out = pl.pallas_call(
        body, name=name,
        out_shape=(sem_shape, sem_shape) + tuple(pltpu.HBM(b.shape, b.dtype) for b in bufs) + (_sds((8, 128), F32),),
        in_specs=[HBM] * nb + [SEM, SEM, ANY],
        out_specs=(SEM, SEM) + tuple([HBM] * nb) + (pl.BlockSpec(memory_space=pltpu.VMEM),),
        input_output_aliases={i: 2 + i for i in range(nb)},
        compiler_params=pltpu.CompilerParams(has_side_effects=EFFECT))(*bufs, first_sems[0], first_sems[1], after)
    return (out[0], out[1]), list(out[2:2 + nb]), out[2 + nb]


def gather_wait(name, bufs, slabs, first_sems, relay_sems, after):
    nb = len(bufs)

    def body(*refs):
        buf = refs[:nb]
        send_a, recv_a, send_b, recv_b = refs[nb:nb + 4]
        x, y, c = _place()
        me = _lin(x, y, c)
        first, origins = _gather_peers(x, y, c)
        sibling = first[0]
        for n, slab in enumerate(slabs):
            mine = slab(buf[n], me)
            for k, peer in enumerate(first):
                pltpu.make_async_remote_copy(
                    src_ref=mine, dst_ref=mine, send_sem=send_a.at[n * N_FIRST + k],
                    recv_sem=recv_a.at[n * N_FIRST + k], device_id=peer, device_id_type=MESH).wait_send()
            theirs = slab(buf[n], _lin(*sibling))
            pltpu.make_async_remote_copy(
                src_ref=theirs, dst_ref=theirs, send_sem=send_a.at[n * N_FIRST], recv_sem=recv_a.at[n * N_FIRST],
                device_id=sibling, device_id_type=MESH).wait_recv()
            for j, (ox, oy, oc) in enumerate(origins):
                sent = slab(buf[n], _lin(ox, oy, oc))
                got = slab(buf[n], _lin(ox, oy, 1 - oc))
                pltpu.make_async_remote_copy(
                    src_ref=sent, dst_ref=got, send_sem=send_b.at[n * N_RELAY + j],
                    recv_sem=recv_b.at[n * N_RELAY + j], device_id=sibling, device_id_type=MESH).wait()

    out = pl.pallas_call(
        body, name=name, out_shape=tuple(pltpu.HBM(b.shape, b.dtype) for b in bufs),
        in_specs=[HBM] * nb + [SEM] * 4 + [ANY], out_specs=tuple([HBM] * nb),
        input_output_aliases={i: i for i in range(nb)},
        compiler_params=pltpu.CompilerParams(has_side_effects=EFFECT))(
            *bufs, first_sems[0], first_sems[1], relay_sems[0], relay_sems[1], after)
    return list(out)


def send_wait(name, bufs, copies, sems, after):
    nb = len(bufs)

    def body(*refs):
        buf = refs[:nb]
        send_sems, recv_sems = refs[nb], refs[nb + 1]
        x, y, c = _place()
        me = _lin(x, y, c)
        for slot, (s, src_slab, d, land_slab) in enumerate(copies):
            for k, peer in enumerate(_peers_of(x, y, c)):
                j = _lin(*peer)
                cp = pltpu.make_async_remote_copy(
                    src_ref=src_slab(buf[s], j, me), dst_ref=land_slab(buf[d], j),
                    send_sem=send_sems.at[slot * N_PEERS + k], recv_sem=recv_sems.at[slot * N_PEERS + k],
                    device_id=peer, device_id_type=MESH)
                cp.wait_send()
                cp.wait_recv()

    out = pl.pallas_call(
        body, name=name, out_shape=tuple(pltpu.HBM(b.shape, b.dtype) for b in bufs),
        in_specs=[HBM] * nb + [SEM, SEM, ANY], out_specs=tuple([HBM] * nb),
        input_output_aliases={i: i for i in range(nb)},
        compiler_params=pltpu.CompilerParams(has_side_effects=EFFECT))(*bufs, sems[0], sems[1], after)
    return list(out)


def local_step(x, target, weights, rep, emit):
    s = x.shape[0]
    bucket = jnp.asarray(_bucket_table())
    bias = bias_table(rep["rel_bias"], bucket, "bias_table")
    h = x
    saved = []
    kv = None
    h_kv = u_kv = None
    small = None
    u = rms_fwd(h, rep["norm_mix"][0:1], "rms_mix_fwd0")
    for l in range(4):
        g_ffn = rep["norm_ffn"][l:l + 1]
        rec = {"h_in": h, "u": u}
        if l < 2:
            w = weights(f"conv{l}", u)
            if l == 0:
                small = w
            cp = small["cp"][l]
            t = pw1_fwd(u, w["pw1"], small["b_pw1"], l, f"pw1_fwd{l}")
            z, y = conv_fwd(t, cp, f"conv_fwd{l}")
            relay_tok = weights(f"ffn{l}", z, relay=True)
            h, uf = mm_nn(f"pw2_fwd{l}", z, w["pw2"], (None, D_MODEL, D_MODEL), lambda i, j: (0, 0, j), D_MODEL,
                          D_MODEL, F32, res=h, norm=g_ffn, token=relay_tok,
                          bias=(small["b_pw2"], pl.BlockSpec((None, 1, D_MODEL), lambda i, j, l=l: (l, 0, j))))
            rec.update(t=t, z=z, y=y, cp=cp)
        else:
            a = l - 2
            w = weights(f"attn{a}", u)
            if a == 0:
                h_kv = h
                w_kv = w["wkv"]
                kv = mm_nn("kv_fwd", u_kv, w_kv, (D_MODEL, 2 * KV_DIM), lambda i, j: (0, 0), 2 * KV_DIM,
                           2 * KV_DIM, BF16)
            q = mm_nn(f"q_fwd{a}", u, w["wq"], (None, D_MODEL, D_MODEL), lambda i, j: (0, 0, j), D_MODEL, D_MODEL,
                      BF16)
            o = attn_fwd(q, kv, bias, rep["sinks"][a:a + 1], f"attn_fwd{a}")
            relay_tok = weights(f"ffn{l}", o, relay=True)
            h, uf = mm_nn(f"o_fwd{a}", o, w["wo"], (None, D_MODEL, D_MODEL), lambda i, j: (0, 0, j), D_MODEL,
                          D_MODEL, F32, res=h, norm=g_ffn, token=relay_tok)
            rec.update(q=q, o=o)
        rec["w"] = w
        rec["h_mid"] = h
        wf = weights(f"ffn{l}", uf)
        relay_tok = weights(("conv1", "attn0", "attn1")[l], uf, relay=True) if l < 3 else None
        nxt = [] if l == 3 else [rep["norm_mix"][l + 1:l + 2]] + ([rep["norm_kv"]] if l == 1 else [])
        h, gu, *normed = ffn_fwd(uf, h, wf["up"], wf["down"], 0, f"ffn_fwd{l}", norms=nxt, token=relay_tok)
        if normed:
            u = normed[0]
        if l == 1:
            u_kv = normed[1]
        rec.update(uf=uf, gu=gu, wf=wf)
        saved.append(rec)

    dh, d_nfin, loss = loss_head(h, rep["norm_final"], target, "loss_head")

    d_mix, d_ffn = [None] * 4, [None] * 4
    cp_grads = [None, None]
    dkv_parts, dbias_parts, dsinks = [], [], [None, None]
    d_nkv = None
    full_rows = lambda tk: (tk, D_MODEL)
    tok = None
    for l in reversed(range(4)):
        rec = saved[l]
        w, wf = rec["w"], rec["wf"]
        grads = {}
        g_mix = rep["norm_mix"][l:l + 1]
        g_ffn = rep["norm_ffn"][l:l + 1]
        dh_mid, d_ffn[l], act, dgu = ffn_bwd(dh, rec["gu"], wf["up"], wf["down"], 0, rec["h_mid"], g_ffn,
                                             f"ffn_bwd{l}", token=tok)
        g_down = mm_tn(
            f"down_grad{l}", act, dh, groups=4, a_block=lambda tk: (None, tk, FF_CHUNK), a_index=lambda j, k: (j, k, 0),
            b_block=full_rows, b_index=lambda j, k: (k, 0), o_block=(None, FF_CHUNK, D_MODEL),
            o_index=lambda j, k: (j, 0, 0), o_shape=(4, FF_CHUNK, D_MODEL), acc_shape=(FF_CHUNK, D_MODEL), tk=s)
        g_up = mm_tn(
            f"up_grad{l}", dgu.reshape(8, s, FF_CHUNK), rec["uf"], groups=8, a_block=lambda tk: (None, tk, FF_CHUNK),
            a_index=lambda j, k: (j, k, 0), b_block=full_rows, b_index=lambda j, k: (k, 0),
            o_block=(None, FF_CHUNK, D_MODEL), o_index=lambda j, k: (j, 0, 0), o_shape=(8, FF_CHUNK, D_MODEL),
            acc_shape=(FF_CHUNK, D_MODEL), tk=s)
        tok = emit(f"ffn{l}", {"up": g_up, "down": g_down.reshape(D_FF, D_MODEL)})
        dh = dh_mid
        if l < 2:
            dz = mm_nt(f"pw2_bwd{l}", dh, w["pw2"], (None, D_MODEL, D_MODEL), lambda i, k: (0, 0, 0), D_MODEL, F32,
                       token=tok)
            grads["pw2"], db2 = mm_tn(
                f"pw2_grad{l}", rec["z"], dh, groups=1, a_block=full_rows, a_index=lambda j, k: (k, 0),
                b_block=full_rows, b_index=lambda j, k: (k, 0), o_block=(D_MODEL, D_MODEL), o_index=lambda j, k: (0, 0),
                o_shape=(D_MODEL, D_MODEL), acc_shape=(D_MODEL, D_MODEL),
                colsum=((1, D_MODEL), pl.BlockSpec((1, D_MODEL), lambda j, k: (0, 0))))
            dt, stats = conv_bwd(rec["t"], rec["y"], dz, rec["cp"], f"conv_bwd{l}")
            grads["pw1"], db1 = pw1_grad(rec["u"], dt, f"pw1_grad{l}")
            cp_grads[l] = (stats, db2, db1)
            tok = emit(f"conv{l}", grads)
            dh, d_mix[l] = pw1_bwd(dt, w["pw1"], rec["h_in"], g_mix, dh, f"pw1_bwd{l}", token=tok)
        else:
            a = l - 2
            do = mm_nt(f"o_bwd{a}", dh, w["wo"], (None, D_MODEL, D_MODEL), lambda i, k: (0, 0, 0), D_MODEL, BF16,
                       token=tok)
            tok = None
            grads["wo"] = mm_tn(
                f"wo_grad{a}", rec["o"], dh, groups=1, a_block=full_rows, a_index=lambda j, k: (k, 0),
                b_block=full_rows, b_index=lambda j, k: (k, 0), o_block=(D_MODEL, D_MODEL), o_index=lambda j, k: (0, 0),
                o_shape=(D_MODEL, D_MODEL), acc_shape=(D_MODEL, D_MODEL))
            dq, dkv_p, dbias_p, dsinks[a] = attn_bwd(rec["q"], kv, do, bias, rep["sinks"][a:a + 1], f"attn_bwd{a}")
            dkv_parts.append(dkv_p)
            dbias_parts.append(dbias_p)
            grads["wq"] = mm_tn(
                f"wq_grad{a}", rec["u"], dq, groups=1, a_block=full_rows, a_index=lambda j, k: (k, 0),
                b_block=full_rows, b_index=lambda j, k: (k, 0), o_block=(D_MODEL, D_MODEL), o_index=lambda j, k: (0, 0),
                o_shape=(D_MODEL, D_MODEL), acc_shape=(D_MODEL, D_MODEL))
            if a == 1:
                tok = emit("attn1", grads)
            dh, d_mix[l] = nt_rms_bwd(f"q_bwd{a}", dq, w["wq"], (None, D_MODEL, D_MODEL), lambda i: (0, 0, 0),
                                      rec["h_in"], g_mix, dh)
        if l == 2:
            dkv = dkv_combine(dkv_parts[0], dkv_parts[1], "dkv_combine")
            grads["wkv"] = mm_tn(
                "wkv_grad", u_kv, dkv, groups=1, a_block=full_rows, a_index=lambda j, k: (k, 0),
                b_block=lambda tk: (tk, 2 * KV_DIM), b_index=lambda j, k: (k, 0), o_block=(D_MODEL, 2 * KV_DIM),
                o_index=lambda j, k: (0, 0), o_shape=(D_MODEL, 2 * KV_DIM), acc_shape=(D_MODEL, 2 * KV_DIM))
            tok = emit("attn0", grads)
            dh, d_nkv = nt_rms_bwd("kv_bwd", dkv, w_kv, (D_MODEL, 2 * KV_DIM), lambda i: (0, 0), h_kv,
                                   rep["norm_kv"], dh)

    d_relb = bias_grad(dbias_parts[0], dbias_parts[1], bucket, "bias_grad")
    d_sinks = jnp.concatenate([dsinks[0][0, :N_HEADS], dsinks[1][0, :N_HEADS]])
    tail = jnp.zeros((D_MODEL,), F32)
    rep_grad = jnp.concatenate([
        jnp.concatenate(d_mix, axis=0), jnp.concatenate(d_ffn, axis=0), d_nkv, d_nfin,
        tail.at[:2 * N_HEADS].set(d_sinks)[None], tail.at[:N_BUCKETS * N_HEADS].set(d_relb.reshape(-1))[None],
        tail.at[0].set(loss[0, 0])[None], jnp.zeros((REP_ROWS - ROW_LOSS - 1, D_MODEL), F32)], axis=0)
    return dh, cp_grads, rep_grad


def _pack_conv(w_dw, b_dw, ln_g, ln_b, b_pw2, b_pw1):
    rows = [w_dw, b_dw[:, None], ln_g[:, None], ln_b[:, None], b_pw2[:, None], b_pw1.reshape(2, 2, 128),
            jnp.zeros((2, PACK_ROWS - ROW_BPW1 - 2, 128), F32)]
    return jnp.concatenate(rows, axis=1)


def _unpack_conv(p):
    return (p[:, :CONV_WIDTH], p[:, ROW_BDW], p[:, ROW_LNG], p[:, ROW_LNB], p[:, ROW_BPW2],
            p[:, ROW_BPW1:ROW_BPW1 + 2].reshape(2, 256))


def _pack_rep(norm_mix, norm_ffn, norm_kv, norm_final, sinks, rel_bias):
    tail = jnp.zeros((D_MODEL,), F32)
    return jnp.concatenate([
        norm_mix, norm_ffn, norm_kv[None], norm_final[None], tail.at[:2 * N_HEADS].set(sinks.reshape(-1))[None],
        tail.at[:N_BUCKETS * N_HEADS].set(rel_bias.reshape(-1))[None],
        jnp.zeros((REP_ROWS - ROW_RELB - 1, D_MODEL), F32)], axis=0)


def _unpack_rep(p):
    return (p[0:4], p[4:8], p[ROW_NKV], p[ROW_NFIN], p[ROW_SINK, :2 * N_HEADS].reshape(2, N_HEADS),
            p[ROW_RELB, :N_BUCKETS * N_HEADS].reshape(N_BUCKETS, N_HEADS))


def kernel(x, norm_mix, norm_ffn, conv_w_pw1, conv_b_pw1, conv_w_dw, conv_b_dw, conv_ln_g, conv_ln_b, conv_w_pw2, conv_b_pw2, norm_kv, w_kv, w_q, w_o, sinks, rel_bias, ffn_w_up, ffn_w_down, norm_final, loss_target, m_norm_mix, m_norm_ffn, m_conv_w_pw1, m_conv_b_pw1, m_conv_w_dw, m_conv_b_dw, m_conv_ln_g, m_conv_ln_b, m_conv_w_pw2, m_conv_b_pw2, m_norm_kv, m_w_kv, m_w_q, m_w_o, m_sinks, m_rel_bias, m_ffn_w_up, m_ffn_w_down, m_norm_final, v_norm_mix, v_norm_ffn, v_conv_w_pw1, v_conv_b_pw1, v_conv_w_dw, v_conv_b_dw, v_conv_ln_g, v_conv_ln_b, v_conv_w_pw2, v_conv_b_pw2, v_norm_kv, v_w_kv, v_w_q, v_w_o, v_sinks, v_rel_bias, v_ffn_w_up, v_ffn_w_down, v_norm_final):
    s = x.shape[1]
    d = D_MODEL
    rsh = d // N_DEV
    dsh = D_FF // N_DEV

    me = _lin(*_place())
    lead_slab = lambda ref, j: ref.at[j]
    rows_of = lambda rows: (lambda ref, j: ref.at[pl.ds(j * rows, rows), :])

    conv_pack = _pack_conv(conv_w_dw, conv_b_dw, conv_ln_g, conv_ln_b, conv_b_pw2, conv_b_pw1)
    ag_order = ["conv0", "ffn0", "conv1", "ffn1", "attn0", "ffn2", "attn1", "ffn3"]
    ag_land, ag_copies, ag_members = [], [], {g: [] for g in ag_order}

    def gather(group, key, shard, land_shape, at, land_slab):
        i = len(ag_land)
        ag_land.append(lax.dynamic_update_slice(lax.empty(land_shape, shard.dtype), shard, at(me)))
        ag_copies.append((ag_order.index(group), i, land_slab))
        ag_members[group].append((key, i, land_slab))

    cols_at0 = lambda ref, j: ref.at[0, j]
    rows_at0 = lambda rows: (lambda ref, j: ref.at[0, pl.ds(j * rows, rows), :])
    col_at = lambda m: (0, m, 0, 0)
    row_at = lambda rows: (lambda m: (0, m * rows, 0))
    for l in range(2):
        gather(f"conv{l}", "pw1", conv_w_pw1[l].astype(BF16)[None, None], (1, N_DEV, d, 256), col_at, cols_at0)
        gather(f"conv{l}", "pw2", conv_w_pw2[l].astype(BF16)[None], (1, d, d), row_at(rsh), rows_at0(rsh))
    gather("conv0", "pack", conv_pack[None], (N_DEV, 2, PACK_ROWS, 128), lambda m: (m, 0, 0, 0), lead_slab)
    gather("attn0", "wkv", w_kv.astype(BF16), (d, 2 * KV_DIM), lambda m: (m * rsh, 0), rows_of(rsh))
    for a in range(2):
        gather(f"attn{a}", "wq", w_q[a].astype(BF16)[None], (1, d, d), row_at(rsh), rows_at0(rsh))
        gather(f"attn{a}", "wo", w_o[a].astype(BF16)[None], (1, d, d), row_at(rsh), rows_at0(rsh))
    up_t, m_up_t, v_up_t = (jnp.swapaxes(a, 1, 2) for a in (ffn_w_up, m_ffn_w_up, v_ffn_w_up))
    for l in range(4):
        gather(f"ffn{l}", "up", up_t[l].astype(BF16)[None, None], (1, N_DEV, FF_CHUNK, d), col_at, cols_at0)
        gather(f"ffn{l}", "down", ffn_w_down[l].astype(BF16)[None], (1, D_FF, d), row_at(dsh), rows_at0(dsh))
    ag_sems, ag_land_thru = gather_start("ag_start", ag_land, ag_copies, len(ag_order))
    relayed = {}

    def weights(group, after, relay=False):
        members = ag_members[group]
        slabs = [slab for _, _, slab in members]
        first_sems = ag_sems[ag_order.index(group)]
        if group not in relayed:
            relayed[group] = gather_relay(f"ag_relay_{group}", [ag_land_thru[i] for _, i, _ in members], slabs,
                                          first_sems, after)
        relay_sems, bufs, token = relayed[group]
        if relay:
            return token
        lands = gather_wait(f"ag_wait_{group}", bufs, slabs, first_sems, relay_sems, after)
        w = {key: land for (key, _, _), land in zip(members, lands)}
        if "up" in w:
            w["up"] = w["up"].reshape(1, 2, 4, FF_CHUNK, d)
            w["down"] = w["down"].reshape(1, 4, FF_CHUNK, d)
        if "pack" in w:
            pack_g = w.pop("pack")
            w["cp"] = jnp.transpose(pack_g, (1, 2, 0, 3)).reshape(2, PACK_ROWS, d)
            w["b_pw1"] = pack_g[:, :, ROW_BPW1:ROW_BPW1 + 2, :].transpose(1, 0, 2, 3).reshape(2, 1, 2 * d)
            w["b_pw2"] = w["cp"][:, ROW_BPW2:ROW_BPW2 + 1, :]
        return w

    shard_shapes = {"pw1": (d, 256), "pw2": (rsh, d), "wkv": (rsh, 2 * KV_DIM), "wq": (rsh, d), "wo": (rsh, d),
                    "up": (FF_CHUNK, d), "down": (dsh, d), "cp": (2, PACK_ROWS, 128), "rep": (REP_ROWS, d)}
    n_layers = {"pw1": 2, "pw2": 2, "wkv": 1, "wq": 2, "wo": 2, "up": 4, "down": 4, "cp": 1, "rep": 1}
    by_lead = (lambda ref, j, me_: ref.at[j], lambda g: lax.dynamic_index_in_dim(g, me, 0, keepdims=False))
    by_rows = lambda rows: (lambda ref, j, me_: ref.at[pl.ds(j * rows, rows), :],
                            lambda g: lax.dynamic_slice_in_dim(g, me * rows, rows, 0))
    all_of = (lambda ref, j, me_: ref, lambda g: g)
    owned = {"pw1": by_lead, "pw2": by_rows(rsh), "wkv": by_rows(rsh), "wq": by_rows(rsh), "wo": by_rows(rsh),
             "up": by_lead, "down": by_rows(dsh), "cp": by_lead, "rep": all_of}
    parts = {}
    pending = {}

    def finish(chain, after):
        keys, bufs, copies, sems, name = pending.pop(chain)
        done = send_wait(f"rs_wait_{name}", bufs, copies, sems, after)
        parts.update(zip(keys, done[len(keys):]))

    def exchange(chain, name, layer, grads):
        keys = list(grads)
        if chain in pending:
            finish(chain, grads[keys[0]])
        lands = []
        for k in keys:
            land = parts.pop(k) if k in parts else lax.empty((N_DEV, n_layers[k]) + shard_shapes[k], grads[k].dtype)
            mine = owned[k][1](grads[k])[None, None]
            lands.append(lax.dynamic_update_slice(land, mine, (me, layer) + (0,) * len(shard_shapes[k])))
        land_at = lambda ref, i: ref.at[i, layer]
        copies = [(0, n, owned[k][0], len(keys) + n, land_at) for n, k in enumerate(keys)]
        sems, thru, token = send_start(f"rs_start_{name}", [grads[k] for k in keys] + lands, copies, 1)
        pending[chain] = (keys, thru, [c[1:] for c in copies], sems[0], name)
        return token

    def emit(group, grads):
        return exchange(group[:-1], group, int(group[-1]), grads)

    rep = {"norm_mix": norm_mix, "norm_ffn": norm_ffn, "norm_kv": norm_kv[None], "norm_final": norm_final[None],
           "sinks": sinks, "rel_bias": rel_bias}

    grad_x, cp_grads, rep_grad = local_step(x[0], loss_target[0], weights, rep, emit)

    cp_full = []
    for l in range(2):
        stats, db2, db1 = cp_grads[l]
        cp_full.append(jnp.concatenate([
            stats[:ROW_BPW2], db2, db1.reshape(N_DEV, 2, 128).transpose(1, 0, 2).reshape(2, d),
            jnp.zeros((PACK_ROWS - ROW_BPW1 - 2, d), F32)], axis=0))
    cp_send = jnp.stack(cp_full).reshape(2, PACK_ROWS, N_DEV, 128).transpose(2, 0, 1, 3)
    tail_token = exchange("tail", "tail", 0, {"cp": cp_send, "rep": rep_grad})

    def update(key, w, m, v, name, token=None):
        p = parts[key]
        w3 = w.reshape(p.shape[1:])
        outs = adamw(w3, m.reshape(w3.shape), v.reshape(w3.shape), p, name, token=token)
        return [o.reshape(w.shape) for o in outs]

    res = {}
    finish("ffn", grad_x)
    up_res = update("up", up_t, m_up_t, v_up_t, "adam_up", tail_token)
    res["ffn_w_up"] = [jnp.swapaxes(o, 1, 2) for o in up_res]
    res["ffn_w_down"] = update("down", ffn_w_down, m_ffn_w_down, v_ffn_w_down, "adam_down", up_res[0])
    finish("attn", res["ffn_w_down"][0])
    res["w_kv"] = update("wkv", w_kv, m_w_kv, v_w_kv, "adam_wkv")
    res["w_q"] = update("wq", w_q, m_w_q, v_w_q, "adam_wq")
    res["w_o"] = update("wo", w_o, m_w_o, v_w_o, "adam_wo")
    finish("conv", res["w_o"][0])
    res["conv_w_pw1"] = update("pw1", conv_w_pw1, m_conv_w_pw1, v_conv_w_pw1, "adam_pw1")
    res["conv_w_pw2"] = update("pw2", conv_w_pw2, m_conv_w_pw2, v_conv_w_pw2, "adam_pw2")
    finish("tail", res["conv_w_pw2"][0])
    m_pack = _pack_conv(m_conv_w_dw, m_conv_b_dw, m_conv_ln_g, m_conv_ln_b, m_conv_b_pw2, m_conv_b_pw1)
    v_pack = _pack_conv(v_conv_w_dw, v_conv_b_dw, v_conv_ln_g, v_conv_ln_b, v_conv_b_pw2, v_conv_b_pw1)
    cp_res = adamw(conv_pack, m_pack, v_pack, parts["cp"].reshape(N_DEV, 2, PACK_ROWS, 128), "adam_conv_pack")
    rep_w = _pack_rep(norm_mix, norm_ffn, norm_kv, norm_final, sinks, rel_bias)
    rep_m = _pack_rep(m_norm_mix, m_norm_ffn, m_norm_kv, m_norm_final, m_sinks, m_rel_bias)
    rep_v = _pack_rep(v_norm_mix, v_norm_ffn, v_norm_kv, v_norm_final, v_sinks, v_rel_bias)
    rep_res = adamw(rep_w[None], rep_m[None], rep_v[None], parts["rep"], "adam_rep")
    loss = rep_res[0][0, ROW_LOSS, 0]

    outs = []
    for kind in range(4):
        cw_dw, cb_dw, cln_g, cln_b, cb_pw2, cb_pw1 = _unpack_conv(cp_res[kind])
        r_mix, r_ffn, r_nkv, r_nfin, r_sinks, r_relb = _unpack_rep(rep_res[kind][0])
        outs += [r_mix, r_ffn, res["conv_w_pw1"][kind], cb_pw1, cw_dw, cb_dw, cln_g, cln_b, res["conv_w_pw2"][kind],
                 cb_pw2, r_nkv, res["w_kv"][kind], res["w_q"][kind], res["w_o"][kind], r_sinks, r_relb,
                 res["ffn_w_up"][kind], res["ffn_w_down"][kind], r_nfin]
    return (loss, grad_x[None], *outs)
```

```python
import functools
import math

import numpy as np
import jax
import jax.numpy as jnp
from jax import lax
from jax.experimental import pallas as pl
from jax.experimental.pallas import tpu as pltpu

F32 = jnp.float32
BF16 = jnp.bfloat16

D_MODEL = 1024
D_FF = 2816
N_HEADS = 16
N_KV_HEADS = 4
GROUP = N_HEADS // N_KV_HEADS
HEAD_DIM = 64
KV_DIM = N_KV_HEADS * HEAD_DIM
BLOCK = 128
CONV_WIDTH = 31
HALO = 32
N_BUCKETS = 32
MAX_DISTANCE = 128
EPS = 1e-6
NEG_INF = -1e30
N_DEV = 8
FF_CHUNK = D_FF // 4
PACK_ROWS = 40
ROW_BDW, ROW_LNG, ROW_LNB, ROW_BPW2, ROW_BPW1 = 31, 32, 33, 34, 35
REP_ROWS = 16
ROW_NKV, ROW_NFIN, ROW_SINK, ROW_RELB, ROW_LOSS = 8, 9, 10, 11, 12

ADAM_LR, ADAM_B1, ADAM_B2, ADAM_EPS, ADAM_WD, ADAM_STEP = 0.001, 0.9, 0.999, 1e-08, 0.01, 10

VMEM_LIMIT_BYTES = 56 * 1024 * 1024
FFN_ROWS = 1024
FFN_BWD_ROWS = 512
GRAD_ROWS = 1024
ANY = pl.BlockSpec(memory_space=pl.ANY)
MESH = pl.DeviceIdType.MESH

NN = (((1,), (0,)), ((), ()))
NT = (((1,), (1,)), ((), ()))
TN = (((0,), (0,)), ((), ()))


def _dot(a, b, dims):
    return lax.dot_general(a, b, dims, preferred_element_type=F32)


def _pcall(body, name, out_shape, *, grid=None, in_specs=None, out_specs=None, scratch=(), sem=None, **kw):
    params = pltpu.CompilerParams(dimension_semantics=sem, vmem_limit_bytes=VMEM_LIMIT_BYTES)
    extra = {} if grid is None else {"grid": grid}
    return pl.pallas_call(body, name=name, out_shape=out_shape, in_specs=in_specs, out_specs=out_specs,
                          scratch_shapes=list(scratch), compiler_params=params, **extra, **kw)


def _sds(shape, dtype):
    return jax.ShapeDtypeStruct(tuple(shape), dtype)


def _row_tile(s, want):
    return want if s % want == 0 else s


def rms_fwd(h, g, name):
    s, d = h.shape
    tm = _row_tile(s, 512)

    def body(h_ref, g_ref, u_ref):
        x = h_ref[...]
        r = lax.rsqrt(jnp.mean(x * x, axis=-1, keepdims=True) + EPS)
        u_ref[...] = (x * r * g_ref[...]).astype(BF16)

    return _pcall(body, name, _sds((s, d), BF16), grid=(s // tm,),
                  in_specs=[pl.BlockSpec((tm, d), lambda i: (i, 0)), pl.BlockSpec((1, d), lambda i: (0, 0))],
                  out_specs=pl.BlockSpec((tm, d), lambda i: (i, 0)), sem=("parallel",))(h, g)


def _rms_rows(x, gain):
    return (x * lax.rsqrt(jnp.mean(x * x, axis=-1, keepdims=True) + EPS) * gain).astype(BF16)


def _mm(name, a, b, *, dims, grid, a_spec, b_spec, o_spec, o_shape, nk=1, acc_shape=None,
        bias=None, res=None, colsum=None, sem=None, token=None, norm=None):
    n_axes = len(grid)

    def body(*refs):
        it = iter(refs)
        a_ref, b_ref = next(it), next(it)
        bias_ref = next(it) if bias is not None else None
        res_ref = next(it) if res is not None else None
        gain_ref = next(it) if norm is not None else None
        if token is not None:
            next(it)
        o_ref = next(it)
        un_ref = next(it) if norm is not None else None
        cs_ref = next(it) if colsum is not None else None
        acc_ref = next(it) if nk > 1 else None
        k = pl.program_id(n_axes - 1)
        p = _dot(a_ref[...].astype(BF16), b_ref[...].astype(BF16), dims)

        def finish(acc):
            if bias_ref is not None:
                acc = acc + bias_ref[...]
            if res_ref is not None:
                acc = acc + res_ref[...]
            o_ref[...] = acc.astype(o_ref.dtype)
            if un_ref is not None:
                un_ref[...] = _rms_rows(acc, gain_ref[...])

        if nk == 1:
            finish(p)
        else:
            @pl.when(k == 0)
            def _():
                acc_ref[...] = p

            @pl.when(k > 0)
            def _():
                acc_ref[...] += p

            @pl.when(k == nk - 1)
            def _():
                finish(acc_ref[...])

        if cs_ref is not None:
            cs = jnp.sum(b_ref[...].astype(F32), axis=0, keepdims=True)

            @pl.when(k == 0)
            def _():
                cs_ref[...] = cs

            @pl.when(k > 0)
            def _():
                cs_ref[...] += cs

    ins, in_specs = [a, b], [a_spec, b_spec]
    gain = None if norm is None else (norm, pl.BlockSpec(norm.shape, lambda *_: (0,) * norm.ndim))
    for extra in (bias, res, gain, None if token is None else (token, ANY)):
        if extra is not None:
            ins.append(extra[0])
            in_specs.append(extra[1])
    out_shape, out_specs = o_shape, o_spec
    if norm is not None:
        out_shape, out_specs = (o_shape, _sds(o_shape.shape, BF16)), (o_spec, o_spec)
    if colsum is not None:
        out_shape, out_specs = (o_shape, _sds(colsum[0], F32)), (o_spec, colsum[1])
    scratch = [pltpu.VMEM(acc_shape, F32)] if nk > 1 else []
    if sem is None:
        sem = ("parallel",) * (n_axes - 1) + ("arbitrary",)
    return _pcall(body, name, out_shape, grid=grid, in_specs=in_specs, out_specs=out_specs, scratch=scratch,
                  sem=sem)(*ins)


def mm_nn(name, a, w, w_block, w_index, n, tn, out_dtype, bias=None, res=None, tm=1024, norm=None, token=None):
    s, k = a.shape
    tm = _row_tile(s, tm)
    col = lambda i, j: (i, j)
    extras = {"token": token}
    if bias is not None:
        extras["bias"] = bias
    if res is not None:
        extras["res"] = (res, pl.BlockSpec((tm, tn), col))
    if norm is not None:
        assert tn == n, "a fused norm needs whole rows"
        extras["norm"] = norm
    return _mm(name, a, w, dims=NN, grid=(s // tm, n // tn), a_spec=pl.BlockSpec((tm, k), lambda i, j: (i, 0)),
               b_spec=pl.BlockSpec(w_block, w_index), o_spec=pl.BlockSpec((tm, tn), col), o_shape=_sds((s, n), out_dtype),
               sem=("parallel", "arbitrary"), **extras)


def mm_nt(name, a, w, w_block, w_index, kout, out_dtype, nk=1, tk=None, tm=1024, token=None):
    s, n = a.shape
    tm = _row_tile(s, tm)
    tk = n if tk is None else tk
    return _mm(name, a, w, dims=NT, grid=(s // tm, nk), a_spec=pl.BlockSpec((tm, tk), lambda i, k: (i, k)),
               b_spec=pl.BlockSpec(w_block, w_index), o_spec=pl.BlockSpec((tm, kout), lambda i, k: (i, 0)),
               o_shape=_sds((s, kout), out_dtype), nk=nk, acc_shape=(tm, kout), token=token)


def mm_tn(name, a, b, *, groups, a_block, a_index, b_block, b_index, o_block, o_index, o_shape, acc_shape,
          colsum=None, tk=GRAD_ROWS):
    s = a.shape[-2]
    tk = _row_tile(s, tk)
    return _mm(name, a, b, dims=TN, grid=(groups, s // tk), a_spec=pl.BlockSpec(a_block(tk), a_index),
               b_spec=pl.BlockSpec(b_block(tk), b_index), o_spec=pl.BlockSpec(o_block, o_index),
               o_shape=_sds(o_shape, BF16), nk=s // tk, acc_shape=acc_shape, colsum=colsum)


FFN_SUB = 256


def _sub_rows(tm):
    sub = FFN_SUB if tm % FFN_SUB == 0 else tm
    return [slice(r * sub, (r + 1) * sub) for r in range(tm // sub)]


def ffn_fwd(u, h, w_up_t, w_down, layer, name, norms=(), token=None, loss=None):
    s, d = u.shape
    tm = _row_tile(s, FFN_ROWS if loss is None else FFN_BWD_ROWS)
    nj = 4
    nn = len(norms)
    nl = 0 if loss is None else 2

    extra = [] if token is None else [token]
    nx = len(extra)

    def body(u_ref, h_ref, wup_ref, wd_ref, *rest):
        gain_refs, loss_in = rest[:nn], rest[nn:nn + nl]
        outs = rest[nn + nl + nx:]
        (hn_ref, gu_ref), un_refs, loss_out = outs[:2], outs[2:2 + nn], outs[2 + nn:]
        i, j = pl.program_id(0), pl.program_id(1)

        @pl.when(j == 0)
        def _():
            hn_ref[...] = h_ref[...]

        for rows in _sub_rows(tm):
            uv = u_ref[rows, :]
            g = _dot(uv, wup_ref[0], NT)
            p = _dot(uv, wup_ref[1], NT)
            gu_ref[0, rows, :] = g.astype(BF16)
            gu_ref[1, rows, :] = p.astype(BF16)
            act = (g * jax.nn.sigmoid(g) * p).astype(BF16)
            hn_ref[rows, :] += _dot(act, wd_ref[...], NN)

        if nn:
            @pl.when(j == nj - 1)
            def _():
                for rows in _sub_rows(tm):
                    for gain_ref, un_ref in zip(gain_refs, un_refs):
                        un_ref[rows, :] = _rms_rows(hn_ref[rows, :], gain_ref[...])

        if nl:
            fgain_ref, target_ref = loss_in
            dh_ref, dgain_ref, lossv_ref = loss_out

            @pl.when((i == 0) & (j == nj - 1))
            def _():
                dgain_ref[...] = jnp.zeros_like(dgain_ref)
                lossv_ref[...] = jnp.zeros_like(lossv_ref)

            @pl.when(j == nj - 1)
            def _():
                gv = fgain_ref[...]
                for rows in _sub_rows(tm):
                    x = hn_ref[rows, :]
                    r = lax.rsqrt(jnp.mean(x * x, axis=-1, keepdims=True) + EPS)
                    xh = x * r
                    err = xh * gv - target_ref[rows, :]
                    lossv_ref[...] += 0.5 * jnp.sum(jnp.mean(err * err, axis=-1, keepdims=True))
                    dy = err * (1.0 / d)
                    dxh = dy * gv
                    dh_ref[rows, :] = r * (dxh - xh * jnp.mean(dxh * xh, axis=-1, keepdims=True))
                    dgain_ref[...] += jnp.sum(dy * xh, axis=0, keepdims=True)

    row = pl.BlockSpec((tm, d), lambda i, j: (i, 0))
    vec = pl.BlockSpec((1, d), lambda i, j: (0, 0))
    loss_ins, loss_specs, loss_shapes, loss_outs = [], [], (), ()
    if nl:
        loss_ins, loss_specs = list(loss), [vec, row]
        loss_shapes = (_sds((s, d), F32), _sds((1, d), F32), _sds((1, BLOCK), F32))
        loss_outs = (row, vec, pl.BlockSpec((1, BLOCK), lambda i, j: (0, 0)))
    return _pcall(
        body, name,
        (_sds((s, d), F32), _sds((2, nj, s, FF_CHUNK), BF16)) + (_sds((s, d), BF16),) * nn + loss_shapes,
        grid=(s // tm, nj),
        in_specs=[row, row,
                  pl.BlockSpec((None, 2, None, FF_CHUNK, d), lambda i, j: (layer, 0, j, 0, 0)),
                  pl.BlockSpec((None, None, FF_CHUNK, d), lambda i, j: (layer, j, 0, 0))] + [vec] * nn + loss_specs
        + [ANY] * nx,
        out_specs=(row, pl.BlockSpec((2, None, tm, FF_CHUNK), lambda i, j: (0, j, i, 0))) + (row,) * nn + loss_outs,
        sem=("arbitrary" if nl else "parallel", "arbitrary"))(u, h, w_up_t, w_down, *norms, *loss_ins, *extra)


def _rms_bwd_rows(x, gain, du, dh_in):
    r = lax.rsqrt(jnp.mean(x * x, axis=-1, keepdims=True) + EPS)
    xh = x * r
    dxh = du * gain
    dx = r * (dxh - xh * jnp.mean(dxh * xh, axis=-1, keepdims=True))
    return dh_in + dx, jnp.sum(du * xh, axis=0, keepdims=True)


def ffn_bwd(dh, gu, w_up_t, w_down, layer, h_mid, gain, name, token=None):
    s, d = dh.shape
    tm = _row_tile(s, FFN_BWD_ROWS)
    nj = 4

    def body(dh_ref, gu_ref, wup_ref, wd_ref, h_ref, gain_ref, *rest):
        dho_ref, dgain_ref, act_ref, dgu_ref, du_ref = rest[-5:]
        i, j = pl.program_id(0), pl.program_id(1)

        @pl.when(j == 0)
        def _():
            du_ref[...] = jnp.zeros_like(du_ref)

        @pl.when((i == 0) & (j == 0))
        def _():
            dgain_ref[...] = jnp.zeros_like(dgain_ref)

        for rows in _sub_rows(tm):
            dact = _dot(dh_ref[rows, :].astype(BF16), wd_ref[...], NT)
            g = gu_ref[0, rows, :].astype(F32)
            p = gu_ref[1, rows, :].astype(F32)
            sig = jax.nn.sigmoid(g)
            sl = g * sig
            act_ref[rows, :] = (sl * p).astype(BF16)
            dp = (dact * sl).astype(BF16)
            dg = (dact * p * (sig * (1.0 + g * (1.0 - sig)))).astype(BF16)
            dgu_ref[0, rows, :] = dg
            dgu_ref[1, rows, :] = dp
            du_ref[rows, :] += _dot(dg, wup_ref[0], NN) + _dot(dp, wup_ref[1], NN)

        @pl.when(j == nj - 1)
        def _():
            for rows in _sub_rows(tm):
                dho, part = _rms_bwd_rows(h_ref[rows, :], gain_ref[...], du_ref[rows, :], dh_ref[rows, :])
                dho_ref[rows, :] = dho
                dgain_ref[...] += part

    row = pl.BlockSpec((tm, d), lambda i, j: (i, 0))
    vec = pl.BlockSpec((1, d), lambda i, j: (0, 0))
    gu_spec = pl.BlockSpec((2, None, tm, FF_CHUNK), lambda i, j: (0, j, i, 0))
    extra = [] if token is None else [token]
    return _pcall(
        body, name,
        (_sds((s, d), F32), _sds((1, d), F32), _sds((nj, s, FF_CHUNK), BF16), _sds((2, nj, s, FF_CHUNK), BF16)),
        grid=(s // tm, nj),
        in_specs=[row, gu_spec,
                  pl.BlockSpec((None, 2, None, FF_CHUNK, d), lambda i, j: (layer, 0, j, 0, 0)),
                  pl.BlockSpec((None, None, FF_CHUNK, d), lambda i, j: (layer, j, 0, 0)), row, vec]
        + [ANY] * len(extra),
        out_specs=(row, vec, pl.BlockSpec((None, tm, FF_CHUNK), lambda i, j: (j, i, 0)), gu_spec),
        scratch=[pltpu.VMEM((tm, d), F32)],
        sem=("arbitrary", "arbitrary"))(dh, gu, w_up_t, w_down, h_mid, gain, *extra)


def nt_rms_bwd(name, a, w, w_block, w_index, h, gain, dh_in):
    s, n = a.shape
    d = h.shape[1]
    tm = _row_tile(s, 1024)

    def body(a_ref, w_ref, h_ref, gain_ref, dhi_ref, dho_ref, dgain_ref):
        i = pl.program_id(0)

        @pl.when(i == 0)
        def _():
            dgain_ref[...] = jnp.zeros_like(dgain_ref)

        du = _dot(a_ref[...].astype(BF16), w_ref[...], NT)
        dho, part = _rms_bwd_rows(h_ref[...], gain_ref[...], du, dhi_ref[...])
        dho_ref[...] = dho
        dgain_ref[...] += part

    row = pl.BlockSpec((tm, d), lambda i: (i, 0))
    vec = pl.BlockSpec((1, d), lambda i: (0, 0))
    return _pcall(body, name, (_sds((s, d), F32), _sds((1, d), F32)), grid=(s // tm,),
                  in_specs=[pl.BlockSpec((tm, n), lambda i: (i, 0)), pl.BlockSpec(w_block, w_index), row, vec, row],
                  out_specs=(row, vec), sem=("arbitrary",))(a, w, h, gain, dh_in)


def pw1_fwd(u, w, b, layer, name):
    s, d = u.shape
    tm = _row_tile(s, FFN_ROWS)
    nb, wb = w.shape[1], w.shape[3]

    def body(u_ref, w_ref, b_ref, t_ref):
        for rows in _sub_rows(tm):
            uv = u_ref[rows, :]
            for j in range(nb):
                cols = slice(j * wb, (j + 1) * wb)
                t_ref[rows, cols] = (_dot(uv, w_ref[j], NN) + b_ref[:, cols]).astype(BF16)

    return _pcall(
        body, name, _sds((s, nb * wb), BF16), grid=(s // tm,),
        in_specs=[pl.BlockSpec((tm, d), lambda i: (i, 0)), pl.BlockSpec((None, nb, d, wb), lambda i: (0, 0, 0, 0)),
                  pl.BlockSpec((None, 1, nb * wb), lambda i: (layer, 0, 0))],
        out_specs=pl.BlockSpec((tm, nb * wb), lambda i: (i, 0)), sem=("parallel",))(u, w, b)


def pw1_bwd(dt, w, h, gain, dh_in, name, token=None):
    s = dt.shape[0]
    nb, d, wb = w.shape[1], w.shape[2], w.shape[3]
    tm = _row_tile(s, 512)

    def body(dt_ref, w_ref, h_ref, gain_ref, dhi_ref, *rest):
        dho_ref, dgain_ref = rest[-2:]
        i = pl.program_id(0)

        @pl.when(i == 0)
        def _():
            dgain_ref[...] = jnp.zeros_like(dgain_ref)

        for rows in _sub_rows(tm):
            du = _dot(dt_ref[rows, 0:wb], w_ref[0], NT)
            for j in range(1, nb):
                du = du + _dot(dt_ref[rows, j * wb:(j + 1) * wb], w_ref[j], NT)
            dho, part = _rms_bwd_rows(h_ref[rows, :], gain_ref[...], du, dhi_ref[rows, :])
            dho_ref[rows, :] = dho
            dgain_ref[...] += part

    row = pl.BlockSpec((tm, d), lambda i: (i, 0))
    vec = pl.BlockSpec((1, d), lambda i: (0, 0))
    extra = [] if token is None else [token]
    return _pcall(
        body, name, (_sds((s, d), F32), _sds((1, d), F32)), grid=(s // tm,),
        in_specs=[pl.BlockSpec((tm, nb * wb), lambda i: (i, 0)),
                  pl.BlockSpec((None, nb, d, wb), lambda i: (0, 0, 0, 0)), row, vec, row] + [ANY] * len(extra),
        out_specs=(row, vec), sem=("arbitrary",))(dt, w, h, gain, dh_in, *extra)


def pw1_grad(u, dt, name):
    s, d = u.shape
    n = dt.shape[1]
    nb = N_DEV
    wb = n // nb
    tk = _row_tile(s, 1024)
    nk = s // tk

    def body(u_ref, dt_ref, g_ref, db_ref, acc_ref):
        k = pl.program_id(0)
        p = _dot(u_ref[...], dt_ref[...], TN)
        cs = jnp.sum(dt_ref[...].astype(F32), axis=0, keepdims=True)

        @pl.when(k == 0)
        def _():
            acc_ref[...] = p
            db_ref[...] = cs

        @pl.when(k > 0)
        def _():
            acc_ref[...] += p
            db_ref[...] += cs

        @pl.when(k == nk - 1)
        def _():
            for j in range(nb):
                g_ref[j] = acc_ref[:, j * wb:(j + 1) * wb].astype(BF16)

    return _pcall(
        body, name, (_sds((nb, d, wb), BF16), _sds((1, n), F32)), grid=(nk,),
        in_specs=[pl.BlockSpec((tk, d), lambda k: (k, 0)), pl.BlockSpec((tk, n), lambda k: (k, 0))],
        out_specs=(pl.BlockSpec((nb, d, wb), lambda k: (0, 0, 0)), pl.BlockSpec((1, n), lambda k: (0, 0))),
        scratch=[pltpu.VMEM((d, n), F32)], sem=("arbitrary",))(u, dt)


def _glu(t):
    t = t.astype(F32)
    return t[:, :D_MODEL] * jax.nn.sigmoid(t[:, D_MODEL:])


CONV_TILE = 256


def _conv_tile(s):
    return CONV_TILE if s % CONV_TILE == 0 else s


CONV_ROWS = 32
CONV_LANES = 512
SUBLANES = 8


def _shifted_copies(sh_ref, rows):
    for b in range(1, SUBLANES):
        sh_ref[b, 0:rows - SUBLANES, :] = sh_ref[0, b:b + rows - SUBLANES, :]


def conv_fwd(t, cp, name):
    s = t.shape[0]
    d = D_MODEL
    ts = _conv_tile(s)
    per = ts // HALO
    rows = HALO + ts
    lead = HALO - (CONV_WIDTH - 1)
    rc = CONV_ROWS

    def body(t_ref, tp_ref, cp_ref, z_ref, y_ref, sh_ref):
        i = pl.program_id(0)
        sh_ref[0, 0:HALO, :] = jnp.where(i > 0, _glu(tp_ref[...]), 0.0)
        sh_ref[0, HALO:rows, :] = _glu(t_ref[...])
        _shifted_copies(sh_ref, rows)

        def chunk(c, carry):
            r0 = pl.multiple_of(c * rc, rc)
            for lc in range(d // CONV_LANES):
                ln = slice(lc * CONV_LANES, (lc + 1) * CONV_LANES)
                acc = jnp.zeros((rc, CONV_LANES), F32) + cp_ref[ROW_BDW:ROW_BDW + 1, ln]
                for k in range(CONV_WIDTH):
                    a8, b = divmod(lead + k, SUBLANES)
                    acc = acc + cp_ref[k:k + 1, ln] * sh_ref[b, pl.ds(r0 + SUBLANES * a8, rc), ln]
                y_ref[pl.ds(r0, rc), ln] = acc
            y = y_ref[pl.ds(r0, rc), :]
            mu = jnp.mean(y, axis=-1, keepdims=True)
            yc = y - mu
            rstd = lax.rsqrt(jnp.mean(yc * yc, axis=-1, keepdims=True) + EPS)
            yn = yc * rstd * cp_ref[ROW_LNG:ROW_LNG + 1, :] + cp_ref[ROW_LNB:ROW_LNB + 1, :]
            z_ref[pl.ds(r0, rc), :] = (yn * jax.nn.sigmoid(yn)).astype(BF16)
            return carry

        lax.fori_loop(0, ts // rc, chunk, 0)

    row = pl.BlockSpec((ts, d), lambda i: (i, 0))
    return _pcall(
        body, name, (_sds((s, d), BF16), _sds((s, d), F32)), grid=(s // ts,),
        in_specs=[pl.BlockSpec((ts, 2 * d), lambda i: (i, 0)),
                  pl.BlockSpec((HALO, 2 * d), lambda i: (jnp.maximum(i * per - 1, 0), 0)),
                  pl.BlockSpec((PACK_ROWS, d), lambda i: (0, 0))],
        out_specs=(row, row),
        scratch=[pltpu.VMEM((SUBLANES, rows, d), F32)], sem=("parallel",))(t, t, cp)


def conv_bwd(t, y, dz, cp, name):
    s = t.shape[0]
    d = D_MODEL
    ts = _conv_tile(s)
    per = ts // HALO
    nt = s // ts
    te = ts + HALO
    rc = CONV_ROWS

    def body(t_ref, y_ref, yn_ref, dz_ref, dzn_ref, cp_ref, dt_ref, st_ref, shd_ref, dw_ref):
        i = pl.program_id(0)
        last = i == nt - 1

        @pl.when(i == 0)
        def _():
            st_ref[...] = jnp.zeros_like(st_ref)
            dw_ref[...] = jnp.zeros_like(dw_ref)

        gain = cp_ref[ROW_LNG:ROW_LNG + 1, :]

        def ln_bwd(yv, dzv):
            mu = jnp.mean(yv, axis=-1, keepdims=True)
            yc = yv - mu
            rstd = lax.rsqrt(jnp.mean(yc * yc, axis=-1, keepdims=True) + EPS)
            yh = yc * rstd
            yn = yh * gain + cp_ref[ROW_LNB:ROW_LNB + 1, :]
            sig = jax.nn.sigmoid(yn)
            dyn = dzv * (sig * (1.0 + yn * (1.0 - sig)))
            dyh = dyn * gain
            dy = rstd * (dyh - jnp.mean(dyh, axis=-1, keepdims=True)
                         - yh * jnp.mean(dyh * yh, axis=-1, keepdims=True))
            return dy, dyn, yh

        def norm_chunk(c, carry):
            r0 = pl.multiple_of(c * rc, rc)
            dy, dyn, yh = ln_bwd(y_ref[pl.ds(r0, rc), :], dz_ref[pl.ds(r0, rc), :])
            shd_ref[0, pl.ds(r0, rc), :] = dy
            st_ref[ROW_BDW:ROW_BDW + 1, :] += jnp.sum(dy, axis=0, keepdims=True)
            st_ref[ROW_LNG:ROW_LNG + 1, :] += jnp.sum(dyn * yh, axis=0, keepdims=True)
            st_ref[ROW_LNB:ROW_LNB + 1, :] += jnp.sum(dyn, axis=0, keepdims=True)
            return carry

        lax.fori_loop(0, ts // rc, norm_chunk, 0)
        dy_halo, _, _ = ln_bwd(yn_ref[...], jnp.where(last, 0.0, dzn_ref[...]))
        shd_ref[0, ts:te, :] = dy_halo
        _shifted_copies(shd_ref, te)

        def tap_chunk(c, carry):
            r0 = pl.multiple_of(c * rc, rc)
            for lc in range(d // CONV_LANES):
                ln = slice(lc * CONV_LANES, (lc + 1) * CONV_LANES)
                ln2 = slice(d + lc * CONV_LANES, d + (lc + 1) * CONV_LANES)
                t1 = t_ref[pl.ds(r0, rc), ln].astype(F32)
                sg = jax.nn.sigmoid(t_ref[pl.ds(r0, rc), ln2].astype(F32))
                a = t1 * sg
                da = jnp.zeros((rc, CONV_LANES), F32)
                for k in range(CONV_WIDTH):
                    a8, b = divmod(CONV_WIDTH - 1 - k, SUBLANES)
                    e = shd_ref[b, pl.ds(r0 + SUBLANES * a8, rc), ln]
                    da = da + cp_ref[k:k + 1, ln] * e
                    dw_ref[k, :, ln] += jnp.sum((a * e).reshape(rc // SUBLANES, SUBLANES, CONV_LANES), axis=0)
                dt_ref[pl.ds(r0, rc), ln] = (da * sg).astype(BF16)
                dt_ref[pl.ds(r0, rc), ln2] = (da * t1 * sg * (1.0 - sg)).astype(BF16)
            return carry

        lax.fori_loop(0, ts // rc, tap_chunk, 0)

        @pl.when(last)
        def _():
            for k in range(CONV_WIDTH):
                st_ref[k:k + 1, :] = jnp.sum(dw_ref[k], axis=0, keepdims=True)

    last_halo = s // HALO - 1
    row = pl.BlockSpec((ts, d), lambda i: (i, 0))
    halo = pl.BlockSpec((HALO, d), lambda i: (jnp.minimum((i + 1) * per, last_halo), 0))
    return _pcall(
        body, name, (_sds((s, 2 * d), BF16), _sds((PACK_ROWS, d), F32)), grid=(nt,),
        in_specs=[pl.BlockSpec((ts, 2 * d), lambda i: (i, 0)), row, halo, row, halo,
                  pl.BlockSpec((PACK_ROWS, d), lambda i: (0, 0))],
        out_specs=(pl.BlockSpec((ts, 2 * d), lambda i: (i, 0)), pl.BlockSpec((PACK_ROWS, d), lambda i: (0, 0))),
        scratch=[pltpu.VMEM((SUBLANES, te, d), F32), pltpu.VMEM((CONV_WIDTH, SUBLANES, d), F32)],
        sem=("arbitrary",))(t, y, y, dz, dz, cp)


def _bucket_table():
    qi = np.arange(BLOCK, dtype=np.int64)[:, None]
    kj = np.arange(2 * BLOCK, dtype=np.int64)[None, :]
    dist = qi + BLOCK - kj
    max_exact = N_BUCKETS // 2
    dd = np.maximum(dist, 0)
    ratio = (np.maximum(dd, 1).astype(np.float32) / np.float32(max_exact)).astype(np.float32)
    log_ratio = (np.log(ratio).astype(np.float32) / np.float32(math.log(MAX_DISTANCE / max_exact))).astype(np.float32)
    large = max_exact + (log_ratio * np.float32(N_BUCKETS - max_exact)).astype(np.int32)
    large = np.minimum(large, N_BUCKETS - 1)
    bucket = np.where(dd < max_exact, dd, large)
    return np.where((dist >= 0) & (dist < BLOCK), bucket, -1).astype(np.int32)


def bias_table(rel_bias, bucket, name):
    def body(rb_ref, bk_ref, o_ref):
        bk = bk_ref[...]
        for h in range(N_HEADS):
            acc = jnp.full((BLOCK, 2 * BLOCK), NEG_INF, F32)
            for b in range(N_BUCKETS):
                acc = jnp.where(bk == b, rb_ref[b, h], acc)
            o_ref[h] = acc

    return _pcall(body, name, _sds((N_HEADS, BLOCK, 2 * BLOCK), F32),
                  in_specs=[pl.BlockSpec(memory_space=pltpu.SMEM), pl.BlockSpec(memory_space=pltpu.VMEM)],
                  out_specs=pl.BlockSpec(memory_space=pltpu.VMEM))(rel_bias, bucket)


def bias_grad(dba, dbb, bucket, name):
    def body(a_ref, b_ref, bk_ref, o_ref):
        bk = bk_ref[...]
        for h in range(N_HEADS):
            db = a_ref[h] + b_ref[h]
            for b in range(N_BUCKETS):
                o_ref[b, h] = jnp.sum(jnp.where(bk == b, db, 0.0))

    vm = pl.BlockSpec(memory_space=pltpu.VMEM)
    return _pcall(body, name, _sds((N_BUCKETS, N_HEADS), F32), in_specs=[vm, vm, vm],
                  out_specs=pl.BlockSpec(memory_space=pltpu.SMEM))(dba, dbb, bucket)


def _band_specs():
    cur = pl.BlockSpec((BLOCK, 2 * KV_DIM), lambda n: (n, 0))
    prev = pl.BlockSpec((BLOCK, 2 * KV_DIM), lambda n: (jnp.maximum(n - 1, 0), 0))
    return cur, prev


def _scores(q_h, k_h, bias_h, first_row, sink):
    sc = _dot(q_h, k_h, NT) * (HEAD_DIM ** -0.5) + bias_h + first_row
    m = jnp.maximum(jnp.max(sc, axis=-1, keepdims=True), sink)
    p = jnp.exp(sc - m)
    e_sink = jnp.exp(sink - m)
    den = jnp.sum(p, axis=-1, keepdims=True) + e_sink
    return p, e_sink, den


def _first_block_row(n):
    col = lax.broadcasted_iota(jnp.int32, (1, 2 * BLOCK), 1)
    return jnp.where((col < BLOCK) & (n == 0), NEG_INF, 0.0)


def _head_lanes(hk, g):
    h = hk * GROUP + g
    return slice(h * HEAD_DIM, (h + 1) * HEAD_DIM)


def _group_rows(x_ref, hk):
    return jnp.concatenate([x_ref[:, _head_lanes(hk, g)] for g in range(GROUP)], axis=0)


def _group_bias(bias_ref, hk):
    return bias_ref[hk * GROUP:(hk + 1) * GROUP].reshape(GROUP * BLOCK, 2 * BLOCK)


def _group_sinks(sink_ref, hk):
    head = lax.broadcasted_iota(jnp.int32, (GROUP * BLOCK, 1), 0) // BLOCK
    col = jnp.zeros((GROUP * BLOCK, 1), F32) + sink_ref[0, hk * GROUP]
    for g in range(1, GROUP):
        col = jnp.where(head == g, sink_ref[0, hk * GROUP + g], col)
    return col


def attn_fwd(q, kv, bias, sinks, name):
    s = q.shape[0]
    nb = s // BLOCK

    def body(sink_ref, q_ref, kvc_ref, kvp_ref, bias_ref, o_ref, band_ref):
        n = pl.program_id(0)
        band_ref[0:BLOCK, :] = kvp_ref[...]
        band_ref[BLOCK:2 * BLOCK, :] = kvc_ref[...]
        first_row = _first_block_row(n)
        for hk in range(N_KV_HEADS):
            k_h = band_ref[:, hk * HEAD_DIM:(hk + 1) * HEAD_DIM]
            v_h = band_ref[:, KV_DIM + hk * HEAD_DIM:KV_DIM + (hk + 1) * HEAD_DIM]
            p, _, den = _scores(_group_rows(q_ref, hk), k_h, _group_bias(bias_ref, hk), first_row,
                                _group_sinks(sink_ref, hk))
            o = _dot((p * (1.0 / den)).astype(BF16), v_h, NN).astype(BF16)
            for g in range(GROUP):
                o_ref[:, _head_lanes(hk, g)] = o[g * BLOCK:(g + 1) * BLOCK]

    cur, prev = _band_specs()
    qs = pl.BlockSpec((BLOCK, D_MODEL), lambda n: (n, 0))
    return _pcall(
        body, name, _sds((s, D_MODEL), BF16), grid=(nb,),
        in_specs=[pl.BlockSpec(memory_space=pltpu.SMEM), qs, cur, prev,
                  pl.BlockSpec((N_HEADS, BLOCK, 2 * BLOCK), lambda n: (0, 0, 0))],
        out_specs=qs, scratch=[pltpu.VMEM((2 * BLOCK, 2 * KV_DIM), BF16)],
        sem=("parallel",))(sinks, q, kv, kv, bias)


def attn_bwd(q, kv, do, bias, sinks, name):
    s = q.shape[0]
    nb = s // BLOCK
    scale = HEAD_DIM ** -0.5

    def body(sink_ref, q_ref, do_ref, kvc_ref, kvp_ref, bias_ref, dq_ref, dkv_ref, db_ref, dsink_ref,
             band_ref, dsacc_ref):
        n = pl.program_id(0)
        band_ref[0:BLOCK, :] = kvp_ref[...]
        band_ref[BLOCK:2 * BLOCK, :] = kvc_ref[...]
        first_row = _first_block_row(n)
        lane = lax.broadcasted_iota(jnp.int32, (BLOCK, BLOCK), 1)

        @pl.when(n == 0)
        def _():
            db_ref[...] = jnp.zeros_like(db_ref)
            dsacc_ref[...] = jnp.zeros_like(dsacc_ref)

        for hk in range(N_KV_HEADS):
            k_h = band_ref[:, hk * HEAD_DIM:(hk + 1) * HEAD_DIM]
            v_h = band_ref[:, KV_DIM + hk * HEAD_DIM:KV_DIM + (hk + 1) * HEAD_DIM]
            q_g = _group_rows(q_ref, hk)
            do_g = _group_rows(do_ref, hk)
            p, e_sink, den = _scores(q_g, k_h, _group_bias(bias_ref, hk), first_row, _group_sinks(sink_ref, hk))
            inv = 1.0 / den
            p = p * inv
            dp = _dot(do_g, v_h, NT)
            delta = jnp.sum(p * dp, axis=-1, keepdims=True)
            ds = p * (dp - delta)
            db_ref[hk * GROUP:(hk + 1) * GROUP] += ds.reshape(GROUP, BLOCK, 2 * BLOCK)
            d_sink = -(e_sink * inv) * delta
            for g in range(GROUP):
                dsacc_ref[...] += jnp.where(lane == hk * GROUP + g, d_sink[g * BLOCK:(g + 1) * BLOCK], 0.0)
            dsb = ds.astype(BF16)
            dq = (_dot(dsb, k_h, NN) * scale).astype(BF16)
            for g in range(GROUP):
                dq_ref[:, _head_lanes(hk, g)] = dq[g * BLOCK:(g + 1) * BLOCK]
            dkv_ref[:, hk * HEAD_DIM:(hk + 1) * HEAD_DIM] = _dot(dsb, q_g, TN) * scale
            dkv_ref[:, KV_DIM + hk * HEAD_DIM:KV_DIM + (hk + 1) * HEAD_DIM] = _dot(p.astype(BF16), do_g, TN)

        @pl.when(n == nb - 1)
        def _():
            dsink_ref[...] = jnp.sum(dsacc_ref[...], axis=0, keepdims=True)

    cur, prev = _band_specs()
    qs = pl.BlockSpec((BLOCK, D_MODEL), lambda n: (n, 0))
    full_b = pl.BlockSpec((N_HEADS, BLOCK, 2 * BLOCK), lambda n: (0, 0, 0))
    return _pcall(
        body, name,
        (_sds((s, D_MODEL), BF16), _sds((nb, 2 * BLOCK, 2 * KV_DIM), F32),
         _sds((N_HEADS, BLOCK, 2 * BLOCK), F32), _sds((1, BLOCK), F32)),
        grid=(nb,),
        in_specs=[pl.BlockSpec(memory_space=pltpu.SMEM), qs, qs, cur, prev, full_b],
        out_specs=(qs, pl.BlockSpec((None, 2 * BLOCK, 2 * KV_DIM), lambda n: (n, 0, 0)), full_b,
                   pl.BlockSpec((1, BLOCK), lambda n: (0, 0))),
        scratch=[pltpu.VMEM((2 * BLOCK, 2 * KV_DIM), BF16), pltpu.VMEM((BLOCK, BLOCK), F32)],
        sem=("arbitrary",))(sinks, q, do, kv, kv, bias)


def dkv_combine(pa, pb, name):
    nb = pa.shape[0]
    pa2 = pa.reshape(2 * nb, BLOCK, 2 * KV_DIM)
    pb2 = pb.reshape(2 * nb, BLOCK, 2 * KV_DIM)

    def body(ac_ref, an_ref, bc_ref, bn_ref, o_ref):
        n = pl.program_id(0)
        nxt = jnp.where(n == nb - 1, 0.0, an_ref[...] + bn_ref[...])
        o_ref[...] = (ac_ref[...] + bc_ref[...] + nxt).astype(BF16)

    cur = pl.BlockSpec((None, BLOCK, 2 * KV_DIM), lambda n: (2 * n + 1, 0, 0))
    nxt = pl.BlockSpec((None, BLOCK, 2 * KV_DIM), lambda n: (jnp.minimum(2 * n + 2, 2 * nb - 2), 0, 0))
    return _pcall(body, name, _sds((nb * BLOCK, 2 * KV_DIM), BF16), grid=(nb,),
                  in_specs=[cur, nxt, cur, nxt], out_specs=pl.BlockSpec((BLOCK, 2 * KV_DIM), lambda n: (n, 0)),
                  sem=("parallel",))(pa2, pa2, pb2, pb2)


def loss_head(h, g, target, name):
    s, d = h.shape
    tm = _row_tile(s, 512)

    def body(h_ref, g_ref, t_ref, dh_ref, dg_ref, loss_ref):
        i = pl.program_id(0)
        x = h_ref[...]
        r = lax.rsqrt(jnp.mean(x * x, axis=-1, keepdims=True) + EPS)
        xh = x * r
        gv = g_ref[...]
        err = xh * gv - t_ref[...]
        part_loss = jnp.zeros((1, BLOCK), F32) + 0.5 * jnp.sum(jnp.mean(err * err, axis=-1, keepdims=True))
        dy = err * (1.0 / d)
        dxh = dy * gv
        dh_ref[...] = r * (dxh - xh * jnp.mean(dxh * xh, axis=-1, keepdims=True))
        part_g = jnp.sum(dy * xh, axis=0, keepdims=True)

        @pl.when(i == 0)
        def _():
            dg_ref[...] = part_g
            loss_ref[...] = part_loss

        @pl.when(i > 0)
        def _():
            dg_ref[...] += part_g
            loss_ref[...] += part_loss

    row = pl.BlockSpec((tm, d), lambda i: (i, 0))
    vec = pl.BlockSpec((1, d), lambda i: (0, 0))
    return _pcall(body, name, (_sds((s, d), F32), _sds((1, d), F32), _sds((1, BLOCK), F32)), grid=(s // tm,),
                  in_specs=[row, vec, row], out_specs=(row, vec, pl.BlockSpec((1, BLOCK), lambda i: (0, 0))),
                  sem=("arbitrary",))(h, g, target)


def adamw(w, m, v, parts, name, token=None):
    nl, r, c = w.shape
    tr = max(t for t in range(1, min(r, 512) + 1) if r % t == 0 and (t % 16 == 0 or t == r))
    c1 = 1.0 / (1.0 - ADAM_B1 ** ADAM_STEP)
    c2 = 1.0 / (1.0 - ADAM_B2 ** ADAM_STEP)

    def body(w_ref, m_ref, v_ref, p_ref, *rest):
        g_ref, d_ref, nm_ref, nv_ref = rest[-4:]
        g = p_ref[0].astype(F32)
        for dev in range(1, N_DEV):
            g = g + p_ref[dev].astype(F32)
        nm = ADAM_B1 * m_ref[...] + (1.0 - ADAM_B1) * g
        nv = ADAM_B2 * v_ref[...] + (1.0 - ADAM_B2) * (g * g)
        g_ref[...] = g
        nm_ref[...] = nm
        nv_ref[...] = nv
        d_ref[...] = -ADAM_LR * ((nm * c1) / (jnp.sqrt(nv * c2) + ADAM_EPS) + ADAM_WD * w_ref[...])

    blk = pl.BlockSpec((None, tr, c), lambda l, i: (l, i, 0))
    out = _sds((nl, r, c), F32)
    extra = [] if token is None else [token]
    return _pcall(body, name, (out, out, out, out), grid=(nl, r // tr),
                  in_specs=[blk, blk, blk, pl.BlockSpec((N_DEV, None, tr, c), lambda l, i: (0, l, i, 0))]
                  + [ANY] * len(extra),
                  out_specs=(blk, blk, blk, blk), sem=("parallel", "parallel"))(w, m, v, parts, *extra)


def _place():
    x, y, c = lax.axis_index("x"), lax.axis_index("y"), lax.axis_index("c")
    return x, y, c


def _lin(px, py, pc):
    return 4 * px + 2 * py + pc


HBM = pl.BlockSpec(memory_space=pltpu.HBM)
SEM = pl.BlockSpec(memory_space=pltpu.SEMAPHORE)
EFFECT = pltpu.SideEffectType.DATAFLOW_SIDE_EFFECTING
N_PEERS = N_DEV - 1


def _peers_of(x, y, c):
    return [(x, y, 1 - c), (1 - x, y, c), (x, 1 - y, c), (1 - x, 1 - y, c),
            (1 - x, y, 1 - c), (x, 1 - y, 1 - c), (1 - x, 1 - y, 1 - c)]


def _in_hbm(a):
    return pltpu.with_memory_space_constraint(a, pltpu.HBM)


def send_start(name, bufs, copies, n_groups):
    nb = len(bufs)
    per_group = [[i for i, cp in enumerate(copies) if cp[0] == g] for g in range(n_groups)]

    def body(*refs):
        buf = refs[:nb]
        sems = refs[nb:nb + 2 * n_groups]
        token = refs[2 * nb + 2 * n_groups]
        x, y, c = _place()
        me = _lin(x, y, c)
        for g in range(n_groups):
            for slot, i in enumerate(per_group[g]):
                _, s, src_slab, d, land_slab = copies[i]
                for k, peer in enumerate(_peers_of(x, y, c)):
                    pltpu.make_async_remote_copy(
                        src_ref=src_slab(buf[s], _lin(*peer), me), dst_ref=land_slab(buf[d], me),
                        send_sem=sems[2 * g].at[slot * N_PEERS + k], recv_sem=sems[2 * g + 1].at[slot * N_PEERS + k],
                        device_id=peer, device_id_type=MESH).start()
        token[...] = jnp.zeros_like(token)

    sem_shapes = []
    for g in range(n_groups):
        sem_shapes += [pltpu.SemaphoreType.DMA((len(per_group[g]) * N_PEERS,))] * 2
    out = pl.pallas_call(
        body, name=name,
        out_shape=tuple(sem_shapes) + tuple(pltpu.HBM(b.shape, b.dtype) for b in bufs) + (_sds((8, 128), F32),),
        in_specs=[HBM] * nb,
        out_specs=tuple([SEM] * len(sem_shapes)) + tuple([HBM] * nb) + (pl.BlockSpec(memory_space=pltpu.VMEM),),
        input_output_aliases={i: len(sem_shapes) + i for i in range(nb)},
        compiler_params=pltpu.CompilerParams(has_side_effects=EFFECT))(*[_in_hbm(b) for b in bufs])
    sems = [(out[2 * g], out[2 * g + 1]) for g in range(n_groups)]
    return sems, list(out[2 * n_groups:2 * n_groups + nb]), out[2 * n_groups + nb]


N_FIRST = 4
N_RELAY = 3


def _gather_peers(x, y, c):
    first = [(x, y, 1 - c), (1 - x, y, c), (x, 1 - y, c), (1 - x, 1 - y, c)]
    return first, first[1:]


def gather_start(name, bufs, copies, n_groups):
    nb = len(bufs)
    per_group = [[i for i, cp in enumerate(copies) if cp[0] == g] for g in range(n_groups)]

    def body(*refs):
        buf = refs[:nb]
        sems = refs[nb:nb + 2 * n_groups]
        token = refs[2 * nb + 2 * n_groups]
        x, y, c = _place()
        me = _lin(x, y, c)
        first, _ = _gather_peers(x, y, c)
        for g in range(n_groups):
            for slot, i in enumerate(per_group[g]):
                _, d, slab = copies[i]
                for k, peer in enumerate(first):
                    pltpu.make_async_remote_copy(
                        src_ref=slab(buf[d], me), dst_ref=slab(buf[d], me),
                        send_sem=sems[2 * g].at[slot * N_FIRST + k], recv_sem=sems[2 * g + 1].at[slot * N_FIRST + k],
                        device_id=peer, device_id_type=MESH).start()
        token[...] = jnp.zeros_like(token)

    sem_shapes = []
    for g in range(n_groups):
        sem_shapes += [pltpu.SemaphoreType.DMA((len(per_group[g]) * N_FIRST,))] * 2
    out = pl.pallas_call(
        body, name=name,
        out_shape=tuple(sem_shapes) + tuple(pltpu.HBM(b.shape, b.dtype) for b in bufs) + (_sds((8, 128), F32),),
        in_specs=[HBM] * nb,
        out_specs=tuple([SEM] * len(sem_shapes)) + tuple([HBM] * nb) + (pl.BlockSpec(memory_space=pltpu.VMEM),),
        input_output_aliases={i: len(sem_shapes) + i for i in range(nb)},
        compiler_params=pltpu.CompilerParams(has_side_effects=EFFECT))(*[_in_hbm(b) for b in bufs])
    return [(out[2 * g], out[2 * g + 1]) for g in range(n_groups)], list(out[2 * n_groups:2 * n_groups + nb])


def gather_relay(name, bufs, slabs, first_sems, after):
    nb = len(bufs)

    def body(*refs):
        buf = refs[:nb]
        send_a, recv_a = refs[nb], refs[nb + 1]
        send_b, recv_b = refs[nb + 3], refs[nb + 4]
        token = refs[2 * nb + 5]
        x, y, c = _place()
        first, origins = _gather_peers(x, y, c)
        for n, slab in enumerate(slabs):
            for j, origin in enumerate(origins):
                block = slab(buf[n], _lin(*origin))
                pltpu.make_async_remote_copy(
                    src_ref=block, dst_ref=block, send_sem=send_a.at[n * N_FIRST + 1 + j],
                    recv_sem=recv_a.at[n * N_FIRST + 1 + j], device_id=origin, device_id_type=MESH).wait_recv()
                pltpu.make_async_remote_copy(
                    src_ref=block, dst_ref=block, send_sem=send_b.at[n * N_RELAY + j],
                    recv_sem=recv_b.at[n * N_RELAY + j], device_id=first[0], device_id_type=MESH).start()
        token[...] = jnp.zeros_like(token)

    sem_shape = pltpu.SemaphoreType.DMA((nb * N_RELAY,))
    out = pl.pallas_call(
        body, name=name,
        out_shape=(sem_shape, sem_shape) + tuple(pltpu.HBM(b.shape, b.dtype) for b in bufs) + (_sds((8, 128), F32),),
        in_specs=[HBM] * nb + [SEM, SEM, ANY],
        out_specs=(SEM, SEM) + tuple([HBM] * nb) + (pl.BlockSpec(memory_space=pltpu.VMEM),),
        input_output_aliases={i: 2 + i for i in range(nb)},
        compiler_params=pltpu.CompilerParams(has_side_effects=EFFECT))(*bufs, first_sems[0], first_sems[1], after)
    return (out[0], out[1]), list(out[2:2 + nb]), out[2 + nb]


def gather_wait(name, bufs, slabs, first_sems, relay_sems, after):
    nb = len(bufs)

    def body(*refs):
        buf = refs[:nb]
        send_a, recv_a, send_b, recv_b = refs[nb:nb + 4]
        x, y, c = _place()
        me = _lin(x, y, c)
        first, origins = _gather_peers(x, y, c)
        sibling = first[0]
        for n, slab in enumerate(slabs):
            mine = slab(buf[n], me)
            for k, peer in enumerate(first):
                pltpu.make_async_remote_copy(
                    src_ref=mine, dst_ref=mine, send_sem=send_a.at[n * N_FIRST + k],
                    recv_sem=recv_a.at[n * N_FIRST + k], device_id=peer, device_id_type=MESH).wait_send()
            theirs = slab(buf[n], _lin(*sibling))
            pltpu.make_async_remote_copy(
                src_ref=theirs, dst_ref=theirs, send_sem=send_a.at[n * N_FIRST], recv_sem=recv_a.at[n * N_FIRST],
                device_id=sibling, device_id_type=MESH).wait_recv()
            for j, (ox, oy, oc) in enumerate(origins):
                sent = slab(buf[n], _lin(ox, oy, oc))
                got = slab(buf[n], _lin(ox, oy, 1 - oc))
                pltpu.make_async_remote_copy(
                    src_ref=sent, dst_ref=got, send_sem=send_b.at[n * N_RELAY + j],
                    recv_sem=recv_b.at[n * N_RELAY + j], device_id=sibling, device_id_type=MESH).wait()

    out = pl.pallas_call(
        body, name=name, out_shape=tuple(pltpu.HBM(b.shape, b.dtype) for b in bufs),
        in_specs=[HBM] * nb + [SEM] * 4 + [ANY], out_specs=tuple([HBM] * nb),
        input_output_aliases={i: i for i in range(nb)},
        compiler_params=pltpu.CompilerParams(has_side_effects=EFFECT))(
            *bufs, first_sems[0], first_sems[1], relay_sems[0], relay_sems[1], after)
    return list(out)


def send_wait(name, bufs, copies, sems, after):
    nb = len(bufs)

    def body(*refs):
        buf = refs[:nb]
        send_sems, recv_sems = refs[nb], refs[nb + 1]
        x, y, c = _place()
        me = _lin(x, y, c)
        for slot, (s, src_slab, d, land_slab) in enumerate(copies):
            for k, peer in enumerate(_peers_of(x, y, c)):
                j = _lin(*peer)
                cp = pltpu.make_async_remote_copy(
                    src_ref=src_slab(buf[s], j, me), dst_ref=land_slab(buf[d], j),
                    send_sem=send_sems.at[slot * N_PEERS + k], recv_sem=recv_sems.at[slot * N_PEERS + k],
                    device_id=peer, device_id_type=MESH)
                cp.wait_send()
                cp.wait_recv()

    out = pl.pallas_call(
        body, name=name, out_shape=tuple(pltpu.HBM(b.shape, b.dtype) for b in bufs),
        in_specs=[HBM] * nb + [SEM, SEM, ANY], out_specs=tuple([HBM] * nb),
        input_output_aliases={i: i for i in range(nb)},
        compiler_params=pltpu.CompilerParams(has_side_effects=EFFECT))(*bufs, sems[0], sems[1], after)
    return list(out)


def local_step(x, target, weights, rep, emit):
    s = x.shape[0]
    bucket = jnp.asarray(_bucket_table())
    bias = bias_table(rep["rel_bias"], bucket, "bias_table")
    h = x
    saved = []
    kv = None
    h_kv = u_kv = None
    small = None
    u = rms_fwd(h, rep["norm_mix"][0:1], "rms_mix_fwd0")
    for l in range(4):
        g_ffn = rep["norm_ffn"][l:l + 1]
        rec = {"h_in": h, "u": u}
        if l < 2:
            w = weights(f"conv{l}", u)
            if l == 0:
                small = w
            cp = small["cp"][l]
            t = pw1_fwd(u, w["pw1"], small["b_pw1"], l, f"pw1_fwd{l}")
            z, y = conv_fwd(t, cp, f"conv_fwd{l}")
            relay_tok = weights(f"ffn{l}", z, relay=True)
            h, uf = mm_nn(f"pw2_fwd{l}", z, w["pw2"], (None, D_MODEL, D_MODEL), lambda i, j: (0, 0, j), D_MODEL,
                          D_MODEL, F32, res=h, norm=g_ffn, token=relay_tok,
                          bias=(small["b_pw2"], pl.BlockSpec((None, 1, D_MODEL), lambda i, j, l=l: (l, 0, j))))
            rec.update(t=t, z=z, y=y, cp=cp)
        else:
            a = l - 2
            w = weights(f"attn{a}", u)
            if a == 0:
                h_kv = h
                w_kv = w["wkv"]
                kv = mm_nn("kv_fwd", u_kv, w_kv, (D_MODEL, 2 * KV_DIM), lambda i, j: (0, 0), 2 * KV_DIM,
                           2 * KV_DIM, BF16)
            q = mm_nn(f"q_fwd{a}", u, w["wq"], (None, D_MODEL, D_MODEL), lambda i, j: (0, 0, j), D_MODEL, D_MODEL,
                      BF16)
            o = attn_fwd(q, kv, bias, rep["sinks"][a:a + 1], f"attn_fwd{a}")
            relay_tok = weights(f"ffn{l}", o, relay=True)
            h, uf = mm_nn(f"o_fwd{a}", o, w["wo"], (None, D_MODEL, D_MODEL), lambda i, j: (0, 0, j), D_MODEL,
                          D_MODEL, F32, res=h, norm=g_ffn, token=relay_tok)
            rec.update(q=q, o=o)
        rec["w"] = w
        rec["h_mid"] = h
        wf = weights(f"ffn{l}", uf)
        relay_tok = weights(("conv1", "attn0", "attn1")[l], uf, relay=True) if l < 3 else None
        nxt = [] if l == 3 else [rep["norm_mix"][l + 1:l + 2]] + ([rep["norm_kv"]] if l == 1 else [])
        last = (rep["norm_final"], target) if l == 3 else None
        h, gu, *normed = ffn_fwd(uf, h, wf["up"], wf["down"], 0, f"ffn_fwd{l}", norms=nxt, token=relay_tok, loss=last)
        if l == 3:
            dh, d_nfin, loss = normed
        else:
            u = normed[0]
        if l == 1:
            u_kv = normed[1]
        rec.update(uf=uf, gu=gu, wf=wf)
        saved.append(rec)

    d_mix, d_ffn = [None] * 4, [None] * 4
    cp_grads = [None, None]
    dkv_parts, dbias_parts, dsinks = [], [], [None, None]
    d_nkv = None
    full_rows = lambda tk: (tk, D_MODEL)
    tok = None
    for l in reversed(range(4)):
        rec = saved[l]
        w, wf = rec["w"], rec["wf"]
        grads = {}
        g_mix = rep["norm_mix"][l:l + 1]
        g_ffn = rep["norm_ffn"][l:l + 1]
        dh_mid, d_ffn[l], act, dgu = ffn_bwd(dh, rec["gu"], wf["up"], wf["down"], 0, rec["h_mid"], g_ffn,
                                             f"ffn_bwd{l}", token=tok)
        g_down = mm_tn(
            f"down_grad{l}", act, dh, groups=4, a_block=lambda tk: (None, tk, FF_CHUNK), a_index=lambda j, k: (j, k, 0),
            b_block=full_rows, b_index=lambda j, k: (k, 0), o_block=(None, FF_CHUNK, D_MODEL),
            o_index=lambda j, k: (j, 0, 0), o_shape=(4, FF_CHUNK, D_MODEL), acc_shape=(FF_CHUNK, D_MODEL), tk=s)
        g_up = mm_tn(
            f"up_grad{l}", dgu.reshape(8, s, FF_CHUNK), rec["uf"], groups=8, a_block=lambda tk: (None, tk, FF_CHUNK),
            a_index=lambda j, k: (j, k, 0), b_block=full_rows, b_index=lambda j, k: (k, 0),
            o_block=(None, FF_CHUNK, D_MODEL), o_index=lambda j, k: (j, 0, 0), o_shape=(8, FF_CHUNK, D_MODEL),
            acc_shape=(FF_CHUNK, D_MODEL), tk=s)
        tok = emit(f"ffn{l}", {"up": g_up, "down": g_down.reshape(D_FF, D_MODEL)})
        dh = dh_mid
        if l < 2:
            dz = mm_nt(f"pw2_bwd{l}", dh, w["pw2"], (None, D_MODEL, D_MODEL), lambda i, k: (0, 0, 0), D_MODEL, F32,
                       token=tok)
            grads["pw2"], db2 = mm_tn(
                f"pw2_grad{l}", rec["z"], dh, groups=1, a_block=full_rows, a_index=lambda j, k: (k, 0),
                b_block=full_rows, b_index=lambda j, k: (k, 0), o_block=(D_MODEL, D_MODEL), o_index=lambda j, k: (0, 0),
                o_shape=(D_MODEL, D_MODEL), acc_shape=(D_MODEL, D_MODEL),
                colsum=((1, D_MODEL), pl.BlockSpec((1, D_MODEL), lambda j, k: (0, 0))))
            dt, stats = conv_bwd(rec["t"], rec["y"], dz, rec["cp"], f"conv_bwd{l}")
            grads["pw1"], db1 = pw1_grad(rec["u"], dt, f"pw1_grad{l}")
            cp_grads[l] = (stats, db2, db1)
            tok = emit(f"conv{l}", grads)
            dh, d_mix[l] = pw1_bwd(dt, w["pw1"], rec["h_in"], g_mix, dh, f"pw1_bwd{l}", token=tok)
        else:
            a = l - 2
            do = mm_nt(f"o_bwd{a}", dh, w["wo"], (None, D_MODEL, D_MODEL), lambda i, k: (0, 0, 0), D_MODEL, BF16,
                       token=tok)
            tok = None
            grads["wo"] = mm_tn(
                f"wo_grad{a}", rec["o"], dh, groups=1, a_block=full_rows, a_index=lambda j, k: (k, 0),
                b_block=full_rows, b_index=lambda j, k: (k, 0), o_block=(D_MODEL, D_MODEL), o_index=lambda j, k: (0, 0),
                o_shape=(D_MODEL, D_MODEL), acc_shape=(D_MODEL, D_MODEL))
            dq, dkv_p, dbias_p, dsinks[a] = attn_bwd(rec["q"], kv, do, bias, rep["sinks"][a:a + 1], f"attn_bwd{a}")
            dkv_parts.append(dkv_p)
            dbias_parts.append(dbias_p)
            grads["wq"] = mm_tn(
                f"wq_grad{a}", rec["u"], dq, groups=1, a_block=full_rows, a_index=lambda j, k: (k, 0),
                b_block=full_rows, b_index=lambda j, k: (k, 0), o_block=(D_MODEL, D_MODEL), o_index=lambda j, k: (0, 0),
                o_shape=(D_MODEL, D_MODEL), acc_shape=(D_MODEL, D_MODEL))
            if a == 1:
                tok = emit("attn1", grads)
            dh, d_mix[l] = nt_rms_bwd(f"q_bwd{a}", dq, w["wq"], (None, D_MODEL, D_MODEL), lambda i: (0, 0, 0),
                                      rec["h_in"], g_mix, dh)
        if l == 2:
            dkv = dkv_combine(dkv_parts[0], dkv_parts[1], "dkv_combine")
            grads["wkv"] = mm_tn(
                "wkv_grad", u_kv, dkv, groups=1, a_block=full_rows, a_index=lambda j, k: (k, 0),
                b_block=lambda tk: (tk, 2 * KV_DIM), b_index=lambda j, k: (k, 0), o_block=(D_MODEL, 2 * KV_DIM),
                o_index=lambda j, k: (0, 0), o_shape=(D_MODEL, 2 * KV_DIM), acc_shape=(D_MODEL, 2 * KV_DIM))
            tok = emit("attn0", grads)
            dh, d_nkv = nt_rms_bwd("kv_bwd", dkv, w_kv, (D_MODEL, 2 * KV_DIM), lambda i: (0, 0), h_kv,
                                   rep["norm_kv"], dh)

    d_relb = bias_grad(dbias_parts[0], dbias_parts[1], bucket, "bias_grad")
    d_sinks = jnp.concatenate([dsinks[0][0, :N_HEADS], dsinks[1][0, :N_HEADS]])
    tail = jnp.zeros((D_MODEL,), F32)
    rep_grad = jnp.concatenate([
        jnp.concatenate(d_mix, axis=0), jnp.concatenate(d_ffn, axis=0), d_nkv, d_nfin,
        tail.at[:2 * N_HEADS].set(d_sinks)[None], tail.at[:N_BUCKETS * N_HEADS].set(d_relb.reshape(-1))[None],
        tail.at[0].set(loss[0, 0])[None], jnp.zeros((REP_ROWS - ROW_LOSS - 1, D_MODEL), F32)], axis=0)
    return dh, cp_grads, rep_grad


def _pack_conv(w_dw, b_dw, ln_g, ln_b, b_pw2, b_pw1):
    rows = [w_dw, b_dw[:, None], ln_g[:, None], ln_b[:, None], b_pw2[:, None], b_pw1.reshape(2, 2, 128),
            jnp.zeros((2, PACK_ROWS - ROW_BPW1 - 2, 128), F32)]
    return jnp.concatenate(rows, axis=1)


def _unpack_conv(p):
    return (p[:, :CONV_WIDTH], p[:, ROW_BDW], p[:, ROW_LNG], p[:, ROW_LNB], p[:, ROW_BPW2],
            p[:, ROW_BPW1:ROW_BPW1 + 2].reshape(2, 256))


def _pack_rep(norm_mix, norm_ffn, norm_kv, norm_final, sinks, rel_bias):
    tail = jnp.zeros((D_MODEL,), F32)
    return jnp.concatenate([
        norm_mix, norm_ffn, norm_kv[None], norm_final[None], tail.at[:2 * N_HEADS].set(sinks.reshape(-1))[None],
        tail.at[:N_BUCKETS * N_HEADS].set(rel_bias.reshape(-1))[None],
        jnp.zeros((REP_ROWS - ROW_RELB - 1, D_MODEL), F32)], axis=0)


def _unpack_rep(p):
    return (p[0:4], p[4:8], p[ROW_NKV], p[ROW_NFIN], p[ROW_SINK, :2 * N_HEADS].reshape(2, N_HEADS),
            p[ROW_RELB, :N_BUCKETS * N_HEADS].reshape(N_BUCKETS, N_HEADS))


def kernel(x, norm_mix, norm_ffn, conv_w_pw1, conv_b_pw1, conv_w_dw, conv_b_dw, conv_ln_g, conv_ln_b, conv_w_pw2, conv_b_pw2, norm_kv, w_kv, w_q, w_o, sinks, rel_bias, ffn_w_up, ffn_w_down, norm_final, loss_target, m_norm_mix, m_norm_ffn, m_conv_w_pw1, m_conv_b_pw1, m_conv_w_dw, m_conv_b_dw, m_conv_ln_g, m_conv_ln_b, m_conv_w_pw2, m_conv_b_pw2, m_norm_kv, m_w_kv, m_w_q, m_w_o, m_sinks, m_rel_bias, m_ffn_w_up, m_ffn_w_down, m_norm_final, v_norm_mix, v_norm_ffn, v_conv_w_pw1, v_conv_b_pw1, v_conv_w_dw, v_conv_b_dw, v_conv_ln_g, v_conv_ln_b, v_conv_w_pw2, v_conv_b_pw2, v_norm_kv, v_w_kv, v_w_q, v_w_o, v_sinks, v_rel_bias, v_ffn_w_up, v_ffn_w_down, v_norm_final):
    s = x.shape[1]
    d = D_MODEL
    rsh = d // N_DEV
    dsh = D_FF // N_DEV

    me = _lin(*_place())
    lead_slab = lambda ref, j: ref.at[j]
    rows_of = lambda rows: (lambda ref, j: ref.at[pl.ds(j * rows, rows), :])

    conv_pack = _pack_conv(conv_w_dw, conv_b_dw, conv_ln_g, conv_ln_b, conv_b_pw2, conv_b_pw1)
    ag_order = ["conv0", "ffn0", "conv1", "ffn1", "attn0", "ffn2", "attn1", "ffn3"]
    ag_land, ag_copies, ag_members = [], [], {g: [] for g in ag_order}

    def gather(group, key, shard, land_shape, at, land_slab):
        i = len(ag_land)
        ag_land.append(lax.dynamic_update_slice(lax.empty(land_shape, shard.dtype), shard, at(me)))
        ag_copies.append((ag_order.index(group), i, land_slab))
        ag_members[group].append((key, i, land_slab))

    cols_at0 = lambda ref, j: ref.at[0, j]
    rows_at0 = lambda rows: (lambda ref, j: ref.at[0, pl.ds(j * rows, rows), :])
    col_at = lambda m: (0, m, 0, 0)
    row_at = lambda rows: (lambda m: (0, m * rows, 0))
    for l in range(2):
        gather(f"conv{l}", "pw1", conv_w_pw1[l].astype(BF16)[None, None], (1, N_DEV, d, 256), col_at, cols_at0)
        gather(f"conv{l}", "pw2", conv_w_pw2[l].astype(BF16)[None], (1, d, d), row_at(rsh), rows_at0(rsh))
    gather("conv0", "pack", conv_pack[None], (N_DEV, 2, PACK_ROWS, 128), lambda m: (m, 0, 0, 0), lead_slab)
    gather("attn0", "wkv", w_kv.astype(BF16), (d, 2 * KV_DIM), lambda m: (m * rsh, 0), rows_of(rsh))
    for a in range(2):
        gather(f"attn{a}", "wq", w_q[a].astype(BF16)[None], (1, d, d), row_at(rsh), rows_at0(rsh))
        gather(f"attn{a}", "wo", w_o[a].astype(BF16)[None], (1, d, d), row_at(rsh), rows_at0(rsh))
    up_t, m_up_t, v_up_t = (jnp.swapaxes(a, 1, 2) for a in (ffn_w_up, m_ffn_w_up, v_ffn_w_up))
    for l in range(4):
        gather(f"ffn{l}", "up", up_t[l].astype(BF16)[None, None], (1, N_DEV, FF_CHUNK, d), col_at, cols_at0)
        gather(f"ffn{l}", "down", ffn_w_down[l].astype(BF16)[None], (1, D_FF, d), row_at(dsh), rows_at0(dsh))
    ag_sems, ag_land_thru = gather_start("ag_start", ag_land, ag_copies, len(ag_order))
    relayed = {}

    def weights(group, after, relay=False):
        members = ag_members[group]
        slabs = [slab for _, _, slab in members]
        first_sems = ag_sems[ag_order.index(group)]
        if group not in relayed:
            relayed[group] = gather_relay(f"ag_relay_{group}", [ag_land_thru[i] for _, i, _ in members], slabs,
                                          first_sems, after)
        relay_sems, bufs, token = relayed[group]
        if relay:
            return token
        lands = gather_wait(f"ag_wait_{group}", bufs, slabs, first_sems, relay_sems, after)
        w = {key: land for (key, _, _), land in zip(members, lands)}
        if "up" in w:
            w["up"] = w["up"].reshape(1, 2, 4, FF_CHUNK, d)
            w["down"] = w["down"].reshape(1, 4, FF_CHUNK, d)
        if "pack" in w:
            pack_g = w.pop("pack")
            w["cp"] = jnp.transpose(pack_g, (1, 2, 0, 3)).reshape(2, PACK_ROWS, d)
            w["b_pw1"] = pack_g[:, :, ROW_BPW1:ROW_BPW1 + 2, :].transpose(1, 0, 2, 3).reshape(2, 1, 2 * d)
            w["b_pw2"] = w["cp"][:, ROW_BPW2:ROW_BPW2 + 1, :]
        return w

    shard_shapes = {"pw1": (d, 256), "pw2": (rsh, d), "wkv": (rsh, 2 * KV_DIM), "wq": (rsh, d), "wo": (rsh, d),
                    "up": (FF_CHUNK, d), "down": (dsh, d), "cp": (2, PACK_ROWS, 128), "rep": (REP_ROWS, d)}
    n_layers = {"pw1": 2, "pw2": 2, "wkv": 1, "wq": 2, "wo": 2, "up": 4, "down": 4, "cp": 1, "rep": 1}
    by_lead = (lambda ref, j, me_: ref.at[j], lambda g: lax.dynamic_index_in_dim(g, me, 0, keepdims=False))
    by_rows = lambda rows: (lambda ref, j, me_: ref.at[pl.ds(j * rows, rows), :],
                            lambda g: lax.dynamic_slice_in_dim(g, me * rows, rows, 0))
    all_of = (lambda ref, j, me_: ref, lambda g: g)
    owned = {"pw1": by_lead, "pw2": by_rows(rsh), "wkv": by_rows(rsh), "wq": by_rows(rsh), "wo": by_rows(rsh),
             "up": by_lead, "down": by_rows(dsh), "cp": by_lead, "rep": all_of}
    parts = {}
    pending = {}

    def finish(chain, after):
        keys, bufs, copies, sems, name = pending.pop(chain)
        done = send_wait(f"rs_wait_{name}", bufs, copies, sems, after)
        parts.update(zip(keys, done[len(keys):]))

    def exchange(chain, name, layer, grads):
        keys = list(grads)
        if chain in pending:
            finish(chain, grads[keys[0]])
        lands = []
        for k in keys:
            land = parts.pop(k) if k in parts else lax.empty((N_DEV, n_layers[k]) + shard_shapes[k], grads[k].dtype)
            mine = owned[k][1](grads[k])[None, None]
            lands.append(lax.dynamic_update_slice(land, mine, (me, layer) + (0,) * len(shard_shapes[k])))
        land_at = lambda ref, i: ref.at[i, layer]
        copies = [(0, n, owned[k][0], len(keys) + n, land_at) for n, k in enumerate(keys)]
        sems, thru, token = send_start(f"rs_start_{name}", [grads[k] for k in keys] + lands, copies, 1)
        pending[chain] = (keys, thru, [c[1:] for c in copies], sems[0], name)
        return token

    def emit(group, grads):
        return exchange(group[:-1], group, int(group[-1]), grads)

    rep = {"norm_mix": norm_mix, "norm_ffn": norm_ffn, "norm_kv": norm_kv[None], "norm_final": norm_final[None],
           "sinks": sinks, "rel_bias": rel_bias}

    grad_x, cp_grads, rep_grad = local_step(x[0], loss_target[0], weights, rep, emit)

    cp_full = []
    for l in range(2):
        stats, db2, db1 = cp_grads[l]
        cp_full.append(jnp.concatenate([
            stats[:ROW_BPW2], db2, db1.reshape(N_DEV, 2, 128).transpose(1, 0, 2).reshape(2, d),
            jnp.zeros((PACK_ROWS - ROW_BPW1 - 2, d), F32)], axis=0))
    cp_send = jnp.stack(cp_full).reshape(2, PACK_ROWS, N_DEV, 128).transpose(2, 0, 1, 3)
    tail_token = exchange("tail", "tail", 0, {"cp": cp_send, "rep": rep_grad})

    def update(key, w, m, v, name, token=None):
        p = parts[key]
        w3 = w.reshape(p.shape[1:])
        outs = adamw(w3, m.reshape(w3.shape), v.reshape(w3.shape), p, name, token=token)
        return [o.reshape(w.shape) for o in outs]

    res = {}
    finish("ffn", grad_x)
    up_res = update("up", up_t, m_up_t, v_up_t, "adam_up", tail_token)
    res["ffn_w_up"] = [jnp.swapaxes(o, 1, 2) for o in up_res]
    res["ffn_w_down"] = update("down", ffn_w_down, m_ffn_w_down, v_ffn_w_down, "adam_down", up_res[0])
    finish("attn", res["ffn_w_down"][0])
    res["w_kv"] = update("wkv", w_kv, m_w_kv, v_w_kv, "adam_wkv")
    res["w_q"] = update("wq", w_q, m_w_q, v_w_q, "adam_wq")
    res["w_o"] = update("wo", w_o, m_w_o, v_w_o, "adam_wo")
    finish("conv", res["w_o"][0])
    res["conv_w_pw1"] = update("pw1", conv_w_pw1, m_conv_w_pw1, v_conv_w_pw1, "adam_pw1")
    res["conv_w_pw2"] = update("pw2", conv_w_pw2, m_conv_w_pw2, v_conv_w_pw2, "adam_pw2")
    finish("tail", res["conv_w_pw2"][0])
    m_pack = _pack_conv(m_conv_w_dw, m_conv_b_dw, m_conv_ln_g, m_conv_ln_b, m_conv_b_pw2, m_conv_b_pw1)
    v_pack = _pack_conv(v_conv_w_dw, v_conv_b_dw, v_conv_ln_g, v_conv_ln_b, v_conv_b_pw2, v_conv_b_pw1)
    cp_res = adamw(conv_pack, m_pack, v_pack, parts["cp"].reshape(N_DEV, 2, PACK_ROWS, 128), "adam_conv_pack")
    rep_w = _pack_rep(norm_mix, norm_ffn, norm_kv, norm_final, sinks, rel_bias)
    rep_m = _pack_rep(m_norm_mix, m_norm_ffn, m_norm_kv, m_norm_final, m_sinks, m_rel_bias)
    rep_v = _pack_rep(v_norm_mix, v_norm_ffn, v_norm_kv, v_norm_final, v_sinks, v_rel_bias)
    rep_res = adamw(rep_w[None], rep_m[None], rep_v[None], parts["rep"], "adam_rep")
    loss = rep_res[0][0, ROW_LOSS, 0]

    outs = []
    for kind in range(4):
        cw_dw, cb_dw, cln_g, cln_b, cb_pw2, cb_pw1 = _unpack_conv(cp_res[kind])
        r_mix, r_ffn, r_nkv, r_nfin, r_sinks, r_relb = _unpack_rep(rep_res[kind][0])
        outs += [r_mix, r_ffn, res["conv_w_pw1"][kind], cb_pw1, cw_dw, cb_dw, cln_g, cln_b, res["conv_w_pw2"][kind],
                 cb_pw2, r_nkv, res["w_kv"][kind], res["w_q"][kind], res["w_o"][kind], r_sinks, r_relb,
                 res["ffn_w_up"][kind], res["ffn_w_down"][kind], r_nfin]
    return (loss, grad_x[None], *outs)
```

```python
import functools
import math

import numpy as np
import jax
import jax.numpy as jnp
from jax import lax
from jax.experimental import pallas as pl
from jax.experimental.pallas import tpu as pltpu

F32 = jnp.float32
BF16 = jnp.bfloat16

D_MODEL = 1024
D_FF = 2816
N_HEADS = 16
N_KV_HEADS = 4
GROUP = N_HEADS // N_KV_HEADS
HEAD_DIM = 64
KV_DIM = N_KV_HEADS * HEAD_DIM
BLOCK = 128
CONV_WIDTH = 31
HALO = 32
N_BUCKETS = 32
MAX_DISTANCE = 128
EPS = 1e-6
NEG_INF = -1e30
N_DEV = 8
FF_CHUNK = D_FF // 4
PACK_ROWS = 40
ROW_BDW, ROW_LNG, ROW_LNB, ROW_BPW2, ROW_BPW1 = 31, 32, 33, 34, 35
REP_ROWS = 16
ROW_NKV, ROW_NFIN, ROW_SINK, ROW_RELB, ROW_LOSS = 8, 9, 10, 11, 12

ADAM_LR, ADAM_B1, ADAM_B2, ADAM_EPS, ADAM_WD, ADAM_STEP = 0.001, 0.9, 0.999, 1e-08, 0.01, 10

VMEM_LIMIT_BYTES = 56 * 1024 * 1024
FFN_ROWS = 1024
FFN_BWD_ROWS = 512
GRAD_ROWS = 1024
ANY = pl.BlockSpec(memory_space=pl.ANY)
MESH = pl.DeviceIdType.MESH

NN = (((1,), (0,)), ((), ()))
NT = (((1,), (1,)), ((), ()))
TN = (((0,), (0,)), ((), ()))


def _dot(a, b, dims):
    return lax.dot_general(a, b, dims, preferred_element_type=F32)


def _pcall(body, name, out_shape, *, grid=None, in_specs=None, out_specs=None, scratch=(), sem=None, **kw):
    params = pltpu.CompilerParams(dimension_semantics=sem, vmem_limit_bytes=VMEM_LIMIT_BYTES)
    extra = {} if grid is None else {"grid": grid}
    return pl.pallas_call(body, name=name, out_shape=out_shape, in_specs=in_specs, out_specs=out_specs,
                          scratch_shapes=list(scratch), compiler_params=params, **extra, **kw)


def _sds(shape, dtype):
    return jax.ShapeDtypeStruct(tuple(shape), dtype)


def _row_tile(s, want):
    return want if s % want == 0 else s


def rms_fwd(h, g, name):
    s, d = h.shape
    tm = _row_tile(s, 512)

    def body(h_ref, g_ref, u_ref):
        x = h_ref[...]
        r = lax.rsqrt(jnp.mean(x * x, axis=-1, keepdims=True) + EPS)
        u_ref[...] = (x * r * g_ref[...]).astype(BF16)

    return _pcall(body, name, _sds((s, d), BF16), grid=(s // tm,),
                  in_specs=[pl.BlockSpec((tm, d), lambda i: (i, 0)), pl.BlockSpec((1, d), lambda i: (0, 0))],
                  out_specs=pl.BlockSpec((tm, d), lambda i: (i, 0)), sem=("parallel",))(h, g)


def _rms_rows(x, gain):
    return (x * lax.rsqrt(jnp.mean(x * x, axis=-1, keepdims=True) + EPS) * gain).astype(BF16)


def _mm(name, a, b, *, dims, grid, a_spec, b_spec, o_spec, o_shape, nk=1, acc_shape=None,
        bias=None, res=None, colsum=None, sem=None, token=None, norm=None):
    n_axes = len(grid)

    def body(*refs):
        it = iter(refs)
        a_ref, b_ref = next(it), next(it)
        bias_ref = next(it) if bias is not None else None
        res_ref = next(it) if res is not None else None
        gain_ref = next(it) if norm is not None else None
        if token is not None:
            next(it)
        o_ref = next(it)
        un_ref = next(it) if norm is not None else None
        cs_ref = next(it) if colsum is not None else None
        acc_ref = next(it) if nk > 1 else None
        k = pl.program_id(n_axes - 1)
        p = _dot(a_ref[...].astype(BF16), b_ref[...].astype(BF16), dims)

        def finish(acc):
            if bias_ref is not None:
                acc = acc + bias_ref[...]
            if res_ref is not None:
                acc = acc + res_ref[...]
            o_ref[...] = acc.astype(o_ref.dtype)
            if un_ref is not None:
                un_ref[...] = _rms_rows(acc, gain_ref[...])

        if nk == 1:
            finish(p)
        else:
            @pl.when(k == 0)
            def _():
                acc_ref[...] = p

            @pl.when(k > 0)
            def _():
                acc_ref[...] += p

            @pl.when(k == nk - 1)
            def _():
                finish(acc_ref[...])

        if cs_ref is not None:
            cs = jnp.sum(b_ref[...].astype(F32), axis=0, keepdims=True)

            @pl.when(k == 0)
            def _():
                cs_ref[...] = cs

            @pl.when(k > 0)
            def _():
                cs_ref[...] += cs

    ins, in_specs = [a, b], [a_spec, b_spec]
    gain = None if norm is None else (norm, pl.BlockSpec(norm.shape, lambda *_: (0,) * norm.ndim))
    for extra in (bias, res, gain, None if token is None else (token, ANY)):
        if extra is not None:
            ins.append(extra[0])
            in_specs.append(extra[1])
    out_shape, out_specs = o_shape, o_spec
    if norm is not None:
        out_shape, out_specs = (o_shape, _sds(o_shape.shape, BF16)), (o_spec, o_spec)
    if colsum is not None:
        out_shape, out_specs = (o_shape, _sds(colsum[0], F32)), (o_spec, colsum[1])
    scratch = [pltpu.VMEM(acc_shape, F32)] if nk > 1 else []
    if sem is None:
        sem = ("parallel",) * (n_axes - 1) + ("arbitrary",)
    return _pcall(body, name, out_shape, grid=grid, in_specs=in_specs, out_specs=out_specs, scratch=scratch,
                  sem=sem)(*ins)


def mm_nn(name, a, w, w_block, w_index, n, tn, out_dtype, bias=None, res=None, tm=1024, norm=None, token=None):
    s, k = a.shape
    tm = _row_tile(s, tm)
    col = lambda i, j: (i, j)
    extras = {"token": token}
    if bias is not None:
        extras["bias"] = bias
    if res is not None:
        extras["res"] = (res, pl.BlockSpec((tm, tn), col))
    if norm is not None:
        assert tn == n, "a fused norm needs whole rows"
        extras["norm"] = norm
    return _mm(name, a, w, dims=NN, grid=(s // tm, n // tn), a_spec=pl.BlockSpec((tm, k), lambda i, j: (i, 0)),
               b_spec=pl.BlockSpec(w_block, w_index), o_spec=pl.BlockSpec((tm, tn), col), o_shape=_sds((s, n), out_dtype),
               sem=("parallel", "arbitrary"), **extras)


def mm_nt(name, a, w, w_block, w_index, kout, out_dtype, nk=1, tk=None, tm=1024, token=None):
    s, n = a.shape
    tm = _row_tile(s, tm)
    tk = n if tk is None else tk
    return _mm(name, a, w, dims=NT, grid=(s // tm, nk), a_spec=pl.BlockSpec((tm, tk), lambda i, k: (i, k)),
               b_spec=pl.BlockSpec(w_block, w_index), o_spec=pl.BlockSpec((tm, kout), lambda i, k: (i, 0)),
               o_shape=_sds((s, kout), out_dtype), nk=nk, acc_shape=(tm, kout), token=token)


def mm_tn(name, a, b, *, groups, a_block, a_index, b_block, b_index, o_block, o_index, o_shape, acc_shape,
          colsum=None, tk=GRAD_ROWS):
    s = a.shape[-2]
    tk = _row_tile(s, tk)
    return _mm(name, a, b, dims=TN, grid=(groups, s // tk), a_spec=pl.BlockSpec(a_block(tk), a_index),
               b_spec=pl.BlockSpec(b_block(tk), b_index), o_spec=pl.BlockSpec(o_block, o_index),
               o_shape=_sds(o_shape, BF16), nk=s // tk, acc_shape=acc_shape, colsum=colsum)


FFN_SUB = 256


def _sub_rows(tm):
    sub = FFN_SUB if tm % FFN_SUB == 0 else tm
    return [slice(r * sub, (r + 1) * sub) for r in range(tm // sub)]


def ffn_fwd(u, h, w_up_t, w_down, layer, name, norms=(), token=None):
    s, d = u.shape
    tm = _row_tile(s, FFN_ROWS)
    nj = 4
    nn = len(norms)

    extra = [] if token is None else [token]
    nx = len(extra)

    def body(u_ref, h_ref, wup_ref, wd_ref, *rest):
        gain_refs, (hn_ref, gu_ref), un_refs = rest[:nn], rest[nn + nx:nn + nx + 2], rest[nn + nx + 2:]
        j = pl.program_id(1)

        @pl.when(j == 0)
        def _():
            hn_ref[...] = h_ref[...]

        for rows in _sub_rows(tm):
            uv = u_ref[rows, :]
            g = _dot(uv, wup_ref[0], NT)
            p = _dot(uv, wup_ref[1], NT)
            gu_ref[0, rows, :] = g.astype(BF16)
            gu_ref[1, rows, :] = p.astype(BF16)
            act = (g * jax.nn.sigmoid(g) * p).astype(BF16)
            hn_ref[rows, :] += _dot(act, wd_ref[...], NN)

        if nn:
            @pl.when(j == nj - 1)
            def _():
                for rows in _sub_rows(tm):
                    for gain_ref, un_ref in zip(gain_refs, un_refs):
                        un_ref[rows, :] = _rms_rows(hn_ref[rows, :], gain_ref[...])

    row = pl.BlockSpec((tm, d), lambda i, j: (i, 0))
    vec = pl.BlockSpec((1, d), lambda i, j: (0, 0))
    return _pcall(
        body, name, (_sds((s, d), F32), _sds((2, nj, s, FF_CHUNK), BF16)) + (_sds((s, d), BF16),) * nn,
        grid=(s // tm, nj),
        in_specs=[row, row,
                  pl.BlockSpec((None, 2, None, FF_CHUNK, d), lambda i, j: (layer, 0, j, 0, 0)),
                  pl.BlockSpec((None, None, FF_CHUNK, d), lambda i, j: (layer, j, 0, 0))] + [vec] * nn + [ANY] * nx,
        out_specs=(row, pl.BlockSpec((2, None, tm, FF_CHUNK), lambda i, j: (0, j, i, 0))) + (row,) * nn,
        sem=("parallel", "arbitrary"))(u, h, w_up_t, w_down, *norms, *extra)


def _rms_bwd_rows(x, gain, du, dh_in):
    r = lax.rsqrt(jnp.mean(x * x, axis=-1, keepdims=True) + EPS)
    xh = x * r
    dxh = du * gain
    dx = r * (dxh - xh * jnp.mean(dxh * xh, axis=-1, keepdims=True))
    return dh_in + dx, jnp.sum(du * xh, axis=0, keepdims=True)


def ffn_bwd(dh, gu, w_up_t, w_down, layer, h_mid, gain, name, token=None):
    s, d = dh.shape
    tm = _row_tile(s, FFN_BWD_ROWS)
    nj = 4

    def body(dh_ref, gu_ref, wup_ref, wd_ref, h_ref, gain_ref, *rest):
        dho_ref, dgain_ref, act_ref, dgu_ref, du_ref = rest[-5:]
        i, j = pl.program_id(0), pl.program_id(1)

        @pl.when(j == 0)
        def _():
            du_ref[...] = jnp.zeros_like(du_ref)

        @pl.when((i == 0) & (j == 0))
        def _():
            dgain_ref[...] = jnp.zeros_like(dgain_ref)

        for rows in _sub_rows(tm):
            dact = _dot(dh_ref[rows, :].astype(BF16), wd_ref[...], NT)
            g = gu_ref[0, rows, :].astype(F32)
            p = gu_ref[1, rows, :].astype(F32)
            sig = jax.nn.sigmoid(g)
            sl = g * sig
            act_ref[rows, :] = (sl * p).astype(BF16)
            dp = (dact * sl).astype(BF16)
            dg = (dact * p * (sig * (1.0 + g * (1.0 - sig)))).astype(BF16)
            dgu_ref[0, rows, :] = dg
            dgu_ref[1, rows, :] = dp
            du_ref[rows, :] += _dot(dg, wup_ref[0], NN) + _dot(dp, wup_ref[1], NN)

        @pl.when(j == nj - 1)
        def _():
            for rows in _sub_rows(tm):
                dho, part = _rms_bwd_rows(h_ref[rows, :], gain_ref[...], du_ref[rows, :], dh_ref[rows, :])
                dho_ref[rows, :] = dho
                dgain_ref[...] += part

    row = pl.BlockSpec((tm, d), lambda i, j: (i, 0))
    vec = pl.BlockSpec((1, d), lambda i, j: (0, 0))
    gu_spec = pl.BlockSpec((2, None, tm, FF_CHUNK), lambda i, j: (0, j, i, 0))
    extra = [] if token is None else [token]
    return _pcall(
        body, name,
        (_sds((s, d), F32), _sds((1, d), F32), _sds((nj, s, FF_CHUNK), BF16), _sds((2, nj, s, FF_CHUNK), BF16)),
        grid=(s // tm, nj),
        in_specs=[row, gu_spec,
                  pl.BlockSpec((None, 2, None, FF_CHUNK, d), lambda i, j: (layer, 0, j, 0, 0)),
                  pl.BlockSpec((None, None, FF_CHUNK, d), lambda i, j: (layer, j, 0, 0)), row, vec]
        + [ANY] * len(extra),
        out_specs=(row, vec, pl.BlockSpec((None, tm, FF_CHUNK), lambda i, j: (j, i, 0)), gu_spec),
        scratch=[pltpu.VMEM((tm, d), F32)],
        sem=("arbitrary", "arbitrary"))(dh, gu, w_up_t, w_down, h_mid, gain, *extra)


def nt_rms_bwd(name, a, w, w_block, w_index, h, gain, dh_in):
    s, n = a.shape
    d = h.shape[1]
    tm = _row_tile(s, 1024)

    def body(a_ref, w_ref, h_ref, gain_ref, dhi_ref, dho_ref, dgain_ref):
        i = pl.program_id(0)

        @pl.when(i == 0)
        def _():
            dgain_ref[...] = jnp.zeros_like(dgain_ref)

        du = _dot(a_ref[...].astype(BF16), w_ref[...], NT)
        dho, part = _rms_bwd_rows(h_ref[...], gain_ref[...], du, dhi_ref[...])
        dho_ref[...] = dho
        dgain_ref[...] += part

    row = pl.BlockSpec((tm, d), lambda i: (i, 0))
    vec = pl.BlockSpec((1, d), lambda i: (0, 0))
    return _pcall(body, name, (_sds((s, d), F32), _sds((1, d), F32)), grid=(s // tm,),
                  in_specs=[pl.BlockSpec((tm, n), lambda i: (i, 0)), pl.BlockSpec(w_block, w_index), row, vec, row],
                  out_specs=(row, vec), sem=("arbitrary",))(a, w, h, gain, dh_in)


def pw1_fwd(u, w, b, layer, name):
    s, d = u.shape
    tm = _row_tile(s, FFN_ROWS)
    nb, wb = w.shape[1], w.shape[3]

    def body(u_ref, w_ref, b_ref, t_ref):
        for rows in _sub_rows(tm):
            uv = u_ref[rows, :]
            for j in range(nb):
                cols = slice(j * wb, (j + 1) * wb)
                t_ref[rows, cols] = (_dot(uv, w_ref[j], NN) + b_ref[:, cols]).astype(BF16)

    return _pcall(
        body, name, _sds((s, nb * wb), BF16), grid=(s // tm,),
        in_specs=[pl.BlockSpec((tm, d), lambda i: (i, 0)), pl.BlockSpec((None, nb, d, wb), lambda i: (0, 0, 0, 0)),
                  pl.BlockSpec((None, 1, nb * wb), lambda i: (layer, 0, 0))],
        out_specs=pl.BlockSpec((tm, nb * wb), lambda i: (i, 0)), sem=("parallel",))(u, w, b)


def pw1_bwd(dt, w, h, gain, dh_in, name, token=None):
    s = dt.shape[0]
    nb, d, wb = w.shape[1], w.shape[2], w.shape[3]
    tm = _row_tile(s, 512)

    def body(dt_ref, w_ref, h_ref, gain_ref, dhi_ref, *rest):
        dho_ref, dgain_ref = rest[-2:]
        i = pl.program_id(0)

        @pl.when(i == 0)
        def _():
            dgain_ref[...] = jnp.zeros_like(dgain_ref)

        for rows in _sub_rows(tm):
            du = _dot(dt_ref[rows, 0:wb], w_ref[0], NT)
            for j in range(1, nb):
                du = du + _dot(dt_ref[rows, j * wb:(j + 1) * wb], w_ref[j], NT)
            dho, part = _rms_bwd_rows(h_ref[rows, :], gain_ref[...], du, dhi_ref[rows, :])
            dho_ref[rows, :] = dho
            dgain_ref[...] += part

    row = pl.BlockSpec((tm, d), lambda i: (i, 0))
    vec = pl.BlockSpec((1, d), lambda i: (0, 0))
    extra = [] if token is None else [token]
    return _pcall(
        body, name, (_sds((s, d), F32), _sds((1, d), F32)), grid=(s // tm,),
        in_specs=[pl.BlockSpec((tm, nb * wb), lambda i: (i, 0)),
                  pl.BlockSpec((None, nb, d, wb), lambda i: (0, 0, 0, 0)), row, vec, row] + [ANY] * len(extra),
        out_specs=(row, vec), sem=("arbitrary",))(dt, w, h, gain, dh_in, *extra)


def pw1_grad(u, dt, name):
    s, d = u.shape
    n = dt.shape[1]
    nb = N_DEV
    wb = n // nb
    tk = _row_tile(s, 1024)
    nk = s // tk

    def body(u_ref, dt_ref, g_ref, db_ref, acc_ref):
        k = pl.program_id(0)
        p = _dot(u_ref[...], dt_ref[...], TN)
        cs = jnp.sum(dt_ref[...].astype(F32), axis=0, keepdims=True)

        @pl.when(k == 0)
        def _():
            acc_ref[...] = p
            db_ref[...] = cs

        @pl.when(k > 0)
        def _():
            acc_ref[...] += p
            db_ref[...] += cs

        @pl.when(k == nk - 1)
        def _():
            for j in range(nb):
                g_ref[j] = acc_ref[:, j * wb:(j + 1) * wb].astype(BF16)

    return _pcall(
        body, name, (_sds((nb, d, wb), BF16), _sds((1, n), F32)), grid=(nk,),
        in_specs=[pl.BlockSpec((tk, d), lambda k: (k, 0)), pl.BlockSpec((tk, n), lambda k: (k, 0))],
        out_specs=(pl.BlockSpec((nb, d, wb), lambda k: (0, 0, 0)), pl.BlockSpec((1, n), lambda k: (0, 0))),
        scratch=[pltpu.VMEM((d, n), F32)], sem=("arbitrary",))(u, dt)


def _glu(t):
    t = t.astype(F32)
    return t[:, :D_MODEL] * jax.nn.sigmoid(t[:, D_MODEL:])


CONV_TILE = 256


def _conv_tile(s):
    return CONV_TILE if s % CONV_TILE == 0 else s


CONV_ROWS = 32
CONV_LANES = 512
SUBLANES = 8


def _shifted_copies(sh_ref, rows):
    for b in range(1, SUBLANES):
        sh_ref[b, 0:rows - SUBLANES, :] = sh_ref[0, b:b + rows - SUBLANES, :]


def conv_fwd(t, cp, name):
    s = t.shape[0]
    d = D_MODEL
    ts = _conv_tile(s)
    per = ts // HALO
    rows = HALO + ts
    lead = HALO - (CONV_WIDTH - 1)
    rc = CONV_ROWS

    def body(t_ref, tp_ref, cp_ref, z_ref, y_ref, sh_ref):
        i = pl.program_id(0)
        sh_ref[0, 0:HALO, :] = jnp.where(i > 0, _glu(tp_ref[...]), 0.0)
        sh_ref[0, HALO:rows, :] = _glu(t_ref[...])
        _shifted_copies(sh_ref, rows)

        def chunk(c, carry):
            r0 = pl.multiple_of(c * rc, rc)
            for lc in range(d // CONV_LANES):
                ln = slice(lc * CONV_LANES, (lc + 1) * CONV_LANES)
                acc = jnp.zeros((rc, CONV_LANES), F32) + cp_ref[ROW_BDW:ROW_BDW + 1, ln]
                for k in range(CONV_WIDTH):
                    a8, b = divmod(lead + k, SUBLANES)
                    acc = acc + cp_ref[k:k + 1, ln] * sh_ref[b, pl.ds(r0 + SUBLANES * a8, rc), ln]
                y_ref[pl.ds(r0, rc), ln] = acc
            y = y_ref[pl.ds(r0, rc), :]
            mu = jnp.mean(y, axis=-1, keepdims=True)
            yc = y - mu
            rstd = lax.rsqrt(jnp.mean(yc * yc, axis=-1, keepdims=True) + EPS)
            yn = yc * rstd * cp_ref[ROW_LNG:ROW_LNG + 1, :] + cp_ref[ROW_LNB:ROW_LNB + 1, :]
            z_ref[pl.ds(r0, rc), :] = (yn * jax.nn.sigmoid(yn)).astype(BF16)
            return carry

        lax.fori_loop(0, ts // rc, chunk, 0)

    row = pl.BlockSpec((ts, d), lambda i: (i, 0))
    return _pcall(
        body, name, (_sds((s, d), BF16), _sds((s, d), F32)), grid=(s // ts,),
        in_specs=[pl.BlockSpec((ts, 2 * d), lambda i: (i, 0)),
                  pl.BlockSpec((HALO, 2 * d), lambda i: (jnp.maximum(i * per - 1, 0), 0)),
                  pl.BlockSpec((PACK_ROWS, d), lambda i: (0, 0))],
        out_specs=(row, row),
        scratch=[pltpu.VMEM((SUBLANES, rows, d), F32)], sem=("parallel",))(t, t, cp)


def conv_bwd(t, y, dz, cp, name):
    s = t.shape[0]
    d = D_MODEL
    ts = _conv_tile(s)
    per = ts // HALO
    nt = s // ts
    te = ts + HALO
    rc = CONV_ROWS

    def body(t_ref, y_ref, yn_ref, dz_ref, dzn_ref, cp_ref, dt_ref, st_ref, shd_ref, dw_ref):
        i = pl.program_id(0)
        last = i == nt - 1

        @pl.when(i == 0)
        def _():
            st_ref[...] = jnp.zeros_like(st_ref)
            dw_ref[...] = jnp.zeros_like(dw_ref)

        gain = cp_ref[ROW_LNG:ROW_LNG + 1, :]

        def ln_bwd(yv, dzv):
            mu = jnp.mean(yv, axis=-1, keepdims=True)
            yc = yv - mu
            rstd = lax.rsqrt(jnp.mean(yc * yc, axis=-1, keepdims=True) + EPS)
            yh = yc * rstd
            yn = yh * gain + cp_ref[ROW_LNB:ROW_LNB + 1, :]
            sig = jax.nn.sigmoid(yn)
            dyn = dzv * (sig * (1.0 + yn * (1.0 - sig)))
            dyh = dyn * gain
            dy = rstd * (dyh - jnp.mean(dyh, axis=-1, keepdims=True)
                         - yh * jnp.mean(dyh * yh, axis=-1, keepdims=True))
            return dy, dyn, yh

        def norm_chunk(c, carry):
            r0 = pl.multiple_of(c * rc, rc)
            dy, dyn, yh = ln_bwd(y_ref[pl.ds(r0, rc), :], dz_ref[pl.ds(r0, rc), :])
            shd_ref[0, pl.ds(r0, rc), :] = dy
            st_ref[ROW_BDW:ROW_BDW + 1, :] += jnp.sum(dy, axis=0, keepdims=True)
            st_ref[ROW_LNG:ROW_LNG + 1, :] += jnp.sum(dyn * yh, axis=0, keepdims=True)
            st_ref[ROW_LNB:ROW_LNB + 1, :] += jnp.sum(dyn, axis=0, keepdims=True)
            return carry

        lax.fori_loop(0, ts // rc, norm_chunk, 0)
        dy_halo, _, _ = ln_bwd(yn_ref[...], jnp.where(last, 0.0, dzn_ref[...]))
        shd_ref[0, ts:te, :] = dy_halo
        _shifted_copies(shd_ref, te)

        def tap_chunk(c, carry):
            r0 = pl.multiple_of(c * rc, rc)
            for lc in range(d // CONV_LANES):
                ln = slice(lc * CONV_LANES, (lc + 1) * CONV_LANES)
                ln2 = slice(d + lc * CONV_LANES, d + (lc + 1) * CONV_LANES)
                t1 = t_ref[pl.ds(r0, rc), ln].astype(F32)
                sg = jax.nn.sigmoid(t_ref[pl.ds(r0, rc), ln2].astype(F32))
                a = t1 * sg
                da = jnp.zeros((rc, CONV_LANES), F32)
                for k in range(CONV_WIDTH):
                    a8, b = divmod(CONV_WIDTH - 1 - k, SUBLANES)
                    e = shd_ref[b, pl.ds(r0 + SUBLANES * a8, rc), ln]
                    da = da + cp_ref[k:k + 1, ln] * e
                    dw_ref[k, :, ln] += jnp.sum((a * e).reshape(rc // SUBLANES, SUBLANES, CONV_LANES), axis=0)
                dt_ref[pl.ds(r0, rc), ln] = (da * sg).astype(BF16)
                dt_ref[pl.ds(r0, rc), ln2] = (da * t1 * sg * (1.0 - sg)).astype(BF16)
            return carry

        lax.fori_loop(0, ts // rc, tap_chunk, 0)

        @pl.when(last)
        def _():
            for k in range(CONV_WIDTH):
                st_ref[k:k + 1, :] = jnp.sum(dw_ref[k], axis=0, keepdims=True)

    last_halo = s // HALO - 1
    row = pl.BlockSpec((ts, d), lambda i: (i, 0))
    halo = pl.BlockSpec((HALO, d), lambda i: (jnp.minimum((i + 1) * per, last_halo), 0))
    return _pcall(
        body, name, (_sds((s, 2 * d), BF16), _sds((PACK_ROWS, d), F32)), grid=(nt,),
        in_specs=[pl.BlockSpec((ts, 2 * d), lambda i: (i, 0)), row, halo, row, halo,
                  pl.BlockSpec((PACK_ROWS, d), lambda i: (0, 0))],
        out_specs=(pl.BlockSpec((ts, 2 * d), lambda i: (i, 0)), pl.BlockSpec((PACK_ROWS, d), lambda i: (0, 0))),
        scratch=[pltpu.VMEM((SUBLANES, te, d), F32), pltpu.VMEM((CONV_WIDTH, SUBLANES, d), F32)],
        sem=("arbitrary",))(t, y, y, dz, dz, cp)


def _bucket_table():
    qi = np.arange(BLOCK, dtype=np.int64)[:, None]
    kj = np.arange(2 * BLOCK, dtype=np.int64)[None, :]
    dist = qi + BLOCK - kj
    max_exact = N_BUCKETS // 2
    dd = np.maximum(dist, 0)
    ratio = (np.maximum(dd, 1).astype(np.float32) / np.float32(max_exact)).astype(np.float32)
    log_ratio = (np.log(ratio).astype(np.float32) / np.float32(math.log(MAX_DISTANCE / max_exact))).astype(np.float32)
    large = max_exact + (log_ratio * np.float32(N_BUCKETS - max_exact)).astype(np.int32)
    large = np.minimum(large, N_BUCKETS - 1)
    bucket = np.where(dd < max_exact, dd, large)
    return np.where((dist >= 0) & (dist < BLOCK), bucket, -1).astype(np.int32)


def bias_table(rel_bias, bucket, name):
    def body(rb_ref, bk_ref, o_ref):
        bk = bk_ref[...]
        for h in range(N_HEADS):
            acc = jnp.full((BLOCK, 2 * BLOCK), NEG_INF, F32)
            for b in range(N_BUCKETS):
                acc = jnp.where(bk == b, rb_ref[b, h], acc)
            o_ref[h] = acc

    return _pcall(body, name, _sds((N_HEADS, BLOCK, 2 * BLOCK), F32),
                  in_specs=[pl.BlockSpec(memory_space=pltpu.SMEM), pl.BlockSpec(memory_space=pltpu.VMEM)],
                  out_specs=pl.BlockSpec(memory_space=pltpu.VMEM))(rel_bias, bucket)


def bias_grad(dba, dbb, bucket, name):
    def body(a_ref, b_ref, bk_ref, o_ref):
        bk = bk_ref[...]
        for h in range(N_HEADS):
            db = a_ref[h] + b_ref[h]
            for b in range(N_BUCKETS):
                o_ref[b, h] = jnp.sum(jnp.where(bk == b, db, 0.0))

    vm = pl.BlockSpec(memory_space=pltpu.VMEM)
    return _pcall(body, name, _sds((N_BUCKETS, N_HEADS), F32), in_specs=[vm, vm, vm],
                  out_specs=pl.BlockSpec(memory_space=pltpu.SMEM))(dba, dbb, bucket)


def _band_specs():
    cur = pl.BlockSpec((BLOCK, 2 * KV_DIM), lambda n: (n, 0))
    prev = pl.BlockSpec((BLOCK, 2 * KV_DIM), lambda n: (jnp.maximum(n - 1, 0), 0))
    return cur, prev


def _scores(q_h, k_h, bias_h, first_row, sink):
    sc = _dot(q_h, k_h, NT) * (HEAD_DIM ** -0.5) + bias_h + first_row
    m = jnp.maximum(jnp.max(sc, axis=-1, keepdims=True), sink)
    p = jnp.exp(sc - m)
    e_sink = jnp.exp(sink - m)
    den = jnp.sum(p, axis=-1, keepdims=True) + e_sink
    return p, e_sink, den


def _first_block_row(n):
    col = lax.broadcasted_iota(jnp.int32, (1, 2 * BLOCK), 1)
    return jnp.where((col < BLOCK) & (n == 0), NEG_INF, 0.0)


def _head_lanes(hk, g):
    h = hk * GROUP + g
    return slice(h * HEAD_DIM, (h + 1) * HEAD_DIM)


def _group_rows(x_ref, hk):
    return jnp.concatenate([x_ref[:, _head_lanes(hk, g)] for g in range(GROUP)], axis=0)


def _group_bias(bias_ref, hk):
    return bias_ref[hk * GROUP:(hk + 1) * GROUP].reshape(GROUP * BLOCK, 2 * BLOCK)


def _group_sinks(sink_ref, hk):
    head = lax.broadcasted_iota(jnp.int32, (GROUP * BLOCK, 1), 0) // BLOCK
    col = jnp.zeros((GROUP * BLOCK, 1), F32) + sink_ref[0, hk * GROUP]
    for g in range(1, GROUP):
        col = jnp.where(head == g, sink_ref[0, hk * GROUP + g], col)
    return col


def attn_fwd(q, kv, bias, sinks, name):
    s = q.shape[0]
    nb = s // BLOCK

    def body(sink_ref, q_ref, kvc_ref, kvp_ref, bias_ref, o_ref, band_ref):
        n = pl.program_id(0)
        band_ref[0:BLOCK, :] = kvp_ref[...]
        band_ref[BLOCK:2 * BLOCK, :] = kvc_ref[...]
        first_row = _first_block_row(n)
        for hk in range(N_KV_HEADS):
            k_h = band_ref[:, hk * HEAD_DIM:(hk + 1) * HEAD_DIM]
            v_h = band_ref[:, KV_DIM + hk * HEAD_DIM:KV_DIM + (hk + 1) * HEAD_DIM]
            p, _, den = _scores(_group_rows(q_ref, hk), k_h, _group_bias(bias_ref, hk), first_row,
                                _group_sinks(sink_ref, hk))
            o = _dot((p * (1.0 / den)).astype(BF16), v_h, NN).astype(BF16)
            for g in range(GROUP):
                o_ref[:, _head_lanes(hk, g)] = o[g * BLOCK:(g + 1) * BLOCK]

    cur, prev = _band_specs()
    qs = pl.BlockSpec((BLOCK, D_MODEL), lambda n: (n, 0))
    return _pcall(
        body, name, _sds((s, D_MODEL), BF16), grid=(nb,),
        in_specs=[pl.BlockSpec(memory_space=pltpu.SMEM), qs, cur, prev,
                  pl.BlockSpec((N_HEADS, BLOCK, 2 * BLOCK), lambda n: (0, 0, 0))],
        out_specs=qs, scratch=[pltpu.VMEM((2 * BLOCK, 2 * KV_DIM), BF16)],
        sem=("parallel",))(sinks, q, kv, kv, bias)


def attn_bwd(q, kv, do, bias, sinks, name):
    s = q.shape[0]
    nb = s // BLOCK
    scale = HEAD_DIM ** -0.5

    def body(sink_ref, q_ref, do_ref, kvc_ref, kvp_ref, bias_ref, dq_ref, dkv_ref, db_ref, dsink_ref,
             band_ref, dsacc_ref):
        n = pl.program_id(0)
        band_ref[0:BLOCK, :] = kvp_ref[...]
        band_ref[BLOCK:2 * BLOCK, :] = kvc_ref[...]
        first_row = _first_block_row(n)
        lane = lax.broadcasted_iota(jnp.int32, (BLOCK, BLOCK), 1)

        @pl.when(n == 0)
        def _():
            db_ref[...] = jnp.zeros_like(db_ref)
            dsacc_ref[...] = jnp.zeros_like(dsacc_ref)

        for hk in range(N_KV_HEADS):
            k_h = band_ref[:, hk * HEAD_DIM:(hk + 1) * HEAD_DIM]
            v_h = band_ref[:, KV_DIM + hk * HEAD_DIM:KV_DIM + (hk + 1) * HEAD_DIM]
            q_g = _group_rows(q_ref, hk)
            do_g = _group_rows(do_ref, hk)
            p, e_sink, den = _scores(q_g, k_h, _group_bias(bias_ref, hk), first_row, _group_sinks(sink_ref, hk))
            inv = 1.0 / den
            p = p * inv
            dp = _dot(do_g, v_h, NT)
            delta = jnp.sum(p * dp, axis=-1, keepdims=True)
            ds = p * (dp - delta)
            db_ref[hk * GROUP:(hk + 1) * GROUP] += ds.reshape(GROUP, BLOCK, 2 * BLOCK)
            d_sink = -(e_sink * inv) * delta
            for g in range(GROUP):
                dsacc_ref[...] += jnp.where(lane == hk * GROUP + g, d_sink[g * BLOCK:(g + 1) * BLOCK], 0.0)
            dsb = ds.astype(BF16)
            dq = (_dot(dsb, k_h, NN) * scale).astype(BF16)
            for g in range(GROUP):
                dq_ref[:, _head_lanes(hk, g)] = dq[g * BLOCK:(g + 1) * BLOCK]
            dkv_ref[:, hk * HEAD_DIM:(hk + 1) * HEAD_DIM] = (_dot(dsb, q_g, TN) * scale).astype(BF16)
            dkv_ref[:, KV_DIM + hk * HEAD_DIM:KV_DIM + (hk + 1) * HEAD_DIM] = _dot(
                p.astype(BF16), do_g, TN).astype(BF16)

        @pl.when(n == nb - 1)
        def _():
            dsink_ref[...] = jnp.sum(dsacc_ref[...], axis=0, keepdims=True)

    cur, prev = _band_specs()
    qs = pl.BlockSpec((BLOCK, D_MODEL), lambda n: (n, 0))
    full_b = pl.BlockSpec((N_HEADS, BLOCK, 2 * BLOCK), lambda n: (0, 0, 0))
    return _pcall(
        body, name,
        (_sds((s, D_MODEL), BF16), _sds((nb, 2 * BLOCK, 2 * KV_DIM), BF16),
         _sds((N_HEADS, BLOCK, 2 * BLOCK), F32), _sds((1, BLOCK), F32)),
        grid=(nb,),
        in_specs=[pl.BlockSpec(memory_space=pltpu.SMEM), qs, qs, cur, prev, full_b],
        out_specs=(qs, pl.BlockSpec((None, 2 * BLOCK, 2 * KV_DIM), lambda n: (n, 0, 0)), full_b,
                   pl.BlockSpec((1, BLOCK), lambda n: (0, 0))),
        scratch=[pltpu.VMEM((2 * BLOCK, 2 * KV_DIM), BF16), pltpu.VMEM((BLOCK, BLOCK), F32)],
        sem=("arbitrary",))(sinks, q, do, kv, kv, bias)


def dkv_combine(pa, pb, name):
    nb = pa.shape[0]
    pa2 = pa.reshape(2 * nb, BLOCK, 2 * KV_DIM)
    pb2 = pb.reshape(2 * nb, BLOCK, 2 * KV_DIM)

    def body(ac_ref, an_ref, bc_ref, bn_ref, o_ref):
        n = pl.program_id(0)
        nxt = jnp.where(n == nb - 1, 0.0, an_ref[...].astype(F32) + bn_ref[...].astype(F32))
        o_ref[...] = (ac_ref[...].astype(F32) + bc_ref[...].astype(F32) + nxt).astype(BF16)

    cur = pl.BlockSpec((None, BLOCK, 2 * KV_DIM), lambda n: (2 * n + 1, 0, 0))
    nxt = pl.BlockSpec((None, BLOCK, 2 * KV_DIM), lambda n: (jnp.minimum(2 * n + 2, 2 * nb - 2), 0, 0))
    return _pcall(body, name, _sds((nb * BLOCK, 2 * KV_DIM), BF16), grid=(nb,),
                  in_specs=[cur, nxt, cur, nxt], out_specs=pl.BlockSpec((BLOCK, 2 * KV_DIM), lambda n: (n, 0)),
                  sem=("parallel",))(pa2, pa2, pb2, pb2)


def loss_head(h, g, target, name):
    s, d = h.shape
    tm = _row_tile(s, 512)

    def body(h_ref, g_ref, t_ref, dh_ref, dg_ref, loss_ref):
        i = pl.program_id(0)
        x = h_ref[...]
        r = lax.rsqrt(jnp.mean(x * x, axis=-1, keepdims=True) + EPS)
        xh = x * r
        gv = g_ref[...]
        err = xh * gv - t_ref[...]
        part_loss = jnp.zeros((1, BLOCK), F32) + 0.5 * jnp.sum(jnp.mean(err * err, axis=-1, keepdims=True))
        dy = err * (1.0 / d)
        dxh = dy * gv
        dh_ref[...] = r * (dxh - xh * jnp.mean(dxh * xh, axis=-1, keepdims=True))
        part_g = jnp.sum(dy * xh, axis=0, keepdims=True)

        @pl.when(i == 0)
        def _():
            dg_ref[...] = part_g
            loss_ref[...] = part_loss

        @pl.when(i > 0)
        def _():
            dg_ref[...] += part_g
            loss_ref[...] += part_loss

    row = pl.BlockSpec((tm, d), lambda i: (i, 0))
    vec = pl.BlockSpec((1, d), lambda i: (0, 0))
    return _pcall(body, name, (_sds((s, d), F32), _sds((1, d), F32), _sds((1, BLOCK), F32)), grid=(s // tm,),
                  in_specs=[row, vec, row], out_specs=(row, vec, pl.BlockSpec((1, BLOCK), lambda i: (0, 0))),
                  sem=("arbitrary",))(h, g, target)


def adamw(w, m, v, parts, name, token=None):
    nl, r, c = w.shape
    tr = max(t for t in range(1, min(r, 512) + 1) if r % t == 0 and (t % 16 == 0 or t == r))
    c1 = 1.0 / (1.0 - ADAM_B1 ** ADAM_STEP)
    c2 = 1.0 / (1.0 - ADAM_B2 ** ADAM_STEP)

    def body(w_ref, m_ref, v_ref, p_ref, *rest):
        g_ref, d_ref, nm_ref, nv_ref = rest[-4:]
        g = p_ref[0].astype(F32)
        for dev in range(1, N_DEV):
            g = g + p_ref[dev].astype(F32)
        nm = ADAM_B1 * m_ref[...] + (1.0 - ADAM_B1) * g
        nv = ADAM_B2 * v_ref[...] + (1.0 - ADAM_B2) * (g * g)
        g_ref[...] = g
        nm_ref[...] = nm
        nv_ref[...] = nv
        d_ref[...] = -ADAM_LR * ((nm * c1) / (jnp.sqrt(nv * c2) + ADAM_EPS) + ADAM_WD * w_ref[...])

    blk = pl.BlockSpec((None, tr, c), lambda l, i: (l, i, 0))
    out = _sds((nl, r, c), F32)
    extra = [] if token is None else [token]
    return _pcall(body, name, (out, out, out, out), grid=(nl, r // tr),
                  in_specs=[blk, blk, blk, pl.BlockSpec((N_DEV, None, tr, c), lambda l, i: (0, l, i, 0))]
                  + [ANY] * len(extra),
                  out_specs=(blk, blk, blk, blk), sem=("parallel", "parallel"))(w, m, v, parts, *extra)


def _place():
    x, y, c = lax.axis_index("x"), lax.axis_index("y"), lax.axis_index("c")
    return x, y, c


def _lin(px, py, pc):
    return 4 * px + 2 * py + pc


HBM = pl.BlockSpec(memory_space=pltpu.HBM)
SEM = pl.BlockSpec(memory_space=pltpu.SEMAPHORE)
EFFECT = pltpu.SideEffectType.DATAFLOW_SIDE_EFFECTING
N_PEERS = N_DEV - 1


def _peers_of(x, y, c):
    return [(x, y, 1 - c), (1 - x, y, c), (x, 1 - y, c), (1 - x, 1 - y, c),
            (1 - x, y, 1 - c), (x, 1 - y, 1 - c), (1 - x, 1 - y, 1 - c)]


def _in_hbm(a):
    return pltpu.with_memory_space_constraint(a, pltpu.HBM)


def send_start(name, bufs, copies, n_groups):
    nb = len(bufs)
    per_group = [[i for i, cp in enumerate(copies) if cp[0] == g] for g in range(n_groups)]

    def body(*refs):
        buf = refs[:nb]
        sems = refs[nb:nb + 2 * n_groups]
        token = refs[2 * nb + 2 * n_groups]
        x, y, c = _place()
        me = _lin(x, y, c)
        for g in range(n_groups):
            for slot, i in enumerate(per_group[g]):
                _, s, src_slab, d, land_slab = copies[i]
                for k, peer in enumerate(_peers_of(x, y, c)):
                    pltpu.make_async_remote_copy(
                        src_ref=src_slab(buf[s], _lin(*peer), me), dst_ref=land_slab(buf[d], me),
                        send_sem=sems[2 * g].at[slot * N_PEERS + k], recv_sem=sems[2 * g + 1].at[slot * N_PEERS + k],
                        device_id=peer, device_id_type=MESH).start()
        token[...] = jnp.zeros_like(token)

    sem_shapes = []
    for g in range(n_groups):
        sem_shapes += [pltpu.SemaphoreType.DMA((len(per_group[g]) * N_PEERS,))] * 2
    out = pl.pallas_call(
        body, name=name,
        out_shape=tuple(sem_shapes) + tuple(pltpu.HBM(b.shape, b.dtype) for b in bufs) + (_sds((8, 128), F32),),
        in_specs=[HBM] * nb,
        out_specs=tuple([SEM] * len(sem_shapes)) + tuple([HBM] * nb) + (pl.BlockSpec(memory_space=pltpu.VMEM),),
        input_output_aliases={i: len(sem_shapes) + i for i in range(nb)},
        compiler_params=pltpu.CompilerParams(has_side_effects=EFFECT))(*[_in_hbm(b) for b in bufs])
    sems = [(out[2 * g], out[2 * g + 1]) for g in range(n_groups)]
    return sems, list(out[2 * n_groups:2 * n_groups + nb]), out[2 * n_groups + nb]


N_FIRST = 4
N_RELAY = 3


def _gather_peers(x, y, c):
    first = [(x, y, 1 - c), (1 - x, y, c), (x, 1 - y, c), (1 - x, 1 - y, c)]
    return first, first[1:]


def gather_start(name, bufs, copies, n_groups):
    nb = len(bufs)
    per_group = [[i for i, cp in enumerate(copies) if cp[0] == g] for g in range(n_groups)]

    def body(*refs):
        buf = refs[:nb]
        sems = refs[nb:nb + 2 * n_groups]
        token = refs[2 * nb + 2 * n_groups]
        x, y, c = _place()
        me = _lin(x, y, c)
        first, _ = _gather_peers(x, y, c)
        for g in range(n_groups):
            for slot, i in enumerate(per_group[g]):
                _, d, slab = copies[i]
                for k, peer in enumerate(first):
                    pltpu.make_async_remote_copy(
                        src_ref=slab(buf[d], me), dst_ref=slab(buf[d], me),
                        send_sem=sems[2 * g].at[slot * N_FIRST + k], recv_sem=sems[2 * g + 1].at[slot * N_FIRST + k],
                        device_id=peer, device_id_type=MESH).start()
        token[...] = jnp.zeros_like(token)

    sem_shapes = []
    for g in range(n_groups):
        sem_shapes += [pltpu.SemaphoreType.DMA((len(per_group[g]) * N_FIRST,))] * 2
    out = pl.pallas_call(
        body, name=name,
        out_shape=tuple(sem_shapes) + tuple(pltpu.HBM(b.shape, b.dtype) for b in bufs) + (_sds((8, 128), F32),),
        in_specs=[HBM] * nb,
        out_specs=tuple([SEM] * len(sem_shapes)) + tuple([HBM] * nb) + (pl.BlockSpec(memory_space=pltpu.VMEM),),
        input_output_aliases={i: len(sem_shapes) + i for i in range(nb)},
        compiler_params=pltpu.CompilerParams(has_side_effects=EFFECT))(*[_in_hbm(b) for b in bufs])
    return [(out[2 * g], out[2 * g + 1]) for g in range(n_groups)], list(out[2 * n_groups:2 * n_groups + nb])


def gather_relay(name, bufs, slabs, first_sems, after):
    nb = len(bufs)

    def body(*refs):
        buf = refs[:nb]
        send_a, recv_a = refs[nb], refs[nb + 1]
        send_b, recv_b = refs[nb + 3], refs[nb + 4]
        token = refs[2 * nb + 5]
        x, y, c = _place()
        first, origins = _gather_peers(x, y, c)
        for n, slab in enumerate(slabs):
            for j, origin in enumerate(origins):
                block = slab(buf[n], _lin(*origin))
                pltpu.make_async_remote_copy(
                    src_ref=block, dst_ref=block, send_sem=send_a.at[n * N_FIRST + 1 + j],
                    recv_sem=recv_a.at[n * N_FIRST + 1 + j], device_id=origin, device_id_type=MESH).wait_recv()
                pltpu.make_async_remote_copy(
                    src_ref=block, dst_ref=block, send_sem=send_b.at[n * N_RELAY + j],
                    recv_sem=recv_b.at[n * N_RELAY + j], device_id=first[0], device_id_type=MESH).start()
        token[...] = jnp.zeros_like(token)

    sem_shape = pltpu.SemaphoreType.DMA((nb * N_RELAY,))
    out = pl.pallas_call(
        body, name=name,
        out_shape=(sem_shape, sem_shape) + tuple(pltpu.HBM(b.shape, b.dtype) for b in bufs) + (_sds((8, 128), F32),),
        in_specs=[HBM] * nb + [SEM, SEM, ANY],
        out_specs=(SEM, SEM) + tuple([HBM] * nb) + (pl.BlockSpec(memory_space=pltpu.VMEM),),
        input_output_aliases={i: 2 + i for i in range(nb)},
        compiler_params=pltpu.CompilerParams(has_side_effects=EFFECT))(*bufs, first_sems[0], first_sems[1], after)
    return (out[0], out[1]), list(out[2:2 + nb]), out[2 + nb]


def gather_wait(name, bufs, slabs, first_sems, relay_sems, after):
    nb = len(bufs)

    def body(*refs):
        buf = refs[:nb]
        send_a, recv_a, send_b, recv_b = refs[nb:nb + 4]
        x, y, c = _place()
        me = _lin(x, y, c)
        first, origins = _gather_peers(x, y, c)
        sibling = first[0]
        for n, slab in enumerate(slabs):
            mine = slab(buf[n], me)
            for k, peer in enumerate(first):
                pltpu.make_async_remote_copy(
                    src_ref=mine, dst_ref=mine, send_sem=send_a.at[n * N_FIRST + k],
                    recv_sem=recv_a.at[n * N_FIRST + k], device_id=peer, device_id_type=MESH).wait_send()
            theirs = slab(buf[n], _lin(*sibling))
            pltpu.make_async_remote_copy(
                src_ref=theirs, dst_ref=theirs, send_sem=send_a.at[n * N_FIRST], recv_sem=recv_a.at[n * N_FIRST],
                device_id=sibling, device_id_type=MESH).wait_recv()
            for j, (ox, oy, oc) in enumerate(origins):
                sent = slab(buf[n], _lin(ox, oy, oc))
                got = slab(buf[n], _lin(ox, oy, 1 - oc))
                pltpu.make_async_remote_copy(
                    src_ref=sent, dst_ref=got, send_sem=send_b.at[n * N_RELAY + j],
                    recv_sem=recv_b.at[n * N_RELAY + j], device_id=sibling, device_id_type=MESH).wait()

    out = pl.pallas_call(
        body, name=name, out_shape=tuple(pltpu.HBM(b.shape, b.dtype) for b in bufs),
        in_specs=[HBM] * nb + [SEM] * 4 + [ANY], out_specs=tuple([HBM] * nb),
        input_output_aliases={i: i for i in range(nb)},
        compiler_params=pltpu.CompilerParams(has_side_effects=EFFECT))(
            *bufs, first_sems[0], first_sems[1], relay_sems[0], relay_sems[1], after)
    return list(out)


def send_wait(name, bufs, copies, sems, after):
    nb = len(bufs)

    def body(*refs):
        buf = refs[:nb]
        send_sems, recv_sems = refs[nb], refs[nb + 1]
        x, y, c = _place()
        me = _lin(x, y, c)
        for slot, (s, src_slab, d, land_slab) in enumerate(copies):
            for k, peer in enumerate(_peers_of(x, y, c)):
                j = _lin(*peer)
                cp = pltpu.make_async_remote_copy(
                    src_ref=src_slab(buf[s], j, me), dst_ref=land_slab(buf[d], j),
                    send_sem=send_sems.at[slot * N_PEERS + k], recv_sem=recv_sems.at[slot * N_PEERS + k],
                    device_id=peer, device_id_type=MESH)
                cp.wait_send()
                cp.wait_recv()

    out = pl.pallas_call(
        body, name=name, out_shape=tuple(pltpu.HBM(b.shape, b.dtype) for b in bufs),
        in_specs=[HBM] * nb + [SEM, SEM, ANY], out_specs=tuple([HBM] * nb),
        input_output_aliases={i: i for i in range(nb)},
        compiler_params=pltpu.CompilerParams(has_side_effects=EFFECT))(*bufs, sems[0], sems[1], after)
    return list(out)


def local_step(x, target, weights, rep, emit):
    s = x.shape[0]
    bucket = jnp.asarray(_bucket_table())
    bias = bias_table(rep["rel_bias"], bucket, "bias_table")
    h = x
    saved = []
    kv = None
    h_kv = u_kv = None
    small = None
    u = rms_fwd(h, rep["norm_mix"][0:1], "rms_mix_fwd0")
    for l in range(4):
        g_ffn = rep["norm_ffn"][l:l + 1]
        rec = {"h_in": h, "u": u}
        if l < 2:
            w = weights(f"conv{l}", u)
            if l == 0:
                small = w
            cp = small["cp"][l]
            t = pw1_fwd(u, w["pw1"], small["b_pw1"], l, f"pw1_fwd{l}")
            z, y = conv_fwd(t, cp, f"conv_fwd{l}")
            relay_tok = weights(f"ffn{l}", z, relay=True)
            h, uf = mm_nn(f"pw2_fwd{l}", z, w["pw2"], (None, D_MODEL, D_MODEL), lambda i, j: (0, 0, j), D_MODEL,
                          D_MODEL, F32, res=h, norm=g_ffn, token=relay_tok,
                          bias=(small["b_pw2"], pl.BlockSpec((None, 1, D_MODEL), lambda i, j, l=l: (l, 0, j))))
            rec.update(t=t, z=z, y=y, cp=cp)
        else:
            a = l - 2
            w = weights(f"attn{a}", u)
            if a == 0:
                h_kv = h
                w_kv = w["wkv"]
                kv = mm_nn("kv_fwd", u_kv, w_kv, (D_MODEL, 2 * KV_DIM), lambda i, j: (0, 0), 2 * KV_DIM,
                           2 * KV_DIM, BF16)
            q = mm_nn(f"q_fwd{a}", u, w["wq"], (None, D_MODEL, D_MODEL), lambda i, j: (0, 0, j), D_MODEL, D_MODEL,
                      BF16)
            o = attn_fwd(q, kv, bias, rep["sinks"][a:a + 1], f"attn_fwd{a}")
            relay_tok = weights(f"ffn{l}", o, relay=True)
            h, uf = mm_nn(f"o_fwd{a}", o, w["wo"], (None, D_MODEL, D_MODEL), lambda i, j: (0, 0, j), D_MODEL,
                          D_MODEL, F32, res=h, norm=g_ffn, token=relay_tok)
            rec.update(q=q, o=o)
        rec["w"] = w
        rec["h_mid"] = h
        wf = weights(f"ffn{l}", uf)
        relay_tok = weights(("conv1", "attn0", "attn1")[l], uf, relay=True) if l < 3 else None
        nxt = [] if l == 3 else [rep["norm_mix"][l + 1:l + 2]] + ([rep["norm_kv"]] if l == 1 else [])
        h, gu, *normed = ffn_fwd(uf, h, wf["up"], wf["down"], 0, f"ffn_fwd{l}", norms=nxt, token=relay_tok)
        if normed:
            u = normed[0]
        if l == 1:
            u_kv = normed[1]
        rec.update(uf=uf, gu=gu, wf=wf)
        saved.append(rec)

    dh, d_nfin, loss = loss_head(h, rep["norm_final"], target, "loss_head")

    d_mix, d_ffn = [None] * 4, [None] * 4
    cp_grads = [None, None]
    dkv_parts, dbias_parts, dsinks = [], [], [None, None]
    d_nkv = None
    full_rows = lambda tk: (tk, D_MODEL)
    tok = None
    for l in reversed(range(4)):
        rec = saved[l]
        w, wf = rec["w"], rec["wf"]
        grads = {}
        g_mix = rep["norm_mix"][l:l + 1]
        g_ffn = rep["norm_ffn"][l:l + 1]
        dh_mid, d_ffn[l], act, dgu = ffn_bwd(dh, rec["gu"], wf["up"], wf["down"], 0, rec["h_mid"], g_ffn,
                                             f"ffn_bwd{l}", token=tok)
        g_down = mm_tn(
            f"down_grad{l}", act, dh, groups=4, a_block=lambda tk: (None, tk, FF_CHUNK), a_index=lambda j, k: (j, k, 0),
            b_block=full_rows, b_index=lambda j, k: (k, 0), o_block=(None, FF_CHUNK, D_MODEL),
            o_index=lambda j, k: (j, 0, 0), o_shape=(4, FF_CHUNK, D_MODEL), acc_shape=(FF_CHUNK, D_MODEL), tk=s)
        g_up = mm_tn(
            f"up_grad{l}", dgu.reshape(8, s, FF_CHUNK), rec["uf"], groups=8, a_block=lambda tk: (None, tk, FF_CHUNK),
            a_index=lambda j, k: (j, k, 0), b_block=full_rows, b_index=lambda j, k: (k, 0),
            o_block=(None, FF_CHUNK, D_MODEL), o_index=lambda j, k: (j, 0, 0), o_shape=(8, FF_CHUNK, D_MODEL),
            acc_shape=(FF_CHUNK, D_MODEL), tk=s)
        tok = emit(f"ffn{l}", {"up": g_up, "down": g_down.reshape(D_FF, D_MODEL)})
        dh = dh_mid
        if l < 2:
            dz = mm_nt(f"pw2_bwd{l}", dh, w["pw2"], (None, D_MODEL, D_MODEL), lambda i, k: (0, 0, 0), D_MODEL, F32,
                       token=tok)
            grads["pw2"], db2 = mm_tn(
                f"pw2_grad{l}", rec["z"], dh, groups=1, a_block=full_rows, a_index=lambda j, k: (k, 0),
                b_block=full_rows, b_index=lambda j, k: (k, 0), o_block=(D_MODEL, D_MODEL), o_index=lambda j, k: (0, 0),
                o_shape=(D_MODEL, D_MODEL), acc_shape=(D_MODEL, D_MODEL),
                colsum=((1, D_MODEL), pl.BlockSpec((1, D_MODEL), lambda j, k: (0, 0))))
            dt, stats = conv_bwd(rec["t"], rec["y"], dz, rec["cp"], f"conv_bwd{l}")
            grads["pw1"], db1 = pw1_grad(rec["u"], dt, f"pw1_grad{l}")
            cp_grads[l] = (stats, db2, db1)
            tok = emit(f"conv{l}", grads)
            dh, d_mix[l] = pw1_bwd(dt, w["pw1"], rec["h_in"], g_mix, dh, f"pw1_bwd{l}", token=tok)
        else:
            a = l - 2
            do = mm_nt(f"o_bwd{a}", dh, w["wo"], (None, D_MODEL, D_MODEL), lambda i, k: (0, 0, 0), D_MODEL, BF16,
                       token=tok)
            tok = None
            grads["wo"] = mm_tn(
                f"wo_grad{a}", rec["o"], dh, groups=1, a_block=full_rows, a_index=lambda j, k: (k, 0),
                b_block=full_rows, b_index=lambda j, k: (k, 0), o_block=(D_MODEL, D_MODEL), o_index=lambda j, k: (0, 0),
                o_shape=(D_MODEL, D_MODEL), acc_shape=(D_MODEL, D_MODEL))
            dq, dkv_p, dbias_p, dsinks[a] = attn_bwd(rec["q"], kv, do, bias, rep["sinks"][a:a + 1], f"attn_bwd{a}")
            dkv_parts.append(dkv_p)
            dbias_parts.append(dbias_p)
            grads["wq"] = mm_tn(
                f"wq_grad{a}", rec["u"], dq, groups=1, a_block=full_rows, a_index=lambda j, k: (k, 0),
                b_block=full_rows, b_index=lambda j, k: (k, 0), o_block=(D_MODEL, D_MODEL), o_index=lambda j, k: (0, 0),
                o_shape=(D_MODEL, D_MODEL), acc_shape=(D_MODEL, D_MODEL))
            if a == 1:
                tok = emit("attn1", grads)
            dh, d_mix[l] = nt_rms_bwd(f"q_bwd{a}", dq, w["wq"], (None, D_MODEL, D_MODEL), lambda i: (0, 0, 0),
                                      rec["h_in"], g_mix, dh)
        if l == 2:
            dkv = dkv_combine(dkv_parts[0], dkv_parts[1], "dkv_combine")
            grads["wkv"] = mm_tn(
                "wkv_grad", u_kv, dkv, groups=1, a_block=full_rows, a_index=lambda j, k: (k, 0),
                b_block=lambda tk: (tk, 2 * KV_DIM), b_index=lambda j, k: (k, 0), o_block=(D_MODEL, 2 * KV_DIM),
                o_index=lambda j, k: (0, 0), o_shape=(D_MODEL, 2 * KV_DIM), acc_shape=(D_MODEL, 2 * KV_DIM))
            tok = emit("attn0", grads)
            dh, d_nkv = nt_rms_bwd("kv_bwd", dkv, w_kv, (D_MODEL, 2 * KV_DIM), lambda i: (0, 0), h_kv,
                                   rep["norm_kv"], dh)

    d_relb = bias_grad(dbias_parts[0], dbias_parts[1], bucket, "bias_grad")
    d_sinks = jnp.concatenate([dsinks[0][0, :N_HEADS], dsinks[1][0, :N_HEADS]])
    tail = jnp.zeros((D_MODEL,), F32)
    rep_grad = jnp.concatenate([
        jnp.concatenate(d_mix, axis=0), jnp.concatenate(d_ffn, axis=0), d_nkv, d_nfin,
        tail.at[:2 * N_HEADS].set(d_sinks)[None], tail.at[:N_BUCKETS * N_HEADS].set(d_relb.reshape(-1))[None],
        tail.at[0].set(loss[0, 0])[None], jnp.zeros((REP_ROWS - ROW_LOSS - 1, D_MODEL), F32)], axis=0)
    return dh, cp_grads, rep_grad


def _pack_conv(w_dw, b_dw, ln_g, ln_b, b_pw2, b_pw1):
    rows = [w_dw, b_dw[:, None], ln_g[:, None], ln_b[:, None], b_pw2[:, None], b_pw1.reshape(2, 2, 128),
            jnp.zeros((2, PACK_ROWS - ROW_BPW1 - 2, 128), F32)]
    return jnp.concatenate(rows, axis=1)


def _unpack_conv(p):
    return (p[:, :CONV_WIDTH], p[:, ROW_BDW], p[:, ROW_LNG], p[:, ROW_LNB], p[:, ROW_BPW2],
            p[:, ROW_BPW1:ROW_BPW1 + 2].reshape(2, 256))


def _pack_rep(norm_mix, norm_ffn, norm_kv, norm_final, sinks, rel_bias):
    tail = jnp.zeros((D_MODEL,), F32)
    return jnp.concatenate([
        norm_mix, norm_ffn, norm_kv[None], norm_final[None], tail.at[:2 * N_HEADS].set(sinks.reshape(-1))[None],
        tail.at[:N_BUCKETS * N_HEADS].set(rel_bias.reshape(-1))[None],
        jnp.zeros((REP_ROWS - ROW_RELB - 1, D_MODEL), F32)], axis=0)


def _unpack_rep(p):
    return (p[0:4], p[4:8], p[ROW_NKV], p[ROW_NFIN], p[ROW_SINK, :2 * N_HEADS].reshape(2, N_HEADS),
            p[ROW_RELB, :N_BUCKETS * N_HEADS].reshape(N_BUCKETS, N_HEADS))


def kernel(x, norm_mix, norm_ffn, conv_w_pw1, conv_b_pw1, conv_w_dw, conv_b_dw, conv_ln_g, conv_ln_b, conv_w_pw2, conv_b_pw2, norm_kv, w_kv, w_q, w_o, sinks, rel_bias, ffn_w_up, ffn_w_down, norm_final, loss_target, m_norm_mix, m_norm_ffn, m_conv_w_pw1, m_conv_b_pw1, m_conv_w_dw, m_conv_b_dw, m_conv_ln_g, m_conv_ln_b, m_conv_w_pw2, m_conv_b_pw2, m_norm_kv, m_w_kv, m_w_q, m_w_o, m_sinks, m_rel_bias, m_ffn_w_up, m_ffn_w_down, m_norm_final, v_norm_mix, v_norm_ffn, v_conv_w_pw1, v_conv_b_pw1, v_conv_w_dw, v_conv_b_dw, v_conv_ln_g, v_conv_ln_b, v_conv_w_pw2, v_conv_b_pw2, v_norm_kv, v_w_kv, v_w_q, v_w_o, v_sinks, v_rel_bias, v_ffn_w_up, v_ffn_w_down, v_norm_final):
    s = x.shape[1]
    d = D_MODEL
    rsh = d // N_DEV
    dsh = D_FF // N_DEV

    me = _lin(*_place())
    lead_slab = lambda ref, j: ref.at[j]
    rows_of = lambda rows: (lambda ref, j: ref.at[pl.ds(j * rows, rows), :])

    conv_pack = _pack_conv(conv_w_dw, conv_b_dw, conv_ln_g, conv_ln_b, conv_b_pw2, conv_b_pw1)
    ag_order = ["conv0", "ffn0", "conv1", "ffn1", "attn0", "ffn2", "attn1", "ffn3"]
    ag_land, ag_copies, ag_members = [], [], {g: [] for g in ag_order}

    def gather(group, key, shard, land_shape, at, land_slab):
        i = len(ag_land)
        ag_land.append(lax.dynamic_update_slice(lax.empty(land_shape, shard.dtype), shard, at(me)))
        ag_copies.append((ag_order.index(group), i, land_slab))
        ag_members[group].append((key, i, land_slab))

    cols_at0 = lambda ref, j: ref.at[0, j]
    rows_at0 = lambda rows: (lambda ref, j: ref.at[0, pl.ds(j * rows, rows), :])
    col_at = lambda m: (0, m, 0, 0)
    row_at = lambda rows: (lambda m: (0, m * rows, 0))
    for l in range(2):
        gather(f"conv{l}", "pw1", conv_w_pw1[l].astype(BF16)[None, None], (1, N_DEV, d, 256), col_at, cols_at0)
        gather(f"conv{l}", "pw2", conv_w_pw2[l].astype(BF16)[None], (1, d, d), row_at(rsh), rows_at0(rsh))
    gather("conv0", "pack", conv_pack[None], (N_DEV, 2, PACK_ROWS, 128), lambda m: (m, 0, 0, 0), lead_slab)
    gather("attn0", "wkv", w_kv.astype(BF16), (d, 2 * KV_DIM), lambda m: (m * rsh, 0), rows_of(rsh))
    for a in range(2):
        gather(f"attn{a}", "wq", w_q[a].astype(BF16)[None], (1, d, d), row_at(rsh), rows_at0(rsh))
        gather(f"attn{a}", "wo", w_o[a].astype(BF16)[None], (1, d, d), row_at(rsh), rows_at0(rsh))
    up_t, m_up_t, v_up_t = (jnp.swapaxes(a, 1, 2) for a in (ffn_w_up, m_ffn_w_up, v_ffn_w_up))
    for l in range(4):
        gather(f"ffn{l}", "up", up_t[l].astype(BF16)[None, None], (1, N_DEV, FF_CHUNK, d), col_at, cols_at0)
        gather(f"ffn{l}", "down", ffn_w_down[l].astype(BF16)[None], (1, D_FF, d), row_at(dsh), rows_at0(dsh))
    ag_sems, ag_land_thru = gather_start("ag_start", ag_land, ag_copies, len(ag_order))
    relayed = {}

    def weights(group, after, relay=False):
        members = ag_members[group]
        slabs = [slab for _, _, slab in members]
        first_sems = ag_sems[ag_order.index(group)]
        if group not in relayed:
            relayed[group] = gather_relay(f"ag_relay_{group}", [ag_land_thru[i] for _, i, _ in members], slabs,
                                          first_sems, after)
        relay_sems, bufs, token = relayed[group]
        if relay:
            return token
        lands = gather_wait(f"ag_wait_{group}", bufs, slabs, first_sems, relay_sems, after)
        w = {key: land for (key, _, _), land in zip(members, lands)}
        if "up" in w:
            w["up"] = w["up"].reshape(1, 2, 4, FF_CHUNK, d)
            w["down"] = w["down"].reshape(1, 4, FF_CHUNK, d)
        if "pack" in w:
            pack_g = w.pop("pack")
            w["cp"] = jnp.transpose(pack_g, (1, 2, 0, 3)).reshape(2, PACK_ROWS, d)
            w["b_pw1"] = pack_g[:, :, ROW_BPW1:ROW_BPW1 + 2, :].transpose(1, 0, 2, 3).reshape(2, 1, 2 * d)
            w["b_pw2"] = w["cp"][:, ROW_BPW2:ROW_BPW2 + 1, :]
        return w

    shard_shapes = {"pw1": (d, 256), "pw2": (rsh, d), "wkv": (rsh, 2 * KV_DIM), "wq": (rsh, d), "wo": (rsh, d),
                    "up": (FF_CHUNK, d), "down": (dsh, d), "cp": (2, PACK_ROWS, 128), "rep": (REP_ROWS, d)}
    n_layers = {"pw1": 2, "pw2": 2, "wkv": 1, "wq": 2, "wo": 2, "up": 4, "down": 4, "cp": 1, "rep": 1}
    by_lead = (lambda ref, j, me_: ref.at[j], lambda g: lax.dynamic_index_in_dim(g, me, 0, keepdims=False))
    by_rows = lambda rows: (lambda ref, j, me_: ref.at[pl.ds(j * rows, rows), :],
                            lambda g: lax.dynamic_slice_in_dim(g, me * rows, rows, 0))
    all_of = (lambda ref, j, me_: ref, lambda g: g)
    owned = {"pw1": by_lead, "pw2": by_rows(rsh), "wkv": by_rows(rsh), "wq": by_rows(rsh), "wo": by_rows(rsh),
             "up": by_lead, "down": by_rows(dsh), "cp": by_lead, "rep": all_of}
    parts = {}
    pending = {}

    def finish(chain, after):
        keys, bufs, copies, sems, name = pending.pop(chain)
        done = send_wait(f"rs_wait_{name}", bufs, copies, sems, after)
        parts.update(zip(keys, done[len(keys):]))

    def exchange(chain, name, layer, grads):
        keys = list(grads)
        if chain in pending:
            finish(chain, grads[keys[0]])
        lands = []
        for k in keys:
            land = parts.pop(k) if k in parts else lax.empty((N_DEV, n_layers[k]) + shard_shapes[k], grads[k].dtype)
            mine = owned[k][1](grads[k])[None, None]
            lands.append(lax.dynamic_update_slice(land, mine, (me, layer) + (0,) * len(shard_shapes[k])))
        land_at = lambda ref, i: ref.at[i, layer]
        copies = [(0, n, owned[k][0], len(keys) + n, land_at) for n, k in enumerate(keys)]
        sems, thru, token = send_start(f"rs_start_{name}", [grads[k] for k in keys] + lands, copies, 1)
        pending[chain] = (keys, thru, [c[1:] for c in copies], sems[0], name)
        return token

    def emit(group, grads):
        return exchange(group[:-1], group, int(group[-1]), grads)

    rep = {"norm_mix": norm_mix, "norm_ffn": norm_ffn, "norm_kv": norm_kv[None], "norm_final": norm_final[None],
           "sinks": sinks, "rel_bias": rel_bias}

    grad_x, cp_grads, rep_grad = local_step(x[0], loss_target[0], weights, rep, emit)

    cp_full = []
    for l in range(2):
        stats, db2, db1 = cp_grads[l]
        cp_full.append(jnp.concatenate([
            stats[:ROW_BPW2], db2, db1.reshape(N_DEV, 2, 128).transpose(1, 0, 2).reshape(2, d),
            jnp.zeros((PACK_ROWS - ROW_BPW1 - 2, d), F32)], axis=0))
    cp_send = jnp.stack(cp_full).reshape(2, PACK_ROWS, N_DEV, 128).transpose(2, 0, 1, 3)
    tail_token = exchange("tail", "tail", 0, {"cp": cp_send, "rep": rep_grad})

    def update(key, w, m, v, name, token=None):
        p = parts[key]
        w3 = w.reshape(p.shape[1:])
        outs = adamw(w3, m.reshape(w3.shape), v.reshape(w3.shape), p, name, token=token)
        return [o.reshape(w.shape) for o in outs]

    res = {}
    finish("ffn", grad_x)
    up_res = update("up", up_t, m_up_t, v_up_t, "adam_up", tail_token)
    res["ffn_w_up"] = [jnp.swapaxes(o, 1, 2) for o in up_res]
    res["ffn_w_down"] = update("down", ffn_w_down, m_ffn_w_down, v_ffn_w_down, "adam_down", up_res[0])
    finish("attn", res["ffn_w_down"][0])
    res["w_kv"] = update("wkv", w_kv, m_w_kv, v_w_kv, "adam_wkv")
    res["w_q"] = update("wq", w_q, m_w_q, v_w_q, "adam_wq")
    res["w_o"] = update("wo", w_o, m_w_o, v_w_o, "adam_wo")
    finish("conv", res["w_o"][0])
    res["conv_w_pw1"] = update("pw1", conv_w_pw1, m_conv_w_pw1, v_conv_w_pw1, "adam_pw1")
    res["conv_w_pw2"] = update("pw2", conv_w_pw2, m_conv_w_pw2, v_conv_w_pw2, "adam_pw2")
    finish("tail", res["conv_w_pw2"][0])
    m_pack = _pack_conv(m_conv_w_dw, m_conv_b_dw, m_conv_ln_g, m_conv_ln_b, m_conv_b_pw2, m_conv_b_pw1)
    v_pack = _pack_conv(v_conv_w_dw, v_conv_b_dw, v_conv_ln_g, v_conv_ln_b, v_conv_b_pw2, v_conv_b_pw1)
    cp_res = adamw(conv_pack, m_pack, v_pack, parts["cp"].reshape(N_DEV, 2, PACK_ROWS, 128), "adam_conv_pack")
    rep_w = _pack_rep(norm_mix, norm_ffn, norm_kv, norm_final, sinks, rel_bias)
    rep_m = _pack_rep(m_norm_mix, m_norm_ffn, m_norm_kv, m_norm_final, m_sinks, m_rel_bias)
    rep_v = _pack_rep(v_norm_mix, v_norm_ffn, v_norm_kv, v_norm_final, v_sinks, v_rel_bias)
    rep_res = adamw(rep_w[None], rep_m[None], rep_v[None], parts["rep"], "adam_rep")
    loss = rep_res[0][0, ROW_LOSS, 0]

    outs = []
    for kind in range(4):
        cw_dw, cb_dw, cln_g, cln_b, cb_pw2, cb_pw1 = _unpack_conv(cp_res[kind])
        r_mix, r_ffn, r_nkv, r_nfin, r_sinks, r_relb = _unpack_rep(rep_res[kind][0])
        outs += [r_mix, r_ffn, res["conv_w_pw1"][kind], cb_pw1, cw_dw, cb_dw, cln_g, cln_b, res["conv_w_pw2"][kind],
                 cb_pw2, r_nkv, res["w_kv"][kind], res["w_q"][kind], res["w_o"][kind], r_sinks, r_relb,
                 res["ffn_w_up"][kind], res["ffn_w_down"][kind], r_nfin]
    return (loss, grad_x[None], *outs)
```
